```python
import math
import jax
import jax.numpy as jnp
from jax import lax
import numpy as np

D_MODEL = 2048
BATCH = 4
SEQ = 4096
DEPTH = 2

D_MIX = D_MODEL
GLA_HEADS = 4
GLA_DK = D_MIX // 16
GLA_DV = D_MIX // 8
GLA_GATE_RANK = 16
GLA_GATE_TAU = 16.0
GLA_CHUNK = 64
DSA_HEADS = 8
DSA_HEAD_DIM = 128
IDX_HEADS = 16
IDX_DIM = 64
DSA_TOPK_MAX = 256
DSA_Q_BLOCK = 64
ROPE_THETA = 500000.0
ROPE_FRACTION = 4
D_FF = 4 * D_MODEL
EPS = 1e-6
NEG_INF = -1e30

IN_SPLITS = (
    GLA_HEADS * GLA_DK,
    GLA_HEADS * GLA_DK,
    GLA_HEADS * GLA_DV,
    GLA_HEADS * GLA_DV,
    GLA_GATE_RANK,
    DSA_HEADS * DSA_HEAD_DIM,
    DSA_HEADS * DSA_HEAD_DIM,
    DSA_HEADS * DSA_HEAD_DIM,
    IDX_HEADS * IDX_DIM,
    IDX_DIM,
    IDX_HEADS,
)
D_IN = sum(IN_SPLITS)

kernel_name = "hybrid_gla_dsa_sandwich_trunk"


def rms_norm(x, g):
    xf = x.astype(jnp.float32)
    y = xf * lax.rsqrt(jnp.mean(xf * xf, axis=-1, keepdims=True) + EPS)
    return (y * g.astype(jnp.float32)).astype(x.dtype)


def rope_tables(positions, head_dim):
    r = head_dim // ROPE_FRACTION
    inv_freq = ROPE_THETA ** (-(jnp.arange(0, r, 2, dtype=jnp.float32) / r))
    ang = positions.astype(jnp.float32)[..., None] * inv_freq
    return jnp.cos(ang)[:, :, None, :], jnp.sin(ang)[:, :, None, :]


def apply_partial_rope(x, cos, sin):
    r = 2 * cos.shape[-1]
    xr = x[..., :r].astype(jnp.float32)
    x1, x2 = xr[..., : r // 2], xr[..., r // 2:]
    rot = jnp.concatenate([x1 * cos - x2 * sin, x2 * cos + x1 * sin], axis=-1)
    return jnp.concatenate([rot.astype(x.dtype), x[..., r:]], axis=-1)


def gla_chunked(q, k, v, log_a):
    B, S, H, DK = q.shape
    DV = v.shape[-1]
    C = GLA_CHUNK
    N = S // C

    def to_chunks(t):
        return t.reshape(B, N, C, H, t.shape[-1]).transpose(0, 3, 1, 2, 4).astype(jnp.float32)

    qc = to_chunks(q) * (DK ** -0.5)
    kc, vc, gc = to_chunks(k), to_chunks(v), to_chunks(log_a)
    b = jnp.cumsum(gc, axis=3)
    b_last = b[:, :, :, -1:, :]
    q_dec = qc * jnp.exp(b)
    k_inv = kc * jnp.exp(-b)
    k_dec = kc * jnp.exp(b_last - b)
    causal = jnp.tril(jnp.ones((C, C), dtype=bool))
    attn = jnp.where(causal, jnp.einsum('bhncd,bhnjd->bhncj', q_dec, k_inv), 0.0)
    o_intra = jnp.einsum('bhncj,bhnje->bhnce', attn, vc)
    d_state = jnp.einsum('bhncd,bhnce->bhnde', k_dec, vc)
    decay = jnp.exp(b_last[:, :, :, 0, :])

    def step(state, inp):
        dec, ds = inp
        return dec[..., None] * state + ds, state

    _, states = lax.scan(step, jnp.zeros((B, H, DK, DV), jnp.float32),
                         (decay.transpose(2, 0, 1, 3), d_state.transpose(2, 0, 1, 3, 4)))
    states = states.transpose(1, 2, 0, 3, 4)
    o = o_intra + jnp.einsum('bhncd,bhnde->bhnce', q_dec, states)
    return o.transpose(0, 2, 3, 1, 4).reshape(B, S, H, DV).astype(v.dtype)


def dsa_attention(q, k, v, iq, ik, iw):
    B, S, H, Dh = q.shape
    QB = DSA_Q_BLOCK
    NB = S // QB
    topk = min(DSA_TOPK_MAX, S // 4)
    key_pos = jnp.arange(S)
    iw = iw * (IDX_HEADS ** -0.5)

    def blocks(t):
        return t.reshape((B, NB, QB) + t.shape[2:]).swapaxes(0, 1)

    def one_block(args):
        qb, iqb, iwb, start = args
        qpos = start + jnp.arange(QB)
        logits = jnp.einsum('bqhd,bsd->bqsh', iqb, ik) * (IDX_DIM ** -0.5)
        score = jnp.einsum('bqsh,bqh->bqs', jax.nn.relu(logits), iwb).astype(jnp.float32)
        visible = key_pos[None, :] <= qpos[:, None]
        score = jnp.where(visible[None], score, NEG_INF)
        _, sel = lax.top_k(score, topk)
        k_sel = jax.vmap(lambda kb, ib: kb[ib])(k, sel)
        v_sel = jax.vmap(lambda vb, ib: vb[ib])(v, sel)
        s = jnp.einsum('bqhd,bqkhd->bqhk', qb, k_sel).astype(jnp.float32) * (Dh ** -0.5)
        valid = (sel <= qpos[None, :, None])[:, :, None, :]
        p = jax.nn.softmax(jnp.where(valid, s, NEG_INF), axis=-1)
        return jnp.einsum('bqhk,bqkhd->bqhd', p.astype(v.dtype), v_sel)

    starts = jnp.arange(NB, dtype=jnp.int32) * QB
    out = lax.map(one_block, (blocks(q), blocks(iq), blocks(iw), starts))
    return out.swapaxes(0, 1).reshape(B, S, H, Dh)


def hybrid_mixer(h, cos_a, sin_a, cos_i, sin_i, w_in, gla_wa2, gla_ba, gla_norm, w_out):
    B, S, _ = h.shape
    proj = h @ w_in
    points = np.cumsum(IN_SPLITS)[:-1].tolist()
    gq, gk, gv, gg, gr, dq, dk, dv, iq, ik, iw = jnp.split(proj, points, axis=-1)

    log_a = jax.nn.log_sigmoid((gr @ gla_wa2 + gla_ba).astype(jnp.float32)) / GLA_GATE_TAU
    o_gla = gla_chunked(gq.reshape(B, S, GLA_HEADS, GLA_DK), gk.reshape(B, S, GLA_HEADS, GLA_DK),
                        gv.reshape(B, S, GLA_HEADS, GLA_DV), log_a.reshape(B, S, GLA_HEADS, GLA_DK))
    o_gla = rms_norm(o_gla, gla_norm) * jax.nn.silu(gg.reshape(B, S, GLA_HEADS, GLA_DV))

    dq = apply_partial_rope(dq.reshape(B, S, DSA_HEADS, DSA_HEAD_DIM), cos_a, sin_a)
    dk = apply_partial_rope(dk.reshape(B, S, DSA_HEADS, DSA_HEAD_DIM), cos_a, sin_a)
    dv = dv.reshape(B, S, DSA_HEADS, DSA_HEAD_DIM)
    iq = apply_partial_rope(iq.reshape(B, S, IDX_HEADS, IDX_DIM), cos_i, sin_i)
    ik = apply_partial_rope(ik[:, :, None, :], cos_i, sin_i)[:, :, 0, :]
    o_dsa = dsa_attention(dq, dk, dv, iq, ik, iw)

    o = jnp.concatenate([o_gla.reshape(B, S, -1), o_dsa.reshape(B, S, -1)], axis=-1)
    return o @ w_out


def sq_relu_mlp(h, w_up, w_down):
    u = jax.nn.relu(h @ w_up)
    return (u * u) @ w_down


def setup_inputs(seed: int = 0) -> dict:
    key = jax.random.key(seed)
    ks = jax.random.split(key, 16)
    f32 = jnp.float32

    def gain(k, shape):
        return 1.0 + 0.02 * jax.random.normal(k, shape, f32)

    def dense(k, shape, fan_in):
        return jax.random.normal(k, shape, f32) * (fan_in ** -0.5)

    x = jax.random.normal(ks[0], (BATCH, SEQ, D_MODEL), f32)
    positions = jnp.broadcast_to(jnp.arange(SEQ, dtype=jnp.int32)[None, :], (BATCH, SEQ))
    return {
        "x": x,
        "positions": positions,
        "norm_mix_pre": gain(ks[1], (DEPTH, D_MODEL)),
        "w_in": dense(ks[2], (DEPTH, D_MODEL, D_IN), D_MODEL),
        "gla_wa2": dense(ks[3], (DEPTH, GLA_GATE_RANK, GLA_HEADS * GLA_DK), GLA_GATE_RANK),
        "gla_ba": 0.02 * jax.random.normal(ks[4], (DEPTH, GLA_HEADS * GLA_DK), f32),
        "gla_norm": gain(ks[5], (DEPTH, GLA_DV)),
        "w_out": dense(ks[6], (DEPTH, D_MIX, D_MODEL), D_MIX),
        "norm_mix_post": gain(ks[7], (DEPTH, D_MODEL)),
        "norm_ffn_pre": gain(ks[8], (DEPTH, D_MODEL)),
        "w_up": dense(ks[9], (DEPTH, D_MODEL, D_FF), D_MODEL),
        "w_down": dense(ks[10], (DEPTH, D_FF, D_MODEL), D_FF),
        "norm_ffn_post": gain(ks[11], (DEPTH, D_MODEL)),
    }


def reference(x, positions, norm_mix_pre, w_in, gla_wa2, gla_ba, gla_norm, w_out,
              norm_mix_post, norm_ffn_pre, w_up, w_down, norm_ffn_post):
    cos_a, sin_a = rope_tables(positions, DSA_HEAD_DIM)
    cos_i, sin_i = rope_tables(positions, IDX_DIM)
    h = x
    for l in range(DEPTH):
        m = hybrid_mixer(rms_norm(h, norm_mix_pre[l]), cos_a, sin_a, cos_i, sin_i,
                         w_in[l], gla_wa2[l], gla_ba[l], gla_norm[l], w_out[l])
        h = h + rms_norm(m, norm_mix_post[l])
        f = sq_relu_mlp(rms_norm(h, norm_ffn_pre[l]), w_up[l], w_down[l])
        h = h + rms_norm(f, norm_ffn_post[l])
    return h
```

```python
import functools

import jax
import jax.numpy as jnp
from jax import lax
from jax.experimental import pallas as pl
from jax.experimental.pallas import tpu as pltpu

F32 = jnp.float32
BF16 = jnp.bfloat16
I32 = jnp.int32

GLA_HEADS = 4
GLA_DK = 128
GLA_DV = 256
GLA_GATE_RANK = 16
GLA_GATE_TAU = 16.0
GLA_CHUNK = 64
DSA_HEADS = 8
DSA_HEAD_DIM = 128
IDX_HEADS = 16
IDX_DIM = 64
DSA_TOPK_MAX = 256
ROPE_THETA = 500000.0
ROPE_FRACTION = 4
EPS = 1e-6
NEG_INF = -1e30
INT_MIN = -(2 ** 31)

LANES = 128
N_GQ = GLA_HEADS * GLA_DK
N_GV = GLA_HEADS * GLA_DV
N_DSA = DSA_HEADS * DSA_HEAD_DIM
N_IQ = IDX_HEADS * IDX_DIM
OFF_GQ = 0
OFF_GK = OFF_GQ + N_GQ
OFF_GV = OFF_GK + N_GQ
OFF_GG = OFF_GV + N_GV
OFF_DQ = OFF_GG + N_GV
OFF_DK = OFF_DQ + N_DSA
OFF_DV = OFF_DK + N_DSA
OFF_IQ = OFF_DV + N_DSA
N_MAIN = OFF_IQ + N_IQ
N_SMALL = 3 * LANES
VMEM_LIMIT = 56 * 1024 * 1024

NT_DIMS = (((1,), (1,)), ((), ()))
TN_DIMS = (((0,), (0,)), ((), ()))


def _params(*sem):
    return pltpu.CompilerParams(dimension_semantics=sem, vmem_limit_bytes=VMEM_LIMIT)


def _rope(x, c, s1, s2, half):
    w = x.shape[1]
    rep = w // LANES
    c, s1, s2 = (jnp.concatenate([t] * rep, axis=1) if rep > 1 else t for t in (c, s1, s2))
    return x * c + pltpu.roll(x, w - half, 1) * s1 + pltpu.roll(x, half, 1) * s2


def _inproj_kernel(x_ref, g_ref, w_ref, ws_ref, tab_ref, main_ref, ikd_ref, misc_ref, xn_ref, *, tn):
    j = pl.program_id(1)
    half_a = DSA_HEAD_DIM // ROPE_FRACTION // 2
    half_i = IDX_DIM // ROPE_FRACTION // 2

    def tabs(k):
        return tuple(tab_ref[:, (3 * k + i) * LANES:(3 * k + i + 1) * LANES] for i in range(3))

    @pl.when(j == 0)
    def _():
        x = x_ref[...]
        ms = jnp.mean(x * x, axis=-1, keepdims=True)
        xn_ref[...] = (x * lax.rsqrt(ms + EPS) * g_ref[...]).astype(BF16)
        small = jnp.dot(xn_ref[...], ws_ref[...], preferred_element_type=F32)
        ikd_ref[...] = _rope(small[:, :2 * LANES], *tabs(1), half_i).astype(BF16)
        misc_ref[...] = small[:, 2 * LANES:]

    acc = jnp.dot(xn_ref[...], w_ref[...], preferred_element_type=F32)
    j_dq, j_dk, j_dv, j_iq = OFF_DQ // tn, OFF_DK // tn, OFF_DV // tn, OFF_IQ // tn

    @pl.when((j < j_dq) | ((j >= j_dv) & (j < j_iq)))
    def _():
        main_ref[...] = acc.astype(BF16)

    @pl.when((j >= j_dq) & (j < j_dk))
    def _():
        y = _rope(acc, *tabs(0), half_a) * (DSA_HEAD_DIM ** -0.5)
        main_ref[...] = y.astype(BF16)

    @pl.when((j >= j_dk) & (j < j_dv))
    def _():
        main_ref[...] = _rope(acc, *tabs(0), half_a).astype(BF16)

    @pl.when(j >= j_iq)
    def _():
        y = _rope(acc, *tabs(1), half_i) * (IDX_DIM ** -0.5)
        main_ref[...] = y.astype(BF16)


def _inproj(h, gain, w_main, w_small, tab):
    t, d = h.shape
    tm = min(1024, t)
    tn = 512
    kern = functools.partial(_inproj_kernel, tn=tn)
    return pl.pallas_call(
        kern,
        grid=(t // tm, N_MAIN // tn),
        in_specs=[
            pl.BlockSpec((tm, d), lambda i, j: (i, 0)),
            pl.BlockSpec((1, d), lambda i, j: (0, 0)),
            pl.BlockSpec((d, tn), lambda i, j: (0, j)),
            pl.BlockSpec((d, N_SMALL), lambda i, j: (0, 0)),
            pl.BlockSpec((tm, 6 * LANES), lambda i, j: (i, 0)),
        ],
        out_specs=[
            pl.BlockSpec((tm, tn), lambda i, j: (i, j)),
            pl.BlockSpec((tm, 2 * LANES), lambda i, j: (i, 0)),
            pl.BlockSpec((tm, LANES), lambda i, j: (i, 0)),
        ],
        out_shape=[
            jax.ShapeDtypeStruct((t, N_MAIN), BF16),
            jax.ShapeDtypeStruct((t, 2 * LANES), BF16),
            jax.ShapeDtypeStruct((t, LANES), F32),
        ],
        scratch_shapes=[pltpu.VMEM((tm, d), BF16)],
        compiler_params=_params("parallel", "arbitrary"),
        name="inproj",
    )(h, gain, w_main, w_small, tab)


def _gla_kernel(q_ref, k_ref, v_ref, gg_ref, misc_ref, w2_ref, ba_ref, gn_ref, o_ref, state_ref, *, blk):
    c_len = GLA_CHUNK

    @pl.when(pl.program_id(1) == 0)
    def _():
        state_ref[...] = jnp.zeros_like(state_ref)

    gpre = jnp.dot(misc_ref[...], w2_ref[...], preferred_element_type=F32,
                   precision=lax.Precision.HIGHEST) + ba_ref[...]
    log_a = (jnp.minimum(gpre, 0.0) - jnp.log1p(jnp.exp(-jnp.abs(gpre)))) * (1.0 / GLA_GATE_TAU)

    r = lax.broadcasted_iota(I32, (c_len, c_len), 0)
    c = lax.broadcasted_iota(I32, (c_len, c_len), 1)
    causal = c <= r
    tril = causal.astype(F32)
    gn = gn_ref[...]

    for hd in range(GLA_HEADS):
        ck = slice(hd * GLA_DK, (hd + 1) * GLA_DK)
        cv = slice(hd * GLA_DV, (hd + 1) * GLA_DV)
        for ci in range(blk // c_len):
            rows = slice(ci * c_len, (ci + 1) * c_len)
            q = q_ref[rows, ck].astype(F32) * (GLA_DK ** -0.5)
            k = k_ref[rows, ck].astype(F32)
            v = v_ref[rows, cv]
            g = log_a[rows, ck]
            b = jnp.dot(tril, g, preferred_element_type=F32, precision=lax.Precision.HIGHEST)
            b_last = b[c_len - 1:c_len, :]
            q_dec = (q * jnp.exp(b)).astype(BF16)
            k_inv = (k * jnp.exp(-b)).astype(BF16)
            k_dec = (k * jnp.exp(b_last - b)).astype(BF16)
            attn = lax.dot_general(q_dec, k_inv, NT_DIMS, preferred_element_type=F32)
            attn = jnp.where(causal, attn, 0.0).astype(BF16)
            st = state_ref[hd]
            o = jnp.dot(attn, v, preferred_element_type=F32)
            o = o + lax.dot_general(q_dec, st.astype(BF16), NT_DIMS, preferred_element_type=F32)
            d_st = lax.dot_general(v, k_dec, TN_DIMS, preferred_element_type=F32)
            state_ref[hd] = st * jnp.exp(b_last) + d_st
            ms = jnp.mean(o * o, axis=-1, keepdims=True)
            y = o * lax.rsqrt(ms + EPS) * gn
            gate = gg_ref[rows, cv].astype(F32)
            y = y * (gate / (1.0 + jnp.exp(-gate)))
            o_ref[rows, cv] = y.astype(BF16)


def _gla(main, misc, w2, ba, gn, batch, seq):
    t = batch * seq
    blk = min(256, seq)
    nb = seq // blk
    kern = functools.partial(_gla_kernel, blk=blk)

    def rowmap(col):
        return lambda b, n: (b * nb + n, col)

    return pl.pallas_call(
        kern,
        grid=(batch, nb),
        in_specs=[
            pl.BlockSpec((blk, N_GQ), rowmap(OFF_GQ // N_GQ)),
            pl.BlockSpec((blk, N_GQ), rowmap(OFF_GK // N_GQ)),
            pl.BlockSpec((blk, N_GV), rowmap(OFF_GV // N_GV)),
            pl.BlockSpec((blk, N_GV), rowmap(OFF_GG // N_GV)),
            pl.BlockSpec((blk, LANES), rowmap(0)),
            pl.BlockSpec((LANES, N_GQ), lambda b, n: (0, 0)),
            pl.BlockSpec((1, N_GQ), lambda b, n: (0, 0)),
            pl.BlockSpec((1, GLA_DV), lambda b, n: (0, 0)),
        ],
        out_specs=pl.BlockSpec((blk, N_GV), rowmap(0)),
        out_shape=jax.ShapeDtypeStruct((t, N_GV), BF16),
        scratch_shapes=[pltpu.VMEM((GLA_HEADS, GLA_DV, GLA_DK), F32)],
        compiler_params=_params("parallel", "arbitrary"),
        name="gla",
    )(main, main, main, main, misc, w2, ba, gn)


def _dsa_kernel(q_ref, iq_ref, misc_ref, k_ref, v_ref, ikd_ref, o_ref,
                key_ref, bias_ref, acc_ref, m_ref, l_ref, *, tq, topk, idx_bits):
    qi = pl.program_id(1)
    nch = qi + 1
    half = tq // 2
    row = lax.broadcasted_iota(I32, (tq, tq), 0)
    col = lax.broadcasted_iota(I32, (tq, tq), 1)
    w_idx = misc_ref[...] * (IDX_HEADS ** -0.5)

    def score_body(c, carry):
        ks = pl.multiple_of(c * tq, tq)
        k_lo = ikd_ref[pl.ds(ks, tq), 0:LANES]
        k_hi = ikd_ref[pl.ds(ks, tq), LANES:2 * LANES]
        acc = jnp.zeros((tq, tq), F32)
        for p in range(IDX_HEADS // 2):
            iq_p = iq_ref[:, p * LANES:(p + 1) * LANES]
            l0 = lax.dot_general(iq_p, k_lo, NT_DIMS, preferred_element_type=F32)
            l1 = lax.dot_general(iq_p, k_hi, NT_DIMS, preferred_element_type=F32)
            acc = acc + w_idx[:, 2 * p:2 * p + 1] * jnp.maximum(l0, 0.0)
            acc = acc + w_idx[:, 2 * p + 1:2 * p + 2] * jnp.maximum(l1, 0.0)
        score = jnp.where(col + (c - qi) * tq > row, NEG_INF, acc)
        score = jnp.where(score == 0.0, 0.0, score)
        bits = pltpu.bitcast(score, I32)
        key_ref[c] = bits ^ ((bits >> 31) & 0x7FFFFFFF)
        return carry

    lax.fori_loop(0, nch, score_body, 0)

    def count(pred):
        def body(c, acc):
            m = jnp.where(pred(key_ref[c], c), 1.0, 0.0)
            return acc + m[:, :half] + m[:, half:]
        acc = lax.fori_loop(0, nch, body, jnp.zeros((tq, half), F32))
        return jnp.sum(acc, axis=1, keepdims=True)

    kf = float(topk)
    thr = jnp.where(count(lambda kc, c: kc >= 0) >= kf, 0, INT_MIN).astype(I32)

    def thr_body(i, thr):
        cand = thr | lax.shift_left(jnp.int32(1), 30 - i)
        return jnp.where(count(lambda kc, c: kc >= cand) >= kf, cand, thr)

    thr = lax.fori_loop(0, 31, thr_body, thr)

    need = kf - count(lambda kc, c: kc > thr)

    def tie_body(i, last):
        cand = last | lax.shift_left(jnp.int32(1), idx_bits - 1 - i)
        below = count(lambda kc, c: (kc == thr) & (c * tq + col < cand))
        return jnp.where(below < need, cand, last)

    last = lax.fori_loop(0, idx_bits, tie_body, jnp.zeros((tq, 1), I32))

    def bias_body(c, carry):
        kc = key_ref[c]
        sel = (kc > thr) | ((kc == thr) & (c * tq + col <= last))
        sel = sel & (col + (c - qi) * tq <= row)
        bias_ref[c] = jnp.where(sel, 0.0, NEG_INF)
        return carry

    lax.fori_loop(0, nch, bias_body, 0)

    m_ref[...] = jnp.full_like(m_ref, NEG_INF)
    l_ref[...] = jnp.zeros_like(l_ref)
    acc_ref[...] = jnp.zeros_like(acc_ref)
    rep = tq // LANES

    def attn_body(c, carry):
        ks = pl.multiple_of(c * tq, tq)
        bias = bias_ref[c]
        for hd in range(DSA_HEADS):
            hs = slice(hd * DSA_HEAD_DIM, (hd + 1) * DSA_HEAD_DIM)
            s = lax.dot_general(q_ref[:, hs], k_ref[pl.ds(ks, tq), hs], NT_DIMS,
                                preferred_element_type=F32) + bias
            m_prev = m_ref[hd]
            m_new = jnp.maximum(m_prev, jnp.max(s, axis=1, keepdims=True))
            alpha = jnp.exp(m_prev - m_new)
            p = jnp.exp(s - jnp.concatenate([m_new] * rep, axis=1))
            l_ref[hd] = alpha * l_ref[hd] + jnp.sum(p, axis=1, keepdims=True)
            m_ref[hd] = m_new
            pv = jnp.dot(p.astype(BF16), v_ref[pl.ds(ks, tq), hs], preferred_element_type=F32)
            acc_ref[:, hs] = alpha * acc_ref[:, hs] + pv
        return carry

    lax.fori_loop(0, nch, attn_body, 0)

    for hd in range(DSA_HEADS):
        hs = slice(hd * DSA_HEAD_DIM, (hd + 1) * DSA_HEAD_DIM)
        o_ref[:, hs] = (acc_ref[:, hs] / l_ref[hd]).astype(BF16)


def _dsa(main, ikd, misc, batch, seq):
    t = batch * seq
    tq = min(256, seq)
    nq = seq // tq
    topk = min(DSA_TOPK_MAX, seq // 4)
    idx_bits = max(1, (seq - 1).bit_length())
    kern = functools.partial(_dsa_kernel, tq=tq, topk=topk, idx_bits=idx_bits)

    def qmap(col):
        return lambda b, i: (b * nq + i, col)

    def kvmap(col):
        return lambda b, i: (b, col)

    return pl.pallas_call(
        kern,
        grid=(batch, nq),
        in_specs=[
            pl.BlockSpec((tq, N_DSA), qmap(OFF_DQ // N_DSA)),
            pl.BlockSpec((tq, N_IQ), qmap(OFF_IQ // N_IQ)),
            pl.BlockSpec((tq, LANES), qmap(0)),
            pl.BlockSpec((seq, N_DSA), kvmap(OFF_DK // N_DSA)),
            pl.BlockSpec((seq, N_DSA), kvmap(OFF_DV // N_DSA)),
            pl.BlockSpec((seq, 2 * LANES), kvmap(0)),
        ],
        out_specs=pl.BlockSpec((tq, N_DSA), qmap(0)),
        out_shape=jax.ShapeDtypeStruct((t, N_DSA), BF16),
        scratch_shapes=[
            pltpu.VMEM((nq, tq, tq), I32),
            pltpu.VMEM((nq, tq, tq), F32),
            pltpu.VMEM((tq, N_DSA), F32),
            pltpu.VMEM((DSA_HEADS, tq, LANES), F32),
            pltpu.VMEM((DSA_HEADS, tq, LANES), F32),
        ],
        compiler_params=_params("parallel", "arbitrary"),
        name="dsa",
    )(main, main, misc, main, main, ikd)


def _outproj_kernel(og_ref, od_ref, h_ref, w_ref, g_ref, o_ref):
    m = jnp.dot(og_ref[...], w_ref[:N_GV, :], preferred_element_type=F32)
    m = m + jnp.dot(od_ref[...], w_ref[N_GV:, :], preferred_element_type=F32)
    ms = jnp.mean(m * m, axis=-1, keepdims=True)
    o_ref[...] = h_ref[...] + m * lax.rsqrt(ms + EPS) * g_ref[...]


def _outproj(o_gla, o_dsa, h, w_out, gain):
    t, d = h.shape
    tm = min(512, t)
    return pl.pallas_call(
        _outproj_kernel,
        grid=(t // tm,),
        in_specs=[
            pl.BlockSpec((tm, N_GV), lambda i: (i, 0)),
            pl.BlockSpec((tm, N_DSA), lambda i: (i, 0)),
            pl.BlockSpec((tm, d), lambda i: (i, 0)),
            pl.BlockSpec((N_GV + N_DSA, d), lambda i: (0, 0)),
            pl.BlockSpec((1, d), lambda i: (0, 0)),
        ],
        out_specs=pl.BlockSpec((tm, d), lambda i: (i, 0)),
        out_shape=jax.ShapeDtypeStruct((t, d), F32),
        compiler_params=_params("parallel"),
        name="outproj",
    )(o_gla, o_dsa, h, w_out, gain)


def _ffn_kernel(h_ref, gpre_ref, wu_ref, wd_ref, gpost_ref, o_ref, xn_ref, acc_ref):
    f = pl.program_id(1)

    @pl.when(f == 0)
    def _():
        x = h_ref[...]
        ms = jnp.mean(x * x, axis=-1, keepdims=True)
        xn_ref[...] = (x * lax.rsqrt(ms + EPS) * gpre_ref[...]).astype(BF16)
        acc_ref[...] = jnp.zeros_like(acc_ref)

    u = jnp.maximum(jnp.dot(xn_ref[...], wu_ref[...], preferred_element_type=F32), 0.0)
    acc_ref[...] += jnp.dot((u * u).astype(BF16), wd_ref[...], preferred_element_type=F32)

    @pl.when(f == pl.num_programs(1) - 1)
    def _():
        y = acc_ref[...]
        ms = jnp.mean(y * y, axis=-1, keepdims=True)
        o_ref[...] = h_ref[...] + y * lax.rsqrt(ms + EPS) * gpost_ref[...]


def _ffn(h, g_pre, w_up, w_down, g_post):
    t, d = h.shape
    d_ff = w_up.shape[1]
    tm = min(512, t)
    tf = 512
    return pl.pallas_call(
        _ffn_kernel,
        grid=(t // tm, d_ff // tf),
        in_specs=[
            pl.BlockSpec((tm, d), lambda i, f: (i, 0)),
            pl.BlockSpec((1, d), lambda i, f: (0, 0)),
            pl.BlockSpec((d, tf), lambda i, f: (0, f)),
            pl.BlockSpec((tf, d), lambda i, f: (f, 0)),
            pl.BlockSpec((1, d), lambda i, f: (0, 0)),
        ],
        out_specs=pl.BlockSpec((tm, d), lambda i, f: (i, 0)),
        out_shape=jax.ShapeDtypeStruct((t, d), F32),
        scratch_shapes=[pltpu.VMEM((tm, d), BF16), pltpu.VMEM((tm, d), F32)],
        compiler_params=_params("parallel", "arbitrary"),
        name="ffn",
    )(h, g_pre, w_up, w_down, g_post)


def _rope_table(positions, head_dim):
    r = head_dim // ROPE_FRACTION
    half = r // 2
    inv_freq = ROPE_THETA ** (-(jnp.arange(0, r, 2, dtype=F32) / r))
    ang = positions.reshape(-1).astype(F32)[:, None] * inv_freq
    cos, sin = jnp.cos(ang), jnp.sin(ang)
    within = jnp.arange(LANES) % head_dim
    idx = within % half
    lo = (within < half)[None, :]
    hi = ((within >= half) & (within < r))[None, :]
    c = jnp.where(lo | hi, cos[:, idx], 1.0)
    s1 = jnp.where(lo, -sin[:, idx], 0.0)
    s2 = jnp.where(hi, sin[:, idx], 0.0)
    return jnp.concatenate([c, s1, s2], axis=1)


def _split_w_in(w_in):
    d = w_in.shape[0]
    o_gr = 2 * N_GQ + 2 * N_GV
    o_dq = o_gr + GLA_GATE_RANK
    o_ik = o_dq + 3 * N_DSA + N_IQ
    o_iw = o_ik + IDX_DIM
    w_main = jnp.concatenate([w_in[:, :o_gr], w_in[:, o_dq:o_ik]], axis=1)
    ik = w_in[:, o_ik:o_iw]
    iw = w_in[:, o_iw:o_iw + IDX_HEADS]
    gr = w_in[:, o_gr:o_dq]
    z = jnp.zeros((d, IDX_DIM), w_in.dtype)
    pad = jnp.zeros((d, LANES - IDX_HEADS - GLA_GATE_RANK), w_in.dtype)
    w_small = jnp.concatenate([ik, z, z, ik, iw, gr, pad], axis=1)
    return w_main.astype(BF16), w_small.astype(BF16)


def kernel(x, positions, norm_mix_pre, w_in, gla_wa2, gla_ba, gla_norm, w_out, norm_mix_post,
           norm_ffn_pre, w_up, w_down, norm_ffn_post):
    batch, seq, d = x.shape
    depth = w_in.shape[0]
    tab = jnp.concatenate([_rope_table(positions, DSA_HEAD_DIM), _rope_table(positions, IDX_DIM)], axis=1)
    h = x.reshape(batch * seq, d)
    for l in range(depth):
        w_main, w_small = _split_w_in(w_in[l])
        w2 = jnp.zeros((LANES, N_GQ), F32).at[IDX_HEADS:IDX_HEADS + GLA_GATE_RANK].set(gla_wa2[l])
        main, ikd, misc = _inproj(h, norm_mix_pre[l][None, :], w_main, w_small, tab)
        o_gla = _gla(main, misc, w2, gla_ba[l][None, :], gla_norm[l][None, :], batch, seq)
        o_dsa = _dsa(main, ikd, misc, batch, seq)
        h = _outproj(o_gla, o_dsa, h, w_out[l].astype(BF16), norm_mix_post[l][None, :])
        h = _ffn(h, norm_ffn_pre[l][None, :], w_up[l].astype(BF16), w_down[l].astype(BF16),
                 norm_ffn_post[l][None, :])
    return h.reshape(batch, seq, d)
```

```python
import functools

import jax
import jax.numpy as jnp
from jax import lax
from jax.experimental import pallas as pl
from jax.experimental.pallas import tpu as pltpu

F32 = jnp.float32
BF16 = jnp.bfloat16
I32 = jnp.int32

GLA_HEADS = 4
GLA_DK = 128
GLA_DV = 256
GLA_GATE_RANK = 16
GLA_GATE_TAU = 16.0
GLA_CHUNK = 64
DSA_HEADS = 8
DSA_HEAD_DIM = 128
IDX_HEADS = 16
IDX_DIM = 64
DSA_TOPK_MAX = 256
ROPE_THETA = 500000.0
ROPE_FRACTION = 4
EPS = 1e-6
NEG_INF = -1e30
INT_MIN = -(2 ** 31)

LANES = 128
N_GQ = GLA_HEADS * GLA_DK
N_GV = GLA_HEADS * GLA_DV
N_DSA = DSA_HEADS * DSA_HEAD_DIM
N_IQ = IDX_HEADS * IDX_DIM
OFF_GQ = 0
OFF_GK = OFF_GQ + N_GQ
OFF_GV = OFF_GK + N_GQ
OFF_GG = OFF_GV + N_GV
OFF_DQ = OFF_GG + N_GV
OFF_DK = OFF_DQ + N_DSA
OFF_IQ = OFF_DK + N_DSA
N_MAIN = OFF_IQ + N_IQ
N_SMALL = 3 * LANES
DSA_BLK = 256
COUNT_ROWS = 32
VMEM_LIMIT = 56 * 1024 * 1024

NT_DIMS = (((1,), (1,)), ((), ()))
TN_DIMS = (((0,), (0,)), ((), ()))


def _params(*sem):
    return pltpu.CompilerParams(dimension_semantics=sem, vmem_limit_bytes=VMEM_LIMIT)


def _rope(x, c, s1, s2, half):
    w = x.shape[1]
    rep = w // LANES
    c, s1, s2 = (jnp.concatenate([t] * rep, axis=1) if rep > 1 else t for t in (c, s1, s2))
    return x * c + pltpu.roll(x, w - half, 1) * s1 + pltpu.roll(x, half, 1) * s2


def _inproj_kernel(x_ref, g_ref, w_ref, ws_ref, wt_ref, tab_ref,
                   main_ref, ikd_ref, misc_ref, dvt_ref, iwt_ref, xn_ref, *, tn):
    j = pl.program_id(1)
    half_a = DSA_HEAD_DIM // ROPE_FRACTION // 2
    half_i = IDX_DIM // ROPE_FRACTION // 2

    def tabs(k):
        return tuple(tab_ref[:, (3 * k + i) * LANES:(3 * k + i + 1) * LANES] for i in range(3))

    @pl.when(j == 0)
    def _():
        x = x_ref[...]
        ms = jnp.mean(x * x, axis=-1, keepdims=True)
        xn_ref[...] = (x * lax.rsqrt(ms + EPS) * g_ref[...]).astype(BF16)
        small = jnp.dot(xn_ref[...], ws_ref[...], preferred_element_type=F32)
        ikd_ref[...] = _rope(small[:, :2 * LANES], *tabs(1), half_i).astype(BF16)
        misc_ref[...] = small[:, 2 * LANES:]
        tr = lax.dot_general(wt_ref[...], xn_ref[...], NT_DIMS, preferred_element_type=F32)
        for ci in range(dvt_ref.shape[0]):
            dvt_ref[ci] = tr[:N_DSA, ci * DSA_BLK:(ci + 1) * DSA_BLK].astype(BF16)
        iwt_ref[...] = tr[N_DSA:, :]

    acc = jnp.dot(xn_ref[...], w_ref[...], preferred_element_type=F32)
    j_dq, j_dk, j_iq = OFF_DQ // tn, OFF_DK // tn, OFF_IQ // tn

    @pl.when(j < j_dq)
    def _():
        main_ref[...] = acc.astype(BF16)

    @pl.when((j >= j_dq) & (j < j_dk))
    def _():
        y = _rope(acc, *tabs(0), half_a) * (DSA_HEAD_DIM ** -0.5)
        main_ref[...] = y.astype(BF16)

    @pl.when((j >= j_dk) & (j < j_iq))
    def _():
        main_ref[...] = _rope(acc, *tabs(0), half_a).astype(BF16)

    @pl.when(j >= j_iq)
    def _():
        y = _rope(acc, *tabs(1), half_i) * (IDX_DIM ** -0.5)
        main_ref[...] = y.astype(BF16)


def _inproj(h, gain, w_main, w_small, w_t, tab):
    t, d = h.shape
    tm = min(1024, t)
    tn = 512
    nt = w_t.shape[0]
    kern = functools.partial(_inproj_kernel, tn=tn)
    return pl.pallas_call(
        kern,
        grid=(t // tm, N_MAIN // tn),
        in_specs=[
            pl.BlockSpec((tm, d), lambda i, j: (i, 0)),
            pl.BlockSpec((1, d), lambda i, j: (0, 0)),
            pl.BlockSpec((d, tn), lambda i, j: (0, j)),
            pl.BlockSpec((d, N_SMALL), lambda i, j: (0, 0)),
            pl.BlockSpec((nt, d), lambda i, j: (0, 0)),
            pl.BlockSpec((tm, 6 * LANES), lambda i, j: (i, 0)),
        ],
        out_specs=[
            pl.BlockSpec((tm, tn), lambda i, j: (i, j)),
            pl.BlockSpec((tm, 2 * LANES), lambda i, j: (i, 0)),
            pl.BlockSpec((tm, LANES), lambda i, j: (i, 0)),
            pl.BlockSpec((tm // DSA_BLK, N_DSA, DSA_BLK), lambda i, j: (i, 0, 0)),
            pl.BlockSpec((LANES, tm), lambda i, j: (0, i)),
        ],
        out_shape=[
            jax.ShapeDtypeStruct((t, N_MAIN), BF16),
            jax.ShapeDtypeStruct((t, 2 * LANES), BF16),
            jax.ShapeDtypeStruct((t, LANES), F32),
            jax.ShapeDtypeStruct((t // DSA_BLK, N_DSA, DSA_BLK), BF16),
            jax.ShapeDtypeStruct((LANES, t), F32),
        ],
        scratch_shapes=[pltpu.VMEM((tm, d), BF16)],
        compiler_params=_params("parallel", "arbitrary"),
        name="inproj",
    )(h, gain, w_main, w_small, w_t, tab)


def _gla_kernel(q_ref, k_ref, v_ref, gg_ref, misc_ref, w2_ref, ba_ref, gn_ref, o_ref, state_ref, *, blk):
    c_len = GLA_CHUNK

    @pl.when(pl.program_id(1) == 0)
    def _():
        state_ref[...] = jnp.zeros_like(state_ref)

    gpre = jnp.dot(misc_ref[...], w2_ref[...], preferred_element_type=F32,
                   precision=lax.Precision.HIGHEST) + ba_ref[...]
    log_a = (jnp.minimum(gpre, 0.0) - jnp.log1p(jnp.exp(-jnp.abs(gpre)))) * (1.0 / GLA_GATE_TAU)

    r = lax.broadcasted_iota(I32, (c_len, c_len), 0)
    c = lax.broadcasted_iota(I32, (c_len, c_len), 1)
    causal = c <= r
    tril = causal.astype(F32)
    gn = gn_ref[...]

    for hd in range(GLA_HEADS):
        ck = slice(hd * GLA_DK, (hd + 1) * GLA_DK)
        cv = slice(hd * GLA_DV, (hd + 1) * GLA_DV)
        for ci in range(blk // c_len):
            rows = slice(ci * c_len, (ci + 1) * c_len)
            q = q_ref[rows, ck].astype(F32) * (GLA_DK ** -0.5)
            k = k_ref[rows, ck].astype(F32)
            v = v_ref[rows, cv]
            g = log_a[rows, ck]
            b = jnp.dot(tril, g, preferred_element_type=F32, precision=lax.Precision.HIGHEST)
            b_last = b[c_len - 1:c_len, :]
            q_dec = (q * jnp.exp(b)).astype(BF16)
            k_inv = (k * jnp.exp(-b)).astype(BF16)
            k_dec = (k * jnp.exp(b_last - b)).astype(BF16)
            attn = lax.dot_general(q_dec, k_inv, NT_DIMS, preferred_element_type=F32)
            attn = jnp.where(causal, attn, 0.0).astype(BF16)
            st = state_ref[hd]
            o = jnp.dot(attn, v, preferred_element_type=F32)
            o = o + lax.dot_general(q_dec, st.astype(BF16), NT_DIMS, preferred_element_type=F32)
            d_st = lax.dot_general(v, k_dec, TN_DIMS, preferred_element_type=F32)
            state_ref[hd] = st * jnp.exp(b_last) + d_st
            ms = jnp.mean(o * o, axis=-1, keepdims=True)
            y = o * lax.rsqrt(ms + EPS) * gn
            gate = gg_ref[rows, cv].astype(F32)
            y = y * (gate / (1.0 + jnp.exp(-gate)))
            o_ref[rows, cv] = y.astype(BF16)


def _gla(main, misc, w2, ba, gn, batch, seq):
    t = batch * seq
    blk = min(256, seq)
    nb = seq // blk
    kern = functools.partial(_gla_kernel, blk=blk)

    def rowmap(col):
        return lambda b, n: (b * nb + n, col)

    return pl.pallas_call(
        kern,
        grid=(batch, nb),
        in_specs=[
            pl.BlockSpec((blk, N_GQ), rowmap(OFF_GQ // N_GQ)),
            pl.BlockSpec((blk, N_GQ), rowmap(OFF_GK // N_GQ)),
            pl.BlockSpec((blk, N_GV), rowmap(OFF_GV // N_GV)),
            pl.BlockSpec((blk, N_GV), rowmap(OFF_GG // N_GV)),
            pl.BlockSpec((blk, LANES), rowmap(0)),
            pl.BlockSpec((LANES, N_GQ), lambda b, n: (0, 0)),
            pl.BlockSpec((1, N_GQ), lambda b, n: (0, 0)),
            pl.BlockSpec((1, GLA_DV), lambda b, n: (0, 0)),
        ],
        out_specs=pl.BlockSpec((blk, N_GV), rowmap(0)),
        out_shape=jax.ShapeDtypeStruct((t, N_GV), BF16),
        scratch_shapes=[pltpu.VMEM((GLA_HEADS, GLA_DV, GLA_DK), F32)],
        compiler_params=_params("parallel", "arbitrary"),
        name="gla",
    )(main, main, main, main, misc, w2, ba, gn)


def _dsa_kernel(q_ref, iq_ref, iwt_ref, k_ref, vt_ref, ikd_ref, o_ref,
                key_ref, bias_ref, acc_ref, s_ref, p_ref, m_ref, l_ref, alpha_ref, *, blk, topk, idx_bits):
    qi = pl.program_id(1)
    nch = qi + 1
    krow = lax.broadcasted_iota(I32, (blk, blk), 0)
    qcol = lax.broadcasted_iota(I32, (blk, blk), 1)
    w_idx = iwt_ref[0:IDX_HEADS, :] * (IDX_HEADS ** -0.5)

    def score_body(c, carry):
        ks = pl.multiple_of(c * blk, blk)
        k_lo = ikd_ref[pl.ds(ks, blk), 0:LANES]
        k_hi = ikd_ref[pl.ds(ks, blk), LANES:2 * LANES]
        acc = jnp.zeros((blk, blk), F32)
        for p in range(IDX_HEADS // 2):
            iq_p = iq_ref[:, p * LANES:(p + 1) * LANES]
            l0 = lax.dot_general(k_lo, iq_p, NT_DIMS, preferred_element_type=F32)
            l1 = lax.dot_general(k_hi, iq_p, NT_DIMS, preferred_element_type=F32)
            acc = acc + w_idx[2 * p:2 * p + 1, :] * jnp.maximum(l0, 0.0)
            acc = acc + w_idx[2 * p + 1:2 * p + 2, :] * jnp.maximum(l1, 0.0)
        score = jnp.where(krow + (c - qi) * blk > qcol, NEG_INF, acc)
        score = jnp.where(score == 0.0, 0.0, score)
        bits = pltpu.bitcast(score, I32)
        key_ref[c] = bits ^ ((bits >> 31) & 0x7FFFFFFF)
        return carry

    lax.fori_loop(0, nch, score_body, 0)

    def count(pred):
        def body(c, acc):
            m = jnp.where(pred(key_ref[c], c), 1.0, 0.0)
            return acc + jnp.sum(m.reshape(blk // COUNT_ROWS, COUNT_ROWS, blk), axis=0)
        acc = lax.fori_loop(0, nch, body, jnp.zeros((COUNT_ROWS, blk), F32))
        return jnp.sum(acc, axis=0, keepdims=True)

    kf = float(topk)
    thr = jnp.where(count(lambda kc, c: kc >= 0) >= kf, 0, INT_MIN).astype(I32)

    def thr_body(i, thr):
        cand = thr | lax.shift_left(jnp.int32(1), 30 - i)
        return jnp.where(count(lambda kc, c: kc >= cand) >= kf, cand, thr)

    thr = lax.fori_loop(0, 31, thr_body, thr)

    n_ge = count(lambda kc, c: kc >= thr)

    def resolve_ties():
        need = kf - count(lambda kc, c: kc > thr)

        def tie_body(i, last):
            cand = last | lax.shift_left(jnp.int32(1), idx_bits - 1 - i)
            below = count(lambda kc, c: (kc == thr) & (c * blk + krow < cand))
            return jnp.where(below < need, cand, last)

        return lax.fori_loop(0, idx_bits, tie_body, jnp.zeros((1, blk), I32))

    last = lax.cond(jnp.max(n_ge) > kf, resolve_ties,
                    lambda: jnp.full((1, blk), 2 ** idx_bits, I32))

    def bias_body(c, carry):
        kc = key_ref[c]
        sel = (kc > thr) | ((kc == thr) & (c * blk + krow <= last))
        sel = sel & (krow + (c - qi) * blk <= qcol)
        bias_ref[c] = jnp.where(sel, 0.0, NEG_INF)
        return carry

    lax.fori_loop(0, nch, bias_body, 0)

    heads = tuple((hd, slice(hd * DSA_HEAD_DIM, (hd + 1) * DSA_HEAD_DIM)) for hd in range(DSA_HEADS))

    def fold(x, op):
        return op(x.reshape(blk // 8, 8, blk), axis=0)

    m_ref[...] = jnp.full_like(m_ref, NEG_INF)
    l_ref[...] = jnp.zeros_like(l_ref)
    acc_ref[...] = jnp.zeros_like(acc_ref)

    def attn_body(c, carry):
        ks = pl.multiple_of(c * blk, blk)

        @pl.when(c >= 0)
        def _():
            for hd, hs in heads:
                s = lax.dot_general(k_ref[pl.ds(ks, blk), hs], q_ref[:, hs], NT_DIMS,
                                    preferred_element_type=F32) + bias_ref[c]
                s_ref[hd] = s
                m_prev = m_ref[hd, 0:1, :]
                m_new = jnp.maximum(m_prev, jnp.max(fold(s, jnp.max), axis=0, keepdims=True))
                alpha = jnp.exp(m_prev - m_new)
                m_ref[hd, 0:1, :] = m_new
                alpha_ref[hd, 0:1, :] = alpha
                p = jnp.exp(s_ref[hd] - m_new)
                l_ref[hd] = alpha * l_ref[hd] + fold(p, jnp.sum)
                p_ref[hd] = p.astype(BF16)

        @pl.when(c >= 0)
        def _():
            for hd, hs in heads:
                pv = jnp.dot(vt_ref[c, hs, :], p_ref[hd], preferred_element_type=F32)
                acc_ref[hs, :] = alpha_ref[hd, 0:1, :] * acc_ref[hs, :] + pv

        return carry

    lax.fori_loop(0, nch, attn_body, 0)

    for hd, hs in heads:
        o_t = acc_ref[hs, :] / jnp.sum(l_ref[hd], axis=0, keepdims=True)
        o_ref[:, hs] = o_t.T.astype(BF16)


def _dsa(main, ikd, iwt, dvt, batch, seq):
    t = batch * seq
    blk = DSA_BLK
    nq = seq // blk
    topk = min(DSA_TOPK_MAX, seq // 4)
    idx_bits = max(1, (seq - 1).bit_length())
    kern = functools.partial(_dsa_kernel, blk=blk, topk=topk, idx_bits=idx_bits)

    def qmap(col):
        return lambda b, i: (b * nq + i, col)

    return pl.pallas_call(
        kern,
        grid=(batch, nq),
        in_specs=[
            pl.BlockSpec((blk, N_DSA), qmap(OFF_DQ // N_DSA)),
            pl.BlockSpec((blk, N_IQ), qmap(OFF_IQ // N_IQ)),
            pl.BlockSpec((LANES, blk), lambda b, i: (0, b * nq + i)),
            pl.BlockSpec((seq, N_DSA), lambda b, i: (b, OFF_DK // N_DSA)),
            pl.BlockSpec((nq, N_DSA, blk), lambda b, i: (b, 0, 0)),
            pl.BlockSpec((seq, 2 * LANES), lambda b, i: (b, 0)),
        ],
        out_specs=pl.BlockSpec((blk, N_DSA), qmap(0)),
        out_shape=jax.ShapeDtypeStruct((t, N_DSA), BF16),
        scratch_shapes=[
            pltpu.VMEM((nq, blk, blk), I32),
            pltpu.VMEM((nq, blk, blk), F32),
            pltpu.VMEM((N_DSA, blk), F32),
            pltpu.VMEM((DSA_HEADS, blk, blk), F32),
            pltpu.VMEM((DSA_HEADS, blk, blk), BF16),
            pltpu.VMEM((DSA_HEADS, 8, blk), F32),
            pltpu.VMEM((DSA_HEADS, 8, blk), F32),
            pltpu.VMEM((DSA_HEADS, 8, blk), F32),
        ],
        compiler_params=_params("parallel", "arbitrary"),
        name="dsa",
    )(main, main, iwt, main, dvt, ikd)


def _outproj_kernel(og_ref, od_ref, h_ref, w_ref, g_ref, o_ref):
    m = jnp.dot(og_ref[...], w_ref[:N_GV, :], preferred_element_type=F32)
    m = m + jnp.dot(od_ref[...], w_ref[N_GV:, :], preferred_element_type=F32)
    ms = jnp.mean(m * m, axis=-1, keepdims=True)
    o_ref[...] = h_ref[...] + m * lax.rsqrt(ms + EPS) * g_ref[...]


def _outproj(o_gla, o_dsa, h, w_out, gain):
    t, d = h.shape
    tm = min(512, t)
    return pl.pallas_call(
        _outproj_kernel,
        grid=(t // tm,),
        in_specs=[
            pl.BlockSpec((tm, N_GV), lambda i: (i, 0)),
            pl.BlockSpec((tm, N_DSA), lambda i: (i, 0)),
            pl.BlockSpec((tm, d), lambda i: (i, 0)),
            pl.BlockSpec((N_GV + N_DSA, d), lambda i: (0, 0)),
            pl.BlockSpec((1, d), lambda i: (0, 0)),
        ],
        out_specs=pl.BlockSpec((tm, d), lambda i: (i, 0)),
        out_shape=jax.ShapeDtypeStruct((t, d), F32),
        compiler_params=_params("parallel"),
        name="outproj",
    )(o_gla, o_dsa, h, w_out, gain)


def _ffn_kernel(h_ref, gpre_ref, wu_ref, wd_ref, gpost_ref, o_ref, xn_ref, acc_ref):
    f = pl.program_id(1)

    @pl.when(f == 0)
    def _():
        x = h_ref[...]
        ms = jnp.mean(x * x, axis=-1, keepdims=True)
        xn_ref[...] = (x * lax.rsqrt(ms + EPS) * gpre_ref[...]).astype(BF16)
        acc_ref[...] = jnp.zeros_like(acc_ref)

    u = jnp.maximum(jnp.dot(xn_ref[...], wu_ref[...], preferred_element_type=F32), 0.0)
    acc_ref[...] += jnp.dot((u * u).astype(BF16), wd_ref[...], preferred_element_type=F32)

    @pl.when(f == pl.num_programs(1) - 1)
    def _():
        y = acc_ref[...]
        ms = jnp.mean(y * y, axis=-1, keepdims=True)
        o_ref[...] = h_ref[...] + y * lax.rsqrt(ms + EPS) * gpost_ref[...]


def _ffn(h, g_pre, w_up, w_down, g_post):
    t, d = h.shape
    d_ff = w_up.shape[1]
    tm = min(512, t)
    tf = 512
    return pl.pallas_call(
        _ffn_kernel,
        grid=(t // tm, d_ff // tf),
        in_specs=[
            pl.BlockSpec((tm, d), lambda i, f: (i, 0)),
            pl.BlockSpec((1, d), lambda i, f: (0, 0)),
            pl.BlockSpec((d, tf), lambda i, f: (0, f)),
            pl.BlockSpec((tf, d), lambda i, f: (f, 0)),
            pl.BlockSpec((1, d), lambda i, f: (0, 0)),
        ],
        out_specs=pl.BlockSpec((tm, d), lambda i, f: (i, 0)),
        out_shape=jax.ShapeDtypeStruct((t, d), F32),
        scratch_shapes=[pltpu.VMEM((tm, d), BF16), pltpu.VMEM((tm, d), F32)],
        compiler_params=_params("parallel", "arbitrary"),
        name="ffn",
    )(h, g_pre, w_up, w_down, g_post)


def _rope_table(positions, head_dim):
    r = head_dim // ROPE_FRACTION
    half = r // 2
    inv_freq = ROPE_THETA ** (-(jnp.arange(0, r, 2, dtype=F32) / r))
    ang = positions.reshape(-1).astype(F32)[:, None] * inv_freq
    cos, sin = jnp.cos(ang), jnp.sin(ang)
    within = jnp.arange(LANES) % head_dim
    idx = within % half
    lo = (within < half)[None, :]
    hi = ((within >= half) & (within < r))[None, :]
    c = jnp.where(lo | hi, cos[:, idx], 1.0)
    s1 = jnp.where(lo, -sin[:, idx], 0.0)
    s2 = jnp.where(hi, sin[:, idx], 0.0)
    return jnp.concatenate([c, s1, s2], axis=1)


def _split_w_in(w_in):
    d = w_in.shape[0]
    o_gr = 2 * N_GQ + 2 * N_GV
    o_dq = o_gr + GLA_GATE_RANK
    o_dv = o_dq + 2 * N_DSA
    o_iq = o_dv + N_DSA
    o_ik = o_iq + N_IQ
    o_iw = o_ik + IDX_DIM
    w_main = jnp.concatenate([w_in[:, :o_gr], w_in[:, o_dq:o_dv], w_in[:, o_iq:o_ik]], axis=1)
    ik = w_in[:, o_ik:o_iw]
    iw = w_in[:, o_iw:o_iw + IDX_HEADS]
    gr = w_in[:, o_gr:o_dq]
    z = jnp.zeros((d, IDX_DIM), w_in.dtype)
    pad = jnp.zeros((d, LANES - IDX_HEADS - GLA_GATE_RANK), w_in.dtype)
    w_small = jnp.concatenate([ik, z, z, ik, iw, gr, pad], axis=1)
    w_t = jnp.concatenate([w_in[:, o_dv:o_iq], iw, jnp.zeros((d, LANES - IDX_HEADS), w_in.dtype)], axis=1).T
    return w_main.astype(BF16), w_small.astype(BF16), w_t.astype(BF16)


def kernel(x, positions, norm_mix_pre, w_in, gla_wa2, gla_ba, gla_norm, w_out, norm_mix_post,
           norm_ffn_pre, w_up, w_down, norm_ffn_post):
    batch, seq, d = x.shape
    depth = w_in.shape[0]
    assert seq % DSA_BLK == 0
    tab = jnp.concatenate([_rope_table(positions, DSA_HEAD_DIM), _rope_table(positions, IDX_DIM)], axis=1)
    h = x.reshape(batch * seq, d)
    for l in range(depth):
        w_main, w_small, w_t = _split_w_in(w_in[l])
        w2 = jnp.zeros((LANES, N_GQ), F32).at[IDX_HEADS:IDX_HEADS + GLA_GATE_RANK].set(gla_wa2[l])
        main, ikd, misc, dvt, iwt = _inproj(h, norm_mix_pre[l][None, :], w_main, w_small, w_t, tab)
        o_gla = _gla(main, misc, w2, gla_ba[l][None, :], gla_norm[l][None, :], batch, seq)
        o_dsa = _dsa(main, ikd, iwt, dvt, batch, seq)
        h = _outproj(o_gla, o_dsa, h, w_out[l].astype(BF16), norm_mix_post[l][None, :])
        h = _ffn(h, norm_ffn_pre[l][None, :], w_up[l].astype(BF16), w_down[l].astype(BF16),
                 norm_ffn_post[l][None, :])
    return h.reshape(batch, seq, d)
```

```python
import functools

import jax
import jax.numpy as jnp
from jax import lax
from jax.experimental import pallas as pl
from jax.experimental.pallas import tpu as pltpu

F32 = jnp.float32
BF16 = jnp.bfloat16
I32 = jnp.int32
I16 = jnp.int16

GLA_HEADS = 4
GLA_DK = 128
GLA_DV = 256
GLA_GATE_RANK = 16
GLA_GATE_TAU = 16.0
GLA_CHUNK = 64
DSA_HEADS = 8
DSA_HEAD_DIM = 128
IDX_HEADS = 16
IDX_DIM = 64
DSA_TOPK_MAX = 256
ROPE_THETA = 500000.0
ROPE_FRACTION = 4
EPS = 1e-6
NEG_INF = -1e30
INT_MIN = -(2 ** 31)

LANES = 128
N_GQ = GLA_HEADS * GLA_DK
N_GV = GLA_HEADS * GLA_DV
N_DSA = DSA_HEADS * DSA_HEAD_DIM
N_IQ = IDX_HEADS * IDX_DIM
OFF_GQ = 0
OFF_GK = OFF_GQ + N_GQ
OFF_GV = OFF_GK + N_GQ
OFF_GG = OFF_GV + N_GV
OFF_DQ = OFF_GG + N_GV
OFF_DK = OFF_DQ + N_DSA
OFF_IQ = OFF_DK + N_DSA
N_MAIN = OFF_IQ + N_IQ
N_SMALL = 3 * LANES
DSA_BLK = 256
ROPE_LANES = 16
ROPE_PAIR_SHIFT = LANES // 2
TAB_PLAIN, TAB_DQ, TAB_DK, TAB_IQ, TAB_IK = range(5)
COUNT_ROWS = 32
COUNT16_ROWS = 64
BF16_EXACT_INT = 256
VMEM_LIMIT = 56 * 1024 * 1024

NT_DIMS = (((1,), (1,)), ((), ()))
TN_DIMS = (((0,), (0,)), ((), ()))


def _params(*sem):
    return pltpu.CompilerParams(dimension_semantics=sem, vmem_limit_bytes=VMEM_LIMIT)


def _rope(x, tab):
    c, s = tab[:, :LANES], tab[:, LANES:]
    parts = []
    for g in range(x.shape[1] // LANES):
        xg = x[:, g * LANES:(g + 1) * LANES]
        parts.append(xg * c + pltpu.roll(xg, ROPE_PAIR_SHIFT, 1) * s)
    return parts[0] if len(parts) == 1 else jnp.concatenate(parts, axis=1)


def _inproj_kernel(x_ref, g_ref, w_ref, ws_ref, wt_ref, tab_ref, tabk_ref,
                   main_ref, ikd_ref, misc_ref, dvt_ref, iwt_ref, xn_ref):
    @pl.when(pl.program_id(1) == 0)
    def _():
        x = x_ref[...]
        ms = jnp.mean(x * x, axis=-1, keepdims=True)
        xn_ref[...] = (x * lax.rsqrt(ms + EPS) * g_ref[...]).astype(BF16)
        small = jnp.dot(xn_ref[...], ws_ref[...], preferred_element_type=F32)
        ikd_ref[...] = _rope(small[:, :2 * LANES], tabk_ref[...]).astype(BF16)
        misc_ref[...] = small[:, 2 * LANES:]
        tr = lax.dot_general(wt_ref[...], xn_ref[...], NT_DIMS, preferred_element_type=F32)
        for ci in range(dvt_ref.shape[0]):
            dvt_ref[ci] = tr[:N_DSA, ci * DSA_BLK:(ci + 1) * DSA_BLK].astype(BF16)
        iwt_ref[...] = tr[N_DSA:, :]

    acc = jnp.dot(xn_ref[...], w_ref[...], preferred_element_type=F32)
    main_ref[...] = _rope(acc, tab_ref[...]).astype(BF16)


def _inproj(h, gain, w_main, w_small, w_t, tab):
    t, d = h.shape
    tm = min(1024, t)
    tn = 512
    nt = w_t.shape[0]
    j_dq, j_dk, j_iq = OFF_DQ // tn, OFF_DK // tn, OFF_IQ // tn

    def tile_table(i, j):
        kind = (j >= j_dq).astype(I32) + (j >= j_dk).astype(I32) + (j >= j_iq).astype(I32)
        return (kind, i, 0)

    return pl.pallas_call(
        _inproj_kernel,
        grid=(t // tm, N_MAIN // tn),
        in_specs=[
            pl.BlockSpec((tm, d), lambda i, j: (i, 0)),
            pl.BlockSpec((1, d), lambda i, j: (0, 0)),
            pl.BlockSpec((d, tn), lambda i, j: (0, j)),
            pl.BlockSpec((d, N_SMALL), lambda i, j: (0, 0)),
            pl.BlockSpec((nt, d), lambda i, j: (0, 0)),
            pl.BlockSpec((None, tm, 2 * LANES), tile_table),
            pl.BlockSpec((None, tm, 2 * LANES), lambda i, j: (TAB_IK, i, 0)),
        ],
        out_specs=[
            pl.BlockSpec((tm, tn), lambda i, j: (i, j)),
            pl.BlockSpec((tm, 2 * LANES), lambda i, j: (i, 0)),
            pl.BlockSpec((tm, LANES), lambda i, j: (i, 0)),
            pl.BlockSpec((tm // DSA_BLK, N_DSA, DSA_BLK), lambda i, j: (i, 0, 0)),
            pl.BlockSpec((LANES, tm), lambda i, j: (0, i)),
        ],
        out_shape=[
            jax.ShapeDtypeStruct((t, N_MAIN), BF16),
            jax.ShapeDtypeStruct((t, 2 * LANES), BF16),
            jax.ShapeDtypeStruct((t, LANES), F32),
            jax.ShapeDtypeStruct((t // DSA_BLK, N_DSA, DSA_BLK), BF16),
            jax.ShapeDtypeStruct((LANES, t), F32),
        ],
        scratch_shapes=[pltpu.VMEM((tm, d), BF16)],
        compiler_params=_params("parallel", "arbitrary"),
        name="inproj",
    )(h, gain, w_main, w_small, w_t, tab, tab)


def _gla_kernel(q_ref, k_ref, v_ref, gg_ref, misc_ref, w2_ref, ba_ref, gn_ref, o_ref, state_ref, *, blk):
    c_len = GLA_CHUNK

    @pl.when(pl.program_id(1) == 0)
    def _():
        state_ref[...] = jnp.zeros_like(state_ref)

    gpre = jnp.dot(misc_ref[...], w2_ref[...], preferred_element_type=F32,
                   precision=lax.Precision.HIGHEST) + ba_ref[...]
    log_a = (jnp.minimum(gpre, 0.0) - jnp.log1p(jnp.exp(-jnp.abs(gpre)))) * (1.0 / GLA_GATE_TAU)

    r = lax.broadcasted_iota(I32, (c_len, c_len), 0)
    c = lax.broadcasted_iota(I32, (c_len, c_len), 1)
    causal = c <= r
    tril = causal.astype(F32)
    gn = gn_ref[...]

    for hd in range(GLA_HEADS):
        ck = slice(hd * GLA_DK, (hd + 1) * GLA_DK)
        cv = slice(hd * GLA_DV, (hd + 1) * GLA_DV)
        for ci in range(blk // c_len):
            rows = slice(ci * c_len, (ci + 1) * c_len)
            q = q_ref[rows, ck].astype(F32) * (GLA_DK ** -0.5)
            k = k_ref[rows, ck].astype(F32)
            v = v_ref[rows, cv]
            g = log_a[rows, ck]
            b = jnp.dot(tril, g, preferred_element_type=F32, precision=lax.Precision.HIGHEST)
            b_last = b[c_len - 1:c_len, :]
            q_dec = (q * jnp.exp(b)).astype(BF16)
            k_inv = (k * jnp.exp(-b)).astype(BF16)
            k_dec = (k * jnp.exp(b_last - b)).astype(BF16)
            attn = lax.dot_general(q_dec, k_inv, NT_DIMS, preferred_element_type=F32)
            attn = jnp.where(causal, attn, 0.0).astype(BF16)
            st = state_ref[hd]
            o = jnp.dot(attn, v, preferred_element_type=F32)
            o = o + lax.dot_general(q_dec, st.astype(BF16), NT_DIMS, preferred_element_type=F32)
            d_st = lax.dot_general(v, k_dec, TN_DIMS, preferred_element_type=F32)
            state_ref[hd] = st * jnp.exp(b_last) + d_st
            ms = jnp.mean(o * o, axis=-1, keepdims=True)
            y = o * lax.rsqrt(ms + EPS) * gn
            gate = gg_ref[rows, cv].astype(F32)
            y = y * (gate / (1.0 + jnp.exp(-gate)))
            o_ref[rows, cv] = y.astype(BF16)


def _gla(main, misc, w2, ba, gn, batch, seq):
    t = batch * seq
    blk = min(256, seq)
    nb = seq // blk
    kern = functools.partial(_gla_kernel, blk=blk)

    def rowmap(col):
        return lambda b, n: (b * nb + n, col)

    return pl.pallas_call(
        kern,
        grid=(batch, nb),
        in_specs=[
            pl.BlockSpec((blk, N_GQ), rowmap(OFF_GQ // N_GQ)),
            pl.BlockSpec((blk, N_GQ), rowmap(OFF_GK // N_GQ)),
            pl.BlockSpec((blk, N_GV), rowmap(OFF_GV // N_GV)),
            pl.BlockSpec((blk, N_GV), rowmap(OFF_GG // N_GV)),
            pl.BlockSpec((blk, LANES), rowmap(0)),
            pl.BlockSpec((LANES, N_GQ), lambda b, n: (0, 0)),
            pl.BlockSpec((1, N_GQ), lambda b, n: (0, 0)),
            pl.BlockSpec((1, GLA_DV), lambda b, n: (0, 0)),
        ],
        out_specs=pl.BlockSpec((blk, N_GV), rowmap(0)),
        out_shape=jax.ShapeDtypeStruct((t, N_GV), BF16),
        scratch_shapes=[pltpu.VMEM((GLA_HEADS, GLA_DV, GLA_DK), F32)],
        compiler_params=_params("parallel", "arbitrary"),
        name="gla",
    )(main, main, main, main, misc, w2, ba, gn)


def _dsa_kernel(q_ref, iq_ref, iwt_ref, k_ref, vt_ref, ikd_ref, o_ref,
                key_ref, hi_ref, lo_ref, bias_ref, acc_ref, s_ref, p_ref, m_ref, l_ref, alpha_ref,
                *, blk, topk, idx_bits):
    qi = pl.program_id(1)
    nch = qi + 1
    krow = lax.broadcasted_iota(I32, (blk, blk), 0)
    qcol = lax.broadcasted_iota(I32, (blk, blk), 1)
    w_idx = iwt_ref[0:IDX_HEADS, :] * (IDX_HEADS ** -0.5)

    def score_body(c, carry):
        ks = pl.multiple_of(c * blk, blk)
        k_lo = ikd_ref[pl.ds(ks, blk), 0:LANES]
        k_hi = ikd_ref[pl.ds(ks, blk), LANES:2 * LANES]
        acc = jnp.zeros((blk, blk), F32)
        for p in range(IDX_HEADS // 2):
            iq_p = iq_ref[:, p * LANES:(p + 1) * LANES]
            l0 = lax.dot_general(k_lo, iq_p, NT_DIMS, preferred_element_type=F32)
            l1 = lax.dot_general(k_hi, iq_p, NT_DIMS, preferred_element_type=F32)
            acc = acc + w_idx[2 * p:2 * p + 1, :] * jnp.maximum(l0, 0.0)
            acc = acc + w_idx[2 * p + 1:2 * p + 2, :] * jnp.maximum(l1, 0.0)
        score = jnp.where(krow + (c - qi) * blk > qcol, NEG_INF, acc)
        score = jnp.where(score == 0.0, 0.0, score)
        bits = pltpu.bitcast(score, I32)
        key = bits ^ ((bits >> 31) & 0x7FFFFFFF)
        key_ref[c] = key
        hi_ref[c] = (key >> 16).astype(I16)
        return carry

    lax.fori_loop(0, nch, score_body, 0)

    def count16(ref, cand):
        def body(c, acc):
            m = jnp.where(ref[c] >= cand, jnp.ones((), BF16), jnp.zeros((), BF16))
            for i in range(blk // COUNT16_ROWS):
                acc = acc + m[i * COUNT16_ROWS:(i + 1) * COUNT16_ROWS]
            return acc
        acc = lax.fori_loop(0, nch, body, jnp.zeros((COUNT16_ROWS, blk), BF16))
        return jnp.sum(acc.astype(F32), axis=0, keepdims=True)

    def kth_largest16(ref, kth):
        v = jnp.where(count16(ref, jnp.zeros((1, blk), I16)) >= kth, 0, -(2 ** 15)).astype(I32)

        def body(i, v):
            cand = v | lax.shift_left(jnp.int32(1), 14 - i)
            return jnp.where(count16(ref, cand.astype(I16)) >= kth, cand, v)

        return lax.fori_loop(0, 15, body, v)

    def count(pred):
        def body(c, acc):
            m = jnp.where(pred(key_ref[c], c), 1.0, 0.0)
            return acc + jnp.sum(m.reshape(blk // COUNT_ROWS, COUNT_ROWS, blk), axis=0)
        acc = lax.fori_loop(0, nch, body, jnp.zeros((COUNT_ROWS, blk), F32))
        return jnp.sum(acc, axis=0, keepdims=True)

    kf = float(topk)
    thr_hi = kth_largest16(hi_ref, kf)
    n_above = count16(hi_ref, (thr_hi + 1).astype(I16))
    n_above = jnp.where(thr_hi == 2 ** 15 - 1, 0.0, n_above)

    def low_body(c, carry):
        key = key_ref[c]
        low = (key & 0xFFFF) - 2 ** 15
        lo_ref[c] = jnp.where((key >> 16) == thr_hi, low, -(2 ** 15)).astype(I16)
        return carry

    lax.fori_loop(0, nch, low_body, 0)
    thr_lo = kth_largest16(lo_ref, kf - n_above)
    thr = thr_hi * 2 ** 16 + (thr_lo + 2 ** 15)

    n_ge = count(lambda kc, c: kc >= thr)

    def resolve_ties():
        need = kf - count(lambda kc, c: kc > thr)

        def tie_body(i, last):
            cand = last | lax.shift_left(jnp.int32(1), idx_bits - 1 - i)
            below = count(lambda kc, c: (kc == thr) & (c * blk + krow < cand))
            return jnp.where(below < need, cand, last)

        return lax.fori_loop(0, idx_bits, tie_body, jnp.zeros((1, blk), I32))

    last = lax.cond(jnp.max(n_ge) > kf, resolve_ties,
                    lambda: jnp.full((1, blk), 2 ** idx_bits, I32))

    def write_bias(c, diagonal):
        kc = key_ref[c]
        tie_pos = jnp.where(kc == thr, c * blk + krow, -1)
        bias = jnp.where(kc >= thr, jnp.where(tie_pos > last, NEG_INF, 0.0), NEG_INF)
        if diagonal:
            bias = jnp.where(krow > qcol, NEG_INF, bias)
        bias_ref[c] = bias

    def bias_body(c, carry):
        write_bias(c, False)
        return carry

    lax.fori_loop(0, qi, bias_body, 0)
    write_bias(qi, True)

    heads = tuple((hd, slice(hd * DSA_HEAD_DIM, (hd + 1) * DSA_HEAD_DIM)) for hd in range(DSA_HEADS))

    def fold(x, op):
        return op(x.reshape(blk // 8, 8, blk), axis=0)

    m_ref[...] = jnp.full_like(m_ref, NEG_INF)
    l_ref[...] = jnp.zeros_like(l_ref)
    acc_ref[...] = jnp.zeros_like(acc_ref)

    def attn_body(c, carry):
        ks = pl.multiple_of(c * blk, blk)

        @pl.when(c >= 0)
        def _():
            for hd, hs in heads:
                s = lax.dot_general(k_ref[pl.ds(ks, blk), hs], q_ref[:, hs], NT_DIMS,
                                    preferred_element_type=F32) + bias_ref[c]
                s_ref[hd] = s
                m_prev = m_ref[hd, 0:1, :]
                m_new = jnp.maximum(m_prev, jnp.max(fold(s, jnp.max), axis=0, keepdims=True))
                alpha = jnp.exp(m_prev - m_new)
                m_ref[hd, 0:1, :] = m_new
                alpha_ref[hd, 0:1, :] = alpha
                p = jnp.exp(s_ref[hd] - m_new)
                l_ref[hd] = alpha * l_ref[hd] + fold(p, jnp.sum)
                p_ref[hd] = p.astype(BF16)

        @pl.when(c >= 0)
        def _():
            for hd, hs in heads:
                pv = jnp.dot(vt_ref[c, hs, :], p_ref[hd], preferred_element_type=F32)
                acc_ref[hs, :] = alpha_ref[hd, 0:1, :] * acc_ref[hs, :] + pv

        return carry

    lax.fori_loop(0, nch, attn_body, 0)

    for hd, hs in heads:
        o_t = acc_ref[hs, :] / jnp.sum(l_ref[hd], axis=0, keepdims=True)
        o_ref[:, hs] = o_t.T.astype(BF16)


def _dsa(main, ikd, iwt, dvt, batch, seq):
    t = batch * seq
    blk = DSA_BLK
    nq = seq // blk
    topk = min(DSA_TOPK_MAX, seq // 4)
    idx_bits = max(1, (seq - 1).bit_length())
    assert (blk // COUNT16_ROWS) * nq <= BF16_EXACT_INT
    kern = functools.partial(_dsa_kernel, blk=blk, topk=topk, idx_bits=idx_bits)

    def qmap(col):
        return lambda b, i: (b * nq + i, col)

    return pl.pallas_call(
        kern,
        grid=(batch, nq),
        in_specs=[
            pl.BlockSpec((blk, N_DSA), qmap(OFF_DQ // N_DSA)),
            pl.BlockSpec((blk, N_IQ), qmap(OFF_IQ // N_IQ)),
            pl.BlockSpec((LANES, blk), lambda b, i: (0, b * nq + i)),
            pl.BlockSpec((seq, N_DSA), lambda b, i: (b, OFF_DK // N_DSA)),
            pl.BlockSpec((nq, N_DSA, blk), lambda b, i: (b, 0, 0)),
            pl.BlockSpec((seq, 2 * LANES), lambda b, i: (b, 0)),
        ],
        out_specs=pl.BlockSpec((blk, N_DSA), qmap(0)),
        out_shape=jax.ShapeDtypeStruct((t, N_DSA), BF16),
        scratch_shapes=[
            pltpu.VMEM((nq, blk, blk), I32),
            pltpu.VMEM((nq, blk, blk), I16),
            pltpu.VMEM((nq, blk, blk), I16),
            pltpu.VMEM((nq, blk, blk), F32),
            pltpu.VMEM((N_DSA, blk), F32),
            pltpu.VMEM((DSA_HEADS, blk, blk), F32),
            pltpu.VMEM((DSA_HEADS, blk, blk), BF16),
            pltpu.VMEM((DSA_HEADS, 8, blk), F32),
            pltpu.VMEM((DSA_HEADS, 8, blk), F32),
            pltpu.VMEM((DSA_HEADS, 8, blk), F32),
        ],
        compiler_params=_params("parallel", "arbitrary"),
        name="dsa",
    )(main, main, iwt, main, dvt, ikd)


def _outproj_kernel(og_ref, od_ref, h_ref, w_ref, g_ref, o_ref):
    m = jnp.dot(og_ref[...], w_ref[:N_GV, :], preferred_element_type=F32)
    m = m + jnp.dot(od_ref[...], w_ref[N_GV:, :], preferred_element_type=F32)
    ms = jnp.mean(m * m, axis=-1, keepdims=True)
    o_ref[...] = h_ref[...] + m * lax.rsqrt(ms + EPS) * g_ref[...]


def _outproj(o_gla, o_dsa, h, w_out, gain):
    t, d = h.shape
    tm = min(512, t)
    return pl.pallas_call(
        _outproj_kernel,
        grid=(t // tm,),
        in_specs=[
            pl.BlockSpec((tm, N_GV), lambda i: (i, 0)),
            pl.BlockSpec((tm, N_DSA), lambda i: (i, 0)),
            pl.BlockSpec((tm, d), lambda i: (i, 0)),
            pl.BlockSpec((N_GV + N_DSA, d), lambda i: (0, 0)),
            pl.BlockSpec((1, d), lambda i: (0, 0)),
        ],
        out_specs=pl.BlockSpec((tm, d), lambda i: (i, 0)),
        out_shape=jax.ShapeDtypeStruct((t, d), F32),
        compiler_params=_params("parallel"),
        name="outproj",
    )(o_gla, o_dsa, h, w_out, gain)


def _ffn_kernel(h_ref, gpre_ref, wu_ref, wd_ref, gpost_ref, o_ref, xn_ref, acc_ref):
    f = pl.program_id(1)

    @pl.when(f == 0)
    def _():
        x = h_ref[...]
        ms = jnp.mean(x * x, axis=-1, keepdims=True)
        xn_ref[...] = (x * lax.rsqrt(ms + EPS) * gpre_ref[...]).astype(BF16)
        acc_ref[...] = jnp.zeros_like(acc_ref)

    u = jnp.maximum(jnp.dot(xn_ref[...], wu_ref[...], preferred_element_type=F32), 0.0)
    acc_ref[...] += jnp.dot((u * u).astype(BF16), wd_ref[...], preferred_element_type=F32)

    @pl.when(f == pl.num_programs(1) - 1)
    def _():
        y = acc_ref[...]
        ms = jnp.mean(y * y, axis=-1, keepdims=True)
        o_ref[...] = h_ref[...] + y * lax.rsqrt(ms + EPS) * gpost_ref[...]


def _ffn(h, g_pre, w_up, w_down, g_post):
    t, d = h.shape
    d_ff = w_up.shape[1]
    tm = min(512, t)
    tf = 512
    return pl.pallas_call(
        _ffn_kernel,
        grid=(t // tm, d_ff // tf),
        in_specs=[
            pl.BlockSpec((tm, d), lambda i, f: (i, 0)),
            pl.BlockSpec((1, d), lambda i, f: (0, 0)),
            pl.BlockSpec((d, tf), lambda i, f: (0, f)),
            pl.BlockSpec((tf, d), lambda i, f: (f, 0)),
            pl.BlockSpec((1, d), lambda i, f: (0, 0)),
        ],
        out_specs=pl.BlockSpec((tm, d), lambda i, f: (i, 0)),
        out_shape=jax.ShapeDtypeStruct((t, d), F32),
        scratch_shapes=[pltpu.VMEM((tm, d), BF16), pltpu.VMEM((tm, d), F32)],
        compiler_params=_params("parallel", "arbitrary"),
        name="ffn",
    )(h, g_pre, w_up, w_down, g_post)


def _rope_table(positions, head_dim, scale):
    r = head_dim // ROPE_FRACTION
    n_freq = r // 2
    inv_freq = ROPE_THETA ** (-(jnp.arange(0, r, 2, dtype=F32) / r))
    ang = positions.reshape(-1).astype(F32)[:, None] * inv_freq
    cos, sin = jnp.cos(ang), jnp.sin(ang)
    lane = jnp.arange(LANES)
    idx = (lane % ROPE_PAIR_SHIFT) % n_freq
    lo = (lane < ROPE_LANES)[None, :]
    hi = ((lane >= ROPE_PAIR_SHIFT) & (lane < ROPE_PAIR_SHIFT + ROPE_LANES))[None, :]
    c = jnp.where(lo | hi, cos[:, idx], 1.0)
    s = jnp.where(lo, -sin[:, idx], jnp.where(hi, sin[:, idx], 0.0))
    return jnp.concatenate([c, s], axis=1) * scale


def _rope_tables(positions):
    t = positions.size
    plain = jnp.concatenate([jnp.ones((t, LANES), F32), jnp.zeros((t, LANES), F32)], axis=1)
    tabs = {TAB_PLAIN: plain,
            TAB_DQ: _rope_table(positions, DSA_HEAD_DIM, DSA_HEAD_DIM ** -0.5),
            TAB_DK: _rope_table(positions, DSA_HEAD_DIM, 1.0),
            TAB_IQ: _rope_table(positions, IDX_DIM, IDX_DIM ** -0.5),
            TAB_IK: _rope_table(positions, IDX_DIM, 1.0)}
    return jnp.stack([tabs[k] for k in range(len(tabs))])


def _pair_layout_dsa(w):
    d = w.shape[0]
    half = DSA_HEAD_DIM // ROPE_FRACTION // 2
    assert half == ROPE_LANES
    w = w.reshape(d, DSA_HEADS, DSA_HEAD_DIM)
    cut = 2 * half + ROPE_PAIR_SHIFT - half
    w = jnp.concatenate([w[:, :, :half], w[:, :, 2 * half:cut], w[:, :, half:2 * half], w[:, :, cut:]], axis=2)
    return w.reshape(d, N_DSA)


def _pair_layout_iq(w):
    d = w.shape[0]
    half = IDX_DIM // ROPE_FRACTION // 2
    assert 2 * half == ROPE_LANES
    w = w.reshape(d, IDX_HEADS // 2, 2, IDX_DIM)
    a, b = w[:, :, 0, :], w[:, :, 1, :]
    w = jnp.concatenate([a[..., :half], b[..., :half], a[..., 2 * half:],
                         a[..., half:2 * half], b[..., half:2 * half], b[..., 2 * half:]], axis=-1)
    return w.reshape(d, N_IQ)


def _pair_layout_ik(ik):
    d = ik.shape[0]
    half = IDX_DIM // ROPE_FRACTION // 2
    x1, x2, rest = ik[:, :half], ik[:, half:2 * half], ik[:, 2 * half:]

    def z(n):
        return jnp.zeros((d, n), ik.dtype)

    n_rest = IDX_DIM - 2 * half
    k_a = jnp.concatenate([x1, z(half), rest, x2, z(half), z(n_rest)], axis=1)
    k_b = jnp.concatenate([z(half), x1, z(n_rest), z(half), x2, rest], axis=1)
    return k_a, k_b


def _split_w_in(w_in):
    d = w_in.shape[0]
    o_gr = 2 * N_GQ + 2 * N_GV
    o_dq = o_gr + GLA_GATE_RANK
    o_dk = o_dq + N_DSA
    o_dv = o_dk + N_DSA
    o_iq = o_dv + N_DSA
    o_ik = o_iq + N_IQ
    o_iw = o_ik + IDX_DIM
    w_main = jnp.concatenate([w_in[:, :o_gr], _pair_layout_dsa(w_in[:, o_dq:o_dk]),
                              _pair_layout_dsa(w_in[:, o_dk:o_dv]), _pair_layout_iq(w_in[:, o_iq:o_ik])], axis=1)
    k_a, k_b = _pair_layout_ik(w_in[:, o_ik:o_iw])
    iw = w_in[:, o_iw:o_iw + IDX_HEADS]
    gr = w_in[:, o_gr:o_dq]
    pad = jnp.zeros((d, LANES - IDX_HEADS - GLA_GATE_RANK), w_in.dtype)
    w_small = jnp.concatenate([k_a, k_b, iw, gr, pad], axis=1)
    w_t = jnp.concatenate([w_in[:, o_dv:o_iq], iw, jnp.zeros((d, LANES - IDX_HEADS), w_in.dtype)], axis=1).T
    return w_main.astype(BF16), w_small.astype(BF16), w_t.astype(BF16)


def kernel(x, positions, norm_mix_pre, w_in, gla_wa2, gla_ba, gla_norm, w_out, norm_mix_post,
           norm_ffn_pre, w_up, w_down, norm_ffn_post):
    batch, seq, d = x.shape
    depth = w_in.shape[0]
    assert seq % DSA_BLK == 0
    tab = _rope_tables(positions)
    h = x.reshape(batch * seq, d)
    for l in range(depth):
        w_main, w_small, w_t = _split_w_in(w_in[l])
        w2 = jnp.zeros((LANES, N_GQ), F32).at[IDX_HEADS:IDX_HEADS + GLA_GATE_RANK].set(gla_wa2[l])
        main, ikd, misc, dvt, iwt = _inproj(h, norm_mix_pre[l][None, :], w_main, w_small, w_t, tab)
        o_gla = _gla(main, misc, w2, gla_ba[l][None, :], gla_norm[l][None, :], batch, seq)
        o_dsa = _dsa(main, ikd, iwt, dvt, batch, seq)
        h = _outproj(o_gla, o_dsa, h, w_out[l].astype(BF16), norm_mix_post[l][None, :])
        h = _ffn(h, norm_ffn_pre[l][None, :], w_up[l].astype(BF16), w_down[l].astype(BF16),
                 norm_ffn_post[l][None, :])
    return h.reshape(batch, seq, d)
```

```python
import functools

import jax
import jax.numpy as jnp
from jax import lax
from jax.experimental import pallas as pl
from jax.experimental.pallas import tpu as pltpu

F32 = jnp.float32
BF16 = jnp.bfloat16
I32 = jnp.int32
I16 = jnp.int16

GLA_HEADS = 4
GLA_DK = 128
GLA_DV = 256
GLA_GATE_RANK = 16
GLA_GATE_TAU = 16.0
GLA_CHUNK = 64
DSA_HEADS = 8
DSA_HEAD_DIM = 128
IDX_HEADS = 16
IDX_DIM = 64
DSA_TOPK_MAX = 256
ROPE_THETA = 500000.0
ROPE_FRACTION = 4
EPS = 1e-6
NEG_INF = -1e30
INT_MIN = -(2 ** 31)

LANES = 128
N_GQ = GLA_HEADS * GLA_DK
N_GV = GLA_HEADS * GLA_DV
N_DSA = DSA_HEADS * DSA_HEAD_DIM
N_IQ = IDX_HEADS * IDX_DIM
OFF_GQ = 0
OFF_GK = OFF_GQ + N_GQ
OFF_GV = OFF_GK + N_GQ
OFF_GG = OFF_GV + N_GV
OFF_DQ = OFF_GG + N_GV
OFF_DK = OFF_DQ + N_DSA
OFF_IQ = OFF_DK + N_DSA
N_MAIN = OFF_IQ + N_IQ
N_SMALL = 3 * LANES
DSA_BLK = 256
ROPE_LANES = 16
ROPE_PAIR_SHIFT = LANES // 2
TAB_PLAIN, TAB_DQ, TAB_DK, TAB_IQ, TAB_IK = range(5)
COUNT_ROWS = 32
COUNT16_ROWS = 64
BF16_EXACT_INT = 256
VMEM_LIMIT = 56 * 1024 * 1024

NT_DIMS = (((1,), (1,)), ((), ()))
TN_DIMS = (((0,), (0,)), ((), ()))


def _params(*sem):
    return pltpu.CompilerParams(dimension_semantics=sem, vmem_limit_bytes=VMEM_LIMIT)


def _rope(x, tab):
    c, s = tab[:, :LANES], tab[:, LANES:]
    parts = []
    for g in range(x.shape[1] // LANES):
        xg = x[:, g * LANES:(g + 1) * LANES]
        parts.append(xg * c + pltpu.roll(xg, ROPE_PAIR_SHIFT, 1) * s)
    return parts[0] if len(parts) == 1 else jnp.concatenate(parts, axis=1)


def _inproj_kernel(x_ref, g_ref, w_ref, ws_ref, wt_ref, tab_ref, tabk_ref,
                   main_ref, ikd_ref, misc_ref, dvt_ref, iwt_ref, xn_ref):
    @pl.when(pl.program_id(1) == 0)
    def _():
        x = x_ref[...]
        ms = jnp.mean(x * x, axis=-1, keepdims=True)
        xn_ref[...] = (x * lax.rsqrt(ms + EPS) * g_ref[...]).astype(BF16)
        small = jnp.dot(xn_ref[...], ws_ref[...], preferred_element_type=F32)
        ikd_ref[...] = _rope(small[:, :2 * LANES], tabk_ref[...]).astype(BF16)
        misc_ref[...] = small[:, 2 * LANES:]
        tr = lax.dot_general(wt_ref[...], xn_ref[...], NT_DIMS, preferred_element_type=F32)
        for ci in range(dvt_ref.shape[0]):
            dvt_ref[ci] = tr[:N_DSA, ci * DSA_BLK:(ci + 1) * DSA_BLK].astype(BF16)
        iwt_ref[...] = tr[N_DSA:, :]

    acc = jnp.dot(xn_ref[...], w_ref[...], preferred_element_type=F32)
    main_ref[...] = _rope(acc, tab_ref[...]).astype(BF16)


def _inproj(h, gain, w_main, w_small, w_t, tab):
    t, d = h.shape
    tm = min(1024, t)
    tn = 512
    nt = w_t.shape[0]
    j_dq, j_dk, j_iq = OFF_DQ // tn, OFF_DK // tn, OFF_IQ // tn

    def tile_table(i, j):
        kind = (j >= j_dq).astype(I32) + (j >= j_dk).astype(I32) + (j >= j_iq).astype(I32)
        return (kind, i, 0)

    return pl.pallas_call(
        _inproj_kernel,
        grid=(t // tm, N_MAIN // tn),
        in_specs=[
            pl.BlockSpec((tm, d), lambda i, j: (i, 0)),
            pl.BlockSpec((1, d), lambda i, j: (0, 0)),
            pl.BlockSpec((d, tn), lambda i, j: (0, j)),
            pl.BlockSpec((d, N_SMALL), lambda i, j: (0, 0), pipeline_mode=pl.Buffered(1)),
            pl.BlockSpec((nt, d), lambda i, j: (0, 0), pipeline_mode=pl.Buffered(1)),
            pl.BlockSpec((None, tm, 2 * LANES), tile_table),
            pl.BlockSpec((None, tm, 2 * LANES), lambda i, j: (TAB_IK, i, 0)),
        ],
        out_specs=[
            pl.BlockSpec((tm, tn), lambda i, j: (i, j)),
            pl.BlockSpec((tm, 2 * LANES), lambda i, j: (i, 0)),
            pl.BlockSpec((tm, LANES), lambda i, j: (i, 0)),
            pl.BlockSpec((tm // DSA_BLK, N_DSA, DSA_BLK), lambda i, j: (i, 0, 0)),
            pl.BlockSpec((LANES, tm), lambda i, j: (0, i)),
        ],
        out_shape=[
            jax.ShapeDtypeStruct((t, N_MAIN), BF16),
            jax.ShapeDtypeStruct((t, 2 * LANES), BF16),
            jax.ShapeDtypeStruct((t, LANES), F32),
            jax.ShapeDtypeStruct((t // DSA_BLK, N_DSA, DSA_BLK), BF16),
            jax.ShapeDtypeStruct((LANES, t), F32),
        ],
        scratch_shapes=[pltpu.VMEM((tm, d), BF16)],
        compiler_params=_params("parallel", "arbitrary"),
        name="inproj",
    )(h, gain, w_main, w_small, w_t, tab, tab)


def _gla_kernel(q_ref, k_ref, v_ref, gg_ref, misc_ref, w2_ref, ba_ref, gn_ref, o_ref, state_ref, *, blk):
    c_len = GLA_CHUNK
    n_ch = blk // c_len
    shift = c_len.bit_length() - 1
    assert 1 << shift == c_len

    @pl.when(pl.program_id(1) == 0)
    def _():
        state_ref[...] = jnp.zeros_like(state_ref)

    gpre = jnp.dot(misc_ref[...], w2_ref[...], preferred_element_type=F32,
                   precision=lax.Precision.HIGHEST) + ba_ref[...]
    log_a = (jnp.minimum(gpre, 0.0) - jnp.log1p(jnp.exp(-jnp.abs(gpre)))) * (1.0 / GLA_GATE_TAU)

    r = lax.broadcasted_iota(I32, (blk, blk), 0)
    c = lax.broadcasted_iota(I32, (blk, blk), 1)
    causal = (c <= r) & ((r >> shift) == (c >> shift))
    b = jnp.dot(causal.astype(F32), log_a, preferred_element_type=F32, precision=lax.Precision.HIGHEST)
    b3 = b.reshape(n_ch, c_len, N_GQ)
    b_last = b3[:, c_len - 1:c_len, :]
    decay = jnp.exp(b_last)
    k = k_ref[...].astype(F32)
    q_dec = (q_ref[...].astype(F32) * (GLA_DK ** -0.5) * jnp.exp(b)).astype(BF16)
    k_inv = (k * jnp.exp(-b)).astype(BF16)
    k_dec = (k * jnp.exp(b_last - b3).reshape(blk, N_GQ)).astype(BF16)
    gn = gn_ref[...]

    for hd in range(GLA_HEADS):
        ck = slice(hd * GLA_DK, (hd + 1) * GLA_DK)
        cv = slice(hd * GLA_DV, (hd + 1) * GLA_DV)
        attn = lax.dot_general(q_dec[:, ck], k_inv[:, ck], NT_DIMS, preferred_element_type=F32)
        attn = jnp.where(causal, attn, 0.0).astype(BF16)
        o = jnp.dot(attn, v_ref[:, cv], preferred_element_type=F32)
        st = state_ref[hd]
        inter = []
        for ci in range(n_ch):
            rows = slice(ci * c_len, (ci + 1) * c_len)
            inter.append(lax.dot_general(q_dec[rows, ck], st.astype(BF16), NT_DIMS,
                                         preferred_element_type=F32))
            d_st = lax.dot_general(v_ref[rows, cv], k_dec[rows, ck], TN_DIMS, preferred_element_type=F32)
            st = st * decay[ci, :, ck] + d_st
        state_ref[hd] = st
        o = o + jnp.concatenate(inter, axis=0)
        ms = jnp.mean(o * o, axis=-1, keepdims=True)
        y = o * lax.rsqrt(ms + EPS) * gn
        gate = gg_ref[:, cv].astype(F32)
        o_ref[:, cv] = (y * (gate / (1.0 + jnp.exp(-gate)))).astype(BF16)


def _gla(main, misc, w2, ba, gn, batch, seq):
    t = batch * seq
    blk = min(256, seq)
    nb = seq // blk
    kern = functools.partial(_gla_kernel, blk=blk)

    def rowmap(col):
        return lambda b, n: (b * nb + n, col)

    return pl.pallas_call(
        kern,
        grid=(batch, nb),
        in_specs=[
            pl.BlockSpec((blk, N_GQ), rowmap(OFF_GQ // N_GQ)),
            pl.BlockSpec((blk, N_GQ), rowmap(OFF_GK // N_GQ)),
            pl.BlockSpec((blk, N_GV), rowmap(OFF_GV // N_GV)),
            pl.BlockSpec((blk, N_GV), rowmap(OFF_GG // N_GV)),
            pl.BlockSpec((blk, LANES), rowmap(0)),
            pl.BlockSpec((LANES, N_GQ), lambda b, n: (0, 0)),
            pl.BlockSpec((1, N_GQ), lambda b, n: (0, 0)),
            pl.BlockSpec((1, GLA_DV), lambda b, n: (0, 0)),
        ],
        out_specs=pl.BlockSpec((blk, N_GV), rowmap(0)),
        out_shape=jax.ShapeDtypeStruct((t, N_GV), BF16),
        scratch_shapes=[pltpu.VMEM((GLA_HEADS, GLA_DV, GLA_DK), F32)],
        compiler_params=_params("parallel", "arbitrary"),
        name="gla",
    )(main, main, main, main, misc, w2, ba, gn)


def _dsa_kernel(q_ref, iq_ref, iwt_ref, k_ref, vt_ref, ikd_ref, o_ref,
                key_ref, hi_ref, lo_ref, bias_ref, acc_ref, s_ref, p_ref, m_ref, l_ref, alpha_ref,
                *, blk, topk, idx_bits):
    qi = pl.program_id(1)
    nch = qi + 1
    krow = lax.broadcasted_iota(I32, (blk, blk), 0)
    qcol = lax.broadcasted_iota(I32, (blk, blk), 1)
    w_idx = iwt_ref[0:IDX_HEADS, :] * (IDX_HEADS ** -0.5)

    def score_body(c, carry):
        ks = pl.multiple_of(c * blk, blk)
        k_lo = ikd_ref[pl.ds(ks, blk), 0:LANES]
        k_hi = ikd_ref[pl.ds(ks, blk), LANES:2 * LANES]
        acc = jnp.zeros((blk, blk), F32)
        for p in range(IDX_HEADS // 2):
            iq_p = iq_ref[:, p * LANES:(p + 1) * LANES]
            l0 = lax.dot_general(k_lo, iq_p, NT_DIMS, preferred_element_type=F32)
            l1 = lax.dot_general(k_hi, iq_p, NT_DIMS, preferred_element_type=F32)
            acc = acc + w_idx[2 * p:2 * p + 1, :] * jnp.maximum(l0, 0.0)
            acc = acc + w_idx[2 * p + 1:2 * p + 2, :] * jnp.maximum(l1, 0.0)
        score = jnp.where(krow + (c - qi) * blk > qcol, NEG_INF, acc)
        score = jnp.where(score == 0.0, 0.0, score)
        bits = pltpu.bitcast(score, I32)
        key = bits ^ ((bits >> 31) & 0x7FFFFFFF)
        key_ref[c] = key
        hi_ref[c] = (key >> 16).astype(I16)
        return carry

    lax.fori_loop(0, nch, score_body, 0)

    def count16(ref, cand):
        def body(c, acc):
            m = jnp.where(ref[c] >= cand, jnp.ones((), BF16), jnp.zeros((), BF16))
            for i in range(blk // COUNT16_ROWS):
                acc = acc + m[i * COUNT16_ROWS:(i + 1) * COUNT16_ROWS]
            return acc
        acc = lax.fori_loop(0, nch, body, jnp.zeros((COUNT16_ROWS, blk), BF16))
        return jnp.sum(acc.astype(F32), axis=0, keepdims=True)

    def kth_largest16(ref, kth):
        v = jnp.where(count16(ref, jnp.zeros((1, blk), I16)) >= kth, 0, -(2 ** 15)).astype(I32)

        def body(i, v):
            cand = v | lax.shift_left(jnp.int32(1), 14 - i)
            return jnp.where(count16(ref, cand.astype(I16)) >= kth, cand, v)

        return lax.fori_loop(0, 15, body, v)

    def count(pred):
        def body(c, acc):
            m = jnp.where(pred(key_ref[c], c), 1.0, 0.0)
            return acc + jnp.sum(m.reshape(blk // COUNT_ROWS, COUNT_ROWS, blk), axis=0)
        acc = lax.fori_loop(0, nch, body, jnp.zeros((COUNT_ROWS, blk), F32))
        return jnp.sum(acc, axis=0, keepdims=True)

    kf = float(topk)
    thr_hi = kth_largest16(hi_ref, kf)
    n_above = count16(hi_ref, (thr_hi + 1).astype(I16))
    n_above = jnp.where(thr_hi == 2 ** 15 - 1, 0.0, n_above)

    def low_body(c, carry):
        key = key_ref[c]
        low = (key & 0xFFFF) - 2 ** 15
        lo_ref[c] = jnp.where((key >> 16) == thr_hi, low, -(2 ** 15)).astype(I16)
        return carry

    lax.fori_loop(0, nch, low_body, 0)
    thr_lo = kth_largest16(lo_ref, kf - n_above)
    thr = thr_hi * 2 ** 16 + (thr_lo + 2 ** 15)

    n_ge = count(lambda kc, c: kc >= thr)

    def resolve_ties():
        need = kf - count(lambda kc, c: kc > thr)

        def tie_body(i, last):
            cand = last | lax.shift_left(jnp.int32(1), idx_bits - 1 - i)
            below = count(lambda kc, c: (kc == thr) & (c * blk + krow < cand))
            return jnp.where(below < need, cand, last)

        return lax.fori_loop(0, idx_bits, tie_body, jnp.zeros((1, blk), I32))

    last = lax.cond(jnp.max(n_ge) > kf, resolve_ties,
                    lambda: jnp.full((1, blk), 2 ** idx_bits, I32))

    def write_bias(c, diagonal):
        kc = key_ref[c]
        tie_pos = jnp.where(kc == thr, c * blk + krow, -1)
        bias = jnp.where(kc >= thr, jnp.where(tie_pos > last, NEG_INF, 0.0), NEG_INF)
        if diagonal:
            bias = jnp.where(krow > qcol, NEG_INF, bias)
        bias_ref[c] = bias

    def bias_body(c, carry):
        write_bias(c, False)
        return carry

    lax.fori_loop(0, qi, bias_body, 0)
    write_bias(qi, True)

    heads = tuple((hd, slice(hd * DSA_HEAD_DIM, (hd + 1) * DSA_HEAD_DIM)) for hd in range(DSA_HEADS))

    def fold(x, op):
        return op(x.reshape(blk // 8, 8, blk), axis=0)

    m_ref[...] = jnp.full_like(m_ref, NEG_INF)
    l_ref[...] = jnp.zeros_like(l_ref)
    acc_ref[...] = jnp.zeros_like(acc_ref)

    def attn_body(c, carry):
        ks = pl.multiple_of(c * blk, blk)

        @pl.when(c >= 0)
        def _():
            for hd, hs in heads:
                s = lax.dot_general(k_ref[pl.ds(ks, blk), hs], q_ref[:, hs], NT_DIMS,
                                    preferred_element_type=F32) + bias_ref[c]
                s_ref[hd] = s
                m_prev = m_ref[hd, 0:1, :]
                m_new = jnp.maximum(m_prev, jnp.max(fold(s, jnp.max), axis=0, keepdims=True))
                alpha = jnp.exp(m_prev - m_new)
                m_ref[hd, 0:1, :] = m_new
                alpha_ref[hd, 0:1, :] = alpha
                p = jnp.exp(s_ref[hd] - m_new)
                l_ref[hd] = alpha * l_ref[hd] + fold(p, jnp.sum)
                p_ref[hd] = p.astype(BF16)

        @pl.when(c >= 0)
        def _():
            for hd, hs in heads:
                pv = jnp.dot(vt_ref[c, hs, :], p_ref[hd], preferred_element_type=F32)
                acc_ref[hs, :] = alpha_ref[hd, 0:1, :] * acc_ref[hs, :] + pv

        return carry

    lax.fori_loop(0, nch, attn_body, 0)

    for hd, hs in heads:
        o_t = acc_ref[hs, :] / jnp.sum(l_ref[hd], axis=0, keepdims=True)
        o_ref[:, hs] = o_t.T.astype(BF16)


def _dsa(main, ikd, iwt, dvt, batch, seq):
    t = batch * seq
    blk = DSA_BLK
    nq = seq // blk
    topk = min(DSA_TOPK_MAX, seq // 4)
    idx_bits = max(1, (seq - 1).bit_length())
    assert (blk // COUNT16_ROWS) * nq <= BF16_EXACT_INT
    kern = functools.partial(_dsa_kernel, blk=blk, topk=topk, idx_bits=idx_bits)

    def qmap(col):
        return lambda b, i: (b * nq + i, col)

    return pl.pallas_call(
        kern,
        grid=(batch, nq),
        in_specs=[
            pl.BlockSpec((blk, N_DSA), qmap(OFF_DQ // N_DSA)),
            pl.BlockSpec((blk, N_IQ), qmap(OFF_IQ // N_IQ)),
            pl.BlockSpec((LANES, blk), lambda b, i: (0, b * nq + i)),
            pl.BlockSpec((seq, N_DSA), lambda b, i: (b, OFF_DK // N_DSA)),
            pl.BlockSpec((nq, N_DSA, blk), lambda b, i: (b, 0, 0)),
            pl.BlockSpec((seq, 2 * LANES), lambda b, i: (b, 0)),
        ],
        out_specs=pl.BlockSpec((blk, N_DSA), qmap(0)),
        out_shape=jax.ShapeDtypeStruct((t, N_DSA), BF16),
        scratch_shapes=[
            pltpu.VMEM((nq, blk, blk), I32),
            pltpu.VMEM((nq, blk, blk), I16),
            pltpu.VMEM((nq, blk, blk), I16),
            pltpu.VMEM((nq, blk, blk), F32),
            pltpu.VMEM((N_DSA, blk), F32),
            pltpu.VMEM((DSA_HEADS, blk, blk), F32),
            pltpu.VMEM((DSA_HEADS, blk, blk), BF16),
            pltpu.VMEM((DSA_HEADS, 8, blk), F32),
            pltpu.VMEM((DSA_HEADS, 8, blk), F32),
            pltpu.VMEM((DSA_HEADS, 8, blk), F32),
        ],
        compiler_params=_params("parallel", "arbitrary"),
        name="dsa",
    )(main, main, iwt, main, dvt, ikd)


def _outproj_kernel(og_ref, od_ref, h_ref, w_ref, g_ref, o_ref):
    m = jnp.dot(og_ref[...], w_ref[:N_GV, :], preferred_element_type=F32)
    m = m + jnp.dot(od_ref[...], w_ref[N_GV:, :], preferred_element_type=F32)
    ms = jnp.mean(m * m, axis=-1, keepdims=True)
    o_ref[...] = h_ref[...] + m * lax.rsqrt(ms + EPS) * g_ref[...]


def _outproj(o_gla, o_dsa, h, w_out, gain):
    t, d = h.shape
    tm = min(512, t)
    return pl.pallas_call(
        _outproj_kernel,
        grid=(t // tm,),
        in_specs=[
            pl.BlockSpec((tm, N_GV), lambda i: (i, 0)),
            pl.BlockSpec((tm, N_DSA), lambda i: (i, 0)),
            pl.BlockSpec((tm, d), lambda i: (i, 0)),
            pl.BlockSpec((N_GV + N_DSA, d), lambda i: (0, 0)),
            pl.BlockSpec((1, d), lambda i: (0, 0)),
        ],
        out_specs=pl.BlockSpec((tm, d), lambda i: (i, 0)),
        out_shape=jax.ShapeDtypeStruct((t, d), F32),
        compiler_params=_params("parallel"),
        name="outproj",
    )(o_gla, o_dsa, h, w_out, gain)


def _ffn_kernel(h_ref, gpre_ref, wu_ref, wd_ref, gpost_ref, o_ref, xn_ref, acc_ref):
    f = pl.program_id(1)

    @pl.when(f == 0)
    def _():
        x = h_ref[...]
        ms = jnp.mean(x * x, axis=-1, keepdims=True)
        xn_ref[...] = (x * lax.rsqrt(ms + EPS) * gpre_ref[...]).astype(BF16)
        acc_ref[...] = jnp.zeros_like(acc_ref)

    u = jnp.maximum(jnp.dot(xn_ref[...], wu_ref[...], preferred_element_type=F32), 0.0)
    acc_ref[...] += jnp.dot((u * u).astype(BF16), wd_ref[...], preferred_element_type=F32)

    @pl.when(f == pl.num_programs(1) - 1)
    def _():
        y = acc_ref[...]
        ms = jnp.mean(y * y, axis=-1, keepdims=True)
        o_ref[...] = h_ref[...] + y * lax.rsqrt(ms + EPS) * gpost_ref[...]


def _ffn(h, g_pre, w_up, w_down, g_post):
    t, d = h.shape
    d_ff = w_up.shape[1]
    tm = min(512, t)
    tf = 512
    return pl.pallas_call(
        _ffn_kernel,
        grid=(t // tm, d_ff // tf),
        in_specs=[
            pl.BlockSpec((tm, d), lambda i, f: (i, 0)),
            pl.BlockSpec((1, d), lambda i, f: (0, 0)),
            pl.BlockSpec((d, tf), lambda i, f: (0, f)),
            pl.BlockSpec((tf, d), lambda i, f: (f, 0)),
            pl.BlockSpec((1, d), lambda i, f: (0, 0)),
        ],
        out_specs=pl.BlockSpec((tm, d), lambda i, f: (i, 0)),
        out_shape=jax.ShapeDtypeStruct((t, d), F32),
        scratch_shapes=[pltpu.VMEM((tm, d), BF16), pltpu.VMEM((tm, d), F32)],
        compiler_params=_params("parallel", "arbitrary"),
        name="ffn",
    )(h, g_pre, w_up, w_down, g_post)


def _rope_table(positions, head_dim, scale):
    r = head_dim // ROPE_FRACTION
    n_freq = r // 2
    inv_freq = ROPE_THETA ** (-(jnp.arange(0, r, 2, dtype=F32) / r))
    ang = positions.reshape(-1).astype(F32)[:, None] * inv_freq
    rep = ROPE_LANES // n_freq
    cos = jnp.concatenate([jnp.cos(ang) * scale] * rep, axis=1)
    sin = jnp.concatenate([jnp.sin(ang) * scale] * rep, axis=1)
    t = cos.shape[0]
    gap = ROPE_PAIR_SHIFT - ROPE_LANES
    one = jnp.full((t, gap), scale, F32)
    zero = jnp.zeros((t, gap), F32)
    return jnp.concatenate([cos, one, cos, one, -sin, zero, sin, zero], axis=1)


def _rope_tables(positions):
    t = positions.size
    plain = jnp.concatenate([jnp.ones((t, LANES), F32), jnp.zeros((t, LANES), F32)], axis=1)
    tabs = {TAB_PLAIN: plain,
            TAB_DQ: _rope_table(positions, DSA_HEAD_DIM, DSA_HEAD_DIM ** -0.5),
            TAB_DK: _rope_table(positions, DSA_HEAD_DIM, 1.0),
            TAB_IQ: _rope_table(positions, IDX_DIM, IDX_DIM ** -0.5),
            TAB_IK: _rope_table(positions, IDX_DIM, 1.0)}
    return jnp.stack([tabs[k] for k in range(len(tabs))])


def _pair_layout_dsa(w):
    d = w.shape[0]
    half = DSA_HEAD_DIM // ROPE_FRACTION // 2
    assert half == ROPE_LANES
    w = w.reshape(d, DSA_HEADS, DSA_HEAD_DIM)
    cut = 2 * half + ROPE_PAIR_SHIFT - half
    w = jnp.concatenate([w[:, :, :half], w[:, :, 2 * half:cut], w[:, :, half:2 * half], w[:, :, cut:]], axis=2)
    return w.reshape(d, N_DSA)


def _pair_layout_iq(w):
    d = w.shape[0]
    half = IDX_DIM // ROPE_FRACTION // 2
    assert 2 * half == ROPE_LANES
    w = w.reshape(d, IDX_HEADS // 2, 2, IDX_DIM)
    a, b = w[:, :, 0, :], w[:, :, 1, :]
    w = jnp.concatenate([a[..., :half], b[..., :half], a[..., 2 * half:],
                         a[..., half:2 * half], b[..., half:2 * half], b[..., 2 * half:]], axis=-1)
    return w.reshape(d, N_IQ)


def _pair_layout_ik(ik):
    d = ik.shape[0]
    half = IDX_DIM // ROPE_FRACTION // 2
    x1, x2, rest = ik[:, :half], ik[:, half:2 * half], ik[:, 2 * half:]

    def z(n):
        return jnp.zeros((d, n), ik.dtype)

    n_rest = IDX_DIM - 2 * half
    k_a = jnp.concatenate([x1, z(half), rest, x2, z(half), z(n_rest)], axis=1)
    k_b = jnp.concatenate([z(half), x1, z(n_rest), z(half), x2, rest], axis=1)
    return k_a, k_b


def _split_w_in(w_in):
    d = w_in.shape[0]
    o_gr = 2 * N_GQ + 2 * N_GV
    o_dq = o_gr + GLA_GATE_RANK
    o_dk = o_dq + N_DSA
    o_dv = o_dk + N_DSA
    o_iq = o_dv + N_DSA
    o_ik = o_iq + N_IQ
    o_iw = o_ik + IDX_DIM
    w_main = jnp.concatenate([w_in[:, :o_gr], _pair_layout_dsa(w_in[:, o_dq:o_dk]),
                              _pair_layout_dsa(w_in[:, o_dk:o_dv]), _pair_layout_iq(w_in[:, o_iq:o_ik])], axis=1)
    k_a, k_b = _pair_layout_ik(w_in[:, o_ik:o_iw])
    iw = w_in[:, o_iw:o_iw + IDX_HEADS]
    gr = w_in[:, o_gr:o_dq]
    pad = jnp.zeros((d, LANES - IDX_HEADS - GLA_GATE_RANK), w_in.dtype)
    w_small = jnp.concatenate([k_a, k_b, iw, gr, pad], axis=1)
    w_t = jnp.concatenate([w_in[:, o_dv:o_iq], iw, jnp.zeros((d, LANES - IDX_HEADS), w_in.dtype)], axis=1).T
    return w_main, w_small, w_t


def kernel(x, positions, norm_mix_pre, w_in, gla_wa2, gla_ba, gla_norm, w_out, norm_mix_post,
           norm_ffn_pre, w_up, w_down, norm_ffn_post):
    batch, seq, d = x.shape
    depth = w_in.shape[0]
    assert seq % DSA_BLK == 0
    tab = _rope_tables(positions)
    w_in, w_out, w_up, w_down = (w.astype(BF16) for w in (w_in, w_out, w_up, w_down))
    w2 = jnp.pad(gla_wa2, ((0, 0), (IDX_HEADS, LANES - IDX_HEADS - GLA_GATE_RANK), (0, 0)))
    h = x.reshape(batch * seq, d)
    for l in range(depth):
        w_main, w_small, w_t = _split_w_in(w_in[l])
        main, ikd, misc, dvt, iwt = _inproj(h, norm_mix_pre[l][None, :], w_main, w_small, w_t, tab)
        o_gla = _gla(main, misc, w2[l], gla_ba[l][None, :], gla_norm[l][None, :], batch, seq)
        o_dsa = _dsa(main, ikd, iwt, dvt, batch, seq)
        h = _outproj(o_gla, o_dsa, h, w_out[l], norm_mix_post[l][None, :])
        h = _ffn(h, norm_ffn_pre[l][None, :], w_up[l], w_down[l], norm_ffn_post[l][None, :])
    return h.reshape(batch, seq, d)
```

```python
import functools

import jax
import jax.numpy as jnp
import numpy as np
from jax import lax
from jax.experimental import pallas as pl
from jax.experimental.pallas import tpu as pltpu

F32 = jnp.float32
BF16 = jnp.bfloat16
I32 = jnp.int32
I16 = jnp.int16

GLA_HEADS = 4
GLA_DK = 128
GLA_DV = 256
GLA_GATE_RANK = 16
GLA_GATE_TAU = 16.0
GLA_CHUNK = 64
DSA_HEADS = 8
DSA_HEAD_DIM = 128
IDX_HEADS = 16
IDX_DIM = 64
DSA_TOPK_MAX = 256
ROPE_THETA = 500000.0
ROPE_FRACTION = 4
EPS = 1e-6
NEG_INF = -1e30
INT_MIN = -(2 ** 31)

LANES = 128
N_GQ = GLA_HEADS * GLA_DK
N_GV = GLA_HEADS * GLA_DV
N_DSA = DSA_HEADS * DSA_HEAD_DIM
N_IQ = IDX_HEADS * IDX_DIM
OFF_GQ = 0
OFF_GK = OFF_GQ + N_GQ
OFF_GV = OFF_GK + N_GQ
OFF_GG = OFF_GV + N_GV
OFF_DQ = OFF_GG + N_GV
OFF_DK = OFF_DQ + N_DSA
OFF_IQ = OFF_DK + N_DSA
N_MAIN = OFF_IQ + N_IQ
N_SMALL = 3 * LANES
DSA_BLK = 256
ROPE_LANES = 16
ROPE_PAIR_SHIFT = LANES // 2
TAB_DSA, TAB_IDX = range(2)
COUNT_ROWS = 32
COUNT16_ROWS = 64
BF16_EXACT_INT = 256
VMEM_LIMIT = 56 * 1024 * 1024

NT_DIMS = (((1,), (1,)), ((), ()))
TN_DIMS = (((0,), (0,)), ((), ()))


def _params(*sem):
    return pltpu.CompilerParams(dimension_semantics=sem, vmem_limit_bytes=VMEM_LIMIT)


def _rope_cs(tab, scale, rot):
    lane = lax.broadcasted_iota(I32, (1, LANES), 1)
    lo = lane < ROPE_LANES
    hi = (lane >= ROPE_PAIR_SHIFT) & (lane < ROPE_PAIR_SHIFT + ROPE_LANES)
    swapped = pltpu.roll(tab, ROPE_PAIR_SHIFT, 1)
    c = jnp.where(hi, swapped, tab)
    s = jnp.where(lo, -swapped, jnp.where(hi, tab, 0.0))
    return (1.0 + rot * (c - 1.0)) * scale, s * (rot * scale)


def _rope(x, c, s):
    parts = []
    for g in range(x.shape[1] // LANES):
        xg = x[:, g * LANES:(g + 1) * LANES]
        parts.append(xg * c + pltpu.roll(xg, ROPE_PAIR_SHIFT, 1) * s)
    return parts[0] if len(parts) == 1 else jnp.concatenate(parts, axis=1)


def _inproj_kernel(x_ref, g_ref, w_ref, ws_ref, wt_ref, tab_ref, tabk_ref,
                   main_ref, ikd_ref, misc_ref, dvt_ref, iwt_ref, xn_ref, *, j_dq, j_dk, j_iq):
    j = pl.program_id(1)

    @pl.when(j == 0)
    def _():
        x = x_ref[...]
        ms = jnp.mean(x * x, axis=-1, keepdims=True)
        xn_ref[...] = (x * lax.rsqrt(ms + EPS) * g_ref[...]).astype(BF16)
        small = jnp.dot(xn_ref[...], ws_ref[...], preferred_element_type=F32)
        ikd_ref[...] = _rope(small[:, :2 * LANES], *_rope_cs(tabk_ref[...], 1.0, 1.0)).astype(BF16)
        misc_ref[...] = small[:, 2 * LANES:]
        tr = lax.dot_general(wt_ref[...], xn_ref[...], NT_DIMS, preferred_element_type=F32)
        for ci in range(dvt_ref.shape[0]):
            dvt_ref[ci] = tr[:N_DSA, ci * DSA_BLK:(ci + 1) * DSA_BLK].astype(BF16)
        iwt_ref[...] = tr[N_DSA:, :]

    rot = jnp.where(j >= j_dq, 1.0, 0.0)
    scale = jnp.where(j < j_dq, 1.0,
                      jnp.where(j < j_dk, DSA_HEAD_DIM ** -0.5, jnp.where(j < j_iq, 1.0, IDX_DIM ** -0.5)))
    c, s = _rope_cs(tab_ref[...], scale, rot)
    acc = jnp.dot(xn_ref[...], w_ref[...], preferred_element_type=F32)
    main_ref[...] = _rope(acc, c, s).astype(BF16)


def _inproj(h, gain, w_main, w_small, w_t, tab):
    t, d = h.shape
    tm = min(1024, t)
    tn = 512
    nt = w_t.shape[0]
    j_dq, j_dk, j_iq = OFF_DQ // tn, OFF_DK // tn, OFF_IQ // tn
    kern = functools.partial(_inproj_kernel, j_dq=j_dq, j_dk=j_dk, j_iq=j_iq)
    return pl.pallas_call(
        kern,
        grid=(t // tm, N_MAIN // tn),
        in_specs=[
            pl.BlockSpec((tm, d), lambda i, j: (i, 0)),
            pl.BlockSpec((1, d), lambda i, j: (0, 0)),
            pl.BlockSpec((d, tn), lambda i, j: (0, j)),
            pl.BlockSpec((d, N_SMALL), lambda i, j: (0, 0), pipeline_mode=pl.Buffered(1)),
            pl.BlockSpec((nt, d), lambda i, j: (0, 0), pipeline_mode=pl.Buffered(1)),
            pl.BlockSpec((None, tm, LANES), lambda i, j: ((j >= j_iq).astype(I32), i, 0)),
            pl.BlockSpec((None, tm, LANES), lambda i, j: (TAB_IDX, i, 0)),
        ],
        out_specs=[
            pl.BlockSpec((tm, tn), lambda i, j: (i, j)),
            pl.BlockSpec((tm, 2 * LANES), lambda i, j: (i, 0)),
            pl.BlockSpec((tm, LANES), lambda i, j: (i, 0)),
            pl.BlockSpec((tm // DSA_BLK, N_DSA, DSA_BLK), lambda i, j: (i, 0, 0)),
            pl.BlockSpec((LANES, tm), lambda i, j: (0, i)),
        ],
        out_shape=[
            jax.ShapeDtypeStruct((t, N_MAIN), BF16),
            jax.ShapeDtypeStruct((t, 2 * LANES), BF16),
            jax.ShapeDtypeStruct((t, LANES), F32),
            jax.ShapeDtypeStruct((t // DSA_BLK, N_DSA, DSA_BLK), BF16),
            jax.ShapeDtypeStruct((LANES, t), F32),
        ],
        scratch_shapes=[pltpu.VMEM((tm, d), BF16)],
        compiler_params=_params("parallel", "arbitrary"),
        name="inproj",
    )(h, gain, w_main, w_small, w_t, tab, tab)


def _gla_kernel(q_ref, k_ref, v_ref, gg_ref, misc_ref, w2_ref, ba_ref, gn_ref, o_ref, state_ref, *, blk):
    c_len = GLA_CHUNK
    n_ch = blk // c_len
    shift = c_len.bit_length() - 1
    assert 1 << shift == c_len

    @pl.when(pl.program_id(1) == 0)
    def _():
        state_ref[...] = jnp.zeros_like(state_ref)

    gpre = jnp.dot(misc_ref[...], w2_ref[...], preferred_element_type=F32,
                   precision=lax.Precision.HIGHEST) + ba_ref[...]
    log_a = (jnp.minimum(gpre, 0.0) - jnp.log1p(jnp.exp(-jnp.abs(gpre)))) * (1.0 / GLA_GATE_TAU)

    r = lax.broadcasted_iota(I32, (blk, blk), 0)
    c = lax.broadcasted_iota(I32, (blk, blk), 1)
    causal = (c <= r) & ((r >> shift) == (c >> shift))
    b = jnp.dot(causal.astype(F32), log_a, preferred_element_type=F32, precision=lax.Precision.HIGHEST)
    b3 = b.reshape(n_ch, c_len, N_GQ)
    b_last = b3[:, c_len - 1:c_len, :]
    decay = jnp.exp(b_last)
    k = k_ref[...].astype(F32)
    q_dec = (q_ref[...].astype(F32) * (GLA_DK ** -0.5) * jnp.exp(b)).astype(BF16)
    k_inv = (k * jnp.exp(-b)).astype(BF16)
    k_dec = (k * jnp.exp(b_last - b3).reshape(blk, N_GQ)).astype(BF16)
    gn = gn_ref[...]

    for hd in range(GLA_HEADS):
        ck = slice(hd * GLA_DK, (hd + 1) * GLA_DK)
        cv = slice(hd * GLA_DV, (hd + 1) * GLA_DV)
        attn = lax.dot_general(q_dec[:, ck], k_inv[:, ck], NT_DIMS, preferred_element_type=F32)
        attn = jnp.where(causal, attn, 0.0).astype(BF16)
        o = jnp.dot(attn, v_ref[:, cv], preferred_element_type=F32)
        st = state_ref[hd]
        inter = []
        for ci in range(n_ch):
            rows = slice(ci * c_len, (ci + 1) * c_len)
            inter.append(lax.dot_general(q_dec[rows, ck], st.astype(BF16), NT_DIMS,
                                         preferred_element_type=F32))
            d_st = lax.dot_general(v_ref[rows, cv], k_dec[rows, ck], TN_DIMS, preferred_element_type=F32)
            st = st * decay[ci, :, ck] + d_st
        state_ref[hd] = st
        o = o + jnp.concatenate(inter, axis=0)
        ms = jnp.mean(o * o, axis=-1, keepdims=True)
        y = o * lax.rsqrt(ms + EPS) * gn
        gate = gg_ref[:, cv].astype(F32)
        o_ref[:, cv] = (y * (gate / (1.0 + jnp.exp(-gate)))).astype(BF16)


def _gla(main, misc, w2, ba, gn, batch, seq):
    t = batch * seq
    blk = min(256, seq)
    nb = seq // blk
    kern = functools.partial(_gla_kernel, blk=blk)

    def rowmap(col):
        return lambda b, n: (b * nb + n, col)

    return pl.pallas_call(
        kern,
        grid=(batch, nb),
        in_specs=[
            pl.BlockSpec((blk, N_GQ), rowmap(OFF_GQ // N_GQ)),
            pl.BlockSpec((blk, N_GQ), rowmap(OFF_GK // N_GQ)),
            pl.BlockSpec((blk, N_GV), rowmap(OFF_GV // N_GV)),
            pl.BlockSpec((blk, N_GV), rowmap(OFF_GG // N_GV)),
            pl.BlockSpec((blk, LANES), rowmap(0)),
            pl.BlockSpec((LANES, N_GQ), lambda b, n: (0, 0)),
            pl.BlockSpec((1, N_GQ), lambda b, n: (0, 0)),
            pl.BlockSpec((1, GLA_DV), lambda b, n: (0, 0)),
        ],
        out_specs=pl.BlockSpec((blk, N_GV), rowmap(0)),
        out_shape=jax.ShapeDtypeStruct((t, N_GV), BF16),
        scratch_shapes=[pltpu.VMEM((GLA_HEADS, GLA_DV, GLA_DK), F32)],
        compiler_params=_params("parallel", "arbitrary"),
        name="gla",
    )(main, main, main, main, misc, w2, ba, gn)


def _dsa_kernel(q_ref, iq_ref, iwt_ref, k_ref, vt_ref, ikd_ref, o_ref,
                key_ref, hi_ref, lo_ref, bias_ref, acc_ref, s_ref, p_ref, m_ref, l_ref, alpha_ref,
                *, blk, topk, idx_bits):
    qi = pl.program_id(1)
    nch = qi + 1
    krow = lax.broadcasted_iota(I32, (blk, blk), 0)
    qcol = lax.broadcasted_iota(I32, (blk, blk), 1)
    w_idx = iwt_ref[0:IDX_HEADS, :] * (IDX_HEADS ** -0.5)

    def score_body(c, carry):
        ks = pl.multiple_of(c * blk, blk)
        k_lo = ikd_ref[pl.ds(ks, blk), 0:LANES]
        k_hi = ikd_ref[pl.ds(ks, blk), LANES:2 * LANES]
        acc = jnp.zeros((blk, blk), F32)
        for p in range(IDX_HEADS // 2):
            iq_p = iq_ref[:, p * LANES:(p + 1) * LANES]
            l0 = lax.dot_general(k_lo, iq_p, NT_DIMS, preferred_element_type=F32)
            l1 = lax.dot_general(k_hi, iq_p, NT_DIMS, preferred_element_type=F32)
            acc = acc + w_idx[2 * p:2 * p + 1, :] * jnp.maximum(l0, 0.0)
            acc = acc + w_idx[2 * p + 1:2 * p + 2, :] * jnp.maximum(l1, 0.0)
        score = jnp.where(krow + (c - qi) * blk > qcol, NEG_INF, acc)
        score = jnp.where(score == 0.0, 0.0, score)
        bits = pltpu.bitcast(score, I32)
        key = bits ^ ((bits >> 31) & 0x7FFFFFFF)
        key_ref[c] = key
        hi_ref[c] = (key >> 16).astype(I16)
        return carry

    lax.fori_loop(0, nch, score_body, 0)

    def count16(ref, cand):
        def body(c, acc):
            m = jnp.where(ref[c] >= cand, jnp.ones((), BF16), jnp.zeros((), BF16))
            for i in range(blk // COUNT16_ROWS):
                acc = acc + m[i * COUNT16_ROWS:(i + 1) * COUNT16_ROWS]
            return acc
        acc = lax.fori_loop(0, nch, body, jnp.zeros((COUNT16_ROWS, blk), BF16))
        return jnp.sum(acc.astype(F32), axis=0, keepdims=True)

    def kth_largest16(ref, kth):
        v = jnp.where(count16(ref, jnp.zeros((1, blk), I16)) >= kth, 0, -(2 ** 15)).astype(I32)

        def body(i, v):
            cand = v | lax.shift_left(jnp.int32(1), 14 - i)
            return jnp.where(count16(ref, cand.astype(I16)) >= kth, cand, v)

        return lax.fori_loop(0, 15, body, v)

    def count(pred):
        def body(c, acc):
            m = jnp.where(pred(key_ref[c], c), 1.0, 0.0)
            return acc + jnp.sum(m.reshape(blk // COUNT_ROWS, COUNT_ROWS, blk), axis=0)
        acc = lax.fori_loop(0, nch, body, jnp.zeros((COUNT_ROWS, blk), F32))
        return jnp.sum(acc, axis=0, keepdims=True)

    kf = float(topk)
    thr_hi = kth_largest16(hi_ref, kf)
    n_above = count16(hi_ref, (thr_hi + 1).astype(I16))
    n_above = jnp.where(thr_hi == 2 ** 15 - 1, 0.0, n_above)

    def low_body(c, carry):
        key = key_ref[c]
        low = (key & 0xFFFF) - 2 ** 15
        lo_ref[c] = jnp.where((key >> 16) == thr_hi, low, -(2 ** 15)).astype(I16)
        return carry

    lax.fori_loop(0, nch, low_body, 0)
    thr_lo = kth_largest16(lo_ref, kf - n_above)
    thr = thr_hi * 2 ** 16 + (thr_lo + 2 ** 15)

    n_ge = count(lambda kc, c: kc >= thr)

    def resolve_ties():
        need = kf - count(lambda kc, c: kc > thr)

        def tie_body(i, last):
            cand = last | lax.shift_left(jnp.int32(1), idx_bits - 1 - i)
            below = count(lambda kc, c: (kc == thr) & (c * blk + krow < cand))
            return jnp.where(below < need, cand, last)

        return lax.fori_loop(0, idx_bits, tie_body, jnp.zeros((1, blk), I32))

    last = lax.cond(jnp.max(n_ge) > kf, resolve_ties,
                    lambda: jnp.full((1, blk), 2 ** idx_bits, I32))

    def write_bias(c, diagonal):
        kc = key_ref[c]
        tie_pos = jnp.where(kc == thr, c * blk + krow, -1)
        bias = jnp.where(kc >= thr, jnp.where(tie_pos > last, NEG_INF, 0.0), NEG_INF)
        if diagonal:
            bias = jnp.where(krow > qcol, NEG_INF, bias)
        bias_ref[c] = bias

    def bias_body(c, carry):
        write_bias(c, False)
        return carry

    lax.fori_loop(0, qi, bias_body, 0)
    write_bias(qi, True)

    heads = tuple((hd, slice(hd * DSA_HEAD_DIM, (hd + 1) * DSA_HEAD_DIM)) for hd in range(DSA_HEADS))

    def fold(x, op):
        return op(x.reshape(blk // 8, 8, blk), axis=0)

    m_ref[...] = jnp.full_like(m_ref, NEG_INF)
    l_ref[...] = jnp.zeros_like(l_ref)
    acc_ref[...] = jnp.zeros_like(acc_ref)

    def attn_body(c, carry):
        ks = pl.multiple_of(c * blk, blk)

        @pl.when(c >= 0)
        def _():
            for hd, hs in heads:
                s = lax.dot_general(k_ref[pl.ds(ks, blk), hs], q_ref[:, hs], NT_DIMS,
                                    preferred_element_type=F32) + bias_ref[c]
                s_ref[hd] = s
                m_prev = m_ref[hd, 0:1, :]
                m_new = jnp.maximum(m_prev, jnp.max(fold(s, jnp.max), axis=0, keepdims=True))
                alpha = jnp.exp(m_prev - m_new)
                m_ref[hd, 0:1, :] = m_new
                alpha_ref[hd, 0:1, :] = alpha
                p = jnp.exp(s_ref[hd] - m_new)
                l_ref[hd] = alpha * l_ref[hd] + fold(p, jnp.sum)
                p_ref[hd] = p.astype(BF16)

        @pl.when(c >= 0)
        def _():
            for hd, hs in heads:
                pv = jnp.dot(vt_ref[c, hs, :], p_ref[hd], preferred_element_type=F32)
                acc_ref[hs, :] = alpha_ref[hd, 0:1, :] * acc_ref[hs, :] + pv

        return carry

    lax.fori_loop(0, nch, attn_body, 0)

    for hd, hs in heads:
        o_t = acc_ref[hs, :] / jnp.sum(l_ref[hd], axis=0, keepdims=True)
        o_ref[:, hs] = o_t.T.astype(BF16)


def _dsa(main, ikd, iwt, dvt, batch, seq):
    t = batch * seq
    blk = DSA_BLK
    nq = seq // blk
    topk = min(DSA_TOPK_MAX, seq // 4)
    idx_bits = max(1, (seq - 1).bit_length())
    assert (blk // COUNT16_ROWS) * nq <= BF16_EXACT_INT
    kern = functools.partial(_dsa_kernel, blk=blk, topk=topk, idx_bits=idx_bits)

    def qmap(col):
        return lambda b, i: (b * nq + i, col)

    return pl.pallas_call(
        kern,
        grid=(batch, nq),
        in_specs=[
            pl.BlockSpec((blk, N_DSA), qmap(OFF_DQ // N_DSA)),
            pl.BlockSpec((blk, N_IQ), qmap(OFF_IQ // N_IQ)),
            pl.BlockSpec((LANES, blk), lambda b, i: (0, b * nq + i)),
            pl.BlockSpec((seq, N_DSA), lambda b, i: (b, OFF_DK // N_DSA)),
            pl.BlockSpec((nq, N_DSA, blk), lambda b, i: (b, 0, 0)),
            pl.BlockSpec((seq, 2 * LANES), lambda b, i: (b, 0)),
        ],
        out_specs=pl.BlockSpec((blk, N_DSA), qmap(0)),
        out_shape=jax.ShapeDtypeStruct((t, N_DSA), BF16),
        scratch_shapes=[
            pltpu.VMEM((nq, blk, blk), I32),
            pltpu.VMEM((nq, blk, blk), I16),
            pltpu.VMEM((nq, blk, blk), I16),
            pltpu.VMEM((nq, blk, blk), F32),
            pltpu.VMEM((N_DSA, blk), F32),
            pltpu.VMEM((DSA_HEADS, blk, blk), F32),
            pltpu.VMEM((DSA_HEADS, blk, blk), BF16),
            pltpu.VMEM((DSA_HEADS, 8, blk), F32),
            pltpu.VMEM((DSA_HEADS, 8, blk), F32),
            pltpu.VMEM((DSA_HEADS, 8, blk), F32),
        ],
        compiler_params=_params("parallel", "arbitrary"),
        name="dsa",
    )(main, main, iwt, main, dvt, ikd)


def _outproj_kernel(og_ref, od_ref, h_ref, w_ref, g_ref, o_ref):
    m = jnp.dot(og_ref[...], w_ref[:N_GV, :], preferred_element_type=F32)
    m = m + jnp.dot(od_ref[...], w_ref[N_GV:, :], preferred_element_type=F32)
    ms = jnp.mean(m * m, axis=-1, keepdims=True)
    o_ref[...] = h_ref[...] + m * lax.rsqrt(ms + EPS) * g_ref[...]


def _outproj(o_gla, o_dsa, h, w_out, gain):
    t, d = h.shape
    tm = min(512, t)
    return pl.pallas_call(
        _outproj_kernel,
        grid=(t // tm,),
        in_specs=[
            pl.BlockSpec((tm, N_GV), lambda i: (i, 0)),
            pl.BlockSpec((tm, N_DSA), lambda i: (i, 0)),
            pl.BlockSpec((tm, d), lambda i: (i, 0)),
            pl.BlockSpec((N_GV + N_DSA, d), lambda i: (0, 0)),
            pl.BlockSpec((1, d), lambda i: (0, 0)),
        ],
        out_specs=pl.BlockSpec((tm, d), lambda i: (i, 0)),
        out_shape=jax.ShapeDtypeStruct((t, d), F32),
        compiler_params=_params("parallel"),
        name="outproj",
    )(o_gla, o_dsa, h, w_out, gain)


def _ffn_kernel(h_ref, gpre_ref, wu_ref, wd_ref, gpost_ref, o_ref, xn_ref, acc_ref):
    f = pl.program_id(1)

    @pl.when(f == 0)
    def _():
        x = h_ref[...]
        ms = jnp.mean(x * x, axis=-1, keepdims=True)
        xn_ref[...] = (x * lax.rsqrt(ms + EPS) * gpre_ref[...]).astype(BF16)
        acc_ref[...] = jnp.zeros_like(acc_ref)

    u = jnp.maximum(jnp.dot(xn_ref[...], wu_ref[...], preferred_element_type=F32), 0.0)
    acc_ref[...] += jnp.dot((u * u).astype(BF16), wd_ref[...], preferred_element_type=F32)

    @pl.when(f == pl.num_programs(1) - 1)
    def _():
        y = acc_ref[...]
        ms = jnp.mean(y * y, axis=-1, keepdims=True)
        o_ref[...] = h_ref[...] + y * lax.rsqrt(ms + EPS) * gpost_ref[...]


def _ffn(h, g_pre, w_up, w_down, g_post):
    t, d = h.shape
    d_ff = w_up.shape[1]
    tm = min(512, t)
    tf = 512
    return pl.pallas_call(
        _ffn_kernel,
        grid=(t // tm, d_ff // tf),
        in_specs=[
            pl.BlockSpec((tm, d), lambda i, f: (i, 0)),
            pl.BlockSpec((1, d), lambda i, f: (0, 0)),
            pl.BlockSpec((d, tf), lambda i, f: (0, f)),
            pl.BlockSpec((tf, d), lambda i, f: (f, 0)),
            pl.BlockSpec((1, d), lambda i, f: (0, 0)),
        ],
        out_specs=pl.BlockSpec((tm, d), lambda i, f: (i, 0)),
        out_shape=jax.ShapeDtypeStruct((t, d), F32),
        scratch_shapes=[pltpu.VMEM((tm, d), BF16), pltpu.VMEM((tm, d), F32)],
        compiler_params=_params("parallel", "arbitrary"),
        name="ffn",
    )(h, g_pre, w_up, w_down, g_post)


def _rope_table(positions, head_dim):
    r = head_dim // ROPE_FRACTION
    n_freq = r // 2
    inv_freq = ROPE_THETA ** (-(jnp.arange(0, r, 2, dtype=F32) / r))
    ang = positions.reshape(-1).astype(F32)[:, None] * inv_freq
    rep = ROPE_LANES // n_freq
    cos = jnp.concatenate([jnp.cos(ang)] * rep, axis=1)
    sin = jnp.concatenate([jnp.sin(ang)] * rep, axis=1)
    one = jnp.ones((cos.shape[0], ROPE_PAIR_SHIFT - ROPE_LANES), F32)
    return jnp.concatenate([cos, one, sin, one], axis=1)


def _rope_tables(positions):
    return jnp.stack([_rope_table(positions, DSA_HEAD_DIM), _rope_table(positions, IDX_DIM)])


def _lane_permutation(sources):
    p = np.zeros((LANES, LANES), np.float32)
    p[np.asarray(sources), np.arange(LANES)] = 1.0
    return jnp.asarray(p, BF16)


def _dsa_lane_sources():
    half = DSA_HEAD_DIM // ROPE_FRACTION // 2
    assert half == ROPE_LANES
    cut = 2 * half + ROPE_PAIR_SHIFT - half
    return (list(range(half)) + list(range(2 * half, cut)) + list(range(half, 2 * half))
            + list(range(cut, DSA_HEAD_DIM)))


def _idx_lane_sources():
    half = IDX_DIM // ROPE_FRACTION // 2
    assert 2 * half == ROPE_LANES

    def a(lo, hi):
        return list(range(lo, hi))

    def b(lo, hi):
        return list(range(IDX_DIM + lo, IDX_DIM + hi))

    return (a(0, half) + b(0, half) + a(2 * half, IDX_DIM)
            + a(half, 2 * half) + b(half, 2 * half) + b(2 * half, IDX_DIM))


def _permute_lane_tiles(w, perm):
    d, n = w.shape
    out = jnp.dot(w.astype(BF16).reshape(d * (n // LANES), LANES), perm, preferred_element_type=BF16)
    return out.reshape(d, n)


def _pair_layout_ik(ik):
    d = ik.shape[0]
    half = IDX_DIM // ROPE_FRACTION // 2
    x1, x2, rest = ik[:, :half], ik[:, half:2 * half], ik[:, 2 * half:]

    def z(n):
        return jnp.zeros((d, n), ik.dtype)

    n_rest = IDX_DIM - 2 * half
    k_a = jnp.concatenate([x1, z(half), rest, x2, z(half), z(n_rest)], axis=1)
    k_b = jnp.concatenate([z(half), x1, z(n_rest), z(half), x2, rest], axis=1)
    return k_a, k_b


def _split_w_in(w_in):
    d = w_in.shape[0]
    o_gr = 2 * N_GQ + 2 * N_GV
    o_dq = o_gr + GLA_GATE_RANK
    o_dv = o_dq + 2 * N_DSA
    o_iq = o_dv + N_DSA
    o_ik = o_iq + N_IQ
    o_iw = o_ik + IDX_DIM
    w_main = jnp.concatenate([
        w_in[:, :o_gr].astype(BF16),
        _permute_lane_tiles(w_in[:, o_dq:o_dv], _lane_permutation(_dsa_lane_sources())),
        _permute_lane_tiles(w_in[:, o_iq:o_ik], _lane_permutation(_idx_lane_sources()))], axis=1)
    k_a, k_b = _pair_layout_ik(w_in[:, o_ik:o_iw])
    iw = w_in[:, o_iw:o_iw + IDX_HEADS]
    gr = w_in[:, o_gr:o_dq]
    pad = jnp.zeros((d, LANES - IDX_HEADS - GLA_GATE_RANK), w_in.dtype)
    w_small = jnp.concatenate([k_a, k_b, iw, gr, pad], axis=1).astype(BF16)
    w_t = jnp.concatenate([w_in[:, o_dv:o_iq], iw, jnp.zeros((d, LANES - IDX_HEADS), w_in.dtype)],
                          axis=1).astype(BF16).T
    return w_main, w_small, w_t


def kernel(x, positions, norm_mix_pre, w_in, gla_wa2, gla_ba, gla_norm, w_out, norm_mix_post,
           norm_ffn_pre, w_up, w_down, norm_ffn_post):
    batch, seq, d = x.shape
    depth = w_in.shape[0]
    assert seq % DSA_BLK == 0
    tab = _rope_tables(positions)
    w2 = jnp.pad(gla_wa2, ((0, 0), (IDX_HEADS, LANES - IDX_HEADS - GLA_GATE_RANK), (0, 0)))
    h = x.reshape(batch * seq, d)
    for l in range(depth):
        w_main, w_small, w_t = _split_w_in(w_in[l])
        main, ikd, misc, dvt, iwt = _inproj(h, norm_mix_pre[l][None, :], w_main, w_small, w_t, tab)
        o_gla = _gla(main, misc, w2[l], gla_ba[l][None, :], gla_norm[l][None, :], batch, seq)
        o_dsa = _dsa(main, ikd, iwt, dvt, batch, seq)
        h = _outproj(o_gla, o_dsa, h, w_out[l].astype(BF16), norm_mix_post[l][None, :])
        h = _ffn(h, norm_ffn_pre[l][None, :], w_up[l].astype(BF16), w_down[l].astype(BF16),
                 norm_ffn_post[l][None, :])
    return h.reshape(batch, seq, d)
```

```python
import functools

import jax
import jax.numpy as jnp
import numpy as np
from jax import lax
from jax.experimental import pallas as pl
from jax.experimental.pallas import tpu as pltpu

F32 = jnp.float32
BF16 = jnp.bfloat16
I32 = jnp.int32
I16 = jnp.int16

GLA_HEADS = 4
GLA_DK = 128
GLA_DV = 256
GLA_GATE_RANK = 16
GLA_GATE_TAU = 16.0
GLA_CHUNK = 64
DSA_HEADS = 8
DSA_HEAD_DIM = 128
IDX_HEADS = 16
IDX_DIM = 64
DSA_TOPK_MAX = 256
ROPE_THETA = 500000.0
ROPE_FRACTION = 4
EPS = 1e-6
NEG_INF = -1e30
INT_MIN = -(2 ** 31)

LANES = 128
N_GQ = GLA_HEADS * GLA_DK
N_GV = GLA_HEADS * GLA_DV
N_DSA = DSA_HEADS * DSA_HEAD_DIM
N_IQ = IDX_HEADS * IDX_DIM
OFF_GQ = 0
OFF_GK = OFF_GQ + N_GQ
OFF_GV = OFF_GK + N_GQ
OFF_GG = OFF_GV + N_GV
OFF_DQ = OFF_GG + N_GV
OFF_DK = OFF_DQ + N_DSA
OFF_IQ = OFF_DK + N_DSA
N_MAIN = OFF_IQ + N_IQ
N_SMALL = 3 * LANES
DSA_BLK = 256
ROPE_LANES = 16
ROPE_PAIR_SHIFT = LANES // 2
TAB_DSA, TAB_IDX = range(2)
COUNT_ROWS = 32
COUNT16_ROWS = 64
BF16_EXACT_INT = 256
VMEM_LIMIT = 56 * 1024 * 1024

NT_DIMS = (((1,), (1,)), ((), ()))
TN_DIMS = (((0,), (0,)), ((), ()))


def _params(*sem):
    return pltpu.CompilerParams(dimension_semantics=sem, vmem_limit_bytes=VMEM_LIMIT)


def _rope_cs(tab, scale, rot):
    lane = lax.broadcasted_iota(I32, (1, LANES), 1)
    lo = lane < ROPE_LANES
    hi = (lane >= ROPE_PAIR_SHIFT) & (lane < ROPE_PAIR_SHIFT + ROPE_LANES)
    swapped = pltpu.roll(tab, ROPE_PAIR_SHIFT, 1)
    c = jnp.where(hi, swapped, tab)
    s = jnp.where(lo, -swapped, jnp.where(hi, tab, 0.0))
    return (1.0 + rot * (c - 1.0)) * scale, s * (rot * scale)


def _rope(x, c, s):
    parts = []
    for g in range(x.shape[1] // LANES):
        xg = x[:, g * LANES:(g + 1) * LANES]
        parts.append(xg * c + pltpu.roll(xg, ROPE_PAIR_SHIFT, 1) * s)
    return parts[0] if len(parts) == 1 else jnp.concatenate(parts, axis=1)


def _inproj_kernel(x_ref, g_ref, w_ref, ws_ref, wt_ref, tab_ref, tabk_ref,
                   main_ref, ikd_ref, misc_ref, dvt_ref, iwt_ref, xn_ref, *, j_dq, j_dk, j_iq):
    j = pl.program_id(1)

    @pl.when(j == 0)
    def _():
        x = x_ref[...]
        ms = jnp.mean(x * x, axis=-1, keepdims=True)
        xn_ref[...] = (x * lax.rsqrt(ms + EPS) * g_ref[...]).astype(BF16)
        small = jnp.dot(xn_ref[...], ws_ref[...], preferred_element_type=F32)
        ikd_ref[...] = _rope(small[:, :2 * LANES], *_rope_cs(tabk_ref[...], 1.0, 1.0)).astype(BF16)
        misc_ref[...] = small[:, 2 * LANES:]
        fm = jnp.dot(xn_ref[...], wt_ref[...], preferred_element_type=F32)
        for ci in range(dvt_ref.shape[0]):
            dvt_ref[ci] = fm[ci * DSA_BLK:(ci + 1) * DSA_BLK, :N_DSA].T.astype(BF16)
        iwt_ref[...] = fm[:, N_DSA:].T

    rot = jnp.where(j >= j_dq, 1.0, 0.0)
    scale = jnp.where(j < j_dq, 1.0,
                      jnp.where(j < j_dk, DSA_HEAD_DIM ** -0.5, jnp.where(j < j_iq, 1.0, IDX_DIM ** -0.5)))
    c, s = _rope_cs(tab_ref[...], scale, rot)
    acc = jnp.dot(xn_ref[...], w_ref[...], preferred_element_type=F32)
    main_ref[...] = _rope(acc, c, s).astype(BF16)


def _inproj(h, gain, w_main, w_small, w_t, tab):
    t, d = h.shape
    tm = min(1024, t)
    tn = 512
    nt = w_t.shape[1]
    j_dq, j_dk, j_iq = OFF_DQ // tn, OFF_DK // tn, OFF_IQ // tn
    kern = functools.partial(_inproj_kernel, j_dq=j_dq, j_dk=j_dk, j_iq=j_iq)
    return pl.pallas_call(
        kern,
        grid=(t // tm, N_MAIN // tn),
        in_specs=[
            pl.BlockSpec((tm, d), lambda i, j: (i, 0)),
            pl.BlockSpec((1, d), lambda i, j: (0, 0)),
            pl.BlockSpec((d, tn), lambda i, j: (0, j)),
            pl.BlockSpec((d, N_SMALL), lambda i, j: (0, 0), pipeline_mode=pl.Buffered(1)),
            pl.BlockSpec((d, nt), lambda i, j: (0, 0), pipeline_mode=pl.Buffered(1)),
            pl.BlockSpec((None, tm, LANES), lambda i, j: ((j >= j_iq).astype(I32), i, 0)),
            pl.BlockSpec((None, tm, LANES), lambda i, j: (TAB_IDX, i, 0)),
        ],
        out_specs=[
            pl.BlockSpec((tm, tn), lambda i, j: (i, j)),
            pl.BlockSpec((tm, 2 * LANES), lambda i, j: (i, 0)),
            pl.BlockSpec((tm, LANES), lambda i, j: (i, 0)),
            pl.BlockSpec((tm // DSA_BLK, N_DSA, DSA_BLK), lambda i, j: (i, 0, 0)),
            pl.BlockSpec((LANES, tm), lambda i, j: (0, i)),
        ],
        out_shape=[
            jax.ShapeDtypeStruct((t, N_MAIN), BF16),
            jax.ShapeDtypeStruct((t, 2 * LANES), BF16),
            jax.ShapeDtypeStruct((t, LANES), F32),
            jax.ShapeDtypeStruct((t // DSA_BLK, N_DSA, DSA_BLK), BF16),
            jax.ShapeDtypeStruct((LANES, t), F32),
        ],
        scratch_shapes=[pltpu.VMEM((tm, d), BF16)],
        compiler_params=_params("parallel", "arbitrary"),
        name="inproj",
    )(h, gain, w_main, w_small, w_t, tab, tab)


def _gla_kernel(q_ref, k_ref, v_ref, gg_ref, misc_ref, w2_ref, ba_ref, gn_ref, o_ref, state_ref, *, blk):
    c_len = GLA_CHUNK
    n_ch = blk // c_len
    shift = c_len.bit_length() - 1
    assert 1 << shift == c_len

    @pl.when(pl.program_id(1) == 0)
    def _():
        state_ref[...] = jnp.zeros_like(state_ref)

    gpre = jnp.dot(misc_ref[...], w2_ref[...], preferred_element_type=F32,
                   precision=lax.Precision.HIGHEST) + ba_ref[...]
    log_a = (jnp.minimum(gpre, 0.0) - jnp.log1p(jnp.exp(-jnp.abs(gpre)))) * (1.0 / GLA_GATE_TAU)

    r = lax.broadcasted_iota(I32, (blk, blk), 0)
    c = lax.broadcasted_iota(I32, (blk, blk), 1)
    causal = (c <= r) & ((r >> shift) == (c >> shift))
    b = jnp.dot(causal.astype(F32), log_a, preferred_element_type=F32, precision=lax.Precision.HIGHEST)
    b3 = b.reshape(n_ch, c_len, N_GQ)
    b_last = b3[:, c_len - 1:c_len, :]
    decay = jnp.exp(b_last)
    k = k_ref[...].astype(F32)
    q_dec = (q_ref[...].astype(F32) * (GLA_DK ** -0.5) * jnp.exp(b)).astype(BF16)
    k_inv = (k * jnp.exp(-b)).astype(BF16)
    k_dec = (k * jnp.exp(b_last - b3).reshape(blk, N_GQ)).astype(BF16)
    gn = gn_ref[...]

    for hd in range(GLA_HEADS):
        ck = slice(hd * GLA_DK, (hd + 1) * GLA_DK)
        cv = slice(hd * GLA_DV, (hd + 1) * GLA_DV)
        attn = lax.dot_general(q_dec[:, ck], k_inv[:, ck], NT_DIMS, preferred_element_type=F32)
        attn = jnp.where(causal, attn, 0.0).astype(BF16)
        o = jnp.dot(attn, v_ref[:, cv], preferred_element_type=F32)
        st = state_ref[hd]
        inter = []
        for ci in range(n_ch):
            rows = slice(ci * c_len, (ci + 1) * c_len)
            inter.append(lax.dot_general(q_dec[rows, ck], st.astype(BF16), NT_DIMS,
                                         preferred_element_type=F32))
            d_st = lax.dot_general(v_ref[rows, cv], k_dec[rows, ck], TN_DIMS, preferred_element_type=F32)
            st = st * decay[ci, :, ck] + d_st
        state_ref[hd] = st
        o = o + jnp.concatenate(inter, axis=0)
        ms = jnp.mean(o * o, axis=-1, keepdims=True)
        y = o * lax.rsqrt(ms + EPS) * gn
        gate = gg_ref[:, cv].astype(F32)
        o_ref[:, cv] = (y * (gate / (1.0 + jnp.exp(-gate)))).astype(BF16)


def _gla(main, misc, w2, ba, gn, batch, seq):
    t = batch * seq
    blk = min(256, seq)
    nb = seq // blk
    kern = functools.partial(_gla_kernel, blk=blk)

    def rowmap(col):
        return lambda b, n: (b * nb + n, col)

    return pl.pallas_call(
        kern,
        grid=(batch, nb),
        in_specs=[
            pl.BlockSpec((blk, N_GQ), rowmap(OFF_GQ // N_GQ)),
            pl.BlockSpec((blk, N_GQ), rowmap(OFF_GK // N_GQ)),
            pl.BlockSpec((blk, N_GV), rowmap(OFF_GV // N_GV)),
            pl.BlockSpec((blk, N_GV), rowmap(OFF_GG // N_GV)),
            pl.BlockSpec((blk, LANES), rowmap(0)),
            pl.BlockSpec((LANES, N_GQ), lambda b, n: (0, 0)),
            pl.BlockSpec((1, N_GQ), lambda b, n: (0, 0)),
            pl.BlockSpec((1, GLA_DV), lambda b, n: (0, 0)),
        ],
        out_specs=pl.BlockSpec((blk, N_GV), rowmap(0)),
        out_shape=jax.ShapeDtypeStruct((t, N_GV), BF16),
        scratch_shapes=[pltpu.VMEM((GLA_HEADS, GLA_DV, GLA_DK), F32)],
        compiler_params=_params("parallel", "arbitrary"),
        name="gla",
    )(main, main, main, main, misc, w2, ba, gn)


def _dsa_kernel(q_ref, iq_ref, iwt_ref, k_ref, vt_ref, ikd_ref, o_ref,
                key_ref, hi_ref, lo_ref, bias_ref, acc_ref, s_ref, p_ref, m_ref, l_ref, alpha_ref,
                *, blk, topk, idx_bits):
    qi = pl.program_id(1)
    nch = qi + 1
    krow = lax.broadcasted_iota(I32, (blk, blk), 0)
    qcol = lax.broadcasted_iota(I32, (blk, blk), 1)
    w_idx = iwt_ref[0:IDX_HEADS, :] * (IDX_HEADS ** -0.5)

    def score_body(c, carry):
        ks = pl.multiple_of(c * blk, blk)
        k_lo = ikd_ref[pl.ds(ks, blk), 0:LANES]
        k_hi = ikd_ref[pl.ds(ks, blk), LANES:2 * LANES]
        acc = jnp.zeros((blk, blk), F32)
        for p in range(IDX_HEADS // 2):
            iq_p = iq_ref[:, p * LANES:(p + 1) * LANES]
            l0 = lax.dot_general(k_lo, iq_p, NT_DIMS, preferred_element_type=F32)
            l1 = lax.dot_general(k_hi, iq_p, NT_DIMS, preferred_element_type=F32)
            acc = acc + w_idx[2 * p:2 * p + 1, :] * jnp.maximum(l0, 0.0)
            acc = acc + w_idx[2 * p + 1:2 * p + 2, :] * jnp.maximum(l1, 0.0)
        score = jnp.where(krow + (c - qi) * blk > qcol, NEG_INF, acc)
        score = jnp.where(score == 0.0, 0.0, score)
        bits = pltpu.bitcast(score, I32)
        key = bits ^ ((bits >> 31) & 0x7FFFFFFF)
        key_ref[c] = key
        hi_ref[c] = (key >> 16).astype(I16)
        return carry

    lax.fori_loop(0, nch, score_body, 0)

    def count16(ref, cand):
        def body(c, acc):
            m = jnp.where(ref[c] >= cand, jnp.ones((), BF16), jnp.zeros((), BF16))
            for i in range(blk // COUNT16_ROWS):
                acc = acc + m[i * COUNT16_ROWS:(i + 1) * COUNT16_ROWS]
            return acc
        acc = lax.fori_loop(0, nch, body, jnp.zeros((COUNT16_ROWS, blk), BF16))
        return jnp.sum(acc.astype(F32), axis=0, keepdims=True)

    def kth_largest16(ref, kth):
        v = jnp.where(count16(ref, jnp.zeros((1, blk), I16)) >= kth, 0, -(2 ** 15)).astype(I32)

        def body(i, v):
            cand = v | lax.shift_left(jnp.int32(1), 14 - i)
            return jnp.where(count16(ref, cand.astype(I16)) >= kth, cand, v)

        return lax.fori_loop(0, 15, body, v)

    def count(pred):
        def body(c, acc):
            m = jnp.where(pred(key_ref[c], c), 1.0, 0.0)
            return acc + jnp.sum(m.reshape(blk // COUNT_ROWS, COUNT_ROWS, blk), axis=0)
        acc = lax.fori_loop(0, nch, body, jnp.zeros((COUNT_ROWS, blk), F32))
        return jnp.sum(acc, axis=0, keepdims=True)

    kf = float(topk)
    thr_hi = kth_largest16(hi_ref, kf)
    n_above = count16(hi_ref, (thr_hi + 1).astype(I16))
    n_above = jnp.where(thr_hi == 2 ** 15 - 1, 0.0, n_above)

    def low_body(c, carry):
        key = key_ref[c]
        low = (key & 0xFFFF) - 2 ** 15
        lo_ref[c] = jnp.where((key >> 16) == thr_hi, low, -(2 ** 15)).astype(I16)
        return carry

    lax.fori_loop(0, nch, low_body, 0)
    thr_lo = kth_largest16(lo_ref, kf - n_above)
    thr = thr_hi * 2 ** 16 + (thr_lo + 2 ** 15)

    n_ge = count(lambda kc, c: kc >= thr)

    def resolve_ties():
        need = kf - count(lambda kc, c: kc > thr)

        def tie_body(i, last):
            cand = last | lax.shift_left(jnp.int32(1), idx_bits - 1 - i)
            below = count(lambda kc, c: (kc == thr) & (c * blk + krow < cand))
            return jnp.where(below < need, cand, last)

        return lax.fori_loop(0, idx_bits, tie_body, jnp.zeros((1, blk), I32))

    last = lax.cond(jnp.max(n_ge) > kf, resolve_ties,
                    lambda: jnp.full((1, blk), 2 ** idx_bits, I32))

    def write_bias(c, diagonal):
        kc = key_ref[c]
        tie_pos = jnp.where(kc == thr, c * blk + krow, -1)
        bias = jnp.where(kc >= thr, jnp.where(tie_pos > last, NEG_INF, 0.0), NEG_INF)
        if diagonal:
            bias = jnp.where(krow > qcol, NEG_INF, bias)
        bias_ref[c] = bias

    def bias_body(c, carry):
        write_bias(c, False)
        return carry

    lax.fori_loop(0, qi, bias_body, 0)
    write_bias(qi, True)

    heads = tuple((hd, slice(hd * DSA_HEAD_DIM, (hd + 1) * DSA_HEAD_DIM)) for hd in range(DSA_HEADS))

    def fold(x, op):
        return op(x.reshape(blk // 8, 8, blk), axis=0)

    m_ref[...] = jnp.full_like(m_ref, NEG_INF)
    l_ref[...] = jnp.zeros_like(l_ref)
    acc_ref[...] = jnp.zeros_like(acc_ref)

    def attn_body(c, carry):
        ks = pl.multiple_of(c * blk, blk)

        @pl.when(c >= 0)
        def _():
            for hd, hs in heads:
                s = lax.dot_general(k_ref[pl.ds(ks, blk), hs], q_ref[:, hs], NT_DIMS,
                                    preferred_element_type=F32) + bias_ref[c]
                s_ref[hd] = s
                m_prev = m_ref[hd, 0:1, :]
                m_new = jnp.maximum(m_prev, jnp.max(fold(s, jnp.max), axis=0, keepdims=True))
                alpha = jnp.exp(m_prev - m_new)
                m_ref[hd, 0:1, :] = m_new
                alpha_ref[hd, 0:1, :] = alpha
                p = jnp.exp(s_ref[hd] - m_new)
                l_ref[hd] = alpha * l_ref[hd] + fold(p, jnp.sum)
                p_ref[hd] = p.astype(BF16)

        @pl.when(c >= 0)
        def _():
            for hd, hs in heads:
                pv = jnp.dot(vt_ref[c, hs, :], p_ref[hd], preferred_element_type=F32)
                acc_ref[hs, :] = alpha_ref[hd, 0:1, :] * acc_ref[hs, :] + pv

        return carry

    lax.fori_loop(0, nch, attn_body, 0)

    for hd, hs in heads:
        o_t = acc_ref[hs, :] / jnp.sum(l_ref[hd], axis=0, keepdims=True)
        o_ref[:, hs] = o_t.T.astype(BF16)


def _dsa(main, ikd, iwt, dvt, batch, seq):
    t = batch * seq
    blk = DSA_BLK
    nq = seq // blk
    topk = min(DSA_TOPK_MAX, seq // 4)
    idx_bits = max(1, (seq - 1).bit_length())
    assert (blk // COUNT16_ROWS) * nq <= BF16_EXACT_INT
    kern = functools.partial(_dsa_kernel, blk=blk, topk=topk, idx_bits=idx_bits)

    def qmap(col):
        return lambda b, i: (b * nq + i, col)

    return pl.pallas_call(
        kern,
        grid=(batch, nq),
        in_specs=[
            pl.BlockSpec((blk, N_DSA), qmap(OFF_DQ // N_DSA)),
            pl.BlockSpec((blk, N_IQ), qmap(OFF_IQ // N_IQ)),
            pl.BlockSpec((LANES, blk), lambda b, i: (0, b * nq + i)),
            pl.BlockSpec((seq, N_DSA), lambda b, i: (b, OFF_DK // N_DSA)),
            pl.BlockSpec((nq, N_DSA, blk), lambda b, i: (b, 0, 0)),
            pl.BlockSpec((seq, 2 * LANES), lambda b, i: (b, 0)),
        ],
        out_specs=pl.BlockSpec((blk, N_DSA), qmap(0)),
        out_shape=jax.ShapeDtypeStruct((t, N_DSA), BF16),
        scratch_shapes=[
            pltpu.VMEM((nq, blk, blk), I32),
            pltpu.VMEM((nq, blk, blk), I16),
            pltpu.VMEM((nq, blk, blk), I16),
            pltpu.VMEM((nq, blk, blk), F32),
            pltpu.VMEM((N_DSA, blk), F32),
            pltpu.VMEM((DSA_HEADS, blk, blk), F32),
            pltpu.VMEM((DSA_HEADS, blk, blk), BF16),
            pltpu.VMEM((DSA_HEADS, 8, blk), F32),
            pltpu.VMEM((DSA_HEADS, 8, blk), F32),
            pltpu.VMEM((DSA_HEADS, 8, blk), F32),
        ],
        compiler_params=_params("parallel", "arbitrary"),
        name="dsa",
    )(main, main, iwt, main, dvt, ikd)


def _outproj_kernel(og_ref, od_ref, h_ref, w_ref, g_ref, o_ref):
    m = jnp.dot(og_ref[...], w_ref[:N_GV, :], preferred_element_type=F32)
    m = m + jnp.dot(od_ref[...], w_ref[N_GV:, :], preferred_element_type=F32)
    ms = jnp.mean(m * m, axis=-1, keepdims=True)
    o_ref[...] = h_ref[...] + m * lax.rsqrt(ms + EPS) * g_ref[...]


def _outproj(o_gla, o_dsa, h, w_out, gain):
    t, d = h.shape
    tm = min(512, t)
    return pl.pallas_call(
        _outproj_kernel,
        grid=(t // tm,),
        in_specs=[
            pl.BlockSpec((tm, N_GV), lambda i: (i, 0)),
            pl.BlockSpec((tm, N_DSA), lambda i: (i, 0)),
            pl.BlockSpec((tm, d), lambda i: (i, 0)),
            pl.BlockSpec((N_GV + N_DSA, d), lambda i: (0, 0)),
            pl.BlockSpec((1, d), lambda i: (0, 0)),
        ],
        out_specs=pl.BlockSpec((tm, d), lambda i: (i, 0)),
        out_shape=jax.ShapeDtypeStruct((t, d), F32),
        compiler_params=_params("parallel"),
        name="outproj",
    )(o_gla, o_dsa, h, w_out, gain)


def _ffn_kernel(h_ref, gpre_ref, wu_ref, wd_ref, gpost_ref, o_ref, xn_ref, acc_ref):
    f = pl.program_id(1)

    @pl.when(f == 0)
    def _():
        x = h_ref[...]
        ms = jnp.mean(x * x, axis=-1, keepdims=True)
        xn_ref[...] = (x * lax.rsqrt(ms + EPS) * gpre_ref[...]).astype(BF16)
        acc_ref[...] = jnp.zeros_like(acc_ref)

    u = jnp.maximum(jnp.dot(xn_ref[...], wu_ref[...], preferred_element_type=F32), 0.0)
    acc_ref[...] += jnp.dot((u * u).astype(BF16), wd_ref[...], preferred_element_type=F32)

    @pl.when(f == pl.num_programs(1) - 1)
    def _():
        y = acc_ref[...]
        ms = jnp.mean(y * y, axis=-1, keepdims=True)
        o_ref[...] = h_ref[...] + y * lax.rsqrt(ms + EPS) * gpost_ref[...]


def _ffn(h, g_pre, w_up, w_down, g_post):
    t, d = h.shape
    d_ff = w_up.shape[1]
    tm = min(512, t)
    tf = 512
    return pl.pallas_call(
        _ffn_kernel,
        grid=(t // tm, d_ff // tf),
        in_specs=[
            pl.BlockSpec((tm, d), lambda i, f: (i, 0)),
            pl.BlockSpec((1, d), lambda i, f: (0, 0)),
            pl.BlockSpec((d, tf), lambda i, f: (0, f)),
            pl.BlockSpec((tf, d), lambda i, f: (f, 0)),
            pl.BlockSpec((1, d), lambda i, f: (0, 0)),
        ],
        out_specs=pl.BlockSpec((tm, d), lambda i, f: (i, 0)),
        out_shape=jax.ShapeDtypeStruct((t, d), F32),
        scratch_shapes=[pltpu.VMEM((tm, d), BF16), pltpu.VMEM((tm, d), F32)],
        compiler_params=_params("parallel", "arbitrary"),
        name="ffn",
    )(h, g_pre, w_up, w_down, g_post)


def _rope_table(positions, head_dim):
    r = head_dim // ROPE_FRACTION
    n_freq = r // 2
    inv_freq = ROPE_THETA ** (-(jnp.arange(0, r, 2, dtype=F32) / r))
    rep = ROPE_LANES // n_freq
    t = positions.size
    per_row = LANES // ROPE_LANES
    ang = positions.reshape(t // per_row, per_row, 1).astype(F32) * jnp.tile(inv_freq, rep)
    ang = ang.reshape(t // per_row, LANES)
    cos = jnp.cos(ang).reshape(t, ROPE_LANES)
    sin = jnp.sin(ang).reshape(t, ROPE_LANES)
    one = jnp.ones((cos.shape[0], ROPE_PAIR_SHIFT - ROPE_LANES), F32)
    return jnp.concatenate([cos, one, sin, one], axis=1)


def _rope_tables(positions):
    return jnp.stack([_rope_table(positions, DSA_HEAD_DIM), _rope_table(positions, IDX_DIM)])


def _dsa_lane_sources():
    half = DSA_HEAD_DIM // ROPE_FRACTION // 2
    assert half == ROPE_LANES
    cut = 2 * half + ROPE_PAIR_SHIFT - half
    return (list(range(half)) + list(range(2 * half, cut)) + list(range(half, 2 * half))
            + list(range(cut, DSA_HEAD_DIM)))


def _idx_lane_sources():
    half = IDX_DIM // ROPE_FRACTION // 2
    assert 2 * half == ROPE_LANES

    def a(lo, hi):
        return list(range(lo, hi))

    def b(lo, hi):
        return list(range(IDX_DIM + lo, IDX_DIM + hi))

    return (a(0, half) + b(0, half) + a(2 * half, IDX_DIM)
            + a(half, 2 * half) + b(half, 2 * half) + b(2 * half, IDX_DIM))


def _weight_plan(d_in):
    o_gr = 2 * N_GQ + 2 * N_GV
    o_dq = o_gr + GLA_GATE_RANK
    o_dk = o_dq + N_DSA
    o_dv = o_dk + N_DSA
    o_iq = o_dv + N_DSA
    o_ik = o_iq + N_IQ
    o_iw = o_ik + IDX_DIM
    assert o_iw + IDX_HEADS == d_in
    half = IDX_DIM // ROPE_FRACTION // 2
    none = [-1]

    def shifted(base, sources):
        return [base + s for s in sources]

    dsa, idx, ident = _dsa_lane_sources(), _idx_lane_sources(), list(range(LANES))
    tiles = {
        "rot": ([shifted(o_dq + LANES * h, dsa) for h in range(DSA_HEADS)]
                + [shifted(o_dk + LANES * h, dsa) for h in range(DSA_HEADS)]
                + [shifted(o_iq + LANES * p, idx) for p in range(IDX_HEADS // 2)]),
        "small": [
            (shifted(o_ik, range(half)) + none * half + shifted(o_ik, range(2 * half, IDX_DIM))
             + shifted(o_ik, range(half, 2 * half)) + none * (LANES - ROPE_PAIR_SHIFT - half)),
            (none * half + shifted(o_ik, range(half)) + none * (ROPE_PAIR_SHIFT - 2 * half) + none * half
             + shifted(o_ik, range(half, 2 * half)) + shifted(o_ik, range(2 * half, IDX_DIM))),
            (shifted(o_iw, range(IDX_HEADS)) + shifted(o_gr, range(GLA_GATE_RANK))
             + none * (LANES - IDX_HEADS - GLA_GATE_RANK)),
        ],
        "fm": ([shifted(o_dv + LANES * t, ident) for t in range(N_DSA // LANES)]
               + [shifted(o_iw, range(IDX_HEADS)) + none * (LANES - IDX_HEADS)]),
    }
    mats, plans = [], {}
    for name, tile_list in tiles.items():
        plans[name] = []
        for cols in tile_list:
            assert len(cols) == LANES
            parts = []
            for src_tile in sorted({c // LANES for c in cols if c >= 0}):
                m = np.zeros((LANES, LANES), np.float32)
                for dst, c in enumerate(cols):
                    if c >= 0 and c // LANES == src_tile:
                        m[c % LANES, dst] = 1.0
                for mat_id, known in enumerate(mats):
                    if np.array_equal(known, m):
                        break
                else:
                    mat_id = len(mats)
                    mats.append(m)
                parts.append((src_tile, mat_id))
            plans[name].append(parts)
    return plans, np.stack(mats)


def _prep_kernel(w_ref, p_ref, main_ref, small_ref, fm_ref, *, plans, n_plain, d_in):
    rows = w_ref.shape[0]
    lane = lax.broadcasted_iota(I32, (rows, LANES), 1)

    def src_tile(t):
        v = w_ref[:, t * LANES:(t + 1) * LANES]
        if (t + 1) * LANES > d_in:
            v = jnp.where(lane < d_in - t * LANES, v, 0.0)
        return v.astype(BF16)

    def gathered(parts):
        acc = None
        for t, mat_id in parts:
            y = jnp.dot(src_tile(t), p_ref[mat_id], preferred_element_type=F32)
            acc = y if acc is None else acc + y
        return acc.astype(BF16)

    main_ref[:, :n_plain] = w_ref[:, :n_plain].astype(BF16)
    for i, parts in enumerate(plans["rot"]):
        main_ref[:, n_plain + i * LANES:n_plain + (i + 1) * LANES] = gathered(parts)
    for i, parts in enumerate(plans["small"]):
        small_ref[:, i * LANES:(i + 1) * LANES] = gathered(parts)
    for i, parts in enumerate(plans["fm"]):
        fm_ref[:, i * LANES:(i + 1) * LANES] = gathered(parts)


def _prep_weights(w_in):
    depth, d, d_in = w_in.shape
    plans, mats = _weight_plan(d_in)
    rb = min(256, d)
    n_plain = OFF_DQ
    d_pad = -(-d_in // LANES) * LANES
    n_fm = N_DSA + LANES
    kern = functools.partial(_prep_kernel, plans=plans, n_plain=n_plain, d_in=d_in)
    return pl.pallas_call(
        kern,
        grid=(depth, d // rb),
        in_specs=[
            pl.BlockSpec((None, rb, d_pad), lambda l, r: (l, r, 0)),
            pl.BlockSpec(mats.shape, lambda l, r: (0, 0, 0)),
        ],
        out_specs=[
            pl.BlockSpec((None, rb, N_MAIN), lambda l, r: (l, r, 0)),
            pl.BlockSpec((None, rb, N_SMALL), lambda l, r: (l, r, 0)),
            pl.BlockSpec((None, rb, n_fm), lambda l, r: (l, r, 0)),
        ],
        out_shape=[
            jax.ShapeDtypeStruct((depth, d, N_MAIN), BF16),
            jax.ShapeDtypeStruct((depth, d, N_SMALL), BF16),
            jax.ShapeDtypeStruct((depth, d, n_fm), BF16),
        ],
        compiler_params=_params("parallel", "parallel"),
        name="prep_weights",
    )(w_in, jnp.asarray(mats, BF16))


def kernel(x, positions, norm_mix_pre, w_in, gla_wa2, gla_ba, gla_norm, w_out, norm_mix_post,
           norm_ffn_pre, w_up, w_down, norm_ffn_post):
    batch, seq, d = x.shape
    depth = w_in.shape[0]
    assert seq % DSA_BLK == 0
    tab = _rope_tables(positions)
    w2 = jnp.pad(gla_wa2, ((0, 0), (IDX_HEADS, LANES - IDX_HEADS - GLA_GATE_RANK), (0, 0)))
    h = x.reshape(batch * seq, d)
    w_main, w_small, w_fm = _prep_weights(w_in)
    for l in range(depth):
        main, ikd, misc, dvt, iwt = _inproj(h, norm_mix_pre[l][None, :], w_main[l], w_small[l], w_fm[l], tab)
        o_gla = _gla(main, misc, w2[l], gla_ba[l][None, :], gla_norm[l][None, :], batch, seq)
        o_dsa = _dsa(main, ikd, iwt, dvt, batch, seq)
        h = _outproj(o_gla, o_dsa, h, w_out[l].astype(BF16), norm_mix_post[l][None, :])
        h = _ffn(h, norm_ffn_pre[l][None, :], w_up[l].astype(BF16), w_down[l].astype(BF16),
                 norm_ffn_post[l][None, :])
    return h.reshape(batch, seq, d)
```

```python
import functools

import jax
import jax.numpy as jnp
import numpy as np
from jax import lax
from jax.experimental import pallas as pl
from jax.experimental.pallas import tpu as pltpu

F32 = jnp.float32
BF16 = jnp.bfloat16
I32 = jnp.int32
I16 = jnp.int16

GLA_HEADS = 4
GLA_DK = 128
GLA_DV = 256
GLA_GATE_RANK = 16
GLA_GATE_TAU = 16.0
GLA_CHUNK = 64
DSA_HEADS = 8
DSA_HEAD_DIM = 128
IDX_HEADS = 16
IDX_DIM = 64
DSA_TOPK_MAX = 256
ROPE_THETA = 500000.0
ROPE_FRACTION = 4
EPS = 1e-6
NEG_INF = -1e30
INT_MIN = -(2 ** 31)

LANES = 128
N_GQ = GLA_HEADS * GLA_DK
N_GV = GLA_HEADS * GLA_DV
N_DSA = DSA_HEADS * DSA_HEAD_DIM
N_IQ = IDX_HEADS * IDX_DIM
OFF_GQ = 0
OFF_GK = OFF_GQ + N_GQ
OFF_GV = OFF_GK + N_GQ
OFF_GG = OFF_GV + N_GV
OFF_DQ = OFF_GG + N_GV
OFF_DK = OFF_DQ + N_DSA
OFF_IQ = OFF_DK + N_DSA
N_MAIN = OFF_IQ + N_IQ
N_SMALL = 3 * LANES
DSA_BLK = 256
ROPE_LANES = 16
ROPE_PAIR_SHIFT = LANES // 2
TAB_DSA, TAB_IDX = range(2)
COUNT_ROWS = 32
COUNT16_ROWS = 64
BF16_EXACT_INT = 256
VMEM_LIMIT = 56 * 1024 * 1024

NT_DIMS = (((1,), (1,)), ((), ()))
TN_DIMS = (((0,), (0,)), ((), ()))


def _params(*sem):
    return pltpu.CompilerParams(dimension_semantics=sem, vmem_limit_bytes=VMEM_LIMIT)


def _rope_cs(tab, scale, rot):
    lane = lax.broadcasted_iota(I32, (1, LANES), 1)
    lo = lane < ROPE_LANES
    hi = (lane >= ROPE_PAIR_SHIFT) & (lane < ROPE_PAIR_SHIFT + ROPE_LANES)
    swapped = pltpu.roll(tab, ROPE_PAIR_SHIFT, 1)
    c = jnp.where(hi, swapped, tab)
    s = jnp.where(lo, -swapped, jnp.where(hi, tab, 0.0))
    return (1.0 + rot * (c - 1.0)) * scale, s * (rot * scale)


def _rope(x, c, s):
    parts = []
    for g in range(x.shape[1] // LANES):
        xg = x[:, g * LANES:(g + 1) * LANES]
        parts.append(xg * c + pltpu.roll(xg, ROPE_PAIR_SHIFT, 1) * s)
    return parts[0] if len(parts) == 1 else jnp.concatenate(parts, axis=1)


def _inproj_kernel(x_ref, g_ref, w_ref, ws_ref, wt_ref, tab_ref, tabk_ref,
                   main_ref, ikd_ref, misc_ref, dvt_ref, iwt_ref, xn_ref, *, j_dq, j_dk, j_iq):
    j = pl.program_id(1)

    @pl.when(j == 0)
    def _():
        x = x_ref[...]
        ms = jnp.mean(x * x, axis=-1, keepdims=True)
        xn_ref[...] = (x * lax.rsqrt(ms + EPS) * g_ref[...]).astype(BF16)
        small = jnp.dot(xn_ref[...], ws_ref[...], preferred_element_type=F32)
        ikd_ref[...] = _rope(small[:, :2 * LANES], *_rope_cs(tabk_ref[...], 1.0, 1.0)).astype(BF16)
        misc_ref[...] = small[:, 2 * LANES:]
        fm = jnp.dot(xn_ref[...], wt_ref[...], preferred_element_type=F32)
        for ci in range(dvt_ref.shape[0]):
            dvt_ref[ci] = fm[ci * DSA_BLK:(ci + 1) * DSA_BLK, :N_DSA].T.astype(BF16)
        iwt_ref[...] = fm[:, N_DSA:].T

    rot = jnp.where(j >= j_dq, 1.0, 0.0)
    scale = jnp.where(j < j_dq, 1.0,
                      jnp.where(j < j_dk, DSA_HEAD_DIM ** -0.5, jnp.where(j < j_iq, 1.0, IDX_DIM ** -0.5)))
    c, s = _rope_cs(tab_ref[...], scale, rot)
    acc = jnp.dot(xn_ref[...], w_ref[...], preferred_element_type=F32)
    main_ref[...] = _rope(acc, c, s).astype(BF16)


def _inproj(h, gain, w_main, w_small, w_t, tab, layer):
    t, d = h.shape
    tm = min(1024, t)
    tn = 512
    nt = w_t.shape[2]
    j_dq, j_dk, j_iq = OFF_DQ // tn, OFF_DK // tn, OFF_IQ // tn
    kern = functools.partial(_inproj_kernel, j_dq=j_dq, j_dk=j_dk, j_iq=j_iq)
    return pl.pallas_call(
        kern,
        grid=(t // tm, N_MAIN // tn),
        in_specs=[
            pl.BlockSpec((tm, d), lambda i, j: (i, 0)),
            pl.BlockSpec((1, d), lambda i, j: (0, 0)),
            pl.BlockSpec((None, d, tn), lambda i, j: (layer, 0, j)),
            pl.BlockSpec((None, d, N_SMALL), lambda i, j: (layer, 0, 0), pipeline_mode=pl.Buffered(1)),
            pl.BlockSpec((None, d, nt), lambda i, j: (layer, 0, 0), pipeline_mode=pl.Buffered(1)),
            pl.BlockSpec((None, tm, LANES), lambda i, j: ((j >= j_iq).astype(I32), i, 0)),
            pl.BlockSpec((None, tm, LANES), lambda i, j: (TAB_IDX, i, 0)),
        ],
        out_specs=[
            pl.BlockSpec((tm, tn), lambda i, j: (i, j)),
            pl.BlockSpec((tm, 2 * LANES), lambda i, j: (i, 0)),
            pl.BlockSpec((tm, LANES), lambda i, j: (i, 0)),
            pl.BlockSpec((tm // DSA_BLK, N_DSA, DSA_BLK), lambda i, j: (i, 0, 0)),
            pl.BlockSpec((LANES, tm), lambda i, j: (0, i)),
        ],
        out_shape=[
            jax.ShapeDtypeStruct((t, N_MAIN), BF16),
            jax.ShapeDtypeStruct((t, 2 * LANES), BF16),
            jax.ShapeDtypeStruct((t, LANES), F32),
            jax.ShapeDtypeStruct((t // DSA_BLK, N_DSA, DSA_BLK), BF16),
            jax.ShapeDtypeStruct((LANES, t), F32),
        ],
        scratch_shapes=[pltpu.VMEM((tm, d), BF16)],
        compiler_params=_params("parallel", "arbitrary"),
        name="inproj",
    )(h, gain, w_main, w_small, w_t, tab, tab)


def _gla_kernel(q_ref, k_ref, v_ref, gg_ref, misc_ref, w2_ref, ba_ref, gn_ref, o_ref, state_ref, *, blk):
    c_len = GLA_CHUNK
    n_ch = blk // c_len
    shift = c_len.bit_length() - 1
    assert 1 << shift == c_len

    @pl.when(pl.program_id(1) == 0)
    def _():
        state_ref[...] = jnp.zeros_like(state_ref)

    gpre = jnp.dot(misc_ref[...], w2_ref[...], preferred_element_type=F32,
                   precision=lax.Precision.HIGHEST) + ba_ref[...]
    log_a = (jnp.minimum(gpre, 0.0) - jnp.log1p(jnp.exp(-jnp.abs(gpre)))) * (1.0 / GLA_GATE_TAU)

    r = lax.broadcasted_iota(I32, (blk, blk), 0)
    c = lax.broadcasted_iota(I32, (blk, blk), 1)
    causal = (c <= r) & ((r >> shift) == (c >> shift))
    b = jnp.dot(causal.astype(F32), log_a, preferred_element_type=F32, precision=lax.Precision.HIGHEST)
    b3 = b.reshape(n_ch, c_len, N_GQ)
    b_last = b3[:, c_len - 1:c_len, :]
    decay = jnp.exp(b_last)
    k = k_ref[...].astype(F32)
    q_dec = (q_ref[...].astype(F32) * (GLA_DK ** -0.5) * jnp.exp(b)).astype(BF16)
    k_inv = (k * jnp.exp(-b)).astype(BF16)
    k_dec = (k * jnp.exp(b_last - b3).reshape(blk, N_GQ)).astype(BF16)
    gn = gn_ref[...]

    for hd in range(GLA_HEADS):
        ck = slice(hd * GLA_DK, (hd + 1) * GLA_DK)
        cv = slice(hd * GLA_DV, (hd + 1) * GLA_DV)
        attn = lax.dot_general(q_dec[:, ck], k_inv[:, ck], NT_DIMS, preferred_element_type=F32)
        attn = jnp.where(causal, attn, 0.0).astype(BF16)
        o = jnp.dot(attn, v_ref[:, cv], preferred_element_type=F32)
        st = state_ref[hd]
        inter = []
        for ci in range(n_ch):
            rows = slice(ci * c_len, (ci + 1) * c_len)
            inter.append(lax.dot_general(q_dec[rows, ck], st.astype(BF16), NT_DIMS,
                                         preferred_element_type=F32))
            d_st = lax.dot_general(v_ref[rows, cv], k_dec[rows, ck], TN_DIMS, preferred_element_type=F32)
            st = st * decay[ci, :, ck] + d_st
        state_ref[hd] = st
        o = o + jnp.concatenate(inter, axis=0)
        ms = jnp.mean(o * o, axis=-1, keepdims=True)
        y = o * lax.rsqrt(ms + EPS) * gn
        gate = gg_ref[:, cv].astype(F32)
        o_ref[:, cv] = (y * (gate / (1.0 + jnp.exp(-gate)))).astype(BF16)


def _gla(main, misc, w2, ba, gn, batch, seq):
    t = batch * seq
    blk = min(256, seq)
    nb = seq // blk
    kern = functools.partial(_gla_kernel, blk=blk)

    def rowmap(col):
        return lambda b, n: (b * nb + n, col)

    return pl.pallas_call(
        kern,
        grid=(batch, nb),
        in_specs=[
            pl.BlockSpec((blk, N_GQ), rowmap(OFF_GQ // N_GQ)),
            pl.BlockSpec((blk, N_GQ), rowmap(OFF_GK // N_GQ)),
            pl.BlockSpec((blk, N_GV), rowmap(OFF_GV // N_GV)),
            pl.BlockSpec((blk, N_GV), rowmap(OFF_GG // N_GV)),
            pl.BlockSpec((blk, LANES), rowmap(0)),
            pl.BlockSpec((LANES, N_GQ), lambda b, n: (0, 0)),
            pl.BlockSpec((1, N_GQ), lambda b, n: (0, 0)),
            pl.BlockSpec((1, GLA_DV), lambda b, n: (0, 0)),
        ],
        out_specs=pl.BlockSpec((blk, N_GV), rowmap(0)),
        out_shape=jax.ShapeDtypeStruct((t, N_GV), BF16),
        scratch_shapes=[pltpu.VMEM((GLA_HEADS, GLA_DV, GLA_DK), F32)],
        compiler_params=_params("parallel", "arbitrary"),
        name="gla",
    )(main, main, main, main, misc, w2, ba, gn)


def _dsa_kernel(q_ref, iq_ref, iwt_ref, k_ref, vt_ref, ikd_ref, o_ref,
                key_ref, hi_ref, lo_ref, bias_ref, acc_ref, s_ref, p_ref, m_ref, l_ref, alpha_ref,
                *, blk, topk, idx_bits):
    qi = pl.program_id(1)
    nch = qi + 1
    krow = lax.broadcasted_iota(I32, (blk, blk), 0)
    qcol = lax.broadcasted_iota(I32, (blk, blk), 1)
    w_idx = iwt_ref[0:IDX_HEADS, :] * (IDX_HEADS ** -0.5)

    def score_body(c, carry):
        ks = pl.multiple_of(c * blk, blk)
        k_lo = ikd_ref[pl.ds(ks, blk), 0:LANES]
        k_hi = ikd_ref[pl.ds(ks, blk), LANES:2 * LANES]
        acc = jnp.zeros((blk, blk), F32)
        for p in range(IDX_HEADS // 2):
            iq_p = iq_ref[:, p * LANES:(p + 1) * LANES]
            l0 = lax.dot_general(k_lo, iq_p, NT_DIMS, preferred_element_type=F32)
            l1 = lax.dot_general(k_hi, iq_p, NT_DIMS, preferred_element_type=F32)
            acc = acc + w_idx[2 * p:2 * p + 1, :] * jnp.maximum(l0, 0.0)
            acc = acc + w_idx[2 * p + 1:2 * p + 2, :] * jnp.maximum(l1, 0.0)
        score = jnp.where(krow + (c - qi) * blk > qcol, NEG_INF, acc)
        score = jnp.where(score == 0.0, 0.0, score)
        bits = pltpu.bitcast(score, I32)
        key = bits ^ ((bits >> 31) & 0x7FFFFFFF)
        key_ref[c] = key
        hi_ref[c] = (key >> 16).astype(I16)
        return carry

    lax.fori_loop(0, nch, score_body, 0)

    def count16(ref, cand):
        def body(c, acc):
            m = jnp.where(ref[c] >= cand, jnp.ones((), BF16), jnp.zeros((), BF16))
            for i in range(blk // COUNT16_ROWS):
                acc = acc + m[i * COUNT16_ROWS:(i + 1) * COUNT16_ROWS]
            return acc
        acc = lax.fori_loop(0, nch, body, jnp.zeros((COUNT16_ROWS, blk), BF16))
        return jnp.sum(acc.astype(F32), axis=0, keepdims=True)

    def kth_largest16(ref, kth):
        v = jnp.where(count16(ref, jnp.zeros((1, blk), I16)) >= kth, 0, -(2 ** 15)).astype(I32)

        def body(i, v):
            cand = v | lax.shift_left(jnp.int32(1), 14 - i)
            return jnp.where(count16(ref, cand.astype(I16)) >= kth, cand, v)

        return lax.fori_loop(0, 15, body, v)

    def count(pred):
        def body(c, acc):
            m = jnp.where(pred(key_ref[c], c), 1.0, 0.0)
            return acc + jnp.sum(m.reshape(blk // COUNT_ROWS, COUNT_ROWS, blk), axis=0)
        acc = lax.fori_loop(0, nch, body, jnp.zeros((COUNT_ROWS, blk), F32))
        return jnp.sum(acc, axis=0, keepdims=True)

    kf = float(topk)
    thr_hi = kth_largest16(hi_ref, kf)
    n_above = count16(hi_ref, (thr_hi + 1).astype(I16))
    n_above = jnp.where(thr_hi == 2 ** 15 - 1, 0.0, n_above)

    def low_body(c, carry):
        key = key_ref[c]
        low = (key & 0xFFFF) - 2 ** 15
        lo_ref[c] = jnp.where((key >> 16) == thr_hi, low, -(2 ** 15)).astype(I16)
        return carry

    lax.fori_loop(0, nch, low_body, 0)
    thr_lo = kth_largest16(lo_ref, kf - n_above)
    thr = thr_hi * 2 ** 16 + (thr_lo + 2 ** 15)

    n_ge = count(lambda kc, c: kc >= thr)

    def resolve_ties():
        need = kf - count(lambda kc, c: kc > thr)

        def tie_body(i, last):
            cand = last | lax.shift_left(jnp.int32(1), idx_bits - 1 - i)
            below = count(lambda kc, c: (kc == thr) & (c * blk + krow < cand))
            return jnp.where(below < need, cand, last)

        return lax.fori_loop(0, idx_bits, tie_body, jnp.zeros((1, blk), I32))

    last = lax.cond(jnp.max(n_ge) > kf, resolve_ties,
                    lambda: jnp.full((1, blk), 2 ** idx_bits, I32))

    def write_bias(c, diagonal):
        kc = key_ref[c]
        tie_pos = jnp.where(kc == thr, c * blk + krow, -1)
        bias = jnp.where(kc >= thr, jnp.where(tie_pos > last, NEG_INF, 0.0), NEG_INF)
        if diagonal:
            bias = jnp.where(krow > qcol, NEG_INF, bias)
        bias_ref[c] = bias

    def bias_body(c, carry):
        write_bias(c, False)
        return carry

    lax.fori_loop(0, qi, bias_body, 0)
    write_bias(qi, True)

    heads = tuple((hd, slice(hd * DSA_HEAD_DIM, (hd + 1) * DSA_HEAD_DIM)) for hd in range(DSA_HEADS))

    def fold(x, op):
        return op(x.reshape(blk // 8, 8, blk), axis=0)

    m_ref[...] = jnp.full_like(m_ref, NEG_INF)
    l_ref[...] = jnp.zeros_like(l_ref)
    acc_ref[...] = jnp.zeros_like(acc_ref)

    def attn_body(c, carry):
        ks = pl.multiple_of(c * blk, blk)

        @pl.when(c >= 0)
        def _():
            for hd, hs in heads:
                s = lax.dot_general(k_ref[pl.ds(ks, blk), hs], q_ref[:, hs], NT_DIMS,
                                    preferred_element_type=F32) + bias_ref[c]
                s_ref[hd] = s
                m_prev = m_ref[hd, 0:1, :]
                m_new = jnp.maximum(m_prev, jnp.max(fold(s, jnp.max), axis=0, keepdims=True))
                alpha = jnp.exp(m_prev - m_new)
                m_ref[hd, 0:1, :] = m_new
                alpha_ref[hd, 0:1, :] = alpha
                p = jnp.exp(s_ref[hd] - m_new)
                l_ref[hd] = alpha * l_ref[hd] + fold(p, jnp.sum)
                p_ref[hd] = p.astype(BF16)

        @pl.when(c >= 0)
        def _():
            for hd, hs in heads:
                pv = jnp.dot(vt_ref[c, hs, :], p_ref[hd], preferred_element_type=F32)
                acc_ref[hs, :] = alpha_ref[hd, 0:1, :] * acc_ref[hs, :] + pv

        return carry

    lax.fori_loop(0, nch, attn_body, 0)

    for hd, hs in heads:
        o_t = acc_ref[hs, :] / jnp.sum(l_ref[hd], axis=0, keepdims=True)
        o_ref[:, hs] = o_t.T.astype(BF16)


def _dsa(main, ikd, iwt, dvt, batch, seq):
    t = batch * seq
    blk = DSA_BLK
    nq = seq // blk
    topk = min(DSA_TOPK_MAX, seq // 4)
    idx_bits = max(1, (seq - 1).bit_length())
    assert (blk // COUNT16_ROWS) * nq <= BF16_EXACT_INT
    kern = functools.partial(_dsa_kernel, blk=blk, topk=topk, idx_bits=idx_bits)

    def qmap(col):
        return lambda b, i: (b * nq + i, col)

    return pl.pallas_call(
        kern,
        grid=(batch, nq),
        in_specs=[
            pl.BlockSpec((blk, N_DSA), qmap(OFF_DQ // N_DSA)),
            pl.BlockSpec((blk, N_IQ), qmap(OFF_IQ // N_IQ)),
            pl.BlockSpec((LANES, blk), lambda b, i: (0, b * nq + i)),
            pl.BlockSpec((seq, N_DSA), lambda b, i: (b, OFF_DK // N_DSA)),
            pl.BlockSpec((nq, N_DSA, blk), lambda b, i: (b, 0, 0)),
            pl.BlockSpec((seq, 2 * LANES), lambda b, i: (b, 0)),
        ],
        out_specs=pl.BlockSpec((blk, N_DSA), qmap(0)),
        out_shape=jax.ShapeDtypeStruct((t, N_DSA), BF16),
        scratch_shapes=[
            pltpu.VMEM((nq, blk, blk), I32),
            pltpu.VMEM((nq, blk, blk), I16),
            pltpu.VMEM((nq, blk, blk), I16),
            pltpu.VMEM((nq, blk, blk), F32),
            pltpu.VMEM((N_DSA, blk), F32),
            pltpu.VMEM((DSA_HEADS, blk, blk), F32),
            pltpu.VMEM((DSA_HEADS, blk, blk), BF16),
            pltpu.VMEM((DSA_HEADS, 8, blk), F32),
            pltpu.VMEM((DSA_HEADS, 8, blk), F32),
            pltpu.VMEM((DSA_HEADS, 8, blk), F32),
        ],
        compiler_params=_params("parallel", "arbitrary"),
        name="dsa",
    )(main, main, iwt, main, dvt, ikd)


def _outproj_kernel(og_ref, od_ref, h_ref, w_ref, g_ref, o_ref):
    m = jnp.dot(og_ref[...], w_ref[:N_GV, :], preferred_element_type=F32)
    m = m + jnp.dot(od_ref[...], w_ref[N_GV:, :], preferred_element_type=F32)
    ms = jnp.mean(m * m, axis=-1, keepdims=True)
    o_ref[...] = h_ref[...] + m * lax.rsqrt(ms + EPS) * g_ref[...]


def _outproj(o_gla, o_dsa, h, w_out, gain, layer):
    t, d = h.shape
    tm = min(512, t)
    return pl.pallas_call(
        _outproj_kernel,
        grid=(t // tm,),
        in_specs=[
            pl.BlockSpec((tm, N_GV), lambda i: (i, 0)),
            pl.BlockSpec((tm, N_DSA), lambda i: (i, 0)),
            pl.BlockSpec((tm, d), lambda i: (i, 0)),
            pl.BlockSpec((None, N_GV + N_DSA, d), lambda i: (layer, 0, 0)),
            pl.BlockSpec((1, d), lambda i: (0, 0)),
        ],
        out_specs=pl.BlockSpec((tm, d), lambda i: (i, 0)),
        out_shape=jax.ShapeDtypeStruct((t, d), F32),
        compiler_params=_params("parallel"),
        name="outproj",
    )(o_gla, o_dsa, h, w_out, gain)


def _ffn_kernel(h_ref, gpre_ref, wu_ref, wd_ref, gpost_ref, o_ref, xn_ref, acc_ref):
    f = pl.program_id(1)

    @pl.when(f == 0)
    def _():
        x = h_ref[...]
        ms = jnp.mean(x * x, axis=-1, keepdims=True)
        xn_ref[...] = (x * lax.rsqrt(ms + EPS) * gpre_ref[...]).astype(BF16)
        acc_ref[...] = jnp.zeros_like(acc_ref)

    u = jnp.maximum(jnp.dot(xn_ref[...], wu_ref[...], preferred_element_type=F32), 0.0)
    acc_ref[...] += jnp.dot((u * u).astype(BF16), wd_ref[...], preferred_element_type=F32)

    @pl.when(f == pl.num_programs(1) - 1)
    def _():
        y = acc_ref[...]
        ms = jnp.mean(y * y, axis=-1, keepdims=True)
        o_ref[...] = h_ref[...] + y * lax.rsqrt(ms + EPS) * gpost_ref[...]


def _ffn(h, g_pre, w_up, w_down, g_post, layer):
    t, d = h.shape
    d_ff = w_up.shape[2]
    tm = min(512, t)
    tf = 512
    return pl.pallas_call(
        _ffn_kernel,
        grid=(t // tm, d_ff // tf),
        in_specs=[
            pl.BlockSpec((tm, d), lambda i, f: (i, 0)),
            pl.BlockSpec((1, d), lambda i, f: (0, 0)),
            pl.BlockSpec((None, d, tf), lambda i, f: (layer, 0, f)),
            pl.BlockSpec((None, tf, d), lambda i, f: (layer, f, 0)),
            pl.BlockSpec((1, d), lambda i, f: (0, 0)),
        ],
        out_specs=pl.BlockSpec((tm, d), lambda i, f: (i, 0)),
        out_shape=jax.ShapeDtypeStruct((t, d), F32),
        scratch_shapes=[pltpu.VMEM((tm, d), BF16), pltpu.VMEM((tm, d), F32)],
        compiler_params=_params("parallel", "arbitrary"),
        name="ffn",
    )(h, g_pre, w_up, w_down, g_post)


def _rope_table(positions, head_dim):
    r = head_dim // ROPE_FRACTION
    n_freq = r // 2
    inv_freq = ROPE_THETA ** (-(jnp.arange(0, r, 2, dtype=F32) / r))
    rep = ROPE_LANES // n_freq
    t = positions.size
    ang = jnp.tile(inv_freq, rep)[:, None] * positions.reshape(1, t).astype(F32)
    cos = jnp.cos(ang).T
    sin = jnp.sin(ang).T
    one = jnp.ones((cos.shape[0], ROPE_PAIR_SHIFT - ROPE_LANES), F32)
    return jnp.concatenate([cos, one, sin, one], axis=1)


def _rope_tables(positions):
    return jnp.stack([_rope_table(positions, DSA_HEAD_DIM), _rope_table(positions, IDX_DIM)])


def _dsa_lane_sources():
    half = DSA_HEAD_DIM // ROPE_FRACTION // 2
    assert half == ROPE_LANES
    cut = 2 * half + ROPE_PAIR_SHIFT - half
    return (list(range(half)) + list(range(2 * half, cut)) + list(range(half, 2 * half))
            + list(range(cut, DSA_HEAD_DIM)))


def _idx_lane_sources():
    half = IDX_DIM // ROPE_FRACTION // 2
    assert 2 * half == ROPE_LANES

    def a(lo, hi):
        return list(range(lo, hi))

    def b(lo, hi):
        return list(range(IDX_DIM + lo, IDX_DIM + hi))

    return (a(0, half) + b(0, half) + a(2 * half, IDX_DIM)
            + a(half, 2 * half) + b(half, 2 * half) + b(2 * half, IDX_DIM))


def _weight_plan(d_in):
    o_gr = 2 * N_GQ + 2 * N_GV
    o_dq = o_gr + GLA_GATE_RANK
    o_dk = o_dq + N_DSA
    o_dv = o_dk + N_DSA
    o_iq = o_dv + N_DSA
    o_ik = o_iq + N_IQ
    o_iw = o_ik + IDX_DIM
    assert o_iw + IDX_HEADS == d_in
    half = IDX_DIM // ROPE_FRACTION // 2
    none = [-1]

    def shifted(base, sources):
        return [base + s for s in sources]

    dsa, idx, ident = _dsa_lane_sources(), _idx_lane_sources(), list(range(LANES))
    tiles = {
        "rot": ([shifted(o_dq + LANES * h, dsa) for h in range(DSA_HEADS)]
                + [shifted(o_dk + LANES * h, dsa) for h in range(DSA_HEADS)]
                + [shifted(o_iq + LANES * p, idx) for p in range(IDX_HEADS // 2)]),
        "small": [
            (shifted(o_ik, range(half)) + none * half + shifted(o_ik, range(2 * half, IDX_DIM))
             + shifted(o_ik, range(half, 2 * half)) + none * (LANES - ROPE_PAIR_SHIFT - half)),
            (none * half + shifted(o_ik, range(half)) + none * (ROPE_PAIR_SHIFT - 2 * half) + none * half
             + shifted(o_ik, range(half, 2 * half)) + shifted(o_ik, range(2 * half, IDX_DIM))),
            (shifted(o_iw, range(IDX_HEADS)) + shifted(o_gr, range(GLA_GATE_RANK))
             + none * (LANES - IDX_HEADS - GLA_GATE_RANK)),
        ],
        "fm": ([shifted(o_dv + LANES * t, ident) for t in range(N_DSA // LANES)]
               + [shifted(o_iw, range(IDX_HEADS)) + none * (LANES - IDX_HEADS)]),
    }
    mats, plans = [], {}
    for name, tile_list in tiles.items():
        plans[name] = []
        for cols in tile_list:
            assert len(cols) == LANES
            parts = []
            for src_tile in sorted({c // LANES for c in cols if c >= 0}):
                m = np.zeros((LANES, LANES), np.float32)
                for dst, c in enumerate(cols):
                    if c >= 0 and c // LANES == src_tile:
                        m[c % LANES, dst] = 1.0
                for mat_id, known in enumerate(mats):
                    if np.array_equal(known, m):
                        break
                else:
                    mat_id = len(mats)
                    mats.append(m)
                parts.append((src_tile, mat_id))
            plans[name].append(parts)
    return plans, np.stack(mats)


def _prep_kernel(w_ref, p_ref, main_ref, small_ref, fm_ref, *, plans, n_plain, d_in):
    rows = w_ref.shape[0]
    lane = lax.broadcasted_iota(I32, (rows, LANES), 1)

    def src_tile(t):
        v = w_ref[:, t * LANES:(t + 1) * LANES]
        if (t + 1) * LANES > d_in:
            v = jnp.where(lane < d_in - t * LANES, v, jnp.zeros((), v.dtype))
        return v

    def gathered(parts):
        acc = None
        for t, mat_id in parts:
            y = jnp.dot(src_tile(t), p_ref[mat_id], preferred_element_type=F32)
            acc = y if acc is None else acc + y
        return acc.astype(BF16)

    main_ref[:, :n_plain] = w_ref[:, :n_plain]
    for i, parts in enumerate(plans["rot"]):
        main_ref[:, n_plain + i * LANES:n_plain + (i + 1) * LANES] = gathered(parts)
    for i, parts in enumerate(plans["small"]):
        small_ref[:, i * LANES:(i + 1) * LANES] = gathered(parts)
    for i, parts in enumerate(plans["fm"]):
        fm_ref[:, i * LANES:(i + 1) * LANES] = gathered(parts)


def _prep_weights(w_in):
    depth, d, d_in = w_in.shape
    plans, mats = _weight_plan(d_in)
    rb = min(256, d)
    n_plain = OFF_DQ
    d_pad = -(-d_in // LANES) * LANES
    n_fm = N_DSA + LANES
    kern = functools.partial(_prep_kernel, plans=plans, n_plain=n_plain, d_in=d_in)
    return pl.pallas_call(
        kern,
        grid=(depth, d // rb),
        in_specs=[
            pl.BlockSpec((None, rb, d_pad), lambda l, r: (l, r, 0)),
            pl.BlockSpec(mats.shape, lambda l, r: (0, 0, 0)),
        ],
        out_specs=[
            pl.BlockSpec((None, rb, N_MAIN), lambda l, r: (l, r, 0)),
            pl.BlockSpec((None, rb, N_SMALL), lambda l, r: (l, r, 0)),
            pl.BlockSpec((None, rb, n_fm), lambda l, r: (l, r, 0)),
        ],
        out_shape=[
            jax.ShapeDtypeStruct((depth, d, N_MAIN), BF16),
            jax.ShapeDtypeStruct((depth, d, N_SMALL), BF16),
            jax.ShapeDtypeStruct((depth, d, n_fm), BF16),
        ],
        compiler_params=_params("parallel", "parallel"),
        name="prep_weights",
    )(w_in, jnp.asarray(mats, BF16))


def kernel(x, positions, norm_mix_pre, w_in, gla_wa2, gla_ba, gla_norm, w_out, norm_mix_post,
           norm_ffn_pre, w_up, w_down, norm_ffn_post):
    batch, seq, d = x.shape
    depth = w_in.shape[0]
    assert seq % DSA_BLK == 0
    tab = _rope_tables(positions)
    w2 = jnp.pad(gla_wa2, ((0, 0), (IDX_HEADS, LANES - IDX_HEADS - GLA_GATE_RANK), (0, 0)))
    h = x.reshape(batch * seq, d)
    w_main, w_small, w_fm = _prep_weights(w_in.astype(BF16))
    w_out, w_up, w_down = (w.astype(BF16) for w in (w_out, w_up, w_down))
    for l in range(depth):
        main, ikd, misc, dvt, iwt = _inproj(h, norm_mix_pre[l][None, :], w_main, w_small, w_fm, tab, l)
        o_gla = _gla(main, misc, w2[l], gla_ba[l][None, :], gla_norm[l][None, :], batch, seq)
        o_dsa = _dsa(main, ikd, iwt, dvt, batch, seq)
        h = _outproj(o_gla, o_dsa, h, w_out, norm_mix_post[l][None, :], l)
        h = _ffn(h, norm_ffn_pre[l][None, :], w_up, w_down, norm_ffn_post[l][None, :], l)
    return h.reshape(batch, seq, d)
```

```python
import functools

import jax
import jax.numpy as jnp
import numpy as np
from jax import lax
from jax.experimental import pallas as pl
from jax.experimental.pallas import tpu as pltpu

F32 = jnp.float32
BF16 = jnp.bfloat16
I32 = jnp.int32
I16 = jnp.int16

GLA_HEADS = 4
GLA_DK = 128
GLA_DV = 256
GLA_GATE_RANK = 16
GLA_GATE_TAU = 16.0
GLA_CHUNK = 64
DSA_HEADS = 8
DSA_HEAD_DIM = 128
IDX_HEADS = 16
IDX_DIM = 64
DSA_TOPK_MAX = 256
ROPE_THETA = 500000.0
ROPE_FRACTION = 4
EPS = 1e-6
NEG_INF = -1e30
INT_MIN = -(2 ** 31)

LANES = 128
N_GQ = GLA_HEADS * GLA_DK
N_GV = GLA_HEADS * GLA_DV
N_DSA = DSA_HEADS * DSA_HEAD_DIM
N_IQ = IDX_HEADS * IDX_DIM
OFF_GQ = 0
OFF_GK = OFF_GQ + N_GQ
OFF_GV = OFF_GK + N_GQ
OFF_GG = OFF_GV + N_GV
OFF_DQ = OFF_GG + N_GV
OFF_DK = OFF_DQ + N_DSA
OFF_IQ = OFF_DK + N_DSA
N_MAIN = OFF_IQ + N_IQ
N_SMALL = 3 * LANES
DSA_BLK = 256
ROPE_LANES = 16
ROPE_PAIR_SHIFT = LANES // 2
TAB_DSA, TAB_IDX = range(2)
COUNT_ROWS = 32
COUNT16_ROWS = 64
BF16_EXACT_INT = 256
VMEM_LIMIT = 56 * 1024 * 1024

NT_DIMS = (((1,), (1,)), ((), ()))
TN_DIMS = (((0,), (0,)), ((), ()))


def _params(*sem):
    return pltpu.CompilerParams(dimension_semantics=sem, vmem_limit_bytes=VMEM_LIMIT)


def _rope_cs(tab, scale, rot):
    lane = lax.broadcasted_iota(I32, (1, LANES), 1)
    lo = lane < ROPE_LANES
    hi = (lane >= ROPE_PAIR_SHIFT) & (lane < ROPE_PAIR_SHIFT + ROPE_LANES)
    swapped = pltpu.roll(tab, ROPE_PAIR_SHIFT, 1)
    c = jnp.where(hi, swapped, tab)
    s = jnp.where(lo, -swapped, jnp.where(hi, tab, 0.0))
    return (1.0 + rot * (c - 1.0)) * scale, s * (rot * scale)


def _rope(x, c, s):
    parts = []
    for g in range(x.shape[1] // LANES):
        xg = x[:, g * LANES:(g + 1) * LANES]
        parts.append(xg * c + pltpu.roll(xg, ROPE_PAIR_SHIFT, 1) * s)
    return parts[0] if len(parts) == 1 else jnp.concatenate(parts, axis=1)


def _inproj_kernel(x_ref, g_ref, w_ref, ws_ref, wt_ref, tab_ref, tabk_ref,
                   main_ref, ikd_ref, misc_ref, dvt_ref, iwt_ref, xn_ref, *, j_dq, j_dk, j_iq):
    j = pl.program_id(1)

    @pl.when(j == 0)
    def _():
        x = x_ref[...]
        ms = jnp.mean(x * x, axis=-1, keepdims=True)
        xn_ref[...] = (x * lax.rsqrt(ms + EPS) * g_ref[...]).astype(BF16)
        small = jnp.dot(xn_ref[...], ws_ref[...], preferred_element_type=F32)
        ikd_ref[...] = _rope(small[:, :2 * LANES], *_rope_cs(tabk_ref[...], 1.0, 1.0)).astype(BF16)
        misc_ref[...] = small[:, 2 * LANES:]
        fm = jnp.dot(xn_ref[...], wt_ref[...], preferred_element_type=F32)
        for ci in range(dvt_ref.shape[0]):
            dvt_ref[ci] = fm[ci * DSA_BLK:(ci + 1) * DSA_BLK, :N_DSA].T.astype(BF16)
        iwt_ref[...] = fm[:, N_DSA:].T

    rot = jnp.where(j >= j_dq, 1.0, 0.0)
    scale = jnp.where(j < j_dq, 1.0,
                      jnp.where(j < j_dk, DSA_HEAD_DIM ** -0.5, jnp.where(j < j_iq, 1.0, IDX_DIM ** -0.5)))
    c, s = _rope_cs(tab_ref[...], scale, rot)
    acc = jnp.dot(xn_ref[...], w_ref[...], preferred_element_type=F32)
    main_ref[...] = _rope(acc, c, s).astype(BF16)


def _inproj(h, gain, w_main, w_small, w_t, tab, layer):
    t, d = h.shape
    tm = min(1024, t)
    tn = 512
    nt = w_t.shape[2]
    j_dq, j_dk, j_iq = OFF_DQ // tn, OFF_DK // tn, OFF_IQ // tn
    kern = functools.partial(_inproj_kernel, j_dq=j_dq, j_dk=j_dk, j_iq=j_iq)
    return pl.pallas_call(
        kern,
        grid=(t // tm, N_MAIN // tn),
        in_specs=[
            pl.BlockSpec((tm, d), lambda i, j: (i, 0)),
            pl.BlockSpec((1, d), lambda i, j: (0, 0)),
            pl.BlockSpec((None, d, tn), lambda i, j: (layer, 0, j)),
            pl.BlockSpec((None, d, N_SMALL), lambda i, j: (layer, 0, 0), pipeline_mode=pl.Buffered(1)),
            pl.BlockSpec((None, d, nt), lambda i, j: (layer, 0, 0), pipeline_mode=pl.Buffered(1)),
            pl.BlockSpec((None, tm, LANES), lambda i, j: ((j >= j_iq).astype(I32), i, 0)),
            pl.BlockSpec((None, tm, LANES), lambda i, j: (TAB_IDX, i, 0)),
        ],
        out_specs=[
            pl.BlockSpec((tm, tn), lambda i, j: (i, j)),
            pl.BlockSpec((tm, 2 * LANES), lambda i, j: (i, 0)),
            pl.BlockSpec((tm, LANES), lambda i, j: (i, 0)),
            pl.BlockSpec((tm // DSA_BLK, N_DSA, DSA_BLK), lambda i, j: (i, 0, 0)),
            pl.BlockSpec((LANES, tm), lambda i, j: (0, i)),
        ],
        out_shape=[
            jax.ShapeDtypeStruct((t, N_MAIN), BF16),
            jax.ShapeDtypeStruct((t, 2 * LANES), BF16),
            jax.ShapeDtypeStruct((t, LANES), F32),
            jax.ShapeDtypeStruct((t // DSA_BLK, N_DSA, DSA_BLK), BF16),
            jax.ShapeDtypeStruct((LANES, t), F32),
        ],
        scratch_shapes=[pltpu.VMEM((tm, d), BF16)],
        compiler_params=_params("parallel", "arbitrary"),
        name="inproj",
    )(h, gain, w_main, w_small, w_t, tab, tab)


def _gla_kernel(q_ref, k_ref, v_ref, gg_ref, misc_ref, w2_ref, ba_ref, gn_ref, o_ref, state_ref, *, blk):
    c_len = GLA_CHUNK
    n_ch = blk // c_len
    shift = c_len.bit_length() - 1
    assert 1 << shift == c_len

    @pl.when(pl.program_id(1) == 0)
    def _():
        state_ref[...] = jnp.zeros_like(state_ref)

    gpre = jnp.dot(misc_ref[...], w2_ref[...], preferred_element_type=F32,
                   precision=lax.Precision.HIGHEST) + ba_ref[...]
    log_a = (jnp.minimum(gpre, 0.0) - jnp.log1p(jnp.exp(-jnp.abs(gpre)))) * (1.0 / GLA_GATE_TAU)

    r = lax.broadcasted_iota(I32, (blk, blk), 0)
    c = lax.broadcasted_iota(I32, (blk, blk), 1)
    causal = (c <= r) & ((r >> shift) == (c >> shift))
    b = jnp.dot(causal.astype(F32), log_a, preferred_element_type=F32, precision=lax.Precision.HIGHEST)
    b3 = b.reshape(n_ch, c_len, N_GQ)
    b_last = b3[:, c_len - 1:c_len, :]
    decay = jnp.exp(b_last)
    k = k_ref[...].astype(F32)
    q_dec = (q_ref[...].astype(F32) * (GLA_DK ** -0.5) * jnp.exp(b)).astype(BF16)
    k_inv = (k * jnp.exp(-b)).astype(BF16)
    k_dec = (k * jnp.exp(b_last - b3).reshape(blk, N_GQ)).astype(BF16)
    gn = gn_ref[...]

    for hd in range(GLA_HEADS):
        ck = slice(hd * GLA_DK, (hd + 1) * GLA_DK)
        cv = slice(hd * GLA_DV, (hd + 1) * GLA_DV)
        attn = lax.dot_general(q_dec[:, ck], k_inv[:, ck], NT_DIMS, preferred_element_type=F32)
        attn = jnp.where(causal, attn, 0.0).astype(BF16)
        o = jnp.dot(attn, v_ref[:, cv], preferred_element_type=F32)
        st = state_ref[hd]
        inter = []
        for ci in range(n_ch):
            rows = slice(ci * c_len, (ci + 1) * c_len)
            inter.append(lax.dot_general(q_dec[rows, ck], st.astype(BF16), NT_DIMS,
                                         preferred_element_type=F32))
            d_st = lax.dot_general(v_ref[rows, cv], k_dec[rows, ck], TN_DIMS, preferred_element_type=F32)
            st = st * decay[ci, :, ck] + d_st
        state_ref[hd] = st
        o = o + jnp.concatenate(inter, axis=0)
        ms = jnp.mean(o * o, axis=-1, keepdims=True)
        y = o * lax.rsqrt(ms + EPS) * gn
        gate = gg_ref[:, cv].astype(F32)
        o_ref[:, cv] = (y * (gate / (1.0 + jnp.exp(-gate)))).astype(BF16)


def _gla(main, misc, w2, ba, gn, batch, seq):
    t = batch * seq
    blk = min(256, seq)
    nb = seq // blk
    kern = functools.partial(_gla_kernel, blk=blk)

    def rowmap(col):
        return lambda b, n: (b * nb + n, col)

    return pl.pallas_call(
        kern,
        grid=(batch, nb),
        in_specs=[
            pl.BlockSpec((blk, N_GQ), rowmap(OFF_GQ // N_GQ)),
            pl.BlockSpec((blk, N_GQ), rowmap(OFF_GK // N_GQ)),
            pl.BlockSpec((blk, N_GV), rowmap(OFF_GV // N_GV)),
            pl.BlockSpec((blk, N_GV), rowmap(OFF_GG // N_GV)),
            pl.BlockSpec((blk, LANES), rowmap(0)),
            pl.BlockSpec((LANES, N_GQ), lambda b, n: (0, 0)),
            pl.BlockSpec((1, N_GQ), lambda b, n: (0, 0)),
            pl.BlockSpec((1, GLA_DV), lambda b, n: (0, 0)),
        ],
        out_specs=pl.BlockSpec((blk, N_GV), rowmap(0)),
        out_shape=jax.ShapeDtypeStruct((t, N_GV), BF16),
        scratch_shapes=[pltpu.VMEM((GLA_HEADS, GLA_DV, GLA_DK), F32)],
        compiler_params=_params("parallel", "arbitrary"),
        name="gla",
    )(main, main, main, main, misc, w2, ba, gn)


def _dsa_kernel(q_ref, iq_ref, iwt_ref, k_ref, vt_ref, ikd_ref, o_ref,
                key_ref, hi_ref, lo_ref, bias_ref, acc_ref, s_ref, p_ref, m_ref, l_ref, alpha_ref,
                *, blk, topk, idx_bits):
    qi = pl.program_id(1)
    nch = qi + 1
    krow = lax.broadcasted_iota(I32, (blk, blk), 0)
    qcol = lax.broadcasted_iota(I32, (blk, blk), 1)
    w_idx = iwt_ref[0:IDX_HEADS, :] * (IDX_HEADS ** -0.5)

    def score_body(c, carry):
        ks = pl.multiple_of(c * blk, blk)
        k_lo = ikd_ref[pl.ds(ks, blk), 0:LANES]
        k_hi = ikd_ref[pl.ds(ks, blk), LANES:2 * LANES]
        acc = jnp.zeros((blk, blk), F32)
        for p in range(IDX_HEADS // 2):
            iq_p = iq_ref[:, p * LANES:(p + 1) * LANES]
            l0 = lax.dot_general(k_lo, iq_p, NT_DIMS, preferred_element_type=F32)
            l1 = lax.dot_general(k_hi, iq_p, NT_DIMS, preferred_element_type=F32)
            acc = acc + w_idx[2 * p:2 * p + 1, :] * jnp.maximum(l0, 0.0)
            acc = acc + w_idx[2 * p + 1:2 * p + 2, :] * jnp.maximum(l1, 0.0)
        score = jnp.where(krow + (c - qi) * blk > qcol, NEG_INF, acc)
        score = jnp.where(score == 0.0, 0.0, score)
        bits = pltpu.bitcast(score, I32)
        key = bits ^ ((bits >> 31) & 0x7FFFFFFF)
        key_ref[c] = key
        hi_ref[c] = (key >> 16).astype(I16)
        return carry

    lax.fori_loop(0, nch, score_body, 0)

    def count16(ref, cand):
        def body(c, acc):
            m = jnp.where(ref[c] >= cand, jnp.ones((), BF16), jnp.zeros((), BF16))
            for i in range(blk // COUNT16_ROWS):
                acc = acc + m[i * COUNT16_ROWS:(i + 1) * COUNT16_ROWS]
            return acc
        acc = lax.fori_loop(0, nch, body, jnp.zeros((COUNT16_ROWS, blk), BF16))
        return jnp.sum(acc.astype(F32), axis=0, keepdims=True)

    def kth_largest16(ref, kth):
        v = jnp.where(count16(ref, jnp.zeros((1, blk), I16)) >= kth, 0, -(2 ** 15)).astype(I32)

        def body(i, v):
            cand = v | lax.shift_left(jnp.int32(1), 14 - i)
            return jnp.where(count16(ref, cand.astype(I16)) >= kth, cand, v)

        return lax.fori_loop(0, 15, body, v)

    def count(pred):
        def body(c, acc):
            m = jnp.where(pred(key_ref[c], c), 1.0, 0.0)
            return acc + jnp.sum(m.reshape(blk // COUNT_ROWS, COUNT_ROWS, blk), axis=0)
        acc = lax.fori_loop(0, nch, body, jnp.zeros((COUNT_ROWS, blk), F32))
        return jnp.sum(acc, axis=0, keepdims=True)

    kf = float(topk)
    thr_hi = kth_largest16(hi_ref, kf)
    n_above = count16(hi_ref, (thr_hi + 1).astype(I16))
    n_above = jnp.where(thr_hi == 2 ** 15 - 1, 0.0, n_above)

    def low_body(c, carry):
        key = key_ref[c]
        low = (key & 0xFFFF) - 2 ** 15
        lo_ref[c] = jnp.where((key >> 16) == thr_hi, low, -(2 ** 15)).astype(I16)
        return carry

    lax.fori_loop(0, nch, low_body, 0)
    thr_lo = kth_largest16(lo_ref, kf - n_above)
    thr = thr_hi * 2 ** 16 + (thr_lo + 2 ** 15)

    n_ge = count(lambda kc, c: kc >= thr)

    def resolve_ties():
        need = kf - count(lambda kc, c: kc > thr)

        def tie_body(i, last):
            cand = last | lax.shift_left(jnp.int32(1), idx_bits - 1 - i)
            below = count(lambda kc, c: (kc == thr) & (c * blk + krow < cand))
            return jnp.where(below < need, cand, last)

        return lax.fori_loop(0, idx_bits, tie_body, jnp.zeros((1, blk), I32))

    last = lax.cond(jnp.max(n_ge) > kf, resolve_ties,
                    lambda: jnp.full((1, blk), 2 ** idx_bits, I32))

    def write_bias(c, diagonal):
        kc = key_ref[c]
        tie_pos = jnp.where(kc == thr, c * blk + krow, -1)
        bias = jnp.where(kc >= thr, jnp.where(tie_pos > last, NEG_INF, 0.0), NEG_INF)
        if diagonal:
            bias = jnp.where(krow > qcol, NEG_INF, bias)
        bias_ref[c] = bias

    def bias_body(c, carry):
        write_bias(c, False)
        return carry

    lax.fori_loop(0, qi, bias_body, 0)
    write_bias(qi, True)

    heads = tuple((hd, slice(hd * DSA_HEAD_DIM, (hd + 1) * DSA_HEAD_DIM)) for hd in range(DSA_HEADS))

    def fold(x, op):
        return op(x.reshape(blk // 8, 8, blk), axis=0)

    m_ref[...] = jnp.full_like(m_ref, NEG_INF)
    l_ref[...] = jnp.zeros_like(l_ref)
    acc_ref[...] = jnp.zeros_like(acc_ref)

    def attn_body(c, carry):
        ks = pl.multiple_of(c * blk, blk)

        @pl.when(c >= 0)
        def _():
            for hd, hs in heads:
                s = lax.dot_general(k_ref[pl.ds(ks, blk), hs], q_ref[:, hs], NT_DIMS,
                                    preferred_element_type=F32) + bias_ref[c]
                s_ref[hd] = s
                m_prev = m_ref[hd, 0:1, :]
                m_new = jnp.maximum(m_prev, jnp.max(fold(s, jnp.max), axis=0, keepdims=True))
                alpha = jnp.exp(m_prev - m_new)
                m_ref[hd, 0:1, :] = m_new
                alpha_ref[hd, 0:1, :] = alpha
                p = jnp.exp(s_ref[hd] - m_new)
                l_ref[hd] = alpha * l_ref[hd] + fold(p, jnp.sum)
                p_ref[hd] = p.astype(BF16)

        @pl.when(c >= 0)
        def _():
            for hd, hs in heads:
                pv = jnp.dot(vt_ref[c, hs, :], p_ref[hd], preferred_element_type=F32)
                acc_ref[hs, :] = alpha_ref[hd, 0:1, :] * acc_ref[hs, :] + pv

        return carry

    lax.fori_loop(0, nch, attn_body, 0)

    for hd, hs in heads:
        o_t = acc_ref[hs, :] / jnp.sum(l_ref[hd], axis=0, keepdims=True)
        o_ref[:, hs] = o_t.T.astype(BF16)


def _dsa(main, ikd, iwt, dvt, batch, seq):
    t = batch * seq
    blk = DSA_BLK
    nq = seq // blk
    topk = min(DSA_TOPK_MAX, seq // 4)
    idx_bits = max(1, (seq - 1).bit_length())
    assert (blk // COUNT16_ROWS) * nq <= BF16_EXACT_INT
    kern = functools.partial(_dsa_kernel, blk=blk, topk=topk, idx_bits=idx_bits)

    def qmap(col):
        return lambda b, i: (b * nq + i, col)

    return pl.pallas_call(
        kern,
        grid=(batch, nq),
        in_specs=[
            pl.BlockSpec((blk, N_DSA), qmap(OFF_DQ // N_DSA)),
            pl.BlockSpec((blk, N_IQ), qmap(OFF_IQ // N_IQ)),
            pl.BlockSpec((LANES, blk), lambda b, i: (0, b * nq + i)),
            pl.BlockSpec((seq, N_DSA), lambda b, i: (b, OFF_DK // N_DSA)),
            pl.BlockSpec((nq, N_DSA, blk), lambda b, i: (b, 0, 0)),
            pl.BlockSpec((seq, 2 * LANES), lambda b, i: (b, 0)),
        ],
        out_specs=pl.BlockSpec((blk, N_DSA), qmap(0)),
        out_shape=jax.ShapeDtypeStruct((t, N_DSA), BF16),
        scratch_shapes=[
            pltpu.VMEM((nq, blk, blk), I32),
            pltpu.VMEM((nq, blk, blk), I16),
            pltpu.VMEM((nq, blk, blk), I16),
            pltpu.VMEM((nq, blk, blk), F32),
            pltpu.VMEM((N_DSA, blk), F32),
            pltpu.VMEM((DSA_HEADS, blk, blk), F32),
            pltpu.VMEM((DSA_HEADS, blk, blk), BF16),
            pltpu.VMEM((DSA_HEADS, 8, blk), F32),
            pltpu.VMEM((DSA_HEADS, 8, blk), F32),
            pltpu.VMEM((DSA_HEADS, 8, blk), F32),
        ],
        compiler_params=_params("parallel", "arbitrary"),
        name="dsa",
    )(main, main, iwt, main, dvt, ikd)


def _outproj_kernel(og_ref, od_ref, h_ref, w_ref, g_ref, o_ref):
    m = jnp.dot(og_ref[...], w_ref[:N_GV, :], preferred_element_type=F32)
    m = m + jnp.dot(od_ref[...], w_ref[N_GV:, :], preferred_element_type=F32)
    ms = jnp.mean(m * m, axis=-1, keepdims=True)
    o_ref[...] = h_ref[...] + m * lax.rsqrt(ms + EPS) * g_ref[...]


def _outproj(o_gla, o_dsa, h, w_out, gain, layer):
    t, d = h.shape
    tm = min(512, t)
    return pl.pallas_call(
        _outproj_kernel,
        grid=(t // tm,),
        in_specs=[
            pl.BlockSpec((tm, N_GV), lambda i: (i, 0)),
            pl.BlockSpec((tm, N_DSA), lambda i: (i, 0)),
            pl.BlockSpec((tm, d), lambda i: (i, 0)),
            pl.BlockSpec((None, N_GV + N_DSA, d), lambda i: (layer, 0, 0)),
            pl.BlockSpec((1, d), lambda i: (0, 0)),
        ],
        out_specs=pl.BlockSpec((tm, d), lambda i: (i, 0)),
        out_shape=jax.ShapeDtypeStruct((t, d), F32),
        compiler_params=_params("parallel"),
        name="outproj",
    )(o_gla, o_dsa, h, w_out, gain)


def _ffn_kernel(h_ref, gpre_ref, wu_ref, wd_ref, gpost_ref, o_ref, xn_ref, acc_ref):
    f = pl.program_id(1)

    @pl.when(f == 0)
    def _():
        x = h_ref[...]
        ms = jnp.mean(x * x, axis=-1, keepdims=True)
        xn_ref[...] = (x * lax.rsqrt(ms + EPS) * gpre_ref[...]).astype(BF16)
        acc_ref[...] = jnp.zeros_like(acc_ref)

    u = jnp.maximum(jnp.dot(xn_ref[...], wu_ref[...], preferred_element_type=F32), 0.0)
    acc_ref[...] += jnp.dot((u * u).astype(BF16), wd_ref[...], preferred_element_type=F32)

    @pl.when(f == pl.num_programs(1) - 1)
    def _():
        y = acc_ref[...]
        ms = jnp.mean(y * y, axis=-1, keepdims=True)
        o_ref[...] = h_ref[...] + y * lax.rsqrt(ms + EPS) * gpost_ref[...]


def _ffn(h, g_pre, w_up, w_down, g_post, layer):
    t, d = h.shape
    d_ff = w_up.shape[2]
    tm = min(512, t)
    tf = 1024
    return pl.pallas_call(
        _ffn_kernel,
        grid=(t // tm, d_ff // tf),
        in_specs=[
            pl.BlockSpec((tm, d), lambda i, f: (i, 0)),
            pl.BlockSpec((1, d), lambda i, f: (0, 0)),
            pl.BlockSpec((None, d, tf), lambda i, f: (layer, 0, f)),
            pl.BlockSpec((None, tf, d), lambda i, f: (layer, f, 0)),
            pl.BlockSpec((1, d), lambda i, f: (0, 0)),
        ],
        out_specs=pl.BlockSpec((tm, d), lambda i, f: (i, 0)),
        out_shape=jax.ShapeDtypeStruct((t, d), F32),
        scratch_shapes=[pltpu.VMEM((tm, d), BF16), pltpu.VMEM((tm, d), F32)],
        compiler_params=_params("parallel", "arbitrary"),
        name="ffn",
    )(h, g_pre, w_up, w_down, g_post)


def _rope_table(positions, head_dim):
    r = head_dim // ROPE_FRACTION
    n_freq = r // 2
    inv_freq = ROPE_THETA ** (-(jnp.arange(0, r, 2, dtype=F32) / r))
    rep = ROPE_LANES // n_freq
    t = positions.size
    ang = jnp.tile(inv_freq, rep)[:, None] * positions.reshape(1, t).astype(F32)
    cos = jnp.cos(ang).T
    sin = jnp.sin(ang).T
    one = jnp.ones((cos.shape[0], ROPE_PAIR_SHIFT - ROPE_LANES), F32)
    return jnp.concatenate([cos, one, sin, one], axis=1)


def _rope_tables(positions):
    return jnp.stack([_rope_table(positions, DSA_HEAD_DIM), _rope_table(positions, IDX_DIM)])


def _dsa_lane_sources():
    half = DSA_HEAD_DIM // ROPE_FRACTION // 2
    assert half == ROPE_LANES
    cut = 2 * half + ROPE_PAIR_SHIFT - half
    return (list(range(half)) + list(range(2 * half, cut)) + list(range(half, 2 * half))
            + list(range(cut, DSA_HEAD_DIM)))


def _idx_lane_sources():
    half = IDX_DIM // ROPE_FRACTION // 2
    assert 2 * half == ROPE_LANES

    def a(lo, hi):
        return list(range(lo, hi))

    def b(lo, hi):
        return list(range(IDX_DIM + lo, IDX_DIM + hi))

    return (a(0, half) + b(0, half) + a(2 * half, IDX_DIM)
            + a(half, 2 * half) + b(half, 2 * half) + b(2 * half, IDX_DIM))


def _weight_plan(d_in):
    o_gr = 2 * N_GQ + 2 * N_GV
    o_dq = o_gr + GLA_GATE_RANK
    o_dk = o_dq + N_DSA
    o_dv = o_dk + N_DSA
    o_iq = o_dv + N_DSA
    o_ik = o_iq + N_IQ
    o_iw = o_ik + IDX_DIM
    assert o_iw + IDX_HEADS == d_in
    half = IDX_DIM // ROPE_FRACTION // 2
    none = [-1]

    def shifted(base, sources):
        return [base + s for s in sources]

    dsa, idx, ident = _dsa_lane_sources(), _idx_lane_sources(), list(range(LANES))
    tiles = {
        "rot": ([shifted(o_dq + LANES * h, dsa) for h in range(DSA_HEADS)]
                + [shifted(o_dk + LANES * h, dsa) for h in range(DSA_HEADS)]
                + [shifted(o_iq + LANES * p, idx) for p in range(IDX_HEADS // 2)]),
        "small": [
            (shifted(o_ik, range(half)) + none * half + shifted(o_ik, range(2 * half, IDX_DIM))
             + shifted(o_ik, range(half, 2 * half)) + none * (LANES - ROPE_PAIR_SHIFT - half)),
            (none * half + shifted(o_ik, range(half)) + none * (ROPE_PAIR_SHIFT - 2 * half) + none * half
             + shifted(o_ik, range(half, 2 * half)) + shifted(o_ik, range(2 * half, IDX_DIM))),
            (shifted(o_iw, range(IDX_HEADS)) + shifted(o_gr, range(GLA_GATE_RANK))
             + none * (LANES - IDX_HEADS - GLA_GATE_RANK)),
        ],
        "fm": ([shifted(o_dv + LANES * t, ident) for t in range(N_DSA // LANES)]
               + [shifted(o_iw, range(IDX_HEADS)) + none * (LANES - IDX_HEADS)]),
    }
    mats, plans = [], {}
    for name, tile_list in tiles.items():
        plans[name] = []
        for cols in tile_list:
            assert len(cols) == LANES
            parts = []
            for src_tile in sorted({c // LANES for c in cols if c >= 0}):
                m = np.zeros((LANES, LANES), np.float32)
                for dst, c in enumerate(cols):
                    if c >= 0 and c // LANES == src_tile:
                        m[c % LANES, dst] = 1.0
                for mat_id, known in enumerate(mats):
                    if np.array_equal(known, m):
                        break
                else:
                    mat_id = len(mats)
                    mats.append(m)
                parts.append((src_tile, mat_id))
            plans[name].append(parts)
    return plans, np.stack(mats)


def _prep_kernel(w_ref, p_ref, main_ref, small_ref, fm_ref, *, plans, n_plain, d_in):
    rows = w_ref.shape[0]
    lane = lax.broadcasted_iota(I32, (rows, LANES), 1)

    def src_tile(t):
        v = w_ref[:, t * LANES:(t + 1) * LANES]
        if (t + 1) * LANES > d_in:
            v = jnp.where(lane < d_in - t * LANES, v, jnp.zeros((), v.dtype))
        return v

    def gathered(parts):
        acc = None
        for t, mat_id in parts:
            y = jnp.dot(src_tile(t), p_ref[mat_id], preferred_element_type=F32)
            acc = y if acc is None else acc + y
        return acc.astype(BF16)

    main_ref[:, :n_plain] = w_ref[:, :n_plain]
    for i, parts in enumerate(plans["rot"]):
        main_ref[:, n_plain + i * LANES:n_plain + (i + 1) * LANES] = gathered(parts)
    for i, parts in enumerate(plans["small"]):
        small_ref[:, i * LANES:(i + 1) * LANES] = gathered(parts)
    for i, parts in enumerate(plans["fm"]):
        fm_ref[:, i * LANES:(i + 1) * LANES] = gathered(parts)


def _prep_weights(w_in):
    depth, d, d_in = w_in.shape
    plans, mats = _weight_plan(d_in)
    rb = min(256, d)
    n_plain = OFF_DQ
    d_pad = -(-d_in // LANES) * LANES
    n_fm = N_DSA + LANES
    kern = functools.partial(_prep_kernel, plans=plans, n_plain=n_plain, d_in=d_in)
    return pl.pallas_call(
        kern,
        grid=(depth, d // rb),
        in_specs=[
            pl.BlockSpec((None, rb, d_pad), lambda l, r: (l, r, 0)),
            pl.BlockSpec(mats.shape, lambda l, r: (0, 0, 0)),
        ],
        out_specs=[
            pl.BlockSpec((None, rb, N_MAIN), lambda l, r: (l, r, 0)),
            pl.BlockSpec((None, rb, N_SMALL), lambda l, r: (l, r, 0)),
            pl.BlockSpec((None, rb, n_fm), lambda l, r: (l, r, 0)),
        ],
        out_shape=[
            jax.ShapeDtypeStruct((depth, d, N_MAIN), BF16),
            jax.ShapeDtypeStruct((depth, d, N_SMALL), BF16),
            jax.ShapeDtypeStruct((depth, d, n_fm), BF16),
        ],
        compiler_params=_params("parallel", "parallel"),
        name="prep_weights",
    )(w_in, jnp.asarray(mats, BF16))


def kernel(x, positions, norm_mix_pre, w_in, gla_wa2, gla_ba, gla_norm, w_out, norm_mix_post,
           norm_ffn_pre, w_up, w_down, norm_ffn_post):
    batch, seq, d = x.shape
    depth = w_in.shape[0]
    assert seq % DSA_BLK == 0
    tab = _rope_tables(positions)
    w2 = jnp.pad(gla_wa2, ((0, 0), (IDX_HEADS, LANES - IDX_HEADS - GLA_GATE_RANK), (0, 0)))
    h = x.reshape(batch * seq, d)
    w_main, w_small, w_fm = _prep_weights(w_in.astype(BF16))
    w_out, w_up, w_down = (w.astype(BF16) for w in (w_out, w_up, w_down))
    for l in range(depth):
        main, ikd, misc, dvt, iwt = _inproj(h, norm_mix_pre[l][None, :], w_main, w_small, w_fm, tab, l)
        o_gla = _gla(main, misc, w2[l], gla_ba[l][None, :], gla_norm[l][None, :], batch, seq)
        o_dsa = _dsa(main, ikd, iwt, dvt, batch, seq)
        h = _outproj(o_gla, o_dsa, h, w_out, norm_mix_post[l][None, :], l)
        h = _ffn(h, norm_ffn_pre[l][None, :], w_up, w_down, norm_ffn_post[l][None, :], l)
    return h.reshape(batch, seq, d)
```

```python
import functools

import jax
import jax.numpy as jnp
import numpy as np
from jax import lax
from jax.experimental import pallas as pl
from jax.experimental.pallas import tpu as pltpu

F32 = jnp.float32
BF16 = jnp.bfloat16
I32 = jnp.int32
I16 = jnp.int16

GLA_HEADS = 4
GLA_DK = 128
GLA_DV = 256
GLA_GATE_RANK = 16
GLA_GATE_TAU = 16.0
GLA_CHUNK = 64
DSA_HEADS = 8
DSA_HEAD_DIM = 128
IDX_HEADS = 16
IDX_DIM = 64
DSA_TOPK_MAX = 256
ROPE_THETA = 500000.0
ROPE_FRACTION = 4
EPS = 1e-6
NEG_INF = -1e30
INT_MIN = -(2 ** 31)

LANES = 128
N_GQ = GLA_HEADS * GLA_DK
N_GV = GLA_HEADS * GLA_DV
N_DSA = DSA_HEADS * DSA_HEAD_DIM
N_IQ = IDX_HEADS * IDX_DIM
OFF_GQ = 0
OFF_GK = OFF_GQ + N_GQ
OFF_GV = OFF_GK + N_GQ
OFF_GG = OFF_GV + N_GV
OFF_DQ = OFF_GG + N_GV
OFF_DK = OFF_DQ + N_DSA
OFF_IQ = OFF_DK + N_DSA
N_MAIN = OFF_IQ + N_IQ
N_SMALL = 3 * LANES
DSA_BLK = 256
ROPE_LANES = 16
ROPE_PAIR_SHIFT = LANES // 2
TAB_DSA, TAB_IDX = range(2)
INPROJ_ROW_GROUPS = 4
COUNT_ROWS = 32
COUNT16_ROWS = 64
BF16_EXACT_INT = 256
VMEM_LIMIT = 56 * 1024 * 1024

NT_DIMS = (((1,), (1,)), ((), ()))
TN_DIMS = (((0,), (0,)), ((), ()))


def _params(*sem):
    return pltpu.CompilerParams(dimension_semantics=sem, vmem_limit_bytes=VMEM_LIMIT)


def _rope_cs(tab, scale, rot):
    lane = lax.broadcasted_iota(I32, (1, LANES), 1)
    lo = lane < ROPE_LANES
    hi = (lane >= ROPE_PAIR_SHIFT) & (lane < ROPE_PAIR_SHIFT + ROPE_LANES)
    swapped = pltpu.roll(tab, ROPE_PAIR_SHIFT, 1)
    c = jnp.where(hi, swapped, tab)
    s = jnp.where(lo, -swapped, jnp.where(hi, tab, 0.0))
    return (1.0 + rot * (c - 1.0)) * scale, s * (rot * scale)


def _rope(x, c, s):
    parts = []
    for g in range(x.shape[1] // LANES):
        xg = x[:, g * LANES:(g + 1) * LANES]
        parts.append(xg * c + pltpu.roll(xg, ROPE_PAIR_SHIFT, 1) * s)
    return parts[0] if len(parts) == 1 else jnp.concatenate(parts, axis=1)


def _inproj_kernel(x_ref, g_ref, w_ref, ws_ref, wt_ref, tab_ref, tabk_ref,
                   main_ref, ikd_ref, misc_ref, dvt_ref, iwt_ref, xn_ref, *, j_dq, j_dk, j_iq):
    j = pl.program_id(1)

    @pl.when(j == 0)
    def _():
        x = x_ref[...]
        ms = jnp.mean(x * x, axis=-1, keepdims=True)
        xn_ref[...] = (x * lax.rsqrt(ms + EPS) * g_ref[...]).astype(BF16)
        small = jnp.dot(xn_ref[...], ws_ref[...], preferred_element_type=F32)
        ikd_ref[...] = _rope(small[:, :2 * LANES], *_rope_cs(tabk_ref[...], 1.0, 1.0)).astype(BF16)
        misc_ref[...] = small[:, 2 * LANES:]
        fm = jnp.dot(xn_ref[...], wt_ref[...], preferred_element_type=F32)
        for ci in range(dvt_ref.shape[0]):
            dvt_ref[ci] = fm[ci * DSA_BLK:(ci + 1) * DSA_BLK, :N_DSA].T.astype(BF16)
        iwt_ref[...] = fm[:, N_DSA:].T

    rot = jnp.where(j >= j_dq, 1.0, 0.0)
    scale = jnp.where(j < j_dq, 1.0,
                      jnp.where(j < j_dk, DSA_HEAD_DIM ** -0.5, jnp.where(j < j_iq, 1.0, IDX_DIM ** -0.5)))
    c, s = _rope_cs(tab_ref[...], scale, rot)
    rows = xn_ref.shape[0] // INPROJ_ROW_GROUPS
    for g in range(INPROJ_ROW_GROUPS):
        rs = slice(g * rows, (g + 1) * rows)
        acc = jnp.dot(xn_ref[rs, :], w_ref[...], preferred_element_type=F32)
        main_ref[rs, :] = _rope(acc, c[rs], s[rs]).astype(BF16)


def _inproj(h, gain, w_main, w_small, w_t, tab, layer):
    t, d = h.shape
    tm = min(1024, t)
    tn = 512
    nt = w_t.shape[2]
    j_dq, j_dk, j_iq = OFF_DQ // tn, OFF_DK // tn, OFF_IQ // tn
    kern = functools.partial(_inproj_kernel, j_dq=j_dq, j_dk=j_dk, j_iq=j_iq)
    return pl.pallas_call(
        kern,
        grid=(t // tm, N_MAIN // tn),
        in_specs=[
            pl.BlockSpec((tm, d), lambda i, j: (i, 0)),
            pl.BlockSpec((1, d), lambda i, j: (0, 0)),
            pl.BlockSpec((None, d, tn), lambda i, j: (layer, 0, j)),
            pl.BlockSpec((None, d, N_SMALL), lambda i, j: (layer, 0, 0), pipeline_mode=pl.Buffered(1)),
            pl.BlockSpec((None, d, nt), lambda i, j: (layer, 0, 0), pipeline_mode=pl.Buffered(1)),
            pl.BlockSpec((None, tm, LANES), lambda i, j: ((j >= j_iq).astype(I32), i, 0)),
            pl.BlockSpec((None, tm, LANES), lambda i, j: (TAB_IDX, i, 0)),
        ],
        out_specs=[
            pl.BlockSpec((tm, tn), lambda i, j: (i, j)),
            pl.BlockSpec((tm, 2 * LANES), lambda i, j: (i, 0)),
            pl.BlockSpec((tm, LANES), lambda i, j: (i, 0)),
            pl.BlockSpec((tm // DSA_BLK, N_DSA, DSA_BLK), lambda i, j: (i, 0, 0)),
            pl.BlockSpec((LANES, tm), lambda i, j: (0, i)),
        ],
        out_shape=[
            jax.ShapeDtypeStruct((t, N_MAIN), BF16),
            jax.ShapeDtypeStruct((t, 2 * LANES), BF16),
            jax.ShapeDtypeStruct((t, LANES), F32),
            jax.ShapeDtypeStruct((t // DSA_BLK, N_DSA, DSA_BLK), BF16),
            jax.ShapeDtypeStruct((LANES, t), F32),
        ],
        scratch_shapes=[pltpu.VMEM((tm, d), BF16)],
        compiler_params=_params("parallel", "arbitrary"),
        name="inproj",
    )(h, gain, w_main, w_small, w_t, tab, tab)


def _gla_kernel(q_ref, k_ref, v_ref, gg_ref, misc_ref, w2_ref, ba_ref, gn_ref, o_ref, state_ref, *, blk):
    c_len = GLA_CHUNK
    n_ch = blk // c_len
    shift = c_len.bit_length() - 1
    assert 1 << shift == c_len

    @pl.when(pl.program_id(1) == 0)
    def _():
        state_ref[...] = jnp.zeros_like(state_ref)

    gpre = jnp.dot(misc_ref[...], w2_ref[...], preferred_element_type=F32,
                   precision=lax.Precision.HIGHEST) + ba_ref[...]
    log_a = (jnp.minimum(gpre, 0.0) - jnp.log1p(jnp.exp(-jnp.abs(gpre)))) * (1.0 / GLA_GATE_TAU)

    r = lax.broadcasted_iota(I32, (blk, blk), 0)
    c = lax.broadcasted_iota(I32, (blk, blk), 1)
    causal = (c <= r) & ((r >> shift) == (c >> shift))
    b = jnp.dot(causal.astype(F32), log_a, preferred_element_type=F32, precision=lax.Precision.HIGHEST)
    b3 = b.reshape(n_ch, c_len, N_GQ)
    b_last = b3[:, c_len - 1:c_len, :]
    decay = jnp.exp(b_last)
    k = k_ref[...].astype(F32)
    q_dec = (q_ref[...].astype(F32) * (GLA_DK ** -0.5) * jnp.exp(b)).astype(BF16)
    k_inv = (k * jnp.exp(-b)).astype(BF16)
    k_dec = (k * jnp.exp(b_last - b3).reshape(blk, N_GQ)).astype(BF16)
    gn = gn_ref[...]

    for hd in range(GLA_HEADS):
        ck = slice(hd * GLA_DK, (hd + 1) * GLA_DK)
        cv = slice(hd * GLA_DV, (hd + 1) * GLA_DV)
        attn = lax.dot_general(q_dec[:, ck], k_inv[:, ck], NT_DIMS, preferred_element_type=F32)
        attn = jnp.where(causal, attn, 0.0).astype(BF16)
        o = jnp.dot(attn, v_ref[:, cv], preferred_element_type=F32)
        st = state_ref[hd]
        inter = []
        for ci in range(n_ch):
            rows = slice(ci * c_len, (ci + 1) * c_len)
            inter.append(lax.dot_general(q_dec[rows, ck], st.astype(BF16), NT_DIMS,
                                         preferred_element_type=F32))
            d_st = lax.dot_general(v_ref[rows, cv], k_dec[rows, ck], TN_DIMS, preferred_element_type=F32)
            st = st * decay[ci, :, ck] + d_st
        state_ref[hd] = st
        o = o + jnp.concatenate(inter, axis=0)
        ms = jnp.mean(o * o, axis=-1, keepdims=True)
        y = o * lax.rsqrt(ms + EPS) * gn
        gate = gg_ref[:, cv].astype(F32)
        o_ref[:, cv] = (y * (gate / (1.0 + jnp.exp(-gate)))).astype(BF16)


def _gla(main, misc, w2, ba, gn, batch, seq):
    t = batch * seq
    blk = min(256, seq)
    nb = seq // blk
    kern = functools.partial(_gla_kernel, blk=blk)

    def rowmap(col):
        return lambda b, n: (b * nb + n, col)

    return pl.pallas_call(
        kern,
        grid=(batch, nb),
        in_specs=[
            pl.BlockSpec((blk, N_GQ), rowmap(OFF_GQ // N_GQ)),
            pl.BlockSpec((blk, N_GQ), rowmap(OFF_GK // N_GQ)),
            pl.BlockSpec((blk, N_GV), rowmap(OFF_GV // N_GV)),
            pl.BlockSpec((blk, N_GV), rowmap(OFF_GG // N_GV)),
            pl.BlockSpec((blk, LANES), rowmap(0)),
            pl.BlockSpec((LANES, N_GQ), lambda b, n: (0, 0)),
            pl.BlockSpec((1, N_GQ), lambda b, n: (0, 0)),
            pl.BlockSpec((1, GLA_DV), lambda b, n: (0, 0)),
        ],
        out_specs=pl.BlockSpec((blk, N_GV), rowmap(0)),
        out_shape=jax.ShapeDtypeStruct((t, N_GV), BF16),
        scratch_shapes=[pltpu.VMEM((GLA_HEADS, GLA_DV, GLA_DK), F32)],
        compiler_params=_params("parallel", "arbitrary"),
        name="gla",
    )(main, main, main, main, misc, w2, ba, gn)


def _dsa_kernel(q_ref, iq_ref, iwt_ref, k_ref, vt_ref, ikd_ref, o_ref,
                key_ref, hi_ref, lo_ref, bias_ref, acc_ref, p_ref, m_ref, l_ref, alpha_ref,
                *, blk, topk, idx_bits):
    qi = pl.program_id(1)
    nch = qi + 1
    krow = lax.broadcasted_iota(I32, (blk, blk), 0)
    qcol = lax.broadcasted_iota(I32, (blk, blk), 1)
    w_idx = iwt_ref[0:IDX_HEADS, :] * (IDX_HEADS ** -0.5)

    def score_body(c, carry):
        ks = pl.multiple_of(c * blk, blk)
        k_lo = ikd_ref[pl.ds(ks, blk), 0:LANES]
        k_hi = ikd_ref[pl.ds(ks, blk), LANES:2 * LANES]
        acc = jnp.zeros((blk, blk), F32)
        for p in range(IDX_HEADS // 2):
            iq_p = iq_ref[:, p * LANES:(p + 1) * LANES]
            l0 = lax.dot_general(k_lo, iq_p, NT_DIMS, preferred_element_type=F32)
            l1 = lax.dot_general(k_hi, iq_p, NT_DIMS, preferred_element_type=F32)
            acc = acc + w_idx[2 * p:2 * p + 1, :] * jnp.maximum(l0, 0.0)
            acc = acc + w_idx[2 * p + 1:2 * p + 2, :] * jnp.maximum(l1, 0.0)
        score = jnp.where(krow + (c - qi) * blk > qcol, NEG_INF, acc)
        score = jnp.where(score == 0.0, 0.0, score)
        bits = pltpu.bitcast(score, I32)
        key = bits ^ ((bits >> 31) & 0x7FFFFFFF)
        key_ref[c] = key
        hi_ref[c] = (key >> 16).astype(I16)
        return carry

    lax.fori_loop(0, nch, score_body, 0)

    def count16(ref, cand):
        def body(c, acc):
            m = jnp.where(ref[c] >= cand, jnp.ones((), BF16), jnp.zeros((), BF16))
            for i in range(blk // COUNT16_ROWS):
                acc = acc + m[i * COUNT16_ROWS:(i + 1) * COUNT16_ROWS]
            return acc
        acc = lax.fori_loop(0, nch, body, jnp.zeros((COUNT16_ROWS, blk), BF16))
        return jnp.sum(acc.astype(F32), axis=0, keepdims=True)

    def kth_largest16(ref, kth):
        v = jnp.where(count16(ref, jnp.zeros((1, blk), I16)) >= kth, 0, -(2 ** 15)).astype(I32)

        def body(i, v):
            cand = v | lax.shift_left(jnp.int32(1), 14 - i)
            return jnp.where(count16(ref, cand.astype(I16)) >= kth, cand, v)

        return lax.fori_loop(0, 15, body, v)

    def count(pred):
        def body(c, acc):
            m = jnp.where(pred(key_ref[c], c), 1.0, 0.0)
            return acc + jnp.sum(m.reshape(blk // COUNT_ROWS, COUNT_ROWS, blk), axis=0)
        acc = lax.fori_loop(0, nch, body, jnp.zeros((COUNT_ROWS, blk), F32))
        return jnp.sum(acc, axis=0, keepdims=True)

    kf = float(topk)
    thr_hi = kth_largest16(hi_ref, kf)
    n_above = count16(hi_ref, (thr_hi + 1).astype(I16))
    n_above = jnp.where(thr_hi == 2 ** 15 - 1, 0.0, n_above)

    def low_body(c, carry):
        key = key_ref[c]
        low = (key & 0xFFFF) - 2 ** 15
        lo_ref[c] = jnp.where((key >> 16) == thr_hi, low, -(2 ** 15)).astype(I16)
        return carry

    lax.fori_loop(0, nch, low_body, 0)
    thr_lo = kth_largest16(lo_ref, kf - n_above)
    thr = thr_hi * 2 ** 16 + (thr_lo + 2 ** 15)

    n_ge = count(lambda kc, c: kc >= thr)

    def resolve_ties():
        need = kf - count(lambda kc, c: kc > thr)

        def tie_body(i, last):
            cand = last | lax.shift_left(jnp.int32(1), idx_bits - 1 - i)
            below = count(lambda kc, c: (kc == thr) & (c * blk + krow < cand))
            return jnp.where(below < need, cand, last)

        return lax.fori_loop(0, idx_bits, tie_body, jnp.zeros((1, blk), I32))

    last = lax.cond(jnp.max(n_ge) > kf, resolve_ties,
                    lambda: jnp.full((1, blk), 2 ** idx_bits, I32))

    def write_bias(c, diagonal):
        kc = key_ref[c]
        tie_pos = jnp.where(kc == thr, c * blk + krow, -1)
        bias = jnp.where(kc >= thr, jnp.where(tie_pos > last, NEG_INF, 0.0), NEG_INF)
        if diagonal:
            bias = jnp.where(krow > qcol, NEG_INF, bias)
        bias_ref[c] = bias

    def bias_body(c, carry):
        write_bias(c, False)
        return carry

    lax.fori_loop(0, qi, bias_body, 0)
    write_bias(qi, True)

    heads = tuple((hd, slice(hd * DSA_HEAD_DIM, (hd + 1) * DSA_HEAD_DIM)) for hd in range(DSA_HEADS))

    def fold(x, op):
        return op(x.reshape(blk // 8, 8, blk), axis=0)

    m_ref[...] = jnp.full_like(m_ref, NEG_INF)
    l_ref[...] = jnp.zeros_like(l_ref)
    acc_ref[...] = jnp.zeros_like(acc_ref)

    def attn_body(c, carry):
        ks = pl.multiple_of(c * blk, blk)

        @pl.when(c >= 0)
        def _():
            for hd, hs in heads:
                s = lax.dot_general(k_ref[pl.ds(ks, blk), hs], q_ref[:, hs], NT_DIMS,
                                    preferred_element_type=F32) + bias_ref[c]
                m_prev = m_ref[hd, 0:1, :]
                m_new = jnp.maximum(m_prev, jnp.max(fold(s, jnp.max), axis=0, keepdims=True))
                alpha = jnp.exp(m_prev - m_new)
                m_ref[hd, 0:1, :] = m_new
                alpha_ref[hd, 0:1, :] = alpha
                p = jnp.exp(s - m_new)
                l_ref[hd] = alpha * l_ref[hd] + fold(p, jnp.sum)
                p_ref[hd] = p.astype(BF16)

        @pl.when(c >= 0)
        def _():
            for hd, hs in heads:
                pv = jnp.dot(vt_ref[c, hs, :], p_ref[hd], preferred_element_type=F32)
                acc_ref[hs, :] = alpha_ref[hd, 0:1, :] * acc_ref[hs, :] + pv

        return carry

    lax.fori_loop(0, nch, attn_body, 0)

    for hd, hs in heads:
        o_t = acc_ref[hs, :] / jnp.sum(l_ref[hd], axis=0, keepdims=True)
        o_ref[:, hs] = o_t.T.astype(BF16)


def _dsa(main, ikd, iwt, dvt, batch, seq):
    t = batch * seq
    blk = DSA_BLK
    nq = seq // blk
    topk = min(DSA_TOPK_MAX, seq // 4)
    idx_bits = max(1, (seq - 1).bit_length())
    assert (blk // COUNT16_ROWS) * nq <= BF16_EXACT_INT
    kern = functools.partial(_dsa_kernel, blk=blk, topk=topk, idx_bits=idx_bits)

    def qmap(col):
        return lambda b, i: (b * nq + i, col)

    return pl.pallas_call(
        kern,
        grid=(batch, nq),
        in_specs=[
            pl.BlockSpec((blk, N_DSA), qmap(OFF_DQ // N_DSA)),
            pl.BlockSpec((blk, N_IQ), qmap(OFF_IQ // N_IQ)),
            pl.BlockSpec((LANES, blk), lambda b, i: (0, b * nq + i)),
            pl.BlockSpec((seq, N_DSA), lambda b, i: (b, OFF_DK // N_DSA)),
            pl.BlockSpec((nq, N_DSA, blk), lambda b, i: (b, 0, 0)),
            pl.BlockSpec((seq, 2 * LANES), lambda b, i: (b, 0)),
        ],
        out_specs=pl.BlockSpec((blk, N_DSA), qmap(0)),
        out_shape=jax.ShapeDtypeStruct((t, N_DSA), BF16),
        scratch_shapes=[
            pltpu.VMEM((nq, blk, blk), I32),
            pltpu.VMEM((nq, blk, blk), I16),
            pltpu.VMEM((nq, blk, blk), I16),
            pltpu.VMEM((nq, blk, blk), F32),
            pltpu.VMEM((N_DSA, blk), F32),
            pltpu.VMEM((DSA_HEADS, blk, blk), BF16),
            pltpu.VMEM((DSA_HEADS, 8, blk), F32),
            pltpu.VMEM((DSA_HEADS, 8, blk), F32),
            pltpu.VMEM((DSA_HEADS, 8, blk), F32),
        ],
        compiler_params=_params("parallel", "arbitrary"),
        name="dsa",
    )(main, main, iwt, main, dvt, ikd)


def _outproj_kernel(og_ref, od_ref, h_ref, w_ref, g_ref, o_ref):
    m = jnp.dot(og_ref[...], w_ref[:N_GV, :], preferred_element_type=F32)
    m = m + jnp.dot(od_ref[...], w_ref[N_GV:, :], preferred_element_type=F32)
    ms = jnp.mean(m * m, axis=-1, keepdims=True)
    o_ref[...] = h_ref[...] + m * lax.rsqrt(ms + EPS) * g_ref[...]


def _outproj(o_gla, o_dsa, h, w_out, gain, layer):
    t, d = h.shape
    tm = min(512, t)
    return pl.pallas_call(
        _outproj_kernel,
        grid=(t // tm,),
        in_specs=[
            pl.BlockSpec((tm, N_GV), lambda i: (i, 0)),
            pl.BlockSpec((tm, N_DSA), lambda i: (i, 0)),
            pl.BlockSpec((tm, d), lambda i: (i, 0)),
            pl.BlockSpec((None, N_GV + N_DSA, d), lambda i: (layer, 0, 0)),
            pl.BlockSpec((1, d), lambda i: (0, 0)),
        ],
        out_specs=pl.BlockSpec((tm, d), lambda i: (i, 0)),
        out_shape=jax.ShapeDtypeStruct((t, d), F32),
        compiler_params=_params("parallel"),
        name="outproj",
    )(o_gla, o_dsa, h, w_out, gain)


def _ffn_kernel(h_ref, gpre_ref, wu_ref, wd_ref, gpost_ref, o_ref, xn_ref, acc_ref):
    f = pl.program_id(1)

    @pl.when(f == 0)
    def _():
        x = h_ref[...]
        ms = jnp.mean(x * x, axis=-1, keepdims=True)
        xn_ref[...] = (x * lax.rsqrt(ms + EPS) * gpre_ref[...]).astype(BF16)
        acc_ref[...] = jnp.zeros_like(acc_ref)

    u = jnp.maximum(jnp.dot(xn_ref[...], wu_ref[...], preferred_element_type=F32), 0.0)
    acc_ref[...] += jnp.dot((u * u).astype(BF16), wd_ref[...], preferred_element_type=F32)

    @pl.when(f == pl.num_programs(1) - 1)
    def _():
        y = acc_ref[...]
        ms = jnp.mean(y * y, axis=-1, keepdims=True)
        o_ref[...] = h_ref[...] + y * lax.rsqrt(ms + EPS) * gpost_ref[...]


def _ffn(h, g_pre, w_up, w_down, g_post, layer):
    t, d = h.shape
    d_ff = w_up.shape[2]
    tm = min(512, t)
    tf = 1024
    return pl.pallas_call(
        _ffn_kernel,
        grid=(t // tm, d_ff // tf),
        in_specs=[
            pl.BlockSpec((tm, d), lambda i, f: (i, 0)),
            pl.BlockSpec((1, d), lambda i, f: (0, 0)),
            pl.BlockSpec((None, d, tf), lambda i, f: (layer, 0, f)),
            pl.BlockSpec((None, tf, d), lambda i, f: (layer, f, 0)),
            pl.BlockSpec((1, d), lambda i, f: (0, 0)),
        ],
        out_specs=pl.BlockSpec((tm, d), lambda i, f: (i, 0)),
        out_shape=jax.ShapeDtypeStruct((t, d), F32),
        scratch_shapes=[pltpu.VMEM((tm, d), BF16), pltpu.VMEM((tm, d), F32)],
        compiler_params=_params("parallel", "arbitrary"),
        name="ffn",
    )(h, g_pre, w_up, w_down, g_post)


def _rope_table(positions, head_dim):
    r = head_dim // ROPE_FRACTION
    n_freq = r // 2
    inv_freq = ROPE_THETA ** (-(jnp.arange(0, r, 2, dtype=F32) / r))
    rep = ROPE_LANES // n_freq
    t = positions.size
    ang = jnp.tile(inv_freq, rep)[:, None] * positions.reshape(1, t).astype(F32)
    cos = jnp.cos(ang).T
    sin = jnp.sin(ang).T
    one = jnp.ones((cos.shape[0], ROPE_PAIR_SHIFT - ROPE_LANES), F32)
    return jnp.concatenate([cos, one, sin, one], axis=1)


def _rope_tables(positions):
    return jnp.stack([_rope_table(positions, DSA_HEAD_DIM), _rope_table(positions, IDX_DIM)])


def _dsa_lane_sources():
    half = DSA_HEAD_DIM // ROPE_FRACTION // 2
    assert half == ROPE_LANES
    cut = 2 * half + ROPE_PAIR_SHIFT - half
    return (list(range(half)) + list(range(2 * half, cut)) + list(range(half, 2 * half))
            + list(range(cut, DSA_HEAD_DIM)))


def _idx_lane_sources():
    half = IDX_DIM // ROPE_FRACTION // 2
    assert 2 * half == ROPE_LANES

    def a(lo, hi):
        return list(range(lo, hi))

    def b(lo, hi):
        return list(range(IDX_DIM + lo, IDX_DIM + hi))

    return (a(0, half) + b(0, half) + a(2 * half, IDX_DIM)
            + a(half, 2 * half) + b(half, 2 * half) + b(2 * half, IDX_DIM))


def _weight_plan(d_in):
    o_gr = 2 * N_GQ + 2 * N_GV
    o_dq = o_gr + GLA_GATE_RANK
    o_dk = o_dq + N_DSA
    o_dv = o_dk + N_DSA
    o_iq = o_dv + N_DSA
    o_ik = o_iq + N_IQ
    o_iw = o_ik + IDX_DIM
    assert o_iw + IDX_HEADS == d_in
    half = IDX_DIM // ROPE_FRACTION // 2
    none = [-1]

    def shifted(base, sources):
        return [base + s for s in sources]

    dsa, idx, ident = _dsa_lane_sources(), _idx_lane_sources(), list(range(LANES))
    tiles = {
        "rot": ([shifted(o_dq + LANES * h, dsa) for h in range(DSA_HEADS)]
                + [shifted(o_dk + LANES * h, dsa) for h in range(DSA_HEADS)]
                + [shifted(o_iq + LANES * p, idx) for p in range(IDX_HEADS // 2)]),
        "small": [
            (shifted(o_ik, range(half)) + none * half + shifted(o_ik, range(2 * half, IDX_DIM))
             + shifted(o_ik, range(half, 2 * half)) + none * (LANES - ROPE_PAIR_SHIFT - half)),
            (none * half + shifted(o_ik, range(half)) + none * (ROPE_PAIR_SHIFT - 2 * half) + none * half
             + shifted(o_ik, range(half, 2 * half)) + shifted(o_ik, range(2 * half, IDX_DIM))),
            (shifted(o_iw, range(IDX_HEADS)) + shifted(o_gr, range(GLA_GATE_RANK))
             + none * (LANES - IDX_HEADS - GLA_GATE_RANK)),
        ],
        "fm": ([shifted(o_dv + LANES * t, ident) for t in range(N_DSA // LANES)]
               + [shifted(o_iw, range(IDX_HEADS)) + none * (LANES - IDX_HEADS)]),
    }
    mats, plans = [], {}
    for name, tile_list in tiles.items():
        plans[name] = []
        for cols in tile_list:
            assert len(cols) == LANES
            parts = []
            for src_tile in sorted({c // LANES for c in cols if c >= 0}):
                m = np.zeros((LANES, LANES), np.float32)
                for dst, c in enumerate(cols):
                    if c >= 0 and c // LANES == src_tile:
                        m[c % LANES, dst] = 1.0
                for mat_id, known in enumerate(mats):
                    if np.array_equal(known, m):
                        break
                else:
                    mat_id = len(mats)
                    mats.append(m)
                parts.append((src_tile, mat_id))
            plans[name].append(parts)
    return plans, np.stack(mats)


def _prep_kernel(w_ref, p_ref, main_ref, small_ref, fm_ref, *, plans, n_plain, d_in):
    rows = w_ref.shape[0]
    lane = lax.broadcasted_iota(I32, (rows, LANES), 1)

    def src_tile(t):
        v = w_ref[:, t * LANES:(t + 1) * LANES]
        if (t + 1) * LANES > d_in:
            v = jnp.where(lane < d_in - t * LANES, v, jnp.zeros((), v.dtype))
        return v

    def gathered(parts):
        acc = None
        for t, mat_id in parts:
            y = jnp.dot(src_tile(t), p_ref[mat_id], preferred_element_type=F32)
            acc = y if acc is None else acc + y
        return acc.astype(BF16)

    main_ref[:, :n_plain] = w_ref[:, :n_plain]
    for i, parts in enumerate(plans["rot"]):
        main_ref[:, n_plain + i * LANES:n_plain + (i + 1) * LANES] = gathered(parts)
    for i, parts in enumerate(plans["small"]):
        small_ref[:, i * LANES:(i + 1) * LANES] = gathered(parts)
    for i, parts in enumerate(plans["fm"]):
        fm_ref[:, i * LANES:(i + 1) * LANES] = gathered(parts)


def _prep_weights(w_in):
    depth, d, d_in = w_in.shape
    plans, mats = _weight_plan(d_in)
    rb = min(256, d)
    n_plain = OFF_DQ
    d_pad = -(-d_in // LANES) * LANES
    n_fm = N_DSA + LANES
    kern = functools.partial(_prep_kernel, plans=plans, n_plain=n_plain, d_in=d_in)
    return pl.pallas_call(
        kern,
        grid=(depth, d // rb),
        in_specs=[
            pl.BlockSpec((None, rb, d_pad), lambda l, r: (l, r, 0)),
            pl.BlockSpec(mats.shape, lambda l, r: (0, 0, 0)),
        ],
        out_specs=[
            pl.BlockSpec((None, rb, N_MAIN), lambda l, r: (l, r, 0)),
            pl.BlockSpec((None, rb, N_SMALL), lambda l, r: (l, r, 0)),
            pl.BlockSpec((None, rb, n_fm), lambda l, r: (l, r, 0)),
        ],
        out_shape=[
            jax.ShapeDtypeStruct((depth, d, N_MAIN), BF16),
            jax.ShapeDtypeStruct((depth, d, N_SMALL), BF16),
            jax.ShapeDtypeStruct((depth, d, n_fm), BF16),
        ],
        compiler_params=_params("parallel", "parallel"),
        name="prep_weights",
    )(w_in, jnp.asarray(mats, BF16))


def kernel(x, positions, norm_mix_pre, w_in, gla_wa2, gla_ba, gla_norm, w_out, norm_mix_post,
           norm_ffn_pre, w_up, w_down, norm_ffn_post):
    batch, seq, d = x.shape
    depth = w_in.shape[0]
    assert seq % DSA_BLK == 0
    tab = _rope_tables(positions)
    w2 = jnp.pad(gla_wa2, ((0, 0), (IDX_HEADS, LANES - IDX_HEADS - GLA_GATE_RANK), (0, 0)))
    h = x.reshape(batch * seq, d)
    w_main, w_small, w_fm = _prep_weights(w_in.astype(BF16))
    w_out, w_up, w_down = (w.astype(BF16) for w in (w_out, w_up, w_down))
    for l in range(depth):
        main, ikd, misc, dvt, iwt = _inproj(h, norm_mix_pre[l][None, :], w_main, w_small, w_fm, tab, l)
        o_gla = _gla(main, misc, w2[l], gla_ba[l][None, :], gla_norm[l][None, :], batch, seq)
        o_dsa = _dsa(main, ikd, iwt, dvt, batch, seq)
        h = _outproj(o_gla, o_dsa, h, w_out, norm_mix_post[l][None, :], l)
        h = _ffn(h, norm_ffn_pre[l][None, :], w_up, w_down, norm_ffn_post[l][None, :], l)
    return h.reshape(batch, seq, d)
```

```python
import functools

import jax
import jax.numpy as jnp
import numpy as np
from jax import lax
from jax.experimental import pallas as pl
from jax.experimental.pallas import tpu as pltpu

F32 = jnp.float32
BF16 = jnp.bfloat16
I32 = jnp.int32
I16 = jnp.int16

GLA_HEADS = 4
GLA_DK = 128
GLA_DV = 256
GLA_GATE_RANK = 16
GLA_GATE_TAU = 16.0
GLA_CHUNK = 64
DSA_HEADS = 8
DSA_HEAD_DIM = 128
IDX_HEADS = 16
IDX_DIM = 64
DSA_TOPK_MAX = 256
ROPE_THETA = 500000.0
ROPE_FRACTION = 4
EPS = 1e-6
NEG_INF = -1e30
INT_MIN = -(2 ** 31)

LANES = 128
N_GQ = GLA_HEADS * GLA_DK
N_GV = GLA_HEADS * GLA_DV
N_DSA = DSA_HEADS * DSA_HEAD_DIM
N_IQ = IDX_HEADS * IDX_DIM
OFF_GQ = 0
OFF_GK = OFF_GQ + N_GQ
OFF_GV = OFF_GK + N_GQ
OFF_GG = OFF_GV + N_GV
OFF_DQ = OFF_GG + N_GV
OFF_DK = OFF_DQ + N_DSA
OFF_IQ = OFF_DK + N_DSA
N_MAIN = OFF_IQ + N_IQ
N_SMALL = 3 * LANES
DSA_BLK = 256
ROPE_LANES = 16
ROPE_PAIR_SHIFT = LANES // 2
TAB_DSA, TAB_IDX = range(2)
INPROJ_ROW_GROUPS = 4
COUNT_ROWS = 32
COUNT16_ROWS = 64
BF16_EXACT_INT = 256
VMEM_LIMIT = 56 * 1024 * 1024

NT_DIMS = (((1,), (1,)), ((), ()))
TN_DIMS = (((0,), (0,)), ((), ()))


def _params(*sem):
    return pltpu.CompilerParams(dimension_semantics=sem, vmem_limit_bytes=VMEM_LIMIT)


def _rope_cs(tab, scale, rot):
    lane = lax.broadcasted_iota(I32, (1, LANES), 1)
    lo = lane < ROPE_LANES
    hi = (lane >= ROPE_PAIR_SHIFT) & (lane < ROPE_PAIR_SHIFT + ROPE_LANES)
    swapped = pltpu.roll(tab, ROPE_PAIR_SHIFT, 1)
    c = jnp.where(hi, swapped, tab)
    s = jnp.where(lo, -swapped, jnp.where(hi, tab, 0.0))
    return (1.0 + rot * (c - 1.0)) * scale, s * (rot * scale)


def _rope(x, c, s):
    parts = []
    for g in range(x.shape[1] // LANES):
        xg = x[:, g * LANES:(g + 1) * LANES]
        parts.append(xg * c + pltpu.roll(xg, ROPE_PAIR_SHIFT, 1) * s)
    return parts[0] if len(parts) == 1 else jnp.concatenate(parts, axis=1)


def _inproj_kernel(x_ref, g_ref, w_ref, ws_ref, wt_ref, tab_ref, tabk_ref,
                   main_ref, ikd_ref, misc_ref, dvt_ref, iwt_ref, xn_ref, *, j_dq, j_dk, j_iq):
    j = pl.program_id(1)

    @pl.when(j == 0)
    def _():
        x = x_ref[...]
        ms = jnp.mean(x * x, axis=-1, keepdims=True)
        xn_ref[...] = (x * lax.rsqrt(ms + EPS) * g_ref[...]).astype(BF16)
        small = lax.dot_general(xn_ref[...], ws_ref[...], NT_DIMS, preferred_element_type=F32)
        ikd_ref[...] = _rope(small[:, :2 * LANES], *_rope_cs(tabk_ref[...], 1.0, 1.0)).astype(BF16)
        misc_ref[...] = small[:, 2 * LANES:]
        fm = lax.dot_general(wt_ref[...], xn_ref[...], NT_DIMS, preferred_element_type=F32)
        for ci in range(dvt_ref.shape[0]):
            dvt_ref[ci] = fm[:N_DSA, ci * DSA_BLK:(ci + 1) * DSA_BLK].astype(BF16)
        iwt_ref[...] = fm[N_DSA:, :]

    rot = jnp.where(j >= j_dq, 1.0, 0.0)
    scale = jnp.where(j < j_dq, 1.0,
                      jnp.where(j < j_dk, DSA_HEAD_DIM ** -0.5, jnp.where(j < j_iq, 1.0, IDX_DIM ** -0.5)))
    c, s = _rope_cs(tab_ref[...], scale, rot)
    rows = xn_ref.shape[0] // INPROJ_ROW_GROUPS
    for g in range(INPROJ_ROW_GROUPS):
        rs = slice(g * rows, (g + 1) * rows)
        acc = lax.dot_general(xn_ref[rs, :], w_ref[...], NT_DIMS, preferred_element_type=F32)
        main_ref[rs, :] = _rope(acc, c[rs], s[rs]).astype(BF16)


def _inproj(h, gain, w_main, w_small, w_t, tab, layer):
    t, d = h.shape
    tm = min(1024, t)
    tn = 512
    nt = w_t.shape[1]
    j_dq, j_dk, j_iq = OFF_DQ // tn, OFF_DK // tn, OFF_IQ // tn
    kern = functools.partial(_inproj_kernel, j_dq=j_dq, j_dk=j_dk, j_iq=j_iq)
    return pl.pallas_call(
        kern,
        grid=(t // tm, N_MAIN // tn),
        in_specs=[
            pl.BlockSpec((tm, d), lambda i, j: (i, 0)),
            pl.BlockSpec((1, d), lambda i, j: (0, 0)),
            pl.BlockSpec((None, tn, d), lambda i, j: (layer, j, 0)),
            pl.BlockSpec((None, N_SMALL, d), lambda i, j: (layer, 0, 0), pipeline_mode=pl.Buffered(1)),
            pl.BlockSpec((None, nt, d), lambda i, j: (layer, 0, 0), pipeline_mode=pl.Buffered(1)),
            pl.BlockSpec((None, tm, LANES), lambda i, j: ((j >= j_iq).astype(I32), i, 0)),
            pl.BlockSpec((None, tm, LANES), lambda i, j: (TAB_IDX, i, 0)),
        ],
        out_specs=[
            pl.BlockSpec((tm, tn), lambda i, j: (i, j)),
            pl.BlockSpec((tm, 2 * LANES), lambda i, j: (i, 0)),
            pl.BlockSpec((tm, LANES), lambda i, j: (i, 0)),
            pl.BlockSpec((tm // DSA_BLK, N_DSA, DSA_BLK), lambda i, j: (i, 0, 0)),
            pl.BlockSpec((LANES, tm), lambda i, j: (0, i)),
        ],
        out_shape=[
            jax.ShapeDtypeStruct((t, N_MAIN), BF16),
            jax.ShapeDtypeStruct((t, 2 * LANES), BF16),
            jax.ShapeDtypeStruct((t, LANES), F32),
            jax.ShapeDtypeStruct((t // DSA_BLK, N_DSA, DSA_BLK), BF16),
            jax.ShapeDtypeStruct((LANES, t), F32),
        ],
        scratch_shapes=[pltpu.VMEM((tm, d), BF16)],
        compiler_params=_params("parallel", "arbitrary"),
        name="inproj",
    )(h, gain, w_main, w_small, w_t, tab, tab)


def _gla_kernel(q_ref, k_ref, v_ref, gg_ref, misc_ref, w2_ref, ba_ref, gn_ref, o_ref, state_ref, *, blk):
    c_len = GLA_CHUNK
    n_ch = blk // c_len
    shift = c_len.bit_length() - 1
    assert 1 << shift == c_len

    @pl.when(pl.program_id(1) == 0)
    def _():
        state_ref[...] = jnp.zeros_like(state_ref)

    gpre = jnp.dot(misc_ref[...], w2_ref[...], preferred_element_type=F32,
                   precision=lax.Precision.HIGHEST) + ba_ref[...]
    log_a = (jnp.minimum(gpre, 0.0) - jnp.log1p(jnp.exp(-jnp.abs(gpre)))) * (1.0 / GLA_GATE_TAU)

    r = lax.broadcasted_iota(I32, (blk, blk), 0)
    c = lax.broadcasted_iota(I32, (blk, blk), 1)
    causal = (c <= r) & ((r >> shift) == (c >> shift))
    b = jnp.dot(causal.astype(F32), log_a, preferred_element_type=F32, precision=lax.Precision.HIGHEST)
    b3 = b.reshape(n_ch, c_len, N_GQ)
    b_last = b3[:, c_len - 1:c_len, :]
    decay = jnp.exp(b_last)
    k = k_ref[...].astype(F32)
    q_dec = (q_ref[...].astype(F32) * (GLA_DK ** -0.5) * jnp.exp(b)).astype(BF16)
    k_inv = (k * jnp.exp(-b)).astype(BF16)
    k_dec = (k * jnp.exp(b_last - b3).reshape(blk, N_GQ)).astype(BF16)
    gn = gn_ref[...]

    for hd in range(GLA_HEADS):
        ck = slice(hd * GLA_DK, (hd + 1) * GLA_DK)
        cv = slice(hd * GLA_DV, (hd + 1) * GLA_DV)
        attn = lax.dot_general(q_dec[:, ck], k_inv[:, ck], NT_DIMS, preferred_element_type=F32)
        attn = jnp.where(causal, attn, 0.0).astype(BF16)
        o = jnp.dot(attn, v_ref[:, cv], preferred_element_type=F32)
        st = state_ref[hd]
        inter = []
        for ci in range(n_ch):
            rows = slice(ci * c_len, (ci + 1) * c_len)
            inter.append(lax.dot_general(q_dec[rows, ck], st.astype(BF16), NT_DIMS,
                                         preferred_element_type=F32))
            d_st = lax.dot_general(v_ref[rows, cv], k_dec[rows, ck], TN_DIMS, preferred_element_type=F32)
            st = st * decay[ci, :, ck] + d_st
        state_ref[hd] = st
        o = o + jnp.concatenate(inter, axis=0)
        ms = jnp.mean(o * o, axis=-1, keepdims=True)
        y = o * lax.rsqrt(ms + EPS) * gn
        gate = gg_ref[:, cv].astype(F32)
        o_ref[:, cv] = (y * (gate / (1.0 + jnp.exp(-gate)))).astype(BF16)


def _gla(main, misc, w2, ba, gn, batch, seq):
    t = batch * seq
    blk = min(256, seq)
    nb = seq // blk
    kern = functools.partial(_gla_kernel, blk=blk)

    def rowmap(col):
        return lambda b, n: (b * nb + n, col)

    return pl.pallas_call(
        kern,
        grid=(batch, nb),
        in_specs=[
            pl.BlockSpec((blk, N_GQ), rowmap(OFF_GQ // N_GQ)),
            pl.BlockSpec((blk, N_GQ), rowmap(OFF_GK // N_GQ)),
            pl.BlockSpec((blk, N_GV), rowmap(OFF_GV // N_GV)),
            pl.BlockSpec((blk, N_GV), rowmap(OFF_GG // N_GV)),
            pl.BlockSpec((blk, LANES), rowmap(0)),
            pl.BlockSpec((LANES, N_GQ), lambda b, n: (0, 0)),
            pl.BlockSpec((1, N_GQ), lambda b, n: (0, 0)),
            pl.BlockSpec((1, GLA_DV), lambda b, n: (0, 0)),
        ],
        out_specs=pl.BlockSpec((blk, N_GV), rowmap(0)),
        out_shape=jax.ShapeDtypeStruct((t, N_GV), BF16),
        scratch_shapes=[pltpu.VMEM((GLA_HEADS, GLA_DV, GLA_DK), F32)],
        compiler_params=_params("parallel", "arbitrary"),
        name="gla",
    )(main, main, main, main, misc, w2, ba, gn)


def _dsa_kernel(q_ref, iq_ref, iwt_ref, k_ref, vt_ref, ikd_ref, o_ref,
                key_ref, hi_ref, lo_ref, bias_ref, acc_ref, p_ref, m_ref, l_ref, alpha_ref,
                *, blk, topk, idx_bits):
    qi = pl.program_id(1)
    nch = qi + 1
    krow = lax.broadcasted_iota(I32, (blk, blk), 0)
    qcol = lax.broadcasted_iota(I32, (blk, blk), 1)
    w_idx = iwt_ref[0:IDX_HEADS, :] * (IDX_HEADS ** -0.5)

    def score_body(c, carry):
        ks = pl.multiple_of(c * blk, blk)
        k_lo = ikd_ref[pl.ds(ks, blk), 0:LANES]
        k_hi = ikd_ref[pl.ds(ks, blk), LANES:2 * LANES]
        acc = jnp.zeros((blk, blk), F32)
        for p in range(IDX_HEADS // 2):
            iq_p = iq_ref[:, p * LANES:(p + 1) * LANES]
            l0 = lax.dot_general(k_lo, iq_p, NT_DIMS, preferred_element_type=F32)
            l1 = lax.dot_general(k_hi, iq_p, NT_DIMS, preferred_element_type=F32)
            acc = acc + w_idx[2 * p:2 * p + 1, :] * jnp.maximum(l0, 0.0)
            acc = acc + w_idx[2 * p + 1:2 * p + 2, :] * jnp.maximum(l1, 0.0)
        score = jnp.where(krow + (c - qi) * blk > qcol, NEG_INF, acc)
        score = jnp.where(score == 0.0, 0.0, score)
        bits = pltpu.bitcast(score, I32)
        key = bits ^ ((bits >> 31) & 0x7FFFFFFF)
        key_ref[c] = key
        hi_ref[c] = (key >> 16).astype(I16)
        return carry

    lax.fori_loop(0, nch, score_body, 0)

    def count16(ref, cand):
        def body(c, acc):
            m = jnp.where(ref[c] >= cand, jnp.ones((), BF16), jnp.zeros((), BF16))
            for i in range(blk // COUNT16_ROWS):
                acc = acc + m[i * COUNT16_ROWS:(i + 1) * COUNT16_ROWS]
            return acc
        acc = lax.fori_loop(0, nch, body, jnp.zeros((COUNT16_ROWS, blk), BF16))
        return jnp.sum(acc.astype(F32), axis=0, keepdims=True)

    def kth_largest16(ref, kth):
        v = jnp.where(count16(ref, jnp.zeros((1, blk), I16)) >= kth, 0, -(2 ** 15)).astype(I32)

        def body(i, v):
            cand = v | lax.shift_left(jnp.int32(1), 14 - i)
            return jnp.where(count16(ref, cand.astype(I16)) >= kth, cand, v)

        return lax.fori_loop(0, 15, body, v)

    def count(pred):
        def body(c, acc):
            m = jnp.where(pred(key_ref[c], c), 1.0, 0.0)
            return acc + jnp.sum(m.reshape(blk // COUNT_ROWS, COUNT_ROWS, blk), axis=0)
        acc = lax.fori_loop(0, nch, body, jnp.zeros((COUNT_ROWS, blk), F32))
        return jnp.sum(acc, axis=0, keepdims=True)

    kf = float(topk)
    thr_hi = kth_largest16(hi_ref, kf)
    n_above = count16(hi_ref, (thr_hi + 1).astype(I16))
    n_above = jnp.where(thr_hi == 2 ** 15 - 1, 0.0, n_above)

    def low_body(c, carry):
        key = key_ref[c]
        low = (key & 0xFFFF) - 2 ** 15
        lo_ref[c] = jnp.where((key >> 16) == thr_hi, low, -(2 ** 15)).astype(I16)
        return carry

    lax.fori_loop(0, nch, low_body, 0)
    thr_lo = kth_largest16(lo_ref, kf - n_above)
    thr = thr_hi * 2 ** 16 + (thr_lo + 2 ** 15)

    n_ge = count(lambda kc, c: kc >= thr)

    def resolve_ties():
        need = kf - count(lambda kc, c: kc > thr)

        def tie_body(i, last):
            cand = last | lax.shift_left(jnp.int32(1), idx_bits - 1 - i)
            below = count(lambda kc, c: (kc == thr) & (c * blk + krow < cand))
            return jnp.where(below < need, cand, last)

        return lax.fori_loop(0, idx_bits, tie_body, jnp.zeros((1, blk), I32))

    last = lax.cond(jnp.max(n_ge) > kf, resolve_ties,
                    lambda: jnp.full((1, blk), 2 ** idx_bits, I32))

    def write_bias(c, diagonal):
        kc = key_ref[c]
        tie_pos = jnp.where(kc == thr, c * blk + krow, -1)
        bias = jnp.where(kc >= thr, jnp.where(tie_pos > last, NEG_INF, 0.0), NEG_INF)
        if diagonal:
            bias = jnp.where(krow > qcol, NEG_INF, bias)
        bias_ref[c] = bias

    def bias_body(c, carry):
        write_bias(c, False)
        return carry

    lax.fori_loop(0, qi, bias_body, 0)
    write_bias(qi, True)

    heads = tuple((hd, slice(hd * DSA_HEAD_DIM, (hd + 1) * DSA_HEAD_DIM)) for hd in range(DSA_HEADS))

    def fold(x, op):
        return op(x.reshape(blk // 8, 8, blk), axis=0)

    m_ref[...] = jnp.full_like(m_ref, NEG_INF)
    l_ref[...] = jnp.zeros_like(l_ref)
    acc_ref[...] = jnp.zeros_like(acc_ref)

    def attn_body(c, carry):
        ks = pl.multiple_of(c * blk, blk)

        @pl.when(c >= 0)
        def _():
            for hd, hs in heads:
                s = lax.dot_general(k_ref[pl.ds(ks, blk), hs], q_ref[:, hs], NT_DIMS,
                                    preferred_element_type=F32) + bias_ref[c]
                m_prev = m_ref[hd, 0:1, :]
                m_new = jnp.maximum(m_prev, jnp.max(fold(s, jnp.max), axis=0, keepdims=True))
                alpha = jnp.exp(m_prev - m_new)
                m_ref[hd, 0:1, :] = m_new
                alpha_ref[hd, 0:1, :] = alpha
                p = jnp.exp(s - m_new)
                l_ref[hd] = alpha * l_ref[hd] + fold(p, jnp.sum)
                p_ref[hd] = p.astype(BF16)

        @pl.when(c >= 0)
        def _():
            for hd, hs in heads:
                pv = jnp.dot(vt_ref[c, hs, :], p_ref[hd], preferred_element_type=F32)
                acc_ref[hs, :] = alpha_ref[hd, 0:1, :] * acc_ref[hs, :] + pv

        return carry

    lax.fori_loop(0, nch, attn_body, 0)

    for hd, hs in heads:
        o_t = acc_ref[hs, :] / jnp.sum(l_ref[hd], axis=0, keepdims=True)
        o_ref[:, hs] = o_t.T.astype(BF16)


def _dsa(main, ikd, iwt, dvt, batch, seq):
    t = batch * seq
    blk = DSA_BLK
    nq = seq // blk
    topk = min(DSA_TOPK_MAX, seq // 4)
    idx_bits = max(1, (seq - 1).bit_length())
    assert (blk // COUNT16_ROWS) * nq <= BF16_EXACT_INT
    kern = functools.partial(_dsa_kernel, blk=blk, topk=topk, idx_bits=idx_bits)

    def qmap(col):
        return lambda b, i: (b * nq + i, col)

    return pl.pallas_call(
        kern,
        grid=(batch, nq),
        in_specs=[
            pl.BlockSpec((blk, N_DSA), qmap(OFF_DQ // N_DSA)),
            pl.BlockSpec((blk, N_IQ), qmap(OFF_IQ // N_IQ)),
            pl.BlockSpec((LANES, blk), lambda b, i: (0, b * nq + i)),
            pl.BlockSpec((seq, N_DSA), lambda b, i: (b, OFF_DK // N_DSA)),
            pl.BlockSpec((nq, N_DSA, blk), lambda b, i: (b, 0, 0)),
            pl.BlockSpec((seq, 2 * LANES), lambda b, i: (b, 0)),
        ],
        out_specs=pl.BlockSpec((blk, N_DSA), qmap(0)),
        out_shape=jax.ShapeDtypeStruct((t, N_DSA), BF16),
        scratch_shapes=[
            pltpu.VMEM((nq, blk, blk), I32),
            pltpu.VMEM((nq, blk, blk), I16),
            pltpu.VMEM((nq, blk, blk), I16),
            pltpu.VMEM((nq, blk, blk), F32),
            pltpu.VMEM((N_DSA, blk), F32),
            pltpu.VMEM((DSA_HEADS, blk, blk), BF16),
            pltpu.VMEM((DSA_HEADS, 8, blk), F32),
            pltpu.VMEM((DSA_HEADS, 8, blk), F32),
            pltpu.VMEM((DSA_HEADS, 8, blk), F32),
        ],
        compiler_params=_params("parallel", "arbitrary"),
        name="dsa",
    )(main, main, iwt, main, dvt, ikd)


def _outproj_kernel(og_ref, od_ref, h_ref, w_ref, g_ref, o_ref):
    m = jnp.dot(og_ref[...], w_ref[:N_GV, :], preferred_element_type=F32)
    m = m + jnp.dot(od_ref[...], w_ref[N_GV:, :], preferred_element_type=F32)
    ms = jnp.mean(m * m, axis=-1, keepdims=True)
    o_ref[...] = h_ref[...] + m * lax.rsqrt(ms + EPS) * g_ref[...]


def _outproj(o_gla, o_dsa, h, w_out, gain, layer):
    t, d = h.shape
    tm = min(512, t)
    return pl.pallas_call(
        _outproj_kernel,
        grid=(t // tm,),
        in_specs=[
            pl.BlockSpec((tm, N_GV), lambda i: (i, 0)),
            pl.BlockSpec((tm, N_DSA), lambda i: (i, 0)),
            pl.BlockSpec((tm, d), lambda i: (i, 0)),
            pl.BlockSpec((None, N_GV + N_DSA, d), lambda i: (layer, 0, 0)),
            pl.BlockSpec((1, d), lambda i: (0, 0)),
        ],
        out_specs=pl.BlockSpec((tm, d), lambda i: (i, 0)),
        out_shape=jax.ShapeDtypeStruct((t, d), F32),
        compiler_params=_params("parallel"),
        name="outproj",
    )(o_gla, o_dsa, h, w_out, gain)


def _ffn_kernel(h_ref, gpre_ref, wu_ref, wd_ref, gpost_ref, o_ref, xn_ref, acc_ref):
    f = pl.program_id(1)

    @pl.when(f == 0)
    def _():
        x = h_ref[...]
        ms = jnp.mean(x * x, axis=-1, keepdims=True)
        xn_ref[...] = (x * lax.rsqrt(ms + EPS) * gpre_ref[...]).astype(BF16)
        acc_ref[...] = jnp.zeros_like(acc_ref)

    u = jnp.maximum(jnp.dot(xn_ref[...], wu_ref[...], preferred_element_type=F32), 0.0)
    acc_ref[...] += jnp.dot((u * u).astype(BF16), wd_ref[...], preferred_element_type=F32)

    @pl.when(f == pl.num_programs(1) - 1)
    def _():
        y = acc_ref[...]
        ms = jnp.mean(y * y, axis=-1, keepdims=True)
        o_ref[...] = h_ref[...] + y * lax.rsqrt(ms + EPS) * gpost_ref[...]


def _ffn(h, g_pre, w_up, w_down, g_post, layer):
    t, d = h.shape
    d_ff = w_up.shape[2]
    tm = min(512, t)
    tf = 1024
    return pl.pallas_call(
        _ffn_kernel,
        grid=(t // tm, d_ff // tf),
        in_specs=[
            pl.BlockSpec((tm, d), lambda i, f: (i, 0)),
            pl.BlockSpec((1, d), lambda i, f: (0, 0)),
            pl.BlockSpec((None, d, tf), lambda i, f: (layer, 0, f)),
            pl.BlockSpec((None, tf, d), lambda i, f: (layer, f, 0)),
            pl.BlockSpec((1, d), lambda i, f: (0, 0)),
        ],
        out_specs=pl.BlockSpec((tm, d), lambda i, f: (i, 0)),
        out_shape=jax.ShapeDtypeStruct((t, d), F32),
        scratch_shapes=[pltpu.VMEM((tm, d), BF16), pltpu.VMEM((tm, d), F32)],
        compiler_params=_params("parallel", "arbitrary"),
        name="ffn",
    )(h, g_pre, w_up, w_down, g_post)


def _rope_table(positions, head_dim):
    r = head_dim // ROPE_FRACTION
    n_freq = r // 2
    inv_freq = ROPE_THETA ** (-(jnp.arange(0, r, 2, dtype=F32) / r))
    rep = ROPE_LANES // n_freq
    t = positions.size
    ang = jnp.tile(inv_freq, rep)[:, None] * positions.reshape(1, t).astype(F32)
    cos = jnp.cos(ang).T
    sin = jnp.sin(ang).T
    one = jnp.ones((cos.shape[0], ROPE_PAIR_SHIFT - ROPE_LANES), F32)
    return jnp.concatenate([cos, one, sin, one], axis=1)


def _rope_tables(positions):
    return jnp.stack([_rope_table(positions, DSA_HEAD_DIM), _rope_table(positions, IDX_DIM)])


def _dsa_lane_sources():
    half = DSA_HEAD_DIM // ROPE_FRACTION // 2
    assert half == ROPE_LANES
    cut = 2 * half + ROPE_PAIR_SHIFT - half
    return (list(range(half)) + list(range(2 * half, cut)) + list(range(half, 2 * half))
            + list(range(cut, DSA_HEAD_DIM)))


def _idx_lane_sources():
    half = IDX_DIM // ROPE_FRACTION // 2
    assert 2 * half == ROPE_LANES

    def a(lo, hi):
        return list(range(lo, hi))

    def b(lo, hi):
        return list(range(IDX_DIM + lo, IDX_DIM + hi))

    return (a(0, half) + b(0, half) + a(2 * half, IDX_DIM)
            + a(half, 2 * half) + b(half, 2 * half) + b(2 * half, IDX_DIM))


def _weight_plan(d_in):
    o_gr = 2 * N_GQ + 2 * N_GV
    o_dq = o_gr + GLA_GATE_RANK
    o_dk = o_dq + N_DSA
    o_dv = o_dk + N_DSA
    o_iq = o_dv + N_DSA
    o_ik = o_iq + N_IQ
    o_iw = o_ik + IDX_DIM
    assert o_iw + IDX_HEADS == d_in
    half = IDX_DIM // ROPE_FRACTION // 2
    none = [-1]

    def shifted(base, sources):
        return [base + s for s in sources]

    dsa, idx, ident = _dsa_lane_sources(), _idx_lane_sources(), list(range(LANES))
    tiles = {
        "rot": ([shifted(o_dq + LANES * h, dsa) for h in range(DSA_HEADS)]
                + [shifted(o_dk + LANES * h, dsa) for h in range(DSA_HEADS)]
                + [shifted(o_iq + LANES * p, idx) for p in range(IDX_HEADS // 2)]),
        "small": [
            (shifted(o_ik, range(half)) + none * half + shifted(o_ik, range(2 * half, IDX_DIM))
             + shifted(o_ik, range(half, 2 * half)) + none * (LANES - ROPE_PAIR_SHIFT - half)),
            (none * half + shifted(o_ik, range(half)) + none * (ROPE_PAIR_SHIFT - 2 * half) + none * half
             + shifted(o_ik, range(half, 2 * half)) + shifted(o_ik, range(2 * half, IDX_DIM))),
            (shifted(o_iw, range(IDX_HEADS)) + shifted(o_gr, range(GLA_GATE_RANK))
             + none * (LANES - IDX_HEADS - GLA_GATE_RANK)),
        ],
        "fm": ([shifted(o_dv + LANES * t, ident) for t in range(N_DSA // LANES)]
               + [shifted(o_iw, range(IDX_HEADS)) + none * (LANES - IDX_HEADS)]),
    }
    mats, plans = [], {}
    for name, tile_list in tiles.items():
        plans[name] = []
        for cols in tile_list:
            assert len(cols) == LANES
            parts = []
            for src_tile in sorted({c // LANES for c in cols if c >= 0}):
                m = np.zeros((LANES, LANES), np.float32)
                for dst, c in enumerate(cols):
                    if c >= 0 and c // LANES == src_tile:
                        m[c % LANES, dst] = 1.0
                for mat_id, known in enumerate(mats):
                    if np.array_equal(known, m):
                        break
                else:
                    mat_id = len(mats)
                    mats.append(m)
                parts.append((src_tile, mat_id))
            plans[name].append(parts)
    return plans, np.stack(mats)


def _prep_kernel(w_ref, p_ref, main_ref, small_ref, fm_ref, *, plans, n_plain, d_in):
    cols = w_ref.shape[1]

    def src_tile(t):
        if (t + 1) * LANES > d_in:
            v = w_ref[t * LANES:d_in, :]
            return jnp.concatenate([v, jnp.zeros(((t + 1) * LANES - d_in, cols), v.dtype)], axis=0)
        return w_ref[t * LANES:(t + 1) * LANES, :]

    def gathered(parts):
        acc = None
        for t, mat_id in parts:
            y = jnp.dot(p_ref[mat_id], src_tile(t), preferred_element_type=F32)
            acc = y if acc is None else acc + y
        return acc.astype(BF16)

    main_ref[:n_plain, :] = w_ref[:n_plain, :]
    for i, parts in enumerate(plans["rot"]):
        main_ref[n_plain + i * LANES:n_plain + (i + 1) * LANES, :] = gathered(parts)
    for i, parts in enumerate(plans["small"]):
        small_ref[i * LANES:(i + 1) * LANES, :] = gathered(parts)
    for i, parts in enumerate(plans["fm"]):
        fm_ref[i * LANES:(i + 1) * LANES, :] = gathered(parts)


def _prep_weights(w_t):
    depth, d_in, d = w_t.shape
    plans, mats = _weight_plan(d_in)
    cb = min(256, d)
    n_plain = OFF_DQ
    n_fm = N_DSA + LANES
    kern = functools.partial(_prep_kernel, plans=plans, n_plain=n_plain, d_in=d_in)
    return pl.pallas_call(
        kern,
        grid=(depth, d // cb),
        in_specs=[
            pl.BlockSpec((None, d_in, cb), lambda l, r: (l, 0, r)),
            pl.BlockSpec(mats.shape, lambda l, r: (0, 0, 0)),
        ],
        out_specs=[
            pl.BlockSpec((None, N_MAIN, cb), lambda l, r: (l, 0, r)),
            pl.BlockSpec((None, N_SMALL, cb), lambda l, r: (l, 0, r)),
            pl.BlockSpec((None, n_fm, cb), lambda l, r: (l, 0, r)),
        ],
        out_shape=[
            jax.ShapeDtypeStruct((depth, N_MAIN, d), BF16),
            jax.ShapeDtypeStruct((depth, N_SMALL, d), BF16),
            jax.ShapeDtypeStruct((depth, n_fm, d), BF16),
        ],
        compiler_params=_params("parallel", "parallel"),
        name="prep_weights",
    )(w_t, jnp.asarray(mats.transpose(0, 2, 1), BF16))


def kernel(x, positions, norm_mix_pre, w_in, gla_wa2, gla_ba, gla_norm, w_out, norm_mix_post,
           norm_ffn_pre, w_up, w_down, norm_ffn_post):
    batch, seq, d = x.shape
    depth = w_in.shape[0]
    assert seq % DSA_BLK == 0
    tab = _rope_tables(positions)
    w2 = jnp.pad(gla_wa2, ((0, 0), (IDX_HEADS, LANES - IDX_HEADS - GLA_GATE_RANK), (0, 0)))
    h = x.reshape(batch * seq, d)
    w_main, w_small, w_fm = _prep_weights(jnp.swapaxes(w_in, 1, 2).astype(BF16))
    w_out, w_up, w_down = (w.astype(BF16) for w in (w_out, w_up, w_down))
    for l in range(depth):
        main, ikd, misc, dvt, iwt = _inproj(h, norm_mix_pre[l][None, :], w_main, w_small, w_fm, tab, l)
        o_gla = _gla(main, misc, w2[l], gla_ba[l][None, :], gla_norm[l][None, :], batch, seq)
        o_dsa = _dsa(main, ikd, iwt, dvt, batch, seq)
        h = _outproj(o_gla, o_dsa, h, w_out, norm_mix_post[l][None, :], l)
        h = _ffn(h, norm_ffn_pre[l][None, :], w_up, w_down, norm_ffn_post[l][None, :], l)
    return h.reshape(batch, seq, d)
```

```python
import functools

import jax
import jax.numpy as jnp
import numpy as np
from jax import lax
from jax.experimental import pallas as pl
from jax.experimental.pallas import tpu as pltpu

F32 = jnp.float32
BF16 = jnp.bfloat16
I32 = jnp.int32
I16 = jnp.int16

GLA_HEADS = 4
GLA_DK = 128
GLA_DV = 256
GLA_GATE_RANK = 16
GLA_GATE_TAU = 16.0
GLA_CHUNK = 64
DSA_HEADS = 8
DSA_HEAD_DIM = 128
IDX_HEADS = 16
IDX_DIM = 64
DSA_TOPK_MAX = 256
ROPE_THETA = 500000.0
ROPE_FRACTION = 4
EPS = 1e-6
NEG_INF = -1e30
INT_MIN = -(2 ** 31)

LANES = 128
N_GQ = GLA_HEADS * GLA_DK
N_GV = GLA_HEADS * GLA_DV
N_DSA = DSA_HEADS * DSA_HEAD_DIM
N_IQ = IDX_HEADS * IDX_DIM
OFF_GQ = 0
OFF_GK = OFF_GQ + N_GQ
OFF_GV = OFF_GK + N_GQ
OFF_GG = OFF_GV + N_GV
OFF_DQ = OFF_GG + N_GV
OFF_DK = OFF_DQ + N_DSA
OFF_IQ = OFF_DK + N_DSA
N_MAIN = OFF_IQ + N_IQ
N_SMALL = 3 * LANES
DSA_BLK = 256
ROPE_LANES = 16
ROPE_PAIR_SHIFT = LANES // 2
TAB_DSA, TAB_IDX = range(2)
INPROJ_ROW_GROUPS = 4
COUNT_ROWS = 32
COUNT16_ROWS = 64
BF16_EXACT_INT = 256
VMEM_LIMIT = 56 * 1024 * 1024

NT_DIMS = (((1,), (1,)), ((), ()))
TN_DIMS = (((0,), (0,)), ((), ()))


def _params(*sem):
    return pltpu.CompilerParams(dimension_semantics=sem, vmem_limit_bytes=VMEM_LIMIT)


def _rope_cs(tab, scale, rot):
    lane = lax.broadcasted_iota(I32, (1, LANES), 1)
    lo = lane < ROPE_LANES
    hi = (lane >= ROPE_PAIR_SHIFT) & (lane < ROPE_PAIR_SHIFT + ROPE_LANES)
    swapped = pltpu.roll(tab, ROPE_PAIR_SHIFT, 1)
    c = jnp.where(hi, swapped, tab)
    s = jnp.where(lo, -swapped, jnp.where(hi, tab, 0.0))
    return (1.0 + rot * (c - 1.0)) * scale, s * (rot * scale)


def _rope(x, c, s):
    parts = []
    for g in range(x.shape[1] // LANES):
        xg = x[:, g * LANES:(g + 1) * LANES]
        parts.append(xg * c + pltpu.roll(xg, ROPE_PAIR_SHIFT, 1) * s)
    return parts[0] if len(parts) == 1 else jnp.concatenate(parts, axis=1)


def _inproj_kernel(x_ref, g_ref, w_ref, ws_ref, wt_ref, tab_ref, tabk_ref,
                   main_ref, ikd_ref, misc_ref, dvt_ref, iwt_ref, xn_ref, *, j_dq, j_dk, j_iq):
    j = pl.program_id(1)

    @pl.when(j == 0)
    def _():
        x = x_ref[...]
        ms = jnp.mean(x * x, axis=-1, keepdims=True)
        xn_ref[...] = (x * lax.rsqrt(ms + EPS) * g_ref[...]).astype(BF16)
        small = lax.dot_general(xn_ref[...], ws_ref[...], NT_DIMS, preferred_element_type=F32)
        ikd_ref[...] = _rope(small[:, :2 * LANES], *_rope_cs(tabk_ref[...], 1.0, 1.0)).astype(BF16)
        misc_ref[...] = small[:, 2 * LANES:]
        fm = lax.dot_general(wt_ref[...], xn_ref[...], NT_DIMS, preferred_element_type=F32)
        for ci in range(dvt_ref.shape[0]):
            dvt_ref[ci] = fm[:N_DSA, ci * DSA_BLK:(ci + 1) * DSA_BLK].astype(BF16)
        iwt_ref[...] = fm[N_DSA:, :]

    rot = jnp.where(j >= j_dq, 1.0, 0.0)
    scale = jnp.where(j < j_dq, 1.0,
                      jnp.where(j < j_dk, DSA_HEAD_DIM ** -0.5, jnp.where(j < j_iq, 1.0, IDX_DIM ** -0.5)))
    c, s = _rope_cs(tab_ref[...], scale, rot)
    rows = xn_ref.shape[0] // INPROJ_ROW_GROUPS
    for g in range(INPROJ_ROW_GROUPS):
        rs = slice(g * rows, (g + 1) * rows)
        acc = lax.dot_general(xn_ref[rs, :], w_ref[...], NT_DIMS, preferred_element_type=F32)
        main_ref[rs, :] = _rope(acc, c[rs], s[rs]).astype(BF16)


def _inproj(h, gain, w_main, w_small, w_t, tab, layer):
    t, d = h.shape
    tm = min(1024, t)
    tn = 512
    nt = w_t.shape[1]
    j_dq, j_dk, j_iq = OFF_DQ // tn, OFF_DK // tn, OFF_IQ // tn
    kern = functools.partial(_inproj_kernel, j_dq=j_dq, j_dk=j_dk, j_iq=j_iq)
    return pl.pallas_call(
        kern,
        grid=(t // tm, N_MAIN // tn),
        in_specs=[
            pl.BlockSpec((tm, d), lambda i, j: (i, 0)),
            pl.BlockSpec((1, d), lambda i, j: (0, 0)),
            pl.BlockSpec((None, tn, d), lambda i, j: (layer, j, 0)),
            pl.BlockSpec((None, N_SMALL, d), lambda i, j: (layer, 0, 0), pipeline_mode=pl.Buffered(1)),
            pl.BlockSpec((None, nt, d), lambda i, j: (layer, 0, 0), pipeline_mode=pl.Buffered(1)),
            pl.BlockSpec((None, tm, LANES), lambda i, j: ((j >= j_iq).astype(I32), i, 0)),
            pl.BlockSpec((None, tm, LANES), lambda i, j: (TAB_IDX, i, 0)),
        ],
        out_specs=[
            pl.BlockSpec((tm, tn), lambda i, j: (i, j)),
            pl.BlockSpec((tm, 2 * LANES), lambda i, j: (i, 0)),
            pl.BlockSpec((tm, LANES), lambda i, j: (i, 0)),
            pl.BlockSpec((tm // DSA_BLK, N_DSA, DSA_BLK), lambda i, j: (i, 0, 0)),
            pl.BlockSpec((LANES, tm), lambda i, j: (0, i)),
        ],
        out_shape=[
            jax.ShapeDtypeStruct((t, N_MAIN), BF16),
            jax.ShapeDtypeStruct((t, 2 * LANES), BF16),
            jax.ShapeDtypeStruct((t, LANES), F32),
            jax.ShapeDtypeStruct((t // DSA_BLK, N_DSA, DSA_BLK), BF16),
            jax.ShapeDtypeStruct((LANES, t), F32),
        ],
        scratch_shapes=[pltpu.VMEM((tm, d), BF16)],
        compiler_params=_params("parallel", "arbitrary"),
        name="inproj",
    )(h, gain, w_main, w_small, w_t, tab, tab)


def _gla_kernel(q_ref, k_ref, v_ref, gg_ref, misc_ref, w2_ref, ba_ref, gn_ref, o_ref, state_ref, *, blk):
    c_len = GLA_CHUNK
    n_ch = blk // c_len
    shift = c_len.bit_length() - 1
    assert 1 << shift == c_len

    @pl.when(pl.program_id(1) == 0)
    def _():
        state_ref[...] = jnp.zeros_like(state_ref)

    gpre = jnp.dot(misc_ref[...], w2_ref[...], preferred_element_type=F32,
                   precision=lax.Precision.HIGHEST) + ba_ref[...]
    log_a = (jnp.minimum(gpre, 0.0) - jnp.log1p(jnp.exp(-jnp.abs(gpre)))) * (1.0 / GLA_GATE_TAU)

    r = lax.broadcasted_iota(I32, (blk, blk), 0)
    c = lax.broadcasted_iota(I32, (blk, blk), 1)
    causal = (c <= r) & ((r >> shift) == (c >> shift))
    b = jnp.dot(causal.astype(F32), log_a, preferred_element_type=F32, precision=lax.Precision.HIGHEST)
    b3 = b.reshape(n_ch, c_len, N_GQ)
    b_last = b3[:, c_len - 1:c_len, :]
    decay = jnp.exp(b_last)
    k = k_ref[...].astype(F32)
    q_dec = (q_ref[...].astype(F32) * (GLA_DK ** -0.5) * jnp.exp(b)).astype(BF16)
    k_inv = (k * jnp.exp(-b)).astype(BF16)
    k_dec = (k * jnp.exp(b_last - b3).reshape(blk, N_GQ)).astype(BF16)
    gn = gn_ref[...]

    for hd in range(GLA_HEADS):
        ck = slice(hd * GLA_DK, (hd + 1) * GLA_DK)
        cv = slice(hd * GLA_DV, (hd + 1) * GLA_DV)
        attn = lax.dot_general(q_dec[:, ck], k_inv[:, ck], NT_DIMS, preferred_element_type=F32)
        attn = jnp.where(causal, attn, 0.0).astype(BF16)
        o = jnp.dot(attn, v_ref[:, cv], preferred_element_type=F32)
        st = state_ref[hd]
        inter = []
        for ci in range(n_ch):
            rows = slice(ci * c_len, (ci + 1) * c_len)
            inter.append(lax.dot_general(q_dec[rows, ck], st.astype(BF16), NT_DIMS,
                                         preferred_element_type=F32))
            d_st = lax.dot_general(v_ref[rows, cv], k_dec[rows, ck], TN_DIMS, preferred_element_type=F32)
            st = st * decay[ci, :, ck] + d_st
        state_ref[hd] = st
        o = o + jnp.concatenate(inter, axis=0)
        ms = jnp.mean(o * o, axis=-1, keepdims=True)
        y = o * lax.rsqrt(ms + EPS) * gn
        gate = gg_ref[:, cv].astype(F32)
        o_ref[:, cv] = (y * (gate / (1.0 + jnp.exp(-gate)))).astype(BF16)


def _gla(main, misc, w2, ba, gn, batch, seq):
    t = batch * seq
    blk = min(256, seq)
    nb = seq // blk
    kern = functools.partial(_gla_kernel, blk=blk)

    def rowmap(col):
        return lambda b, n: (b * nb + n, col)

    return pl.pallas_call(
        kern,
        grid=(batch, nb),
        in_specs=[
            pl.BlockSpec((blk, N_GQ), rowmap(OFF_GQ // N_GQ)),
            pl.BlockSpec((blk, N_GQ), rowmap(OFF_GK // N_GQ)),
            pl.BlockSpec((blk, N_GV), rowmap(OFF_GV // N_GV)),
            pl.BlockSpec((blk, N_GV), rowmap(OFF_GG // N_GV)),
            pl.BlockSpec((blk, LANES), rowmap(0)),
            pl.BlockSpec((LANES, N_GQ), lambda b, n: (0, 0)),
            pl.BlockSpec((1, N_GQ), lambda b, n: (0, 0)),
            pl.BlockSpec((1, GLA_DV), lambda b, n: (0, 0)),
        ],
        out_specs=pl.BlockSpec((blk, N_GV), rowmap(0)),
        out_shape=jax.ShapeDtypeStruct((t, N_GV), BF16),
        scratch_shapes=[pltpu.VMEM((GLA_HEADS, GLA_DV, GLA_DK), F32)],
        compiler_params=_params("parallel", "arbitrary"),
        name="gla",
    )(main, main, main, main, misc, w2, ba, gn)


def _dsa_kernel(q_ref, iq_ref, iwt_ref, k_ref, vt_ref, ikd_ref, o_ref,
                key_ref, hi_ref, lo_ref, bias_ref, acc_ref, p_ref, m_ref, l_ref, alpha_ref,
                *, blk, topk, idx_bits):
    qi = pl.program_id(1)
    nch = qi + 1
    krow = lax.broadcasted_iota(I32, (blk, blk), 0)
    qcol = lax.broadcasted_iota(I32, (blk, blk), 1)
    w_idx = iwt_ref[0:IDX_HEADS, :] * (IDX_HEADS ** -0.5)

    def score_body(c, carry):
        ks = pl.multiple_of(c * blk, blk)
        k_lo = ikd_ref[pl.ds(ks, blk), 0:LANES]
        k_hi = ikd_ref[pl.ds(ks, blk), LANES:2 * LANES]
        acc = jnp.zeros((blk, blk), F32)
        for p in range(IDX_HEADS // 2):
            iq_p = iq_ref[:, p * LANES:(p + 1) * LANES]
            l0 = lax.dot_general(k_lo, iq_p, NT_DIMS, preferred_element_type=F32)
            l1 = lax.dot_general(k_hi, iq_p, NT_DIMS, preferred_element_type=F32)
            acc = acc + w_idx[2 * p:2 * p + 1, :] * jnp.maximum(l0, 0.0)
            acc = acc + w_idx[2 * p + 1:2 * p + 2, :] * jnp.maximum(l1, 0.0)
        score = jnp.where(krow + (c - qi) * blk > qcol, NEG_INF, acc)
        score = jnp.where(score == 0.0, 0.0, score)
        bits = pltpu.bitcast(score, I32)
        key = bits ^ ((bits >> 31) & 0x7FFFFFFF)
        key_ref[c] = key
        hi_ref[c] = (key >> 16).astype(I16)
        return carry

    lax.fori_loop(0, nch, score_body, 0)

    def count16(ref, cand):
        def body(c, acc):
            m = jnp.where(ref[c] >= cand, jnp.ones((), BF16), jnp.zeros((), BF16))
            for i in range(blk // COUNT16_ROWS):
                acc = acc + m[i * COUNT16_ROWS:(i + 1) * COUNT16_ROWS]
            return acc
        acc = lax.fori_loop(0, nch, body, jnp.zeros((COUNT16_ROWS, blk), BF16))
        return jnp.sum(acc.astype(F32), axis=0, keepdims=True)

    def kth_largest16(ref, kth):
        v = jnp.where(count16(ref, jnp.zeros((1, blk), I16)) >= kth, 0, -(2 ** 15)).astype(I32)

        def body(i, v):
            cand = v | lax.shift_left(jnp.int32(1), 14 - i)
            return jnp.where(count16(ref, cand.astype(I16)) >= kth, cand, v)

        return lax.fori_loop(0, 15, body, v)

    def count(pred):
        def body(c, acc):
            m = jnp.where(pred(key_ref[c], c), 1.0, 0.0)
            return acc + jnp.sum(m.reshape(blk // COUNT_ROWS, COUNT_ROWS, blk), axis=0)
        acc = lax.fori_loop(0, nch, body, jnp.zeros((COUNT_ROWS, blk), F32))
        return jnp.sum(acc, axis=0, keepdims=True)

    kf = float(topk)
    thr_hi = kth_largest16(hi_ref, kf)
    n_above = count16(hi_ref, (thr_hi + 1).astype(I16))
    n_above = jnp.where(thr_hi == 2 ** 15 - 1, 0.0, n_above)

    def low_body(c, carry):
        key = key_ref[c]
        low = (key & 0xFFFF) - 2 ** 15
        lo_ref[c] = jnp.where((key >> 16) == thr_hi, low, -(2 ** 15)).astype(I16)
        return carry

    lax.fori_loop(0, nch, low_body, 0)
    thr_lo = kth_largest16(lo_ref, kf - n_above)
    thr = thr_hi * 2 ** 16 + (thr_lo + 2 ** 15)

    n_ge = count(lambda kc, c: kc >= thr)

    def resolve_ties():
        need = kf - count(lambda kc, c: kc > thr)

        def tie_body(i, last):
            cand = last | lax.shift_left(jnp.int32(1), idx_bits - 1 - i)
            below = count(lambda kc, c: (kc == thr) & (c * blk + krow < cand))
            return jnp.where(below < need, cand, last)

        return lax.fori_loop(0, idx_bits, tie_body, jnp.zeros((1, blk), I32))

    last = lax.cond(jnp.max(n_ge) > kf, resolve_ties,
                    lambda: jnp.full((1, blk), 2 ** idx_bits, I32))

    def write_bias(c, diagonal):
        kc = key_ref[c]
        tie_pos = jnp.where(kc == thr, c * blk + krow, -1)
        bias = jnp.where(kc >= thr, jnp.where(tie_pos > last, NEG_INF, 0.0), NEG_INF)
        if diagonal:
            bias = jnp.where(krow > qcol, NEG_INF, bias)
        bias_ref[c] = bias

    def bias_body(c, carry):
        write_bias(c, False)
        return carry

    lax.fori_loop(0, qi, bias_body, 0)
    write_bias(qi, True)

    heads = tuple((hd, slice(hd * DSA_HEAD_DIM, (hd + 1) * DSA_HEAD_DIM)) for hd in range(DSA_HEADS))

    def fold(x, op):
        return op(x.reshape(blk // 8, 8, blk), axis=0)

    m_ref[...] = jnp.full_like(m_ref, NEG_INF)
    l_ref[...] = jnp.zeros_like(l_ref)
    acc_ref[...] = jnp.zeros_like(acc_ref)

    def attn_body(c, carry):
        ks = pl.multiple_of(c * blk, blk)

        @pl.when(c >= 0)
        def _():
            for hd, hs in heads:
                s = lax.dot_general(k_ref[pl.ds(ks, blk), hs], q_ref[:, hs], NT_DIMS,
                                    preferred_element_type=F32) + bias_ref[c]
                m_prev = m_ref[hd, 0:1, :]
                m_new = jnp.maximum(m_prev, jnp.max(fold(s, jnp.max), axis=0, keepdims=True))
                alpha = jnp.exp(m_prev - m_new)
                m_ref[hd, 0:1, :] = m_new
                alpha_ref[hd, 0:1, :] = alpha
                p = jnp.exp(s - m_new)
                l_ref[hd] = alpha * l_ref[hd] + fold(p, jnp.sum)
                p_ref[hd] = p.astype(BF16)

        @pl.when(c >= 0)
        def _():
            for hd, hs in heads:
                pv = jnp.dot(vt_ref[c, hs, :], p_ref[hd], preferred_element_type=F32)
                acc_ref[hs, :] = alpha_ref[hd, 0:1, :] * acc_ref[hs, :] + pv

        return carry

    lax.fori_loop(0, nch, attn_body, 0)

    for hd, hs in heads:
        o_t = acc_ref[hs, :] / jnp.sum(l_ref[hd], axis=0, keepdims=True)
        o_ref[:, hs] = o_t.T.astype(BF16)


def _dsa(main, ikd, iwt, dvt, batch, seq):
    t = batch * seq
    blk = DSA_BLK
    nq = seq // blk
    topk = min(DSA_TOPK_MAX, seq // 4)
    idx_bits = max(1, (seq - 1).bit_length())
    assert (blk // COUNT16_ROWS) * nq <= BF16_EXACT_INT
    kern = functools.partial(_dsa_kernel, blk=blk, topk=topk, idx_bits=idx_bits)

    def qmap(col):
        return lambda b, i: (b * nq + i, col)

    return pl.pallas_call(
        kern,
        grid=(batch, nq),
        in_specs=[
            pl.BlockSpec((blk, N_DSA), qmap(OFF_DQ // N_DSA)),
            pl.BlockSpec((blk, N_IQ), qmap(OFF_IQ // N_IQ)),
            pl.BlockSpec((LANES, blk), lambda b, i: (0, b * nq + i)),
            pl.BlockSpec((seq, N_DSA), lambda b, i: (b, OFF_DK // N_DSA)),
            pl.BlockSpec((nq, N_DSA, blk), lambda b, i: (b, 0, 0)),
            pl.BlockSpec((seq, 2 * LANES), lambda b, i: (b, 0)),
        ],
        out_specs=pl.BlockSpec((blk, N_DSA), qmap(0)),
        out_shape=jax.ShapeDtypeStruct((t, N_DSA), BF16),
        scratch_shapes=[
            pltpu.VMEM((nq, blk, blk), I32),
            pltpu.VMEM((nq, blk, blk), I16),
            pltpu.VMEM((nq, blk, blk), I16),
            pltpu.VMEM((nq, blk, blk), F32),
            pltpu.VMEM((N_DSA, blk), F32),
            pltpu.VMEM((DSA_HEADS, blk, blk), BF16),
            pltpu.VMEM((DSA_HEADS, 8, blk), F32),
            pltpu.VMEM((DSA_HEADS, 8, blk), F32),
            pltpu.VMEM((DSA_HEADS, 8, blk), F32),
        ],
        compiler_params=_params("parallel", "arbitrary"),
        name="dsa",
    )(main, main, iwt, main, dvt, ikd)


def _outproj_kernel(og_ref, od_ref, h_ref, w_ref, g_ref, o_ref):
    m = jnp.dot(og_ref[...], w_ref[:N_GV, :], preferred_element_type=F32)
    m = m + jnp.dot(od_ref[...], w_ref[N_GV:, :], preferred_element_type=F32)
    ms = jnp.mean(m * m, axis=-1, keepdims=True)
    o_ref[...] = h_ref[...] + m * lax.rsqrt(ms + EPS) * g_ref[...]


def _outproj(o_gla, o_dsa, h, w_out, gain, layer):
    t, d = h.shape
    tm = min(512, t)
    return pl.pallas_call(
        _outproj_kernel,
        grid=(t // tm,),
        in_specs=[
            pl.BlockSpec((tm, N_GV), lambda i: (i, 0)),
            pl.BlockSpec((tm, N_DSA), lambda i: (i, 0)),
            pl.BlockSpec((tm, d), lambda i: (i, 0)),
            pl.BlockSpec((None, N_GV + N_DSA, d), lambda i: (layer, 0, 0)),
            pl.BlockSpec((1, d), lambda i: (0, 0)),
        ],
        out_specs=pl.BlockSpec((tm, d), lambda i: (i, 0)),
        out_shape=jax.ShapeDtypeStruct((t, d), F32),
        compiler_params=_params("parallel"),
        name="outproj",
    )(o_gla, o_dsa, h, w_out, gain)


def _ffn_kernel(h_ref, gpre_ref, wu_ref, wd_ref, gpost_ref, o_ref, xn_ref, acc_ref):
    f = pl.program_id(1)

    @pl.when(f == 0)
    def _():
        x = h_ref[...]
        ms = jnp.mean(x * x, axis=-1, keepdims=True)
        xn_ref[...] = (x * lax.rsqrt(ms + EPS) * gpre_ref[...]).astype(BF16)
        acc_ref[...] = jnp.zeros_like(acc_ref)

    u = jnp.maximum(jnp.dot(xn_ref[...], wu_ref[...], preferred_element_type=F32), 0.0)
    acc_ref[...] += jnp.dot((u * u).astype(BF16), wd_ref[...], preferred_element_type=F32)

    @pl.when(f == pl.num_programs(1) - 1)
    def _():
        y = acc_ref[...]
        ms = jnp.mean(y * y, axis=-1, keepdims=True)
        o_ref[...] = h_ref[...] + y * lax.rsqrt(ms + EPS) * gpost_ref[...]


def _ffn(h, g_pre, w_up, w_down, g_post, layer):
    t, d = h.shape
    d_ff = w_up.shape[2]
    tm = min(512, t)
    tf = 1024
    return pl.pallas_call(
        _ffn_kernel,
        grid=(t // tm, d_ff // tf),
        in_specs=[
            pl.BlockSpec((tm, d), lambda i, f: (i, 0)),
            pl.BlockSpec((1, d), lambda i, f: (0, 0)),
            pl.BlockSpec((None, d, tf), lambda i, f: (layer, 0, f)),
            pl.BlockSpec((None, tf, d), lambda i, f: (layer, f, 0)),
            pl.BlockSpec((1, d), lambda i, f: (0, 0)),
        ],
        out_specs=pl.BlockSpec((tm, d), lambda i, f: (i, 0)),
        out_shape=jax.ShapeDtypeStruct((t, d), F32),
        scratch_shapes=[pltpu.VMEM((tm, d), BF16), pltpu.VMEM((tm, d), F32)],
        compiler_params=_params("parallel", "arbitrary"),
        name="ffn",
    )(h, g_pre, w_up, w_down, g_post)


def _rope_table(positions, head_dim):
    r = head_dim // ROPE_FRACTION
    n_freq = r // 2
    inv_freq = ROPE_THETA ** (-(jnp.arange(0, r, 2, dtype=F32) / r))
    rep = ROPE_LANES // n_freq
    t = positions.size
    ang = positions.reshape(t, 1).astype(F32) * jnp.tile(inv_freq, rep)
    flat = lax.optimization_barrier(ang.reshape(-1))
    cos, sin = lax.optimization_barrier((jnp.cos(flat), jnp.sin(flat)))
    cos, sin = cos.reshape(t, ROPE_LANES), sin.reshape(t, ROPE_LANES)
    one = jnp.ones((cos.shape[0], ROPE_PAIR_SHIFT - ROPE_LANES), F32)
    return jnp.concatenate([cos, one, sin, one], axis=1)


def _rope_tables(positions):
    return jnp.stack([_rope_table(positions, DSA_HEAD_DIM), _rope_table(positions, IDX_DIM)])


def _dsa_lane_sources():
    half = DSA_HEAD_DIM // ROPE_FRACTION // 2
    assert half == ROPE_LANES
    cut = 2 * half + ROPE_PAIR_SHIFT - half
    return (list(range(half)) + list(range(2 * half, cut)) + list(range(half, 2 * half))
            + list(range(cut, DSA_HEAD_DIM)))


def _idx_lane_sources():
    half = IDX_DIM // ROPE_FRACTION // 2
    assert 2 * half == ROPE_LANES

    def a(lo, hi):
        return list(range(lo, hi))

    def b(lo, hi):
        return list(range(IDX_DIM + lo, IDX_DIM + hi))

    return (a(0, half) + b(0, half) + a(2 * half, IDX_DIM)
            + a(half, 2 * half) + b(half, 2 * half) + b(2 * half, IDX_DIM))


def _weight_plan(d_in):
    o_gr = 2 * N_GQ + 2 * N_GV
    o_dq = o_gr + GLA_GATE_RANK
    o_dk = o_dq + N_DSA
    o_dv = o_dk + N_DSA
    o_iq = o_dv + N_DSA
    o_ik = o_iq + N_IQ
    o_iw = o_ik + IDX_DIM
    assert o_iw + IDX_HEADS == d_in
    half = IDX_DIM // ROPE_FRACTION // 2
    none = [-1]

    def shifted(base, sources):
        return [base + s for s in sources]

    dsa, idx, ident = _dsa_lane_sources(), _idx_lane_sources(), list(range(LANES))
    tiles = {
        "rot": ([shifted(o_dq + LANES * h, dsa) for h in range(DSA_HEADS)]
                + [shifted(o_dk + LANES * h, dsa) for h in range(DSA_HEADS)]
                + [shifted(o_iq + LANES * p, idx) for p in range(IDX_HEADS // 2)]),
        "small": [
            (shifted(o_ik, range(half)) + none * half + shifted(o_ik, range(2 * half, IDX_DIM))
             + shifted(o_ik, range(half, 2 * half)) + none * (LANES - ROPE_PAIR_SHIFT - half)),
            (none * half + shifted(o_ik, range(half)) + none * (ROPE_PAIR_SHIFT - 2 * half) + none * half
             + shifted(o_ik, range(half, 2 * half)) + shifted(o_ik, range(2 * half, IDX_DIM))),
            (shifted(o_iw, range(IDX_HEADS)) + shifted(o_gr, range(GLA_GATE_RANK))
             + none * (LANES - IDX_HEADS - GLA_GATE_RANK)),
        ],
        "fm": ([shifted(o_dv + LANES * t, ident) for t in range(N_DSA // LANES)]
               + [shifted(o_iw, range(IDX_HEADS)) + none * (LANES - IDX_HEADS)]),
    }
    mats, plans = [], {}
    for name, tile_list in tiles.items():
        plans[name] = []
        for cols in tile_list:
            assert len(cols) == LANES
            parts = []
            for src_tile in sorted({c // LANES for c in cols if c >= 0}):
                m = np.zeros((LANES, LANES), np.float32)
                for dst, c in enumerate(cols):
                    if c >= 0 and c // LANES == src_tile:
                        m[c % LANES, dst] = 1.0
                for mat_id, known in enumerate(mats):
                    if np.array_equal(known, m):
                        break
                else:
                    mat_id = len(mats)
                    mats.append(m)
                parts.append((src_tile, mat_id))
            plans[name].append(parts)
    return plans, np.stack(mats)


def _prep_kernel(w_ref, p_ref, main_ref, small_ref, fm_ref, *, plans, n_plain, d_in):
    cols = w_ref.shape[1]

    def src_tile(t):
        if (t + 1) * LANES > d_in:
            v = w_ref[t * LANES:d_in, :].astype(BF16)
            return jnp.concatenate([v, jnp.zeros(((t + 1) * LANES - d_in, cols), BF16)], axis=0)
        return w_ref[t * LANES:(t + 1) * LANES, :].astype(BF16)

    def gathered(parts):
        acc = None
        for t, mat_id in parts:
            y = jnp.dot(p_ref[mat_id], src_tile(t), preferred_element_type=F32)
            acc = y if acc is None else acc + y
        return acc.astype(BF16)

    main_ref[:n_plain, :] = w_ref[:n_plain, :].astype(BF16)
    for i, parts in enumerate(plans["rot"]):
        main_ref[n_plain + i * LANES:n_plain + (i + 1) * LANES, :] = gathered(parts)
    for i, parts in enumerate(plans["small"]):
        small_ref[i * LANES:(i + 1) * LANES, :] = gathered(parts)
    for i, parts in enumerate(plans["fm"]):
        fm_ref[i * LANES:(i + 1) * LANES, :] = gathered(parts)


def _prep_weights(w_t):
    depth, d_in, d = w_t.shape
    plans, mats = _weight_plan(d_in)
    cb = min(256, d)
    n_plain = OFF_DQ
    n_fm = N_DSA + LANES
    kern = functools.partial(_prep_kernel, plans=plans, n_plain=n_plain, d_in=d_in)
    return pl.pallas_call(
        kern,
        grid=(depth, d // cb),
        in_specs=[
            pl.BlockSpec((None, d_in, cb), lambda l, r: (l, 0, r)),
            pl.BlockSpec(mats.shape, lambda l, r: (0, 0, 0)),
        ],
        out_specs=[
            pl.BlockSpec((None, N_MAIN, cb), lambda l, r: (l, 0, r)),
            pl.BlockSpec((None, N_SMALL, cb), lambda l, r: (l, 0, r)),
            pl.BlockSpec((None, n_fm, cb), lambda l, r: (l, 0, r)),
        ],
        out_shape=[
            jax.ShapeDtypeStruct((depth, N_MAIN, d), BF16),
            jax.ShapeDtypeStruct((depth, N_SMALL, d), BF16),
            jax.ShapeDtypeStruct((depth, n_fm, d), BF16),
        ],
        compiler_params=_params("parallel", "parallel"),
        name="prep_weights",
    )(w_t, jnp.asarray(mats.transpose(0, 2, 1), BF16))


def kernel(x, positions, norm_mix_pre, w_in, gla_wa2, gla_ba, gla_norm, w_out, norm_mix_post,
           norm_ffn_pre, w_up, w_down, norm_ffn_post):
    batch, seq, d = x.shape
    depth = w_in.shape[0]
    assert seq % DSA_BLK == 0
    tab = _rope_tables(positions)
    w2 = jnp.pad(gla_wa2, ((0, 0), (IDX_HEADS, LANES - IDX_HEADS - GLA_GATE_RANK), (0, 0)))
    h = x.reshape(batch * seq, d)
    w_main, w_small, w_fm = _prep_weights(jnp.swapaxes(w_in, 1, 2))
    w_out, w_up, w_down = (w.astype(BF16) for w in (w_out, w_up, w_down))
    for l in range(depth):
        main, ikd, misc, dvt, iwt = _inproj(h, norm_mix_pre[l][None, :], w_main, w_small, w_fm, tab, l)
        o_gla = _gla(main, misc, w2[l], gla_ba[l][None, :], gla_norm[l][None, :], batch, seq)
        o_dsa = _dsa(main, ikd, iwt, dvt, batch, seq)
        h = _outproj(o_gla, o_dsa, h, w_out, norm_mix_post[l][None, :], l)
        h = _ffn(h, norm_ffn_pre[l][None, :], w_up, w_down, norm_ffn_post[l][None, :], l)
    return h.reshape(batch, seq, d)
```

```python
import functools

import jax
import jax.numpy as jnp
import numpy as np
from jax import lax
from jax.experimental import pallas as pl
from jax.experimental.pallas import tpu as pltpu

F32 = jnp.float32
BF16 = jnp.bfloat16
I32 = jnp.int32
I16 = jnp.int16

GLA_HEADS = 4
GLA_DK = 128
GLA_DV = 256
GLA_GATE_RANK = 16
GLA_GATE_TAU = 16.0
GLA_CHUNK = 64
DSA_HEADS = 8
DSA_HEAD_DIM = 128
IDX_HEADS = 16
IDX_DIM = 64
DSA_TOPK_MAX = 256
ROPE_THETA = 500000.0
ROPE_FRACTION = 4
EPS = 1e-6
NEG_INF = -1e30
INT_MIN = -(2 ** 31)

LANES = 128
N_GQ = GLA_HEADS * GLA_DK
N_GV = GLA_HEADS * GLA_DV
N_DSA = DSA_HEADS * DSA_HEAD_DIM
N_IQ = IDX_HEADS * IDX_DIM
OFF_GQ = 0
OFF_GK = OFF_GQ + N_GQ
OFF_GV = OFF_GK + N_GQ
OFF_GG = OFF_GV + N_GV
OFF_DQ = OFF_GG + N_GV
OFF_DK = OFF_DQ + N_DSA
OFF_IQ = OFF_DK + N_DSA
N_MAIN = OFF_IQ + N_IQ
N_SMALL = 3 * LANES
DSA_BLK = 256
ROPE_LANES = 16
ROPE_PAIR_SHIFT = LANES // 2
TAB_DSA, TAB_IDX = range(2)
INPROJ_ROW_GROUPS = 4
OUTPROJ_ROW_GROUPS = 2
FFN_ROW_GROUPS = 2
COUNT_ROWS = 32
COUNT16_ROWS = 64
BF16_EXACT_INT = 256
VMEM_LIMIT = 56 * 1024 * 1024

NT_DIMS = (((1,), (1,)), ((), ()))
TN_DIMS = (((0,), (0,)), ((), ()))


def _params(*sem):
    return pltpu.CompilerParams(dimension_semantics=sem, vmem_limit_bytes=VMEM_LIMIT)


def _rope_cs(tab, scale, rot):
    lane = lax.broadcasted_iota(I32, (1, LANES), 1)
    lo = lane < ROPE_LANES
    hi = (lane >= ROPE_PAIR_SHIFT) & (lane < ROPE_PAIR_SHIFT + ROPE_LANES)
    swapped = pltpu.roll(tab, ROPE_PAIR_SHIFT, 1)
    c = jnp.where(hi, swapped, tab)
    s = jnp.where(lo, -swapped, jnp.where(hi, tab, 0.0))
    return (1.0 + rot * (c - 1.0)) * scale, s * (rot * scale)


def _rope(x, c, s):
    parts = []
    for g in range(x.shape[1] // LANES):
        xg = x[:, g * LANES:(g + 1) * LANES]
        parts.append(xg * c + pltpu.roll(xg, ROPE_PAIR_SHIFT, 1) * s)
    return parts[0] if len(parts) == 1 else jnp.concatenate(parts, axis=1)


def _inproj_kernel(x_ref, g_ref, w_ref, ws_ref, wt_ref, tab_ref, tabk_ref,
                   main_ref, ikd_ref, misc_ref, dvt_ref, iwt_ref, xn_ref, *, j_dq, j_dk, j_iq):
    j = pl.program_id(1)

    @pl.when(j == 0)
    def _():
        x = x_ref[...]
        ms = jnp.mean(x * x, axis=-1, keepdims=True)
        xn_ref[...] = (x * lax.rsqrt(ms + EPS) * g_ref[...]).astype(BF16)
        small = lax.dot_general(xn_ref[...], ws_ref[...], NT_DIMS, preferred_element_type=F32)
        ikd_ref[...] = _rope(small[:, :2 * LANES], *_rope_cs(tabk_ref[...], 1.0, 1.0)).astype(BF16)
        misc_ref[...] = small[:, 2 * LANES:]
        fm = lax.dot_general(wt_ref[...], xn_ref[...], NT_DIMS, preferred_element_type=F32)
        for ci in range(dvt_ref.shape[0]):
            dvt_ref[ci] = fm[:N_DSA, ci * DSA_BLK:(ci + 1) * DSA_BLK].astype(BF16)
        iwt_ref[...] = fm[N_DSA:, :]

    rot = jnp.where(j >= j_dq, 1.0, 0.0)
    scale = jnp.where(j < j_dq, 1.0,
                      jnp.where(j < j_dk, DSA_HEAD_DIM ** -0.5, jnp.where(j < j_iq, 1.0, IDX_DIM ** -0.5)))
    c, s = _rope_cs(tab_ref[...], scale, rot)
    rows = xn_ref.shape[0] // INPROJ_ROW_GROUPS
    for g in range(INPROJ_ROW_GROUPS):
        rs = slice(g * rows, (g + 1) * rows)
        acc = lax.dot_general(xn_ref[rs, :], w_ref[...], NT_DIMS, preferred_element_type=F32)
        main_ref[rs, :] = _rope(acc, c[rs], s[rs]).astype(BF16)


def _inproj(h, gain, w_main, w_small, w_t, tab, layer):
    t, d = h.shape
    tm = min(1024, t)
    tn = 512
    nt = w_t.shape[1]
    j_dq, j_dk, j_iq = OFF_DQ // tn, OFF_DK // tn, OFF_IQ // tn
    kern = functools.partial(_inproj_kernel, j_dq=j_dq, j_dk=j_dk, j_iq=j_iq)
    return pl.pallas_call(
        kern,
        grid=(t // tm, N_MAIN // tn),
        in_specs=[
            pl.BlockSpec((tm, d), lambda i, j: (i, 0)),
            pl.BlockSpec((1, d), lambda i, j: (0, 0)),
            pl.BlockSpec((None, tn, d), lambda i, j: (layer, j, 0)),
            pl.BlockSpec((None, N_SMALL, d), lambda i, j: (layer, 0, 0), pipeline_mode=pl.Buffered(1)),
            pl.BlockSpec((None, nt, d), lambda i, j: (layer, 0, 0), pipeline_mode=pl.Buffered(1)),
            pl.BlockSpec((None, tm, LANES), lambda i, j: ((j >= j_iq).astype(I32), i, 0)),
            pl.BlockSpec((None, tm, LANES), lambda i, j: (TAB_IDX, i, 0)),
        ],
        out_specs=[
            pl.BlockSpec((tm, tn), lambda i, j: (i, j)),
            pl.BlockSpec((tm, 2 * LANES), lambda i, j: (i, 0)),
            pl.BlockSpec((tm, LANES), lambda i, j: (i, 0)),
            pl.BlockSpec((tm // DSA_BLK, N_DSA, DSA_BLK), lambda i, j: (i, 0, 0)),
            pl.BlockSpec((LANES, tm), lambda i, j: (0, i)),
        ],
        out_shape=[
            jax.ShapeDtypeStruct((t, N_MAIN), BF16),
            jax.ShapeDtypeStruct((t, 2 * LANES), BF16),
            jax.ShapeDtypeStruct((t, LANES), F32),
            jax.ShapeDtypeStruct((t // DSA_BLK, N_DSA, DSA_BLK), BF16),
            jax.ShapeDtypeStruct((LANES, t), F32),
        ],
        scratch_shapes=[pltpu.VMEM((tm, d), BF16)],
        compiler_params=_params("parallel", "arbitrary"),
        name="inproj",
    )(h, gain, w_main, w_small, w_t, tab, tab)


def _gla_kernel(q_ref, k_ref, v_ref, gg_ref, misc_ref, w2_ref, ba_ref, gn_ref, o_ref, state_ref, *, blk):
    c_len = GLA_CHUNK
    n_ch = blk // c_len
    shift = c_len.bit_length() - 1
    assert 1 << shift == c_len

    @pl.when(pl.program_id(1) == 0)
    def _():
        state_ref[...] = jnp.zeros_like(state_ref)

    gpre = jnp.dot(misc_ref[...], w2_ref[...], preferred_element_type=F32,
                   precision=lax.Precision.HIGHEST) + ba_ref[...]
    log_a = (jnp.minimum(gpre, 0.0) - jnp.log1p(jnp.exp(-jnp.abs(gpre)))) * (1.0 / GLA_GATE_TAU)

    r = lax.broadcasted_iota(I32, (blk, blk), 0)
    c = lax.broadcasted_iota(I32, (blk, blk), 1)
    causal = (c <= r) & ((r >> shift) == (c >> shift))
    b = jnp.dot(causal.astype(F32), log_a, preferred_element_type=F32, precision=lax.Precision.HIGHEST)
    b3 = b.reshape(n_ch, c_len, N_GQ)
    b_last = b3[:, c_len - 1:c_len, :]
    decay = jnp.exp(b_last)
    k = k_ref[...].astype(F32)
    q_dec = (q_ref[...].astype(F32) * (GLA_DK ** -0.5) * jnp.exp(b)).astype(BF16)
    k_inv = (k * jnp.exp(-b)).astype(BF16)
    k_dec = (k * jnp.exp(b_last - b3).reshape(blk, N_GQ)).astype(BF16)
    gn = gn_ref[...]

    for hd in range(GLA_HEADS):
        ck = slice(hd * GLA_DK, (hd + 1) * GLA_DK)
        cv = slice(hd * GLA_DV, (hd + 1) * GLA_DV)
        attn = lax.dot_general(q_dec[:, ck], k_inv[:, ck], NT_DIMS, preferred_element_type=F32)
        attn = jnp.where(causal, attn, 0.0).astype(BF16)
        o = jnp.dot(attn, v_ref[:, cv], preferred_element_type=F32)
        st = state_ref[hd]
        inter = []
        for ci in range(n_ch):
            rows = slice(ci * c_len, (ci + 1) * c_len)
            inter.append(lax.dot_general(q_dec[rows, ck], st.astype(BF16), NT_DIMS,
                                         preferred_element_type=F32))
            d_st = lax.dot_general(v_ref[rows, cv], k_dec[rows, ck], TN_DIMS, preferred_element_type=F32)
            st = st * decay[ci, :, ck] + d_st
        state_ref[hd] = st
        o = o + jnp.concatenate(inter, axis=0)
        ms = jnp.mean(o * o, axis=-1, keepdims=True)
        y = o * lax.rsqrt(ms + EPS) * gn
        gate = gg_ref[:, cv].astype(F32)
        o_ref[:, cv] = (y * (gate / (1.0 + jnp.exp(-gate)))).astype(BF16)


def _gla(main, misc, w2, ba, gn, batch, seq):
    t = batch * seq
    blk = min(256, seq)
    nb = seq // blk
    kern = functools.partial(_gla_kernel, blk=blk)

    def rowmap(col):
        return lambda b, n: (b * nb + n, col)

    return pl.pallas_call(
        kern,
        grid=(batch, nb),
        in_specs=[
            pl.BlockSpec((blk, N_GQ), rowmap(OFF_GQ // N_GQ)),
            pl.BlockSpec((blk, N_GQ), rowmap(OFF_GK // N_GQ)),
            pl.BlockSpec((blk, N_GV), rowmap(OFF_GV // N_GV)),
            pl.BlockSpec((blk, N_GV), rowmap(OFF_GG // N_GV)),
            pl.BlockSpec((blk, LANES), rowmap(0)),
            pl.BlockSpec((LANES, N_GQ), lambda b, n: (0, 0)),
            pl.BlockSpec((1, N_GQ), lambda b, n: (0, 0)),
            pl.BlockSpec((1, GLA_DV), lambda b, n: (0, 0)),
        ],
        out_specs=pl.BlockSpec((blk, N_GV), rowmap(0)),
        out_shape=jax.ShapeDtypeStruct((t, N_GV), BF16),
        scratch_shapes=[pltpu.VMEM((GLA_HEADS, GLA_DV, GLA_DK), F32)],
        compiler_params=_params("parallel", "arbitrary"),
        name="gla",
    )(main, main, main, main, misc, w2, ba, gn)


def _dsa_kernel(q_ref, iq_ref, iwt_ref, k_ref, vt_ref, ikd_ref, o_ref,
                key_ref, hi_ref, lo_ref, bias_ref, acc_ref, p_ref, m_ref, l_ref, alpha_ref,
                *, blk, topk, idx_bits):
    qi = pl.program_id(1)
    nch = qi + 1
    krow = lax.broadcasted_iota(I32, (blk, blk), 0)
    qcol = lax.broadcasted_iota(I32, (blk, blk), 1)
    w_idx = iwt_ref[0:IDX_HEADS, :] * (IDX_HEADS ** -0.5)

    def score_body(c, carry):
        ks = pl.multiple_of(c * blk, blk)
        k_lo = ikd_ref[pl.ds(ks, blk), 0:LANES]
        k_hi = ikd_ref[pl.ds(ks, blk), LANES:2 * LANES]
        acc = jnp.zeros((blk, blk), F32)
        for p in range(IDX_HEADS // 2):
            iq_p = iq_ref[:, p * LANES:(p + 1) * LANES]
            l0 = lax.dot_general(k_lo, iq_p, NT_DIMS, preferred_element_type=F32)
            l1 = lax.dot_general(k_hi, iq_p, NT_DIMS, preferred_element_type=F32)
            acc = acc + w_idx[2 * p:2 * p + 1, :] * jnp.maximum(l0, 0.0)
            acc = acc + w_idx[2 * p + 1:2 * p + 2, :] * jnp.maximum(l1, 0.0)
        score = jnp.where(krow + (c - qi) * blk > qcol, NEG_INF, acc)
        score = jnp.where(score == 0.0, 0.0, score)
        bits = pltpu.bitcast(score, I32)
        key = bits ^ ((bits >> 31) & 0x7FFFFFFF)
        key_ref[c] = key
        hi_ref[c] = (key >> 16).astype(I16)
        return carry

    lax.fori_loop(0, nch, score_body, 0)

    def count16(ref, cand):
        def body(c, acc):
            m = jnp.where(ref[c] >= cand, jnp.ones((), BF16), jnp.zeros((), BF16))
            for i in range(blk // COUNT16_ROWS):
                acc = acc + m[i * COUNT16_ROWS:(i + 1) * COUNT16_ROWS]
            return acc
        acc = lax.fori_loop(0, nch, body, jnp.zeros((COUNT16_ROWS, blk), BF16))
        return jnp.sum(acc.astype(F32), axis=0, keepdims=True)

    def kth_largest16(ref, kth):
        v = jnp.where(count16(ref, jnp.zeros((1, blk), I16)) >= kth, 0, -(2 ** 15)).astype(I32)

        def body(i, v):
            cand = v | lax.shift_left(jnp.int32(1), 14 - i)
            return jnp.where(count16(ref, cand.astype(I16)) >= kth, cand, v)

        return lax.fori_loop(0, 15, body, v)

    def count(pred):
        def body(c, acc):
            m = jnp.where(pred(key_ref[c], c), 1.0, 0.0)
            return acc + jnp.sum(m.reshape(blk // COUNT_ROWS, COUNT_ROWS, blk), axis=0)
        acc = lax.fori_loop(0, nch, body, jnp.zeros((COUNT_ROWS, blk), F32))
        return jnp.sum(acc, axis=0, keepdims=True)

    kf = float(topk)
    thr_hi = kth_largest16(hi_ref, kf)
    n_above = count16(hi_ref, (thr_hi + 1).astype(I16))
    n_above = jnp.where(thr_hi == 2 ** 15 - 1, 0.0, n_above)

    def low_body(c, carry):
        key = key_ref[c]
        low = (key & 0xFFFF) - 2 ** 15
        lo_ref[c] = jnp.where((key >> 16) == thr_hi, low, -(2 ** 15)).astype(I16)
        return carry

    lax.fori_loop(0, nch, low_body, 0)
    thr_lo = kth_largest16(lo_ref, kf - n_above)
    thr = thr_hi * 2 ** 16 + (thr_lo + 2 ** 15)

    n_ge = count(lambda kc, c: kc >= thr)

    def resolve_ties():
        need = kf - count(lambda kc, c: kc > thr)

        def tie_body(i, last):
            cand = last | lax.shift_left(jnp.int32(1), idx_bits - 1 - i)
            below = count(lambda kc, c: (kc == thr) & (c * blk + krow < cand))
            return jnp.where(below < need, cand, last)

        return lax.fori_loop(0, idx_bits, tie_body, jnp.zeros((1, blk), I32))

    last = lax.cond(jnp.max(n_ge) > kf, resolve_ties,
                    lambda: jnp.full((1, blk), 2 ** idx_bits, I32))

    def write_bias(c, diagonal):
        kc = key_ref[c]
        tie_pos = jnp.where(kc == thr, c * blk + krow, -1)
        bias = jnp.where(kc >= thr, jnp.where(tie_pos > last, NEG_INF, 0.0), NEG_INF)
        if diagonal:
            bias = jnp.where(krow > qcol, NEG_INF, bias)
        bias_ref[c] = bias

    def bias_body(c, carry):
        write_bias(c, False)
        return carry

    lax.fori_loop(0, qi, bias_body, 0)
    write_bias(qi, True)

    heads = tuple((hd, slice(hd * DSA_HEAD_DIM, (hd + 1) * DSA_HEAD_DIM)) for hd in range(DSA_HEADS))

    def fold(x, op):
        return op(x.reshape(blk // 8, 8, blk), axis=0)

    m_ref[...] = jnp.full_like(m_ref, NEG_INF)
    l_ref[...] = jnp.zeros_like(l_ref)
    acc_ref[...] = jnp.zeros_like(acc_ref)

    def attn_body(c, carry):
        ks = pl.multiple_of(c * blk, blk)

        @pl.when(c >= 0)
        def _():
            for hd, hs in heads:
                s = lax.dot_general(k_ref[pl.ds(ks, blk), hs], q_ref[:, hs], NT_DIMS,
                                    preferred_element_type=F32) + bias_ref[c]
                m_prev = m_ref[hd, 0:1, :]
                m_new = jnp.maximum(m_prev, jnp.max(fold(s, jnp.max), axis=0, keepdims=True))
                alpha = jnp.exp(m_prev - m_new)
                m_ref[hd, 0:1, :] = m_new
                alpha_ref[hd, 0:1, :] = alpha
                p = jnp.exp(s - m_new)
                l_ref[hd] = alpha * l_ref[hd] + fold(p, jnp.sum)
                p_ref[hd] = p.astype(BF16)

        @pl.when(c >= 0)
        def _():
            for hd, hs in heads:
                pv = jnp.dot(vt_ref[c, hs, :], p_ref[hd], preferred_element_type=F32)
                acc_ref[hs, :] = alpha_ref[hd, 0:1, :] * acc_ref[hs, :] + pv

        return carry

    lax.fori_loop(0, nch, attn_body, 0)

    for hd, hs in heads:
        o_t = acc_ref[hs, :] / jnp.sum(l_ref[hd], axis=0, keepdims=True)
        o_ref[:, hs] = o_t.T.astype(BF16)


def _dsa(main, ikd, iwt, dvt, batch, seq):
    t = batch * seq
    blk = DSA_BLK
    nq = seq // blk
    topk = min(DSA_TOPK_MAX, seq // 4)
    idx_bits = max(1, (seq - 1).bit_length())
    assert (blk // COUNT16_ROWS) * nq <= BF16_EXACT_INT
    kern = functools.partial(_dsa_kernel, blk=blk, topk=topk, idx_bits=idx_bits)

    def qmap(col):
        return lambda b, i: (b * nq + i, col)

    return pl.pallas_call(
        kern,
        grid=(batch, nq),
        in_specs=[
            pl.BlockSpec((blk, N_DSA), qmap(OFF_DQ // N_DSA)),
            pl.BlockSpec((blk, N_IQ), qmap(OFF_IQ // N_IQ)),
            pl.BlockSpec((LANES, blk), lambda b, i: (0, b * nq + i)),
            pl.BlockSpec((seq, N_DSA), lambda b, i: (b, OFF_DK // N_DSA)),
            pl.BlockSpec((nq, N_DSA, blk), lambda b, i: (b, 0, 0)),
            pl.BlockSpec((seq, 2 * LANES), lambda b, i: (b, 0)),
        ],
        out_specs=pl.BlockSpec((blk, N_DSA), qmap(0)),
        out_shape=jax.ShapeDtypeStruct((t, N_DSA), BF16),
        scratch_shapes=[
            pltpu.VMEM((nq, blk, blk), I32),
            pltpu.VMEM((nq, blk, blk), I16),
            pltpu.VMEM((nq, blk, blk), I16),
            pltpu.VMEM((nq, blk, blk), F32),
            pltpu.VMEM((N_DSA, blk), F32),
            pltpu.VMEM((DSA_HEADS, blk, blk), BF16),
            pltpu.VMEM((DSA_HEADS, 8, blk), F32),
            pltpu.VMEM((DSA_HEADS, 8, blk), F32),
            pltpu.VMEM((DSA_HEADS, 8, blk), F32),
        ],
        compiler_params=_params("parallel", "arbitrary"),
        name="dsa",
    )(main, main, iwt, main, dvt, ikd)


def _outproj_kernel(og_ref, od_ref, h_ref, w_ref, g_ref, o_ref):
    rows = h_ref.shape[0] // OUTPROJ_ROW_GROUPS
    for g in range(OUTPROJ_ROW_GROUPS):
        rs = slice(g * rows, (g + 1) * rows)
        m = jnp.dot(og_ref[rs, :], w_ref[:N_GV, :], preferred_element_type=F32)
        m = m + jnp.dot(od_ref[rs, :], w_ref[N_GV:, :], preferred_element_type=F32)
        ms = jnp.mean(m * m, axis=-1, keepdims=True)
        o_ref[rs, :] = h_ref[rs, :] + m * lax.rsqrt(ms + EPS) * g_ref[...]


def _outproj(o_gla, o_dsa, h, w_out, gain, layer):
    t, d = h.shape
    tm = min(512, t)
    return pl.pallas_call(
        _outproj_kernel,
        grid=(t // tm,),
        in_specs=[
            pl.BlockSpec((tm, N_GV), lambda i: (i, 0)),
            pl.BlockSpec((tm, N_DSA), lambda i: (i, 0)),
            pl.BlockSpec((tm, d), lambda i: (i, 0)),
            pl.BlockSpec((None, N_GV + N_DSA, d), lambda i: (layer, 0, 0)),
            pl.BlockSpec((1, d), lambda i: (0, 0)),
        ],
        out_specs=pl.BlockSpec((tm, d), lambda i: (i, 0)),
        out_shape=jax.ShapeDtypeStruct((t, d), F32),
        compiler_params=_params("parallel"),
        name="outproj",
    )(o_gla, o_dsa, h, w_out, gain)


def _ffn_kernel(h_ref, gpre_ref, wu_ref, wd_ref, gpost_ref, o_ref, xn_ref, acc_ref):
    f = pl.program_id(1)
    last = pl.num_programs(1) - 1
    rows = h_ref.shape[0] // FFN_ROW_GROUPS
    groups = [slice(g * rows, (g + 1) * rows) for g in range(FFN_ROW_GROUPS)]

    def up(rs):
        u = jnp.maximum(jnp.dot(xn_ref[rs, :], wu_ref[...], preferred_element_type=F32), 0.0)
        return (u * u).astype(BF16)

    @pl.when(f == 0)
    def _():
        for rs in groups:
            x = h_ref[rs, :]
            ms = jnp.mean(x * x, axis=-1, keepdims=True)
            xn_ref[rs, :] = (x * lax.rsqrt(ms + EPS) * gpre_ref[...]).astype(BF16)
            acc_ref[rs, :] = jnp.dot(up(rs), wd_ref[...], preferred_element_type=F32)

    @pl.when((f > 0) & (f < last))
    def _():
        acc_ref[...] += jnp.dot(up(slice(None)), wd_ref[...], preferred_element_type=F32)

    @pl.when(f == last)
    def _():
        for rs in groups:
            y = acc_ref[rs, :] + jnp.dot(up(rs), wd_ref[...], preferred_element_type=F32)
            ms = jnp.mean(y * y, axis=-1, keepdims=True)
            o_ref[rs, :] = h_ref[rs, :] + y * lax.rsqrt(ms + EPS) * gpost_ref[...]


def _ffn(h, g_pre, w_up, w_down, g_post, layer):
    t, d = h.shape
    d_ff = w_up.shape[2]
    tm = min(512, t)
    tf = 1024
    assert d_ff // tf >= 2
    return pl.pallas_call(
        _ffn_kernel,
        grid=(t // tm, d_ff // tf),
        in_specs=[
            pl.BlockSpec((tm, d), lambda i, f: (i, 0)),
            pl.BlockSpec((1, d), lambda i, f: (0, 0)),
            pl.BlockSpec((None, d, tf), lambda i, f: (layer, 0, f)),
            pl.BlockSpec((None, tf, d), lambda i, f: (layer, f, 0)),
            pl.BlockSpec((1, d), lambda i, f: (0, 0)),
        ],
        out_specs=pl.BlockSpec((tm, d), lambda i, f: (i, 0)),
        out_shape=jax.ShapeDtypeStruct((t, d), F32),
        scratch_shapes=[pltpu.VMEM((tm, d), BF16), pltpu.VMEM((tm, d), F32)],
        compiler_params=_params("parallel", "arbitrary"),
        name="ffn",
    )(h, g_pre, w_up, w_down, g_post)


def _rope_table(positions, head_dim):
    r = head_dim // ROPE_FRACTION
    n_freq = r // 2
    inv_freq = ROPE_THETA ** (-(jnp.arange(0, r, 2, dtype=F32) / r))
    rep = ROPE_LANES // n_freq
    t = positions.size
    ang = positions.reshape(t, 1).astype(F32) * jnp.tile(inv_freq, rep)
    flat = lax.optimization_barrier(ang.reshape(-1))
    cos, sin = lax.optimization_barrier((jnp.cos(flat), jnp.sin(flat)))
    cos, sin = cos.reshape(t, ROPE_LANES), sin.reshape(t, ROPE_LANES)
    one = jnp.ones((cos.shape[0], ROPE_PAIR_SHIFT - ROPE_LANES), F32)
    return jnp.concatenate([cos, one, sin, one], axis=1)


def _rope_tables(positions):
    return jnp.stack([_rope_table(positions, DSA_HEAD_DIM), _rope_table(positions, IDX_DIM)])


def _dsa_lane_sources():
    half = DSA_HEAD_DIM // ROPE_FRACTION // 2
    assert half == ROPE_LANES
    cut = 2 * half + ROPE_PAIR_SHIFT - half
    return (list(range(half)) + list(range(2 * half, cut)) + list(range(half, 2 * half))
            + list(range(cut, DSA_HEAD_DIM)))


def _idx_lane_sources():
    half = IDX_DIM // ROPE_FRACTION // 2
    assert 2 * half == ROPE_LANES

    def a(lo, hi):
        return list(range(lo, hi))

    def b(lo, hi):
        return list(range(IDX_DIM + lo, IDX_DIM + hi))

    return (a(0, half) + b(0, half) + a(2 * half, IDX_DIM)
            + a(half, 2 * half) + b(half, 2 * half) + b(2 * half, IDX_DIM))


def _weight_plan(d_in):
    o_gr = 2 * N_GQ + 2 * N_GV
    o_dq = o_gr + GLA_GATE_RANK
    o_dk = o_dq + N_DSA
    o_dv = o_dk + N_DSA
    o_iq = o_dv + N_DSA
    o_ik = o_iq + N_IQ
    o_iw = o_ik + IDX_DIM
    assert o_iw + IDX_HEADS == d_in
    half = IDX_DIM // ROPE_FRACTION // 2
    none = [-1]

    def shifted(base, sources):
        return [base + s for s in sources]

    dsa, idx, ident = _dsa_lane_sources(), _idx_lane_sources(), list(range(LANES))
    tiles = {
        "rot": ([shifted(o_dq + LANES * h, dsa) for h in range(DSA_HEADS)]
                + [shifted(o_dk + LANES * h, dsa) for h in range(DSA_HEADS)]
                + [shifted(o_iq + LANES * p, idx) for p in range(IDX_HEADS // 2)]),
        "small": [
            (shifted(o_ik, range(half)) + none * half + shifted(o_ik, range(2 * half, IDX_DIM))
             + shifted(o_ik, range(half, 2 * half)) + none * (LANES - ROPE_PAIR_SHIFT - half)),
            (none * half + shifted(o_ik, range(half)) + none * (ROPE_PAIR_SHIFT - 2 * half) + none * half
             + shifted(o_ik, range(half, 2 * half)) + shifted(o_ik, range(2 * half, IDX_DIM))),
            (shifted(o_iw, range(IDX_HEADS)) + shifted(o_gr, range(GLA_GATE_RANK))
             + none * (LANES - IDX_HEADS - GLA_GATE_RANK)),
        ],
        "fm": ([shifted(o_dv + LANES * t, ident) for t in range(N_DSA // LANES)]
               + [shifted(o_iw, range(IDX_HEADS)) + none * (LANES - IDX_HEADS)]),
    }
    mats, plans = [], {}
    for name, tile_list in tiles.items():
        plans[name] = []
        for cols in tile_list:
            assert len(cols) == LANES
            parts = []
            for src_tile in sorted({c // LANES for c in cols if c >= 0}):
                m = np.zeros((LANES, LANES), np.float32)
                for dst, c in enumerate(cols):
                    if c >= 0 and c // LANES == src_tile:
                        m[c % LANES, dst] = 1.0
                for mat_id, known in enumerate(mats):
                    if np.array_equal(known, m):
                        break
                else:
                    mat_id = len(mats)
                    mats.append(m)
                parts.append((src_tile, mat_id))
            plans[name].append(parts)
    return plans, np.stack(mats)


def _prep_kernel(w_ref, p_ref, main_ref, small_ref, fm_ref, *, plans, n_plain, d_in):
    cols = w_ref.shape[1]

    def src_tile(t):
        if (t + 1) * LANES > d_in:
            v = w_ref[t * LANES:d_in, :].astype(BF16)
            return jnp.concatenate([v, jnp.zeros(((t + 1) * LANES - d_in, cols), BF16)], axis=0)
        return w_ref[t * LANES:(t + 1) * LANES, :].astype(BF16)

    def gathered(parts):
        acc = None
        for t, mat_id in parts:
            y = jnp.dot(p_ref[mat_id], src_tile(t), preferred_element_type=F32)
            acc = y if acc is None else acc + y
        return acc.astype(BF16)

    main_ref[:n_plain, :] = w_ref[:n_plain, :].astype(BF16)
    for i, parts in enumerate(plans["rot"]):
        main_ref[n_plain + i * LANES:n_plain + (i + 1) * LANES, :] = gathered(parts)
    for i, parts in enumerate(plans["small"]):
        small_ref[i * LANES:(i + 1) * LANES, :] = gathered(parts)
    for i, parts in enumerate(plans["fm"]):
        fm_ref[i * LANES:(i + 1) * LANES, :] = gathered(parts)


def _prep_weights(w_t):
    depth, d_in, d = w_t.shape
    plans, mats = _weight_plan(d_in)
    cb = min(256, d)
    n_plain = OFF_DQ
    n_fm = N_DSA + LANES
    kern = functools.partial(_prep_kernel, plans=plans, n_plain=n_plain, d_in=d_in)
    return pl.pallas_call(
        kern,
        grid=(depth, d // cb),
        in_specs=[
            pl.BlockSpec((None, d_in, cb), lambda l, r: (l, 0, r)),
            pl.BlockSpec(mats.shape, lambda l, r: (0, 0, 0)),
        ],
        out_specs=[
            pl.BlockSpec((None, N_MAIN, cb), lambda l, r: (l, 0, r)),
            pl.BlockSpec((None, N_SMALL, cb), lambda l, r: (l, 0, r)),
            pl.BlockSpec((None, n_fm, cb), lambda l, r: (l, 0, r)),
        ],
        out_shape=[
            jax.ShapeDtypeStruct((depth, N_MAIN, d), BF16),
            jax.ShapeDtypeStruct((depth, N_SMALL, d), BF16),
            jax.ShapeDtypeStruct((depth, n_fm, d), BF16),
        ],
        compiler_params=_params("parallel", "parallel"),
        name="prep_weights",
    )(w_t, jnp.asarray(mats.transpose(0, 2, 1), BF16))


def kernel(x, positions, norm_mix_pre, w_in, gla_wa2, gla_ba, gla_norm, w_out, norm_mix_post,
           norm_ffn_pre, w_up, w_down, norm_ffn_post):
    batch, seq, d = x.shape
    depth = w_in.shape[0]
    assert seq % DSA_BLK == 0
    tab = _rope_tables(positions)
    w2 = jnp.pad(gla_wa2, ((0, 0), (IDX_HEADS, LANES - IDX_HEADS - GLA_GATE_RANK), (0, 0)))
    h = x.reshape(batch * seq, d)
    w_main, w_small, w_fm = _prep_weights(jnp.swapaxes(w_in, 1, 2))
    w_out, w_up, w_down = (w.astype(BF16) for w in (w_out, w_up, w_down))
    for l in range(depth):
        main, ikd, misc, dvt, iwt = _inproj(h, norm_mix_pre[l][None, :], w_main, w_small, w_fm, tab, l)
        o_gla = _gla(main, misc, w2[l], gla_ba[l][None, :], gla_norm[l][None, :], batch, seq)
        o_dsa = _dsa(main, ikd, iwt, dvt, batch, seq)
        h = _outproj(o_gla, o_dsa, h, w_out, norm_mix_post[l][None, :], l)
        h = _ffn(h, norm_ffn_pre[l][None, :], w_up, w_down, norm_ffn_post[l][None, :], l)
    return h.reshape(batch, seq, d)
```

```python
import functools

import jax
import jax.numpy as jnp
import numpy as np
from jax import lax
from jax.experimental import pallas as pl
from jax.experimental.pallas import tpu as pltpu

F32 = jnp.float32
BF16 = jnp.bfloat16
I32 = jnp.int32
I16 = jnp.int16

GLA_HEADS = 4
GLA_DK = 128
GLA_DV = 256
GLA_GATE_RANK = 16
GLA_GATE_TAU = 16.0
GLA_CHUNK = 64
DSA_HEADS = 8
DSA_HEAD_DIM = 128
IDX_HEADS = 16
IDX_DIM = 64
DSA_TOPK_MAX = 256
ROPE_THETA = 500000.0
ROPE_FRACTION = 4
EPS = 1e-6
NEG_INF = -1e30
INT_MIN = -(2 ** 31)

LANES = 128
N_GQ = GLA_HEADS * GLA_DK
N_GV = GLA_HEADS * GLA_DV
N_DSA = DSA_HEADS * DSA_HEAD_DIM
N_IQ = IDX_HEADS * IDX_DIM
OFF_GQ = 0
OFF_GK = OFF_GQ + N_GQ
OFF_GV = OFF_GK + N_GQ
OFF_GG = OFF_GV + N_GV
OFF_DQ = OFF_GG + N_GV
OFF_DK = OFF_DQ + N_DSA
OFF_IQ = OFF_DK + N_DSA
N_MAIN = OFF_IQ + N_IQ
N_SMALL = 3 * LANES
DSA_BLK = 256
ROPE_LANES = 16
ROPE_PAIR_SHIFT = LANES // 2
TAB_DSA, TAB_IDX = range(2)
INPROJ_ROW_GROUPS = 4
OUTPROJ_ROW_GROUPS = 1
FFN_ROW_GROUPS = 2
COUNT_ROWS = 32
COUNT16_ROWS = 64
BF16_EXACT_INT = 256
VMEM_LIMIT = 60 * 1024 * 1024

NT_DIMS = (((1,), (1,)), ((), ()))
TN_DIMS = (((0,), (0,)), ((), ()))


def _params(*sem):
    return pltpu.CompilerParams(dimension_semantics=sem, vmem_limit_bytes=VMEM_LIMIT)


def _rope_cs(tab, scale, rot):
    lane = lax.broadcasted_iota(I32, (1, LANES), 1)
    lo = lane < ROPE_LANES
    hi = (lane >= ROPE_PAIR_SHIFT) & (lane < ROPE_PAIR_SHIFT + ROPE_LANES)
    swapped = pltpu.roll(tab, ROPE_PAIR_SHIFT, 1)
    c = jnp.where(hi, swapped, tab)
    s = jnp.where(lo, -swapped, jnp.where(hi, tab, 0.0))
    return (1.0 + rot * (c - 1.0)) * scale, s * (rot * scale)


def _rope(x, c, s):
    parts = []
    for g in range(x.shape[1] // LANES):
        xg = x[:, g * LANES:(g + 1) * LANES]
        parts.append(xg * c + pltpu.roll(xg, ROPE_PAIR_SHIFT, 1) * s)
    return parts[0] if len(parts) == 1 else jnp.concatenate(parts, axis=1)


def _inproj_kernel(x_ref, g_ref, w_ref, ws_ref, wt_ref, tab_ref, tabk_ref,
                   main_ref, ikd_ref, misc_ref, dvt_ref, iwt_ref, xn_ref, *, j_dq, j_dk, j_iq):
    j = pl.program_id(1)

    @pl.when(j == 0)
    def _():
        x = x_ref[...]
        ms = jnp.mean(x * x, axis=-1, keepdims=True)
        xn_ref[...] = (x * lax.rsqrt(ms + EPS) * g_ref[...]).astype(BF16)
        small = lax.dot_general(xn_ref[...], ws_ref[...], NT_DIMS, preferred_element_type=F32)
        ikd_ref[...] = _rope(small[:, :2 * LANES], *_rope_cs(tabk_ref[...], 1.0, 1.0)).astype(BF16)
        misc_ref[...] = small[:, 2 * LANES:]
        fm = lax.dot_general(wt_ref[...], xn_ref[...], NT_DIMS, preferred_element_type=F32)
        for ci in range(dvt_ref.shape[0]):
            dvt_ref[ci] = fm[:N_DSA, ci * DSA_BLK:(ci + 1) * DSA_BLK].astype(BF16)
        iwt_ref[...] = fm[N_DSA:, :]

    rot = jnp.where(j >= j_dq, 1.0, 0.0)
    scale = jnp.where(j < j_dq, 1.0,
                      jnp.where(j < j_dk, DSA_HEAD_DIM ** -0.5, jnp.where(j < j_iq, 1.0, IDX_DIM ** -0.5)))
    c, s = _rope_cs(tab_ref[...], scale, rot)
    rows = xn_ref.shape[0] // INPROJ_ROW_GROUPS
    for g in range(INPROJ_ROW_GROUPS):
        rs = slice(g * rows, (g + 1) * rows)
        acc = lax.dot_general(xn_ref[rs, :], w_ref[...], NT_DIMS, preferred_element_type=F32)
        main_ref[rs, :] = _rope(acc, c[rs], s[rs]).astype(BF16)


def _inproj(h, gain, w_main, w_small, w_t, tab, layer):
    t, d = h.shape
    tm = min(1024, t)
    tn = 1024
    nt = w_t.shape[1]
    j_dq, j_dk, j_iq = OFF_DQ // tn, OFF_DK // tn, OFF_IQ // tn
    kern = functools.partial(_inproj_kernel, j_dq=j_dq, j_dk=j_dk, j_iq=j_iq)
    return pl.pallas_call(
        kern,
        grid=(t // tm, N_MAIN // tn),
        in_specs=[
            pl.BlockSpec((tm, d), lambda i, j: (i, 0)),
            pl.BlockSpec((1, d), lambda i, j: (0, 0)),
            pl.BlockSpec((None, tn, d), lambda i, j: (layer, j, 0)),
            pl.BlockSpec((None, N_SMALL, d), lambda i, j: (layer, 0, 0), pipeline_mode=pl.Buffered(1)),
            pl.BlockSpec((None, nt, d), lambda i, j: (layer, 0, 0), pipeline_mode=pl.Buffered(1)),
            pl.BlockSpec((None, tm, LANES), lambda i, j: ((j >= j_iq).astype(I32), i, 0)),
            pl.BlockSpec((None, tm, LANES), lambda i, j: (TAB_IDX, i, 0)),
        ],
        out_specs=[
            pl.BlockSpec((tm, tn), lambda i, j: (i, j)),
            pl.BlockSpec((tm, 2 * LANES), lambda i, j: (i, 0)),
            pl.BlockSpec((tm, LANES), lambda i, j: (i, 0)),
            pl.BlockSpec((tm // DSA_BLK, N_DSA, DSA_BLK), lambda i, j: (i, 0, 0)),
            pl.BlockSpec((LANES, tm), lambda i, j: (0, i)),
        ],
        out_shape=[
            jax.ShapeDtypeStruct((t, N_MAIN), BF16),
            jax.ShapeDtypeStruct((t, 2 * LANES), BF16),
            jax.ShapeDtypeStruct((t, LANES), F32),
            jax.ShapeDtypeStruct((t // DSA_BLK, N_DSA, DSA_BLK), BF16),
            jax.ShapeDtypeStruct((LANES, t), F32),
        ],
        scratch_shapes=[pltpu.VMEM((tm, d), BF16)],
        compiler_params=_params("parallel", "arbitrary"),
        name="inproj",
    )(h, gain, w_main, w_small, w_t, tab, tab)


def _gla_kernel(q_ref, k_ref, v_ref, gg_ref, misc_ref, w2_ref, ba_ref, gn_ref, o_ref, state_ref, *, blk):
    c_len = GLA_CHUNK
    n_ch = blk // c_len
    shift = c_len.bit_length() - 1
    assert 1 << shift == c_len

    @pl.when(pl.program_id(1) == 0)
    def _():
        state_ref[...] = jnp.zeros_like(state_ref)

    gpre = jnp.dot(misc_ref[...], w2_ref[...], preferred_element_type=F32,
                   precision=lax.Precision.HIGHEST) + ba_ref[...]
    log_a = (jnp.minimum(gpre, 0.0) - jnp.log1p(jnp.exp(-jnp.abs(gpre)))) * (1.0 / GLA_GATE_TAU)

    r = lax.broadcasted_iota(I32, (blk, blk), 0)
    c = lax.broadcasted_iota(I32, (blk, blk), 1)
    causal = (c <= r) & ((r >> shift) == (c >> shift))
    b = jnp.dot(causal.astype(F32), log_a, preferred_element_type=F32, precision=lax.Precision.HIGHEST)
    b3 = b.reshape(n_ch, c_len, N_GQ)
    b_last = b3[:, c_len - 1:c_len, :]
    decay = jnp.exp(b_last)
    k = k_ref[...].astype(F32)
    q_dec = (q_ref[...].astype(F32) * (GLA_DK ** -0.5) * jnp.exp(b)).astype(BF16)
    k_inv = (k * jnp.exp(-b)).astype(BF16)
    k_dec = (k * jnp.exp(b_last - b3).reshape(blk, N_GQ)).astype(BF16)
    gn = gn_ref[...]

    for hd in range(GLA_HEADS):
        ck = slice(hd * GLA_DK, (hd + 1) * GLA_DK)
        cv = slice(hd * GLA_DV, (hd + 1) * GLA_DV)
        attn = lax.dot_general(q_dec[:, ck], k_inv[:, ck], NT_DIMS, preferred_element_type=F32)
        attn = jnp.where(causal, attn, 0.0).astype(BF16)
        o = jnp.dot(attn, v_ref[:, cv], preferred_element_type=F32)
        st = state_ref[hd]
        inter = []
        for ci in range(n_ch):
            rows = slice(ci * c_len, (ci + 1) * c_len)
            inter.append(lax.dot_general(q_dec[rows, ck], st.astype(BF16), NT_DIMS,
                                         preferred_element_type=F32))
            d_st = lax.dot_general(v_ref[rows, cv], k_dec[rows, ck], TN_DIMS, preferred_element_type=F32)
            st = st * decay[ci, :, ck] + d_st
        state_ref[hd] = st
        o = o + jnp.concatenate(inter, axis=0)
        ms = jnp.mean(o * o, axis=-1, keepdims=True)
        y = o * lax.rsqrt(ms + EPS) * gn
        gate = gg_ref[:, cv].astype(F32)
        o_ref[:, cv] = (y * (gate / (1.0 + jnp.exp(-gate)))).astype(BF16)


def _gla(main, misc, w2, ba, gn, batch, seq):
    t = batch * seq
    blk = min(256, seq)
    nb = seq // blk
    kern = functools.partial(_gla_kernel, blk=blk)

    def rowmap(col):
        return lambda b, n: (b * nb + n, col)

    return pl.pallas_call(
        kern,
        grid=(batch, nb),
        in_specs=[
            pl.BlockSpec((blk, N_GQ), rowmap(OFF_GQ // N_GQ)),
            pl.BlockSpec((blk, N_GQ), rowmap(OFF_GK // N_GQ)),
            pl.BlockSpec((blk, N_GV), rowmap(OFF_GV // N_GV)),
            pl.BlockSpec((blk, N_GV), rowmap(OFF_GG // N_GV)),
            pl.BlockSpec((blk, LANES), rowmap(0)),
            pl.BlockSpec((LANES, N_GQ), lambda b, n: (0, 0)),
            pl.BlockSpec((1, N_GQ), lambda b, n: (0, 0)),
            pl.BlockSpec((1, GLA_DV), lambda b, n: (0, 0)),
        ],
        out_specs=pl.BlockSpec((blk, N_GV), rowmap(0)),
        out_shape=jax.ShapeDtypeStruct((t, N_GV), BF16),
        scratch_shapes=[pltpu.VMEM((GLA_HEADS, GLA_DV, GLA_DK), F32)],
        compiler_params=_params("parallel", "arbitrary"),
        name="gla",
    )(main, main, main, main, misc, w2, ba, gn)


def _dsa_kernel(q_ref, iq_ref, iwt_ref, k_ref, vt_ref, ikd_ref, o_ref,
                key_ref, hi_ref, lo_ref, bias_ref, acc_ref, p_ref, m_ref, l_ref, alpha_ref,
                *, blk, topk, idx_bits):
    qi = pl.program_id(1)
    nch = qi + 1
    krow = lax.broadcasted_iota(I32, (blk, blk), 0)
    qcol = lax.broadcasted_iota(I32, (blk, blk), 1)
    w_idx = iwt_ref[0:IDX_HEADS, :] * (IDX_HEADS ** -0.5)

    def score_body(c, carry):
        ks = pl.multiple_of(c * blk, blk)
        k_lo = ikd_ref[pl.ds(ks, blk), 0:LANES]
        k_hi = ikd_ref[pl.ds(ks, blk), LANES:2 * LANES]
        acc = jnp.zeros((blk, blk), F32)
        for p in range(IDX_HEADS // 2):
            iq_p = iq_ref[:, p * LANES:(p + 1) * LANES]
            l0 = lax.dot_general(k_lo, iq_p, NT_DIMS, preferred_element_type=F32)
            l1 = lax.dot_general(k_hi, iq_p, NT_DIMS, preferred_element_type=F32)
            acc = acc + w_idx[2 * p:2 * p + 1, :] * jnp.maximum(l0, 0.0)
            acc = acc + w_idx[2 * p + 1:2 * p + 2, :] * jnp.maximum(l1, 0.0)
        score = jnp.where(krow + (c - qi) * blk > qcol, NEG_INF, acc)
        score = jnp.where(score == 0.0, 0.0, score)
        bits = pltpu.bitcast(score, I32)
        key = bits ^ ((bits >> 31) & 0x7FFFFFFF)
        key_ref[c] = key
        hi_ref[c] = (key >> 16).astype(I16)
        return carry

    lax.fori_loop(0, nch, score_body, 0)

    def count16(ref, cand):
        def body(c, acc):
            m = jnp.where(ref[c] >= cand, jnp.ones((), BF16), jnp.zeros((), BF16))
            for i in range(blk // COUNT16_ROWS):
                acc = acc + m[i * COUNT16_ROWS:(i + 1) * COUNT16_ROWS]
            return acc
        acc = lax.fori_loop(0, nch, body, jnp.zeros((COUNT16_ROWS, blk), BF16))
        return jnp.sum(acc.astype(F32), axis=0, keepdims=True)

    def kth_largest16(ref, kth):
        v = jnp.where(count16(ref, jnp.zeros((1, blk), I16)) >= kth, 0, -(2 ** 15)).astype(I32)

        def body(i, v):
            cand = v | lax.shift_left(jnp.int32(1), 14 - i)
            return jnp.where(count16(ref, cand.astype(I16)) >= kth, cand, v)

        return lax.fori_loop(0, 15, body, v)

    def count(pred):
        def body(c, acc):
            m = jnp.where(pred(key_ref[c], c), 1.0, 0.0)
            return acc + jnp.sum(m.reshape(blk // COUNT_ROWS, COUNT_ROWS, blk), axis=0)
        acc = lax.fori_loop(0, nch, body, jnp.zeros((COUNT_ROWS, blk), F32))
        return jnp.sum(acc, axis=0, keepdims=True)

    kf = float(topk)
    thr_hi = kth_largest16(hi_ref, kf)
    n_above = count16(hi_ref, (thr_hi + 1).astype(I16))
    n_above = jnp.where(thr_hi == 2 ** 15 - 1, 0.0, n_above)

    def low_body(c, carry):
        key = key_ref[c]
        low = (key & 0xFFFF) - 2 ** 15
        lo_ref[c] = jnp.where((key >> 16) == thr_hi, low, -(2 ** 15)).astype(I16)
        return carry

    lax.fori_loop(0, nch, low_body, 0)
    thr_lo = kth_largest16(lo_ref, kf - n_above)
    thr = thr_hi * 2 ** 16 + (thr_lo + 2 ** 15)

    n_ge = count(lambda kc, c: kc >= thr)

    def resolve_ties():
        need = kf - count(lambda kc, c: kc > thr)

        def tie_body(i, last):
            cand = last | lax.shift_left(jnp.int32(1), idx_bits - 1 - i)
            below = count(lambda kc, c: (kc == thr) & (c * blk + krow < cand))
            return jnp.where(below < need, cand, last)

        return lax.fori_loop(0, idx_bits, tie_body, jnp.zeros((1, blk), I32))

    last = lax.cond(jnp.max(n_ge) > kf, resolve_ties,
                    lambda: jnp.full((1, blk), 2 ** idx_bits, I32))

    def write_bias(c, diagonal):
        kc = key_ref[c]
        tie_pos = jnp.where(kc == thr, c * blk + krow, -1)
        bias = jnp.where(kc >= thr, jnp.where(tie_pos > last, NEG_INF, 0.0), NEG_INF)
        if diagonal:
            bias = jnp.where(krow > qcol, NEG_INF, bias)
        bias_ref[c] = bias

    def bias_body(c, carry):
        write_bias(c, False)
        return carry

    lax.fori_loop(0, qi, bias_body, 0)
    write_bias(qi, True)

    heads = tuple((hd, slice(hd * DSA_HEAD_DIM, (hd + 1) * DSA_HEAD_DIM)) for hd in range(DSA_HEADS))

    def fold(x, op):
        return op(x.reshape(blk // 8, 8, blk), axis=0)

    m_ref[...] = jnp.full_like(m_ref, NEG_INF)
    l_ref[...] = jnp.zeros_like(l_ref)
    acc_ref[...] = jnp.zeros_like(acc_ref)

    def attn_body(c, carry):
        ks = pl.multiple_of(c * blk, blk)

        @pl.when(c >= 0)
        def _():
            for hd, hs in heads:
                s = lax.dot_general(k_ref[pl.ds(ks, blk), hs], q_ref[:, hs], NT_DIMS,
                                    preferred_element_type=F32) + bias_ref[c]
                m_prev = m_ref[hd, 0:1, :]
                m_new = jnp.maximum(m_prev, jnp.max(fold(s, jnp.max), axis=0, keepdims=True))
                alpha = jnp.exp(m_prev - m_new)
                m_ref[hd, 0:1, :] = m_new
                alpha_ref[hd, 0:1, :] = alpha
                p = jnp.exp(s - m_new)
                l_ref[hd] = alpha * l_ref[hd] + fold(p, jnp.sum)
                p_ref[hd] = p.astype(BF16)

        @pl.when(c >= 0)
        def _():
            for hd, hs in heads:
                pv = jnp.dot(vt_ref[c, hs, :], p_ref[hd], preferred_element_type=F32)
                acc_ref[hs, :] = alpha_ref[hd, 0:1, :] * acc_ref[hs, :] + pv

        return carry

    lax.fori_loop(0, nch, attn_body, 0)

    for hd, hs in heads:
        o_t = acc_ref[hs, :] / jnp.sum(l_ref[hd], axis=0, keepdims=True)
        o_ref[:, hs] = o_t.T.astype(BF16)


def _dsa(main, ikd, iwt, dvt, batch, seq):
    t = batch * seq
    blk = DSA_BLK
    nq = seq // blk
    topk = min(DSA_TOPK_MAX, seq // 4)
    idx_bits = max(1, (seq - 1).bit_length())
    assert (blk // COUNT16_ROWS) * nq <= BF16_EXACT_INT
    kern = functools.partial(_dsa_kernel, blk=blk, topk=topk, idx_bits=idx_bits)

    def qmap(col):
        return lambda b, i: (b * nq + i, col)

    return pl.pallas_call(
        kern,
        grid=(batch, nq),
        in_specs=[
            pl.BlockSpec((blk, N_DSA), qmap(OFF_DQ // N_DSA)),
            pl.BlockSpec((blk, N_IQ), qmap(OFF_IQ // N_IQ)),
            pl.BlockSpec((LANES, blk), lambda b, i: (0, b * nq + i)),
            pl.BlockSpec((seq, N_DSA), lambda b, i: (b, OFF_DK // N_DSA)),
            pl.BlockSpec((nq, N_DSA, blk), lambda b, i: (b, 0, 0)),
            pl.BlockSpec((seq, 2 * LANES), lambda b, i: (b, 0)),
        ],
        out_specs=pl.BlockSpec((blk, N_DSA), qmap(0)),
        out_shape=jax.ShapeDtypeStruct((t, N_DSA), BF16),
        scratch_shapes=[
            pltpu.VMEM((nq, blk, blk), I32),
            pltpu.VMEM((nq, blk, blk), I16),
            pltpu.VMEM((nq, blk, blk), I16),
            pltpu.VMEM((nq, blk, blk), F32),
            pltpu.VMEM((N_DSA, blk), F32),
            pltpu.VMEM((DSA_HEADS, blk, blk), BF16),
            pltpu.VMEM((DSA_HEADS, 8, blk), F32),
            pltpu.VMEM((DSA_HEADS, 8, blk), F32),
            pltpu.VMEM((DSA_HEADS, 8, blk), F32),
        ],
        compiler_params=_params("parallel", "arbitrary"),
        name="dsa",
    )(main, main, iwt, main, dvt, ikd)


def _outproj_kernel(og_ref, od_ref, h_ref, w_ref, g_ref, o_ref):
    rows = h_ref.shape[0] // OUTPROJ_ROW_GROUPS
    for g in range(OUTPROJ_ROW_GROUPS):
        rs = slice(g * rows, (g + 1) * rows)
        m = jnp.dot(og_ref[rs, :], w_ref[:N_GV, :], preferred_element_type=F32)
        m = m + jnp.dot(od_ref[rs, :], w_ref[N_GV:, :], preferred_element_type=F32)
        ms = jnp.mean(m * m, axis=-1, keepdims=True)
        o_ref[rs, :] = h_ref[rs, :] + m * lax.rsqrt(ms + EPS) * g_ref[...]


def _outproj(o_gla, o_dsa, h, w_out, gain, layer):
    t, d = h.shape
    tm = min(512, t)
    return pl.pallas_call(
        _outproj_kernel,
        grid=(t // tm,),
        in_specs=[
            pl.BlockSpec((tm, N_GV), lambda i: (i, 0)),
            pl.BlockSpec((tm, N_DSA), lambda i: (i, 0)),
            pl.BlockSpec((tm, d), lambda i: (i, 0)),
            pl.BlockSpec((None, N_GV + N_DSA, d), lambda i: (layer, 0, 0)),
            pl.BlockSpec((1, d), lambda i: (0, 0)),
        ],
        out_specs=pl.BlockSpec((tm, d), lambda i: (i, 0)),
        out_shape=jax.ShapeDtypeStruct((t, d), F32),
        compiler_params=_params("parallel"),
        name="outproj",
    )(o_gla, o_dsa, h, w_out, gain)


def _ffn_kernel(h_ref, gpre_ref, wu_ref, wd_ref, gpost_ref, o_ref, xn_ref, acc_ref):
    f = pl.program_id(1)
    last = pl.num_programs(1) - 1
    rows = h_ref.shape[0] // FFN_ROW_GROUPS
    groups = [slice(g * rows, (g + 1) * rows) for g in range(FFN_ROW_GROUPS)]

    def up(rs):
        u = jnp.maximum(jnp.dot(xn_ref[rs, :], wu_ref[...], preferred_element_type=F32), 0.0)
        return (u * u).astype(BF16)

    @pl.when(f == 0)
    def _():
        for rs in groups:
            x = h_ref[rs, :]
            ms = jnp.mean(x * x, axis=-1, keepdims=True)
            xn_ref[rs, :] = (x * lax.rsqrt(ms + EPS) * gpre_ref[...]).astype(BF16)
            acc_ref[rs, :] = jnp.dot(up(rs), wd_ref[...], preferred_element_type=F32)

    @pl.when((f > 0) & (f < last))
    def _():
        acc_ref[...] += jnp.dot(up(slice(None)), wd_ref[...], preferred_element_type=F32)

    @pl.when(f == last)
    def _():
        for rs in groups:
            y = acc_ref[rs, :] + jnp.dot(up(rs), wd_ref[...], preferred_element_type=F32)
            ms = jnp.mean(y * y, axis=-1, keepdims=True)
            o_ref[rs, :] = h_ref[rs, :] + y * lax.rsqrt(ms + EPS) * gpost_ref[...]


def _ffn(h, g_pre, w_up, w_down, g_post, layer):
    t, d = h.shape
    d_ff = w_up.shape[2]
    tm = min(512, t)
    tf = 1024
    assert d_ff // tf >= 2
    return pl.pallas_call(
        _ffn_kernel,
        grid=(t // tm, d_ff // tf),
        in_specs=[
            pl.BlockSpec((tm, d), lambda i, f: (i, 0)),
            pl.BlockSpec((1, d), lambda i, f: (0, 0)),
            pl.BlockSpec((None, d, tf), lambda i, f: (layer, 0, f)),
            pl.BlockSpec((None, tf, d), lambda i, f: (layer, f, 0)),
            pl.BlockSpec((1, d), lambda i, f: (0, 0)),
        ],
        out_specs=pl.BlockSpec((tm, d), lambda i, f: (i, 0)),
        out_shape=jax.ShapeDtypeStruct((t, d), F32),
        scratch_shapes=[pltpu.VMEM((tm, d), BF16), pltpu.VMEM((tm, d), F32)],
        compiler_params=_params("parallel", "arbitrary"),
        name="ffn",
    )(h, g_pre, w_up, w_down, g_post)


def _rope_table(positions, head_dim):
    r = head_dim // ROPE_FRACTION
    n_freq = r // 2
    inv_freq = ROPE_THETA ** (-(jnp.arange(0, r, 2, dtype=F32) / r))
    rep = ROPE_LANES // n_freq
    t = positions.size
    ang = positions.reshape(t, 1).astype(F32) * jnp.tile(inv_freq, rep)
    flat = lax.optimization_barrier(ang.reshape(-1))
    cos, sin = lax.optimization_barrier((jnp.cos(flat), jnp.sin(flat)))
    cos, sin = cos.reshape(t, ROPE_LANES), sin.reshape(t, ROPE_LANES)
    one = jnp.ones((cos.shape[0], ROPE_PAIR_SHIFT - ROPE_LANES), F32)
    return jnp.concatenate([cos, one, sin, one], axis=1)


def _rope_tables(positions):
    return jnp.stack([_rope_table(positions, DSA_HEAD_DIM), _rope_table(positions, IDX_DIM)])


def _dsa_lane_sources():
    half = DSA_HEAD_DIM // ROPE_FRACTION // 2
    assert half == ROPE_LANES
    cut = 2 * half + ROPE_PAIR_SHIFT - half
    return (list(range(half)) + list(range(2 * half, cut)) + list(range(half, 2 * half))
            + list(range(cut, DSA_HEAD_DIM)))


def _idx_lane_sources():
    half = IDX_DIM // ROPE_FRACTION // 2
    assert 2 * half == ROPE_LANES

    def a(lo, hi):
        return list(range(lo, hi))

    def b(lo, hi):
        return list(range(IDX_DIM + lo, IDX_DIM + hi))

    return (a(0, half) + b(0, half) + a(2 * half, IDX_DIM)
            + a(half, 2 * half) + b(half, 2 * half) + b(2 * half, IDX_DIM))


def _weight_plan(d_in):
    o_gr = 2 * N_GQ + 2 * N_GV
    o_dq = o_gr + GLA_GATE_RANK
    o_dk = o_dq + N_DSA
    o_dv = o_dk + N_DSA
    o_iq = o_dv + N_DSA
    o_ik = o_iq + N_IQ
    o_iw = o_ik + IDX_DIM
    assert o_iw + IDX_HEADS == d_in
    half = IDX_DIM // ROPE_FRACTION // 2
    none = [-1]

    def shifted(base, sources):
        return [base + s for s in sources]

    dsa, idx, ident = _dsa_lane_sources(), _idx_lane_sources(), list(range(LANES))
    tiles = {
        "rot": ([shifted(o_dq + LANES * h, dsa) for h in range(DSA_HEADS)]
                + [shifted(o_dk + LANES * h, dsa) for h in range(DSA_HEADS)]
                + [shifted(o_iq + LANES * p, idx) for p in range(IDX_HEADS // 2)]),
        "small": [
            (shifted(o_ik, range(half)) + none * half + shifted(o_ik, range(2 * half, IDX_DIM))
             + shifted(o_ik, range(half, 2 * half)) + none * (LANES - ROPE_PAIR_SHIFT - half)),
            (none * half + shifted(o_ik, range(half)) + none * (ROPE_PAIR_SHIFT - 2 * half) + none * half
             + shifted(o_ik, range(half, 2 * half)) + shifted(o_ik, range(2 * half, IDX_DIM))),
            (shifted(o_iw, range(IDX_HEADS)) + shifted(o_gr, range(GLA_GATE_RANK))
             + none * (LANES - IDX_HEADS - GLA_GATE_RANK)),
        ],
        "fm": ([shifted(o_dv + LANES * t, ident) for t in range(N_DSA // LANES)]
               + [shifted(o_iw, range(IDX_HEADS)) + none * (LANES - IDX_HEADS)]),
    }
    mats, plans = [], {}
    for name, tile_list in tiles.items():
        plans[name] = []
        for cols in tile_list:
            assert len(cols) == LANES
            parts = []
            for src_tile in sorted({c // LANES for c in cols if c >= 0}):
                m = np.zeros((LANES, LANES), np.float32)
                for dst, c in enumerate(cols):
                    if c >= 0 and c // LANES == src_tile:
                        m[c % LANES, dst] = 1.0
                for mat_id, known in enumerate(mats):
                    if np.array_equal(known, m):
                        break
                else:
                    mat_id = len(mats)
                    mats.append(m)
                parts.append((src_tile, mat_id))
            plans[name].append(parts)
    return plans, np.stack(mats)


def _prep_kernel(w_ref, p_ref, main_ref, small_ref, fm_ref, *, plans, n_plain, d_in):
    cols = w_ref.shape[1]

    def src_tile(t):
        if (t + 1) * LANES > d_in:
            v = w_ref[t * LANES:d_in, :].astype(BF16)
            return jnp.concatenate([v, jnp.zeros(((t + 1) * LANES - d_in, cols), BF16)], axis=0)
        return w_ref[t * LANES:(t + 1) * LANES, :].astype(BF16)

    def gathered(parts):
        acc = None
        for t, mat_id in parts:
            y = jnp.dot(p_ref[mat_id], src_tile(t), preferred_element_type=F32)
            acc = y if acc is None else acc + y
        return acc.astype(BF16)

    main_ref[:n_plain, :] = w_ref[:n_plain, :].astype(BF16)
    for i, parts in enumerate(plans["rot"]):
        main_ref[n_plain + i * LANES:n_plain + (i + 1) * LANES, :] = gathered(parts)
    for i, parts in enumerate(plans["small"]):
        small_ref[i * LANES:(i + 1) * LANES, :] = gathered(parts)
    for i, parts in enumerate(plans["fm"]):
        fm_ref[i * LANES:(i + 1) * LANES, :] = gathered(parts)


def _prep_weights(w_t):
    depth, d_in, d = w_t.shape
    plans, mats = _weight_plan(d_in)
    cb = min(256, d)
    n_plain = OFF_DQ
    n_fm = N_DSA + LANES
    kern = functools.partial(_prep_kernel, plans=plans, n_plain=n_plain, d_in=d_in)
    return pl.pallas_call(
        kern,
        grid=(depth, d // cb),
        in_specs=[
            pl.BlockSpec((None, d_in, cb), lambda l, r: (l, 0, r)),
            pl.BlockSpec(mats.shape, lambda l, r: (0, 0, 0)),
        ],
        out_specs=[
            pl.BlockSpec((None, N_MAIN, cb), lambda l, r: (l, 0, r)),
            pl.BlockSpec((None, N_SMALL, cb), lambda l, r: (l, 0, r)),
            pl.BlockSpec((None, n_fm, cb), lambda l, r: (l, 0, r)),
        ],
        out_shape=[
            jax.ShapeDtypeStruct((depth, N_MAIN, d), BF16),
            jax.ShapeDtypeStruct((depth, N_SMALL, d), BF16),
            jax.ShapeDtypeStruct((depth, n_fm, d), BF16),
        ],
        compiler_params=_params("parallel", "parallel"),
        name="prep_weights",
    )(w_t, jnp.asarray(mats.transpose(0, 2, 1), BF16))


def kernel(x, positions, norm_mix_pre, w_in, gla_wa2, gla_ba, gla_norm, w_out, norm_mix_post,
           norm_ffn_pre, w_up, w_down, norm_ffn_post):
    batch, seq, d = x.shape
    depth = w_in.shape[0]
    assert seq % DSA_BLK == 0
    tab = _rope_tables(positions)
    w2 = jnp.pad(gla_wa2, ((0, 0), (IDX_HEADS, LANES - IDX_HEADS - GLA_GATE_RANK), (0, 0)))
    h = x.reshape(batch * seq, d)
    w_main, w_small, w_fm = _prep_weights(jnp.swapaxes(w_in, 1, 2))
    w_out, w_up, w_down = (w.astype(BF16) for w in (w_out, w_up, w_down))
    for l in range(depth):
        main, ikd, misc, dvt, iwt = _inproj(h, norm_mix_pre[l][None, :], w_main, w_small, w_fm, tab, l)
        o_gla = _gla(main, misc, w2[l], gla_ba[l][None, :], gla_norm[l][None, :], batch, seq)
        o_dsa = _dsa(main, ikd, iwt, dvt, batch, seq)
        h = _outproj(o_gla, o_dsa, h, w_out, norm_mix_post[l][None, :], l)
        h = _ffn(h, norm_ffn_pre[l][None, :], w_up, w_down, norm_ffn_post[l][None, :], l)
    return h.reshape(batch, seq, d)
```

```python
import functools

import jax
import jax.numpy as jnp
import numpy as np
from jax import lax
from jax.experimental import pallas as pl
from jax.experimental.pallas import tpu as pltpu

F32 = jnp.float32
BF16 = jnp.bfloat16
I32 = jnp.int32
I16 = jnp.int16

GLA_HEADS = 4
GLA_DK = 128
GLA_DV = 256
GLA_GATE_RANK = 16
GLA_GATE_TAU = 16.0
GLA_CHUNK = 64
DSA_HEADS = 8
DSA_HEAD_DIM = 128
IDX_HEADS = 16
IDX_DIM = 64
DSA_TOPK_MAX = 256
ROPE_THETA = 500000.0
ROPE_FRACTION = 4
EPS = 1e-6
NEG_INF = -1e30
INT_MIN = -(2 ** 31)

LANES = 128
N_GQ = GLA_HEADS * GLA_DK
N_GV = GLA_HEADS * GLA_DV
N_DSA = DSA_HEADS * DSA_HEAD_DIM
N_IQ = IDX_HEADS * IDX_DIM
OFF_GQ = 0
OFF_GK = OFF_GQ + N_GQ
OFF_GV = OFF_GK + N_GQ
OFF_GG = OFF_GV + N_GV
OFF_DQ = OFF_GG + N_GV
OFF_DK = OFF_DQ + N_DSA
OFF_IQ = OFF_DK + N_DSA
N_MAIN = OFF_IQ + N_IQ
N_SMALL = 3 * LANES
DSA_BLK = 256
ROPE_LANES = 16
ROPE_PAIR_SHIFT = LANES // 2
TAB_DSA, TAB_IDX = range(2)
INPROJ_ROW_GROUPS = 4
FFN_ROW_GROUPS = 2
COUNT_ROWS = 32
COUNT16_ROWS = 64
BF16_EXACT_INT = 256
VMEM_LIMIT = 60 * 1024 * 1024

NT_DIMS = (((1,), (1,)), ((), ()))
TN_DIMS = (((0,), (0,)), ((), ()))


def _params(*sem):
    return pltpu.CompilerParams(dimension_semantics=sem, vmem_limit_bytes=VMEM_LIMIT)


def _rope_cs(tab, scale, rot):
    lane = lax.broadcasted_iota(I32, (1, LANES), 1)
    lo = lane < ROPE_LANES
    hi = (lane >= ROPE_PAIR_SHIFT) & (lane < ROPE_PAIR_SHIFT + ROPE_LANES)
    swapped = pltpu.roll(tab, ROPE_PAIR_SHIFT, 1)
    c = jnp.where(hi, swapped, tab)
    s = jnp.where(lo, -swapped, jnp.where(hi, tab, 0.0))
    return (1.0 + rot * (c - 1.0)) * scale, s * (rot * scale)


def _rope(x, c, s):
    parts = []
    for g in range(x.shape[1] // LANES):
        xg = x[:, g * LANES:(g + 1) * LANES]
        parts.append(xg * c + pltpu.roll(xg, ROPE_PAIR_SHIFT, 1) * s)
    return parts[0] if len(parts) == 1 else jnp.concatenate(parts, axis=1)


def _inproj_kernel(x_ref, g_ref, w_ref, ws_ref, wt_ref, tab_ref, tabk_ref,
                   main_ref, ikd_ref, misc_ref, dvt_ref, iwt_ref, xn_ref, *, j_dq, j_dk, j_iq):
    j = pl.program_id(1)

    @pl.when(j == 0)
    def _():
        x = x_ref[...]
        ms = jnp.mean(x * x, axis=-1, keepdims=True)
        xn_ref[...] = (x * lax.rsqrt(ms + EPS) * g_ref[...]).astype(BF16)
        small = lax.dot_general(xn_ref[...], ws_ref[...], NT_DIMS, preferred_element_type=F32)
        ikd_ref[...] = _rope(small[:, :2 * LANES], *_rope_cs(tabk_ref[...], 1.0, 1.0)).astype(BF16)
        misc_ref[...] = small[:, 2 * LANES:]
        iwt_ref[...] = small[:, 2 * LANES:].T
        fm = lax.dot_general(wt_ref[...], xn_ref[...], NT_DIMS, preferred_element_type=F32)
        for ci in range(dvt_ref.shape[0]):
            dvt_ref[ci] = fm[:, ci * DSA_BLK:(ci + 1) * DSA_BLK].astype(BF16)

    rot = jnp.where(j >= j_dq, 1.0, 0.0)
    scale = jnp.where(j < j_dq, 1.0,
                      jnp.where(j < j_dk, DSA_HEAD_DIM ** -0.5, jnp.where(j < j_iq, 1.0, IDX_DIM ** -0.5)))
    c, s = _rope_cs(tab_ref[...], scale, rot)
    rows = xn_ref.shape[0] // INPROJ_ROW_GROUPS
    for g in range(INPROJ_ROW_GROUPS):
        rs = slice(g * rows, (g + 1) * rows)
        acc = lax.dot_general(xn_ref[rs, :], w_ref[...], NT_DIMS, preferred_element_type=F32)
        main_ref[rs, :] = _rope(acc, c[rs], s[rs]).astype(BF16)


def _inproj(h, gain, w_main, w_small, w_t, tab, layer):
    t, d = h.shape
    tm = min(1024, t)
    tn = 1024
    nt = w_t.shape[1]
    j_dq, j_dk, j_iq = OFF_DQ // tn, OFF_DK // tn, OFF_IQ // tn
    kern = functools.partial(_inproj_kernel, j_dq=j_dq, j_dk=j_dk, j_iq=j_iq)
    return pl.pallas_call(
        kern,
        grid=(t // tm, N_MAIN // tn),
        in_specs=[
            pl.BlockSpec((tm, d), lambda i, j: (i, 0)),
            pl.BlockSpec((1, d), lambda i, j: (0, 0)),
            pl.BlockSpec((None, tn, d), lambda i, j: (layer, j, 0)),
            pl.BlockSpec((None, N_SMALL, d), lambda i, j: (layer, 0, 0), pipeline_mode=pl.Buffered(1)),
            pl.BlockSpec((None, nt, d), lambda i, j: (layer, 0, 0), pipeline_mode=pl.Buffered(1)),
            pl.BlockSpec((None, tm, LANES), lambda i, j: ((j >= j_iq).astype(I32), i, 0)),
            pl.BlockSpec((None, tm, LANES), lambda i, j: (TAB_IDX, i, 0)),
        ],
        out_specs=[
            pl.BlockSpec((tm, tn), lambda i, j: (i, j)),
            pl.BlockSpec((tm, 2 * LANES), lambda i, j: (i, 0)),
            pl.BlockSpec((tm, LANES), lambda i, j: (i, 0)),
            pl.BlockSpec((tm // DSA_BLK, N_DSA, DSA_BLK), lambda i, j: (i, 0, 0)),
            pl.BlockSpec((LANES, tm), lambda i, j: (0, i)),
        ],
        out_shape=[
            jax.ShapeDtypeStruct((t, N_MAIN), BF16),
            jax.ShapeDtypeStruct((t, 2 * LANES), BF16),
            jax.ShapeDtypeStruct((t, LANES), F32),
            jax.ShapeDtypeStruct((t // DSA_BLK, N_DSA, DSA_BLK), BF16),
            jax.ShapeDtypeStruct((LANES, t), F32),
        ],
        scratch_shapes=[pltpu.VMEM((tm, d), BF16)],
        compiler_params=_params("parallel", "arbitrary"),
        name="inproj",
    )(h, gain, w_main, w_small, w_t, tab, tab)


def _gla_kernel(q_ref, k_ref, v_ref, gg_ref, misc_ref, w2_ref, ba_ref, gn_ref, o_ref, state_ref, *, blk):
    c_len = GLA_CHUNK
    n_ch = blk // c_len
    shift = c_len.bit_length() - 1
    assert 1 << shift == c_len

    @pl.when(pl.program_id(1) == 0)
    def _():
        state_ref[...] = jnp.zeros_like(state_ref)

    gpre = jnp.dot(misc_ref[...], w2_ref[...], preferred_element_type=F32,
                   precision=lax.Precision.HIGHEST) + ba_ref[...]
    log_a = (jnp.minimum(gpre, 0.0) - jnp.log1p(jnp.exp(-jnp.abs(gpre)))) * (1.0 / GLA_GATE_TAU)

    r = lax.broadcasted_iota(I32, (blk, blk), 0)
    c = lax.broadcasted_iota(I32, (blk, blk), 1)
    causal = (c <= r) & ((r >> shift) == (c >> shift))
    b = jnp.dot(causal.astype(F32), log_a, preferred_element_type=F32, precision=lax.Precision.HIGHEST)
    b3 = b.reshape(n_ch, c_len, N_GQ)
    b_last = b3[:, c_len - 1:c_len, :]
    decay = jnp.exp(b_last)
    k = k_ref[...].astype(F32)
    q_dec = (q_ref[...].astype(F32) * (GLA_DK ** -0.5) * jnp.exp(b)).astype(BF16)
    k_inv = (k * jnp.exp(-b)).astype(BF16)
    k_dec = (k * jnp.exp(b_last - b3).reshape(blk, N_GQ)).astype(BF16)
    gn = gn_ref[...]

    for hd in range(GLA_HEADS):
        ck = slice(hd * GLA_DK, (hd + 1) * GLA_DK)
        cv = slice(hd * GLA_DV, (hd + 1) * GLA_DV)
        attn = lax.dot_general(q_dec[:, ck], k_inv[:, ck], NT_DIMS, preferred_element_type=F32)
        attn = jnp.where(causal, attn, 0.0).astype(BF16)
        o = jnp.dot(attn, v_ref[:, cv], preferred_element_type=F32)
        st = state_ref[hd]
        inter = []
        for ci in range(n_ch):
            rows = slice(ci * c_len, (ci + 1) * c_len)
            inter.append(lax.dot_general(q_dec[rows, ck], st.astype(BF16), NT_DIMS,
                                         preferred_element_type=F32))
            d_st = lax.dot_general(v_ref[rows, cv], k_dec[rows, ck], TN_DIMS, preferred_element_type=F32)
            st = st * decay[ci, :, ck] + d_st
        state_ref[hd] = st
        o = o + jnp.concatenate(inter, axis=0)
        ms = jnp.mean(o * o, axis=-1, keepdims=True)
        y = o * lax.rsqrt(ms + EPS) * gn
        gate = gg_ref[:, cv].astype(F32)
        o_ref[:, cv] = (y * (gate / (1.0 + jnp.exp(-gate)))).astype(BF16)


def _gla(main, misc, w2, ba, gn, batch, seq):
    t = batch * seq
    blk = min(256, seq)
    nb = seq // blk
    kern = functools.partial(_gla_kernel, blk=blk)

    def rowmap(col):
        return lambda b, n: (b * nb + n, col)

    return pl.pallas_call(
        kern,
        grid=(batch, nb),
        in_specs=[
            pl.BlockSpec((blk, N_GQ), rowmap(OFF_GQ // N_GQ)),
            pl.BlockSpec((blk, N_GQ), rowmap(OFF_GK // N_GQ)),
            pl.BlockSpec((blk, N_GV), rowmap(OFF_GV // N_GV)),
            pl.BlockSpec((blk, N_GV), rowmap(OFF_GG // N_GV)),
            pl.BlockSpec((blk, LANES), rowmap(0)),
            pl.BlockSpec((LANES, N_GQ), lambda b, n: (0, 0)),
            pl.BlockSpec((1, N_GQ), lambda b, n: (0, 0)),
            pl.BlockSpec((1, GLA_DV), lambda b, n: (0, 0)),
        ],
        out_specs=pl.BlockSpec((blk, N_GV), rowmap(0)),
        out_shape=jax.ShapeDtypeStruct((t, N_GV), BF16),
        scratch_shapes=[pltpu.VMEM((GLA_HEADS, GLA_DV, GLA_DK), F32)],
        compiler_params=_params("parallel", "arbitrary"),
        name="gla",
    )(main, main, main, main, misc, w2, ba, gn)


def _dsa_kernel(q_ref, iq_ref, iwt_ref, k_ref, vt_ref, ikd_ref, o_ref,
                key_ref, hi_ref, lo_ref, bias_ref, acc_ref, p_ref, m_ref, l_ref, alpha_ref,
                *, blk, topk, idx_bits):
    qi = pl.program_id(1)
    nch = qi + 1
    krow = lax.broadcasted_iota(I32, (blk, blk), 0)
    qcol = lax.broadcasted_iota(I32, (blk, blk), 1)
    w_idx = iwt_ref[0:IDX_HEADS, :] * (IDX_HEADS ** -0.5)

    def score_body(c, carry):
        ks = pl.multiple_of(c * blk, blk)
        k_lo = ikd_ref[pl.ds(ks, blk), 0:LANES]
        k_hi = ikd_ref[pl.ds(ks, blk), LANES:2 * LANES]
        acc = jnp.zeros((blk, blk), F32)
        for p in range(IDX_HEADS // 2):
            iq_p = iq_ref[:, p * LANES:(p + 1) * LANES]
            l0 = lax.dot_general(k_lo, iq_p, NT_DIMS, preferred_element_type=F32)
            l1 = lax.dot_general(k_hi, iq_p, NT_DIMS, preferred_element_type=F32)
            acc = acc + w_idx[2 * p:2 * p + 1, :] * jnp.maximum(l0, 0.0)
            acc = acc + w_idx[2 * p + 1:2 * p + 2, :] * jnp.maximum(l1, 0.0)
        score = jnp.where(krow + (c - qi) * blk > qcol, NEG_INF, acc)
        score = jnp.where(score == 0.0, 0.0, score)
        bits = pltpu.bitcast(score, I32)
        key = bits ^ ((bits >> 31) & 0x7FFFFFFF)
        key_ref[c] = key
        hi_ref[c] = (key >> 16).astype(I16)
        return carry

    lax.fori_loop(0, nch, score_body, 0)

    def count16(ref, cand):
        def body(c, acc):
            m = jnp.where(ref[c] >= cand, jnp.ones((), BF16), jnp.zeros((), BF16))
            for i in range(blk // COUNT16_ROWS):
                acc = acc + m[i * COUNT16_ROWS:(i + 1) * COUNT16_ROWS]
            return acc
        acc = lax.fori_loop(0, nch, body, jnp.zeros((COUNT16_ROWS, blk), BF16))
        return jnp.sum(acc.astype(F32), axis=0, keepdims=True)

    def kth_largest16(ref, kth):
        v = jnp.where(count16(ref, jnp.zeros((1, blk), I16)) >= kth, 0, -(2 ** 15)).astype(I32)

        def body(i, v):
            cand = v | lax.shift_left(jnp.int32(1), 14 - i)
            return jnp.where(count16(ref, cand.astype(I16)) >= kth, cand, v)

        return lax.fori_loop(0, 15, body, v)

    def count(pred):
        def body(c, acc):
            m = jnp.where(pred(key_ref[c], c), 1.0, 0.0)
            return acc + jnp.sum(m.reshape(blk // COUNT_ROWS, COUNT_ROWS, blk), axis=0)
        acc = lax.fori_loop(0, nch, body, jnp.zeros((COUNT_ROWS, blk), F32))
        return jnp.sum(acc, axis=0, keepdims=True)

    kf = float(topk)
    thr_hi = kth_largest16(hi_ref, kf)
    n_above = count16(hi_ref, (thr_hi + 1).astype(I16))
    n_above = jnp.where(thr_hi == 2 ** 15 - 1, 0.0, n_above)

    def low_body(c, carry):
        key = key_ref[c]
        low = (key & 0xFFFF) - 2 ** 15
        lo_ref[c] = jnp.where((key >> 16) == thr_hi, low, -(2 ** 15)).astype(I16)
        return carry

    lax.fori_loop(0, nch, low_body, 0)
    thr_lo = kth_largest16(lo_ref, kf - n_above)
    thr = thr_hi * 2 ** 16 + (thr_lo + 2 ** 15)

    n_ge = count(lambda kc, c: kc >= thr)

    def resolve_ties():
        need = kf - count(lambda kc, c: kc > thr)

        def tie_body(i, last):
            cand = last | lax.shift_left(jnp.int32(1), idx_bits - 1 - i)
            below = count(lambda kc, c: (kc == thr) & (c * blk + krow < cand))
            return jnp.where(below < need, cand, last)

        return lax.fori_loop(0, idx_bits, tie_body, jnp.zeros((1, blk), I32))

    last = lax.cond(jnp.max(n_ge) > kf, resolve_ties,
                    lambda: jnp.full((1, blk), 2 ** idx_bits, I32))

    def write_bias(c, diagonal):
        kc = key_ref[c]
        tie_pos = jnp.where(kc == thr, c * blk + krow, -1)
        bias = jnp.where(kc >= thr, jnp.where(tie_pos > last, NEG_INF, 0.0), NEG_INF)
        if diagonal:
            bias = jnp.where(krow > qcol, NEG_INF, bias)
        bias_ref[c] = bias

    def bias_body(c, carry):
        write_bias(c, False)
        return carry

    lax.fori_loop(0, qi, bias_body, 0)
    write_bias(qi, True)

    heads = tuple((hd, slice(hd * DSA_HEAD_DIM, (hd + 1) * DSA_HEAD_DIM)) for hd in range(DSA_HEADS))

    def fold(x, op):
        return op(x.reshape(blk // 8, 8, blk), axis=0)

    m_ref[...] = jnp.full_like(m_ref, NEG_INF)
    l_ref[...] = jnp.zeros_like(l_ref)
    acc_ref[...] = jnp.zeros_like(acc_ref)

    def attn_body(c, carry):
        ks = pl.multiple_of(c * blk, blk)

        @pl.when(c >= 0)
        def _():
            for hd, hs in heads:
                s = lax.dot_general(k_ref[pl.ds(ks, blk), hs], q_ref[:, hs], NT_DIMS,
                                    preferred_element_type=F32) + bias_ref[c]
                m_prev = m_ref[hd, 0:1, :]
                m_new = jnp.maximum(m_prev, jnp.max(fold(s, jnp.max), axis=0, keepdims=True))
                alpha = jnp.exp(m_prev - m_new)
                m_ref[hd, 0:1, :] = m_new
                alpha_ref[hd, 0:1, :] = alpha
                p = jnp.exp(s - m_new)
                l_ref[hd] = alpha * l_ref[hd] + fold(p, jnp.sum)
                p_ref[hd] = p.astype(BF16)

        @pl.when(c >= 0)
        def _():
            for hd, hs in heads:
                pv = jnp.dot(vt_ref[c, hs, :], p_ref[hd], preferred_element_type=F32)
                acc_ref[hs, :] = alpha_ref[hd, 0:1, :] * acc_ref[hs, :] + pv

        return carry

    lax.fori_loop(0, nch, attn_body, 0)

    for hd, hs in heads:
        o_t = acc_ref[hs, :] / jnp.sum(l_ref[hd], axis=0, keepdims=True)
        o_ref[:, hs] = o_t.T.astype(BF16)


def _dsa(main, ikd, iwt, dvt, batch, seq):
    t = batch * seq
    blk = DSA_BLK
    nq = seq // blk
    topk = min(DSA_TOPK_MAX, seq // 4)
    idx_bits = max(1, (seq - 1).bit_length())
    assert (blk // COUNT16_ROWS) * nq <= BF16_EXACT_INT
    kern = functools.partial(_dsa_kernel, blk=blk, topk=topk, idx_bits=idx_bits)

    def qmap(col):
        return lambda b, i: (b * nq + i, col)

    return pl.pallas_call(
        kern,
        grid=(batch, nq),
        in_specs=[
            pl.BlockSpec((blk, N_DSA), qmap(OFF_DQ // N_DSA)),
            pl.BlockSpec((blk, N_IQ), qmap(OFF_IQ // N_IQ)),
            pl.BlockSpec((LANES, blk), lambda b, i: (0, b * nq + i)),
            pl.BlockSpec((seq, N_DSA), lambda b, i: (b, OFF_DK // N_DSA)),
            pl.BlockSpec((nq, N_DSA, blk), lambda b, i: (b, 0, 0)),
            pl.BlockSpec((seq, 2 * LANES), lambda b, i: (b, 0)),
        ],
        out_specs=pl.BlockSpec((blk, N_DSA), qmap(0)),
        out_shape=jax.ShapeDtypeStruct((t, N_DSA), BF16),
        scratch_shapes=[
            pltpu.VMEM((nq, blk, blk), I32),
            pltpu.VMEM((nq, blk, blk), I16),
            pltpu.VMEM((nq, blk, blk), I16),
            pltpu.VMEM((nq, blk, blk), F32),
            pltpu.VMEM((N_DSA, blk), F32),
            pltpu.VMEM((DSA_HEADS, blk, blk), BF16),
            pltpu.VMEM((DSA_HEADS, 8, blk), F32),
            pltpu.VMEM((DSA_HEADS, 8, blk), F32),
            pltpu.VMEM((DSA_HEADS, 8, blk), F32),
        ],
        compiler_params=_params("parallel", "arbitrary"),
        name="dsa",
    )(main, main, iwt, main, dvt, ikd)


def _outproj_kernel(og_ref, od_ref, h_ref, w_ref, g_ref, o_ref):
    m = jnp.dot(og_ref[...], w_ref[:N_GV, :], preferred_element_type=F32)
    m = m + jnp.dot(od_ref[...], w_ref[N_GV:, :], preferred_element_type=F32)
    ms = jnp.mean(m * m, axis=-1, keepdims=True)
    o_ref[...] = h_ref[...] + m * lax.rsqrt(ms + EPS) * g_ref[...]


def _outproj(o_gla, o_dsa, h, w_out, gain, layer):
    t, d = h.shape
    tm = min(512, t)
    return pl.pallas_call(
        _outproj_kernel,
        grid=(t // tm,),
        in_specs=[
            pl.BlockSpec((tm, N_GV), lambda i: (i, 0)),
            pl.BlockSpec((tm, N_DSA), lambda i: (i, 0)),
            pl.BlockSpec((tm, d), lambda i: (i, 0)),
            pl.BlockSpec((None, N_GV + N_DSA, d), lambda i: (layer, 0, 0)),
            pl.BlockSpec((1, d), lambda i: (0, 0)),
        ],
        out_specs=pl.BlockSpec((tm, d), lambda i: (i, 0)),
        out_shape=jax.ShapeDtypeStruct((t, d), F32),
        compiler_params=_params("parallel"),
        name="outproj",
    )(o_gla, o_dsa, h, w_out, gain)


def _ffn_kernel(h_ref, gpre_ref, wu_ref, wd_ref, gpost_ref, o_ref, xn_ref, acc_ref):
    f = pl.program_id(1)
    last = pl.num_programs(1) - 1
    rows = h_ref.shape[0] // FFN_ROW_GROUPS
    groups = [slice(g * rows, (g + 1) * rows) for g in range(FFN_ROW_GROUPS)]

    def up(rs):
        u = jnp.maximum(jnp.dot(xn_ref[rs, :], wu_ref[...], preferred_element_type=F32), 0.0)
        return (u * u).astype(BF16)

    @pl.when(f == 0)
    def _():
        for rs in groups:
            x = h_ref[rs, :]
            ms = jnp.mean(x * x, axis=-1, keepdims=True)
            xn_ref[rs, :] = (x * lax.rsqrt(ms + EPS) * gpre_ref[...]).astype(BF16)
            acc_ref[rs, :] = jnp.dot(up(rs), wd_ref[...], preferred_element_type=F32)

    @pl.when((f > 0) & (f < last))
    def _():
        acc_ref[...] += jnp.dot(up(slice(None)), wd_ref[...], preferred_element_type=F32)

    @pl.when(f == last)
    def _():
        for rs in groups:
            y = acc_ref[rs, :] + jnp.dot(up(rs), wd_ref[...], preferred_element_type=F32)
            ms = jnp.mean(y * y, axis=-1, keepdims=True)
            o_ref[rs, :] = h_ref[rs, :] + y * lax.rsqrt(ms + EPS) * gpost_ref[...]


def _ffn(h, g_pre, w_up, w_down, g_post, layer):
    t, d = h.shape
    d_ff = w_up.shape[2]
    tm = min(512, t)
    tf = 1024
    assert d_ff // tf >= 2
    return pl.pallas_call(
        _ffn_kernel,
        grid=(t // tm, d_ff // tf),
        in_specs=[
            pl.BlockSpec((tm, d), lambda i, f: (i, 0)),
            pl.BlockSpec((1, d), lambda i, f: (0, 0)),
            pl.BlockSpec((None, d, tf), lambda i, f: (layer, 0, f)),
            pl.BlockSpec((None, tf, d), lambda i, f: (layer, f, 0)),
            pl.BlockSpec((1, d), lambda i, f: (0, 0)),
        ],
        out_specs=pl.BlockSpec((tm, d), lambda i, f: (i, 0)),
        out_shape=jax.ShapeDtypeStruct((t, d), F32),
        scratch_shapes=[pltpu.VMEM((tm, d), BF16), pltpu.VMEM((tm, d), F32)],
        compiler_params=_params("parallel", "arbitrary"),
        name="ffn",
    )(h, g_pre, w_up, w_down, g_post)


def _rope_table(positions, head_dim):
    r = head_dim // ROPE_FRACTION
    n_freq = r // 2
    inv_freq = ROPE_THETA ** (-(jnp.arange(0, r, 2, dtype=F32) / r))
    rep = ROPE_LANES // n_freq
    t = positions.size
    ang = positions.reshape(t, 1).astype(F32) * jnp.tile(inv_freq, rep)
    flat = lax.optimization_barrier(ang.reshape(-1))
    cos, sin = lax.optimization_barrier((jnp.cos(flat), jnp.sin(flat)))
    cos, sin = cos.reshape(t, ROPE_LANES), sin.reshape(t, ROPE_LANES)
    one = jnp.ones((cos.shape[0], ROPE_PAIR_SHIFT - ROPE_LANES), F32)
    return jnp.concatenate([cos, one, sin, one], axis=1)


def _rope_tables(positions):
    return jnp.stack([_rope_table(positions, DSA_HEAD_DIM), _rope_table(positions, IDX_DIM)])


def _dsa_lane_sources():
    half = DSA_HEAD_DIM // ROPE_FRACTION // 2
    assert half == ROPE_LANES
    cut = 2 * half + ROPE_PAIR_SHIFT - half
    return (list(range(half)) + list(range(2 * half, cut)) + list(range(half, 2 * half))
            + list(range(cut, DSA_HEAD_DIM)))


def _idx_lane_sources():
    half = IDX_DIM // ROPE_FRACTION // 2
    assert 2 * half == ROPE_LANES

    def a(lo, hi):
        return list(range(lo, hi))

    def b(lo, hi):
        return list(range(IDX_DIM + lo, IDX_DIM + hi))

    return (a(0, half) + b(0, half) + a(2 * half, IDX_DIM)
            + a(half, 2 * half) + b(half, 2 * half) + b(2 * half, IDX_DIM))


def _weight_plan(d_in):
    o_gr = 2 * N_GQ + 2 * N_GV
    o_dq = o_gr + GLA_GATE_RANK
    o_dk = o_dq + N_DSA
    o_dv = o_dk + N_DSA
    o_iq = o_dv + N_DSA
    o_ik = o_iq + N_IQ
    o_iw = o_ik + IDX_DIM
    assert o_iw + IDX_HEADS == d_in
    half = IDX_DIM // ROPE_FRACTION // 2
    none = [-1]

    def shifted(base, sources):
        return [base + s for s in sources]

    dsa, idx, ident = _dsa_lane_sources(), _idx_lane_sources(), list(range(LANES))
    tiles = {
        "rot": ([shifted(o_dq + LANES * h, dsa) for h in range(DSA_HEADS)]
                + [shifted(o_dk + LANES * h, dsa) for h in range(DSA_HEADS)]
                + [shifted(o_iq + LANES * p, idx) for p in range(IDX_HEADS // 2)]),
        "small": [
            (shifted(o_ik, range(half)) + none * half + shifted(o_ik, range(2 * half, IDX_DIM))
             + shifted(o_ik, range(half, 2 * half)) + none * (LANES - ROPE_PAIR_SHIFT - half)),
            (none * half + shifted(o_ik, range(half)) + none * (ROPE_PAIR_SHIFT - 2 * half) + none * half
             + shifted(o_ik, range(half, 2 * half)) + shifted(o_ik, range(2 * half, IDX_DIM))),
            (shifted(o_iw, range(IDX_HEADS)) + shifted(o_gr, range(GLA_GATE_RANK))
             + none * (LANES - IDX_HEADS - GLA_GATE_RANK)),
        ],
        "fm": [shifted(o_dv + LANES * t, ident) for t in range(N_DSA // LANES)],
    }
    mats, plans = [], {}
    for name, tile_list in tiles.items():
        plans[name] = []
        for cols in tile_list:
            assert len(cols) == LANES
            parts = []
            for src_tile in sorted({c // LANES for c in cols if c >= 0}):
                m = np.zeros((LANES, LANES), np.float32)
                for dst, c in enumerate(cols):
                    if c >= 0 and c // LANES == src_tile:
                        m[c % LANES, dst] = 1.0
                for mat_id, known in enumerate(mats):
                    if np.array_equal(known, m):
                        break
                else:
                    mat_id = len(mats)
                    mats.append(m)
                parts.append((src_tile, mat_id))
            plans[name].append(parts)
    return plans, np.stack(mats)


def _prep_kernel(w_ref, p_ref, main_ref, small_ref, fm_ref, *, plans, n_plain, d_in):
    cols = w_ref.shape[1]

    def src_tile(t):
        if (t + 1) * LANES > d_in:
            v = w_ref[t * LANES:d_in, :].astype(BF16)
            return jnp.concatenate([v, jnp.zeros(((t + 1) * LANES - d_in, cols), BF16)], axis=0)
        return w_ref[t * LANES:(t + 1) * LANES, :].astype(BF16)

    def gathered(parts):
        acc = None
        for t, mat_id in parts:
            y = jnp.dot(p_ref[mat_id], src_tile(t), preferred_element_type=F32)
            acc = y if acc is None else acc + y
        return acc.astype(BF16)

    main_ref[:n_plain, :] = w_ref[:n_plain, :].astype(BF16)
    for i, parts in enumerate(plans["rot"]):
        main_ref[n_plain + i * LANES:n_plain + (i + 1) * LANES, :] = gathered(parts)
    for i, parts in enumerate(plans["small"]):
        small_ref[i * LANES:(i + 1) * LANES, :] = gathered(parts)
    for i, parts in enumerate(plans["fm"]):
        fm_ref[i * LANES:(i + 1) * LANES, :] = gathered(parts)


def _prep_weights(w_t):
    depth, d_in, d = w_t.shape
    plans, mats = _weight_plan(d_in)
    cb = min(256, d)
    n_plain = OFF_DQ
    n_fm = N_DSA
    kern = functools.partial(_prep_kernel, plans=plans, n_plain=n_plain, d_in=d_in)
    return pl.pallas_call(
        kern,
        grid=(depth, d // cb),
        in_specs=[
            pl.BlockSpec((None, d_in, cb), lambda l, r: (l, 0, r)),
            pl.BlockSpec(mats.shape, lambda l, r: (0, 0, 0)),
        ],
        out_specs=[
            pl.BlockSpec((None, N_MAIN, cb), lambda l, r: (l, 0, r)),
            pl.BlockSpec((None, N_SMALL, cb), lambda l, r: (l, 0, r)),
            pl.BlockSpec((None, n_fm, cb), lambda l, r: (l, 0, r)),
        ],
        out_shape=[
            jax.ShapeDtypeStruct((depth, N_MAIN, d), BF16),
            jax.ShapeDtypeStruct((depth, N_SMALL, d), BF16),
            jax.ShapeDtypeStruct((depth, n_fm, d), BF16),
        ],
        compiler_params=_params("parallel", "parallel"),
        name="prep_weights",
    )(w_t, jnp.asarray(mats.transpose(0, 2, 1), BF16))


def kernel(x, positions, norm_mix_pre, w_in, gla_wa2, gla_ba, gla_norm, w_out, norm_mix_post,
           norm_ffn_pre, w_up, w_down, norm_ffn_post):
    batch, seq, d = x.shape
    depth = w_in.shape[0]
    assert seq % DSA_BLK == 0
    tab = _rope_tables(positions)
    w2 = jnp.pad(gla_wa2, ((0, 0), (IDX_HEADS, LANES - IDX_HEADS - GLA_GATE_RANK), (0, 0)))
    h = x.reshape(batch * seq, d)
    w_main, w_small, w_fm = _prep_weights(jnp.swapaxes(w_in, 1, 2))
    w_out, w_up, w_down = (w.astype(BF16) for w in (w_out, w_up, w_down))
    for l in range(depth):
        main, ikd, misc, dvt, iwt = _inproj(h, norm_mix_pre[l][None, :], w_main, w_small, w_fm, tab, l)
        o_gla = _gla(main, misc, w2[l], gla_ba[l][None, :], gla_norm[l][None, :], batch, seq)
        o_dsa = _dsa(main, ikd, iwt, dvt, batch, seq)
        h = _outproj(o_gla, o_dsa, h, w_out, norm_mix_post[l][None, :], l)
        h = _ffn(h, norm_ffn_pre[l][None, :], w_up, w_down, norm_ffn_post[l][None, :], l)
    return h.reshape(batch, seq, d)
```

```python
import functools

import jax
import jax.numpy as jnp
import numpy as np
from jax import lax
from jax.experimental import pallas as pl
from jax.experimental.pallas import tpu as pltpu

F32 = jnp.float32
BF16 = jnp.bfloat16
I32 = jnp.int32
I16 = jnp.int16

GLA_HEADS = 4
GLA_DK = 128
GLA_DV = 256
GLA_GATE_RANK = 16
GLA_GATE_TAU = 16.0
GLA_CHUNK = 64
DSA_HEADS = 8
DSA_HEAD_DIM = 128
IDX_HEADS = 16
IDX_DIM = 64
DSA_TOPK_MAX = 256
ROPE_THETA = 500000.0
ROPE_FRACTION = 4
EPS = 1e-6
NEG_INF = -1e30
INT_MIN = -(2 ** 31)

LANES = 128
N_GQ = GLA_HEADS * GLA_DK
N_GV = GLA_HEADS * GLA_DV
N_DSA = DSA_HEADS * DSA_HEAD_DIM
N_IQ = IDX_HEADS * IDX_DIM
OFF_GQ = 0
OFF_GK = OFF_GQ + N_GQ
OFF_GV = OFF_GK + N_GQ
OFF_GG = OFF_GV + N_GV
OFF_DQ = OFF_GG + N_GV
OFF_DK = OFF_DQ + N_DSA
OFF_IQ = OFF_DK + N_DSA
N_MAIN = OFF_IQ + N_IQ
N_SMALL = 3 * LANES
DSA_BLK = 256
ROPE_LANES = 16
ROPE_PAIR_SHIFT = LANES // 2
TAB_DSA, TAB_IDX = range(2)
INPROJ_ROW_GROUPS = 4
FFN_ROW_GROUPS = 2
COUNT_ROWS = 32
COUNT16_ROWS = 64
BF16_EXACT_INT = 256
VMEM_LIMIT = 60 * 1024 * 1024

NT_DIMS = (((1,), (1,)), ((), ()))
TN_DIMS = (((0,), (0,)), ((), ()))


def _params(*sem):
    return pltpu.CompilerParams(dimension_semantics=sem, vmem_limit_bytes=VMEM_LIMIT)


def _rope_cs(tab, scale, rot):
    lane = lax.broadcasted_iota(I32, (1, LANES), 1)
    lo = lane < ROPE_LANES
    hi = (lane >= ROPE_PAIR_SHIFT) & (lane < ROPE_PAIR_SHIFT + ROPE_LANES)
    swapped = pltpu.roll(tab, ROPE_PAIR_SHIFT, 1)
    c = jnp.where(hi, swapped, tab)
    s = jnp.where(lo, -swapped, jnp.where(hi, tab, 0.0))
    return (1.0 + rot * (c - 1.0)) * scale, s * (rot * scale)


def _rope(x, c, s):
    parts = []
    for g in range(x.shape[1] // LANES):
        xg = x[:, g * LANES:(g + 1) * LANES]
        parts.append(xg * c + pltpu.roll(xg, ROPE_PAIR_SHIFT, 1) * s)
    return parts[0] if len(parts) == 1 else jnp.concatenate(parts, axis=1)


def _inproj_kernel(x_ref, g_ref, w_ref, ws_ref, wt_ref, tab_ref, tabk_ref,
                   main_ref, ikd_ref, misc_ref, dvt_ref, iwt_ref, xn_ref, *, j_dq, j_dk, j_iq):
    j = pl.program_id(1)

    @pl.when(j == 0)
    def _():
        x = x_ref[...]
        ms = jnp.mean(x * x, axis=-1, keepdims=True)
        xn_ref[...] = (x * lax.rsqrt(ms + EPS) * g_ref[...]).astype(BF16)
        small = lax.dot_general(xn_ref[...], ws_ref[...], NT_DIMS, preferred_element_type=F32)
        ikd_ref[...] = _rope(small[:, :2 * LANES], *_rope_cs(tabk_ref[...], 1.0, 1.0)).astype(BF16)
        misc_ref[...] = small[:, 2 * LANES:]
        iwt_ref[...] = small[:, 2 * LANES:].T
        fm = lax.dot_general(wt_ref[...], xn_ref[...], NT_DIMS, preferred_element_type=F32)
        for ci in range(dvt_ref.shape[0]):
            dvt_ref[ci] = fm[:, ci * DSA_BLK:(ci + 1) * DSA_BLK].astype(BF16)

    rot = jnp.where(j >= j_dq, 1.0, 0.0)
    scale = jnp.where(j < j_dq, 1.0,
                      jnp.where(j < j_dk, DSA_HEAD_DIM ** -0.5, jnp.where(j < j_iq, 1.0, IDX_DIM ** -0.5)))
    c, s = _rope_cs(tab_ref[...], scale, rot)
    rows = xn_ref.shape[0] // INPROJ_ROW_GROUPS
    for g in range(INPROJ_ROW_GROUPS):
        rs = slice(g * rows, (g + 1) * rows)
        acc = lax.dot_general(xn_ref[rs, :], w_ref[...], NT_DIMS, preferred_element_type=F32)
        main_ref[rs, :] = _rope(acc, c[rs], s[rs]).astype(BF16)


def _inproj(h, gain, w_main, w_small, w_t, tab, layer):
    t, d = h.shape
    tm = min(1024, t)
    tn = 1024
    nt = w_t.shape[1]
    j_dq, j_dk, j_iq = OFF_DQ // tn, OFF_DK // tn, OFF_IQ // tn
    kern = functools.partial(_inproj_kernel, j_dq=j_dq, j_dk=j_dk, j_iq=j_iq)
    return pl.pallas_call(
        kern,
        grid=(t // tm, N_MAIN // tn),
        in_specs=[
            pl.BlockSpec((tm, d), lambda i, j: (i, 0)),
            pl.BlockSpec((1, d), lambda i, j: (0, 0)),
            pl.BlockSpec((None, tn, d), lambda i, j: (layer, j, 0)),
            pl.BlockSpec((None, N_SMALL, d), lambda i, j: (layer, 0, 0), pipeline_mode=pl.Buffered(1)),
            pl.BlockSpec((None, nt, d), lambda i, j: (layer, 0, 0), pipeline_mode=pl.Buffered(1)),
            pl.BlockSpec((None, tm, LANES), lambda i, j: ((j >= j_iq).astype(I32), i, 0)),
            pl.BlockSpec((None, tm, LANES), lambda i, j: (TAB_IDX, i, 0)),
        ],
        out_specs=[
            pl.BlockSpec((tm, tn), lambda i, j: (i, j)),
            pl.BlockSpec((tm, 2 * LANES), lambda i, j: (i, 0)),
            pl.BlockSpec((tm, LANES), lambda i, j: (i, 0)),
            pl.BlockSpec((tm // DSA_BLK, N_DSA, DSA_BLK), lambda i, j: (i, 0, 0)),
            pl.BlockSpec((LANES, tm), lambda i, j: (0, i)),
        ],
        out_shape=[
            jax.ShapeDtypeStruct((t, N_MAIN), BF16),
            jax.ShapeDtypeStruct((t, 2 * LANES), BF16),
            jax.ShapeDtypeStruct((t, LANES), F32),
            jax.ShapeDtypeStruct((t // DSA_BLK, N_DSA, DSA_BLK), BF16),
            jax.ShapeDtypeStruct((LANES, t), F32),
        ],
        scratch_shapes=[pltpu.VMEM((tm, d), BF16)],
        compiler_params=_params("parallel", "arbitrary"),
        name="inproj",
    )(h, gain, w_main, w_small, w_t, tab, tab)


def _gla_kernel(q_ref, k_ref, v_ref, gg_ref, misc_ref, w2_ref, ba_ref, gn_ref, o_ref, state_ref, *, blk):
    c_len = GLA_CHUNK
    n_ch = blk // c_len
    shift = c_len.bit_length() - 1
    assert 1 << shift == c_len

    @pl.when(pl.program_id(1) == 0)
    def _():
        state_ref[...] = jnp.zeros_like(state_ref)

    gpre = jnp.dot(misc_ref[...], w2_ref[...], preferred_element_type=F32,
                   precision=lax.Precision.HIGHEST) + ba_ref[...]
    log_a = (jnp.minimum(gpre, 0.0) - jnp.log1p(jnp.exp(-jnp.abs(gpre)))) * (1.0 / GLA_GATE_TAU)

    r = lax.broadcasted_iota(I32, (blk, blk), 0)
    c = lax.broadcasted_iota(I32, (blk, blk), 1)
    causal = (c <= r) & ((r >> shift) == (c >> shift))
    b = jnp.dot(causal.astype(F32), log_a, preferred_element_type=F32, precision=lax.Precision.HIGHEST)
    b3 = b.reshape(n_ch, c_len, N_GQ)
    b_last = b3[:, c_len - 1:c_len, :]
    decay = jnp.exp(b_last)
    k = k_ref[...].astype(F32)
    q_dec = (q_ref[...].astype(F32) * (GLA_DK ** -0.5) * jnp.exp(b)).astype(BF16)
    k_inv = (k * jnp.exp(-b)).astype(BF16)
    k_dec = (k * jnp.exp(b_last - b3).reshape(blk, N_GQ)).astype(BF16)
    gn = gn_ref[...]

    for hd in range(GLA_HEADS):
        ck = slice(hd * GLA_DK, (hd + 1) * GLA_DK)
        cv = slice(hd * GLA_DV, (hd + 1) * GLA_DV)
        attn = lax.dot_general(q_dec[:, ck], k_inv[:, ck], NT_DIMS, preferred_element_type=F32)
        attn = jnp.where(causal, attn, 0.0).astype(BF16)
        o = jnp.dot(attn, v_ref[:, cv], preferred_element_type=F32)
        st = state_ref[hd]
        inter = []
        for ci in range(n_ch):
            rows = slice(ci * c_len, (ci + 1) * c_len)
            inter.append(lax.dot_general(q_dec[rows, ck], st.astype(BF16), NT_DIMS,
                                         preferred_element_type=F32))
            d_st = lax.dot_general(v_ref[rows, cv], k_dec[rows, ck], TN_DIMS, preferred_element_type=F32)
            st = st * decay[ci, :, ck] + d_st
        state_ref[hd] = st
        o = o + jnp.concatenate(inter, axis=0)
        ms = jnp.mean(o * o, axis=-1, keepdims=True)
        y = o * lax.rsqrt(ms + EPS) * gn
        gate = gg_ref[:, cv].astype(F32)
        o_ref[:, cv] = (y * (gate / (1.0 + jnp.exp(-gate)))).astype(BF16)


def _gla(main, misc, w2, ba, gn, batch, seq):
    t = batch * seq
    blk = min(256, seq)
    nb = seq // blk
    kern = functools.partial(_gla_kernel, blk=blk)

    def rowmap(col):
        return lambda b, n: (b * nb + n, col)

    return pl.pallas_call(
        kern,
        grid=(batch, nb),
        in_specs=[
            pl.BlockSpec((blk, N_GQ), rowmap(OFF_GQ // N_GQ)),
            pl.BlockSpec((blk, N_GQ), rowmap(OFF_GK // N_GQ)),
            pl.BlockSpec((blk, N_GV), rowmap(OFF_GV // N_GV)),
            pl.BlockSpec((blk, N_GV), rowmap(OFF_GG // N_GV)),
            pl.BlockSpec((blk, LANES), rowmap(0)),
            pl.BlockSpec((LANES, N_GQ), lambda b, n: (0, 0)),
            pl.BlockSpec((1, N_GQ), lambda b, n: (0, 0)),
            pl.BlockSpec((1, GLA_DV), lambda b, n: (0, 0)),
        ],
        out_specs=pl.BlockSpec((blk, N_GV), rowmap(0)),
        out_shape=jax.ShapeDtypeStruct((t, N_GV), BF16),
        scratch_shapes=[pltpu.VMEM((GLA_HEADS, GLA_DV, GLA_DK), F32)],
        compiler_params=_params("parallel", "arbitrary"),
        name="gla",
    )(main, main, main, main, misc, w2, ba, gn)


def _dsa_kernel(q_ref, iq_ref, iwt_ref, k_ref, vt_ref, ikd_ref, o_ref,
                key_ref, hi_ref, lo_ref, bias_ref, acc_ref, s_ref, p_ref, m_ref, l_ref, alpha_ref,
                *, blk, topk, idx_bits):
    qi = pl.program_id(1)
    nch = qi + 1
    krow = lax.broadcasted_iota(I32, (blk, blk), 0)
    qcol = lax.broadcasted_iota(I32, (blk, blk), 1)
    w_idx = iwt_ref[0:IDX_HEADS, :] * (IDX_HEADS ** -0.5)

    def score_body(c, carry):
        ks = pl.multiple_of(c * blk, blk)
        k_lo = ikd_ref[pl.ds(ks, blk), 0:LANES]
        k_hi = ikd_ref[pl.ds(ks, blk), LANES:2 * LANES]
        acc = jnp.zeros((blk, blk), F32)
        for p in range(IDX_HEADS // 2):
            iq_p = iq_ref[:, p * LANES:(p + 1) * LANES]
            l0 = lax.dot_general(k_lo, iq_p, NT_DIMS, preferred_element_type=F32)
            l1 = lax.dot_general(k_hi, iq_p, NT_DIMS, preferred_element_type=F32)
            acc = acc + w_idx[2 * p:2 * p + 1, :] * jnp.maximum(l0, 0.0)
            acc = acc + w_idx[2 * p + 1:2 * p + 2, :] * jnp.maximum(l1, 0.0)
        score = jnp.where(krow + (c - qi) * blk > qcol, NEG_INF, acc)
        score = jnp.where(score == 0.0, 0.0, score)
        bits = pltpu.bitcast(score, I32)
        key = bits ^ ((bits >> 31) & 0x7FFFFFFF)
        key_ref[c] = key
        hi_ref[c] = (key >> 16).astype(I16)
        return carry

    lax.fori_loop(0, nch, score_body, 0)

    def count16(ref, cand):
        def body(c, acc):
            m = jnp.where(ref[c] >= cand, jnp.ones((), BF16), jnp.zeros((), BF16))
            for i in range(blk // COUNT16_ROWS):
                acc = acc + m[i * COUNT16_ROWS:(i + 1) * COUNT16_ROWS]
            return acc
        acc = lax.fori_loop(0, nch, body, jnp.zeros((COUNT16_ROWS, blk), BF16))
        return jnp.sum(acc.astype(F32), axis=0, keepdims=True)

    def kth_largest16(ref, kth):
        v = jnp.where(count16(ref, jnp.zeros((1, blk), I16)) >= kth, 0, -(2 ** 15)).astype(I32)

        def body(i, v):
            cand = v | lax.shift_left(jnp.int32(1), 14 - i)
            return jnp.where(count16(ref, cand.astype(I16)) >= kth, cand, v)

        return lax.fori_loop(0, 15, body, v)

    def count(pred):
        def body(c, acc):
            m = jnp.where(pred(key_ref[c], c), 1.0, 0.0)
            return acc + jnp.sum(m.reshape(blk // COUNT_ROWS, COUNT_ROWS, blk), axis=0)
        acc = lax.fori_loop(0, nch, body, jnp.zeros((COUNT_ROWS, blk), F32))
        return jnp.sum(acc, axis=0, keepdims=True)

    kf = float(topk)
    thr_hi = kth_largest16(hi_ref, kf)
    n_above = count16(hi_ref, (thr_hi + 1).astype(I16))
    n_above = jnp.where(thr_hi == 2 ** 15 - 1, 0.0, n_above)

    def low_body(c, carry):
        key = key_ref[c]
        low = (key & 0xFFFF) - 2 ** 15
        lo_ref[c] = jnp.where((key >> 16) == thr_hi, low, -(2 ** 15)).astype(I16)
        return carry

    lax.fori_loop(0, nch, low_body, 0)
    thr_lo = kth_largest16(lo_ref, kf - n_above)
    thr = thr_hi * 2 ** 16 + (thr_lo + 2 ** 15)

    n_ge = count(lambda kc, c: kc >= thr)

    def resolve_ties():
        need = kf - count(lambda kc, c: kc > thr)

        def tie_body(i, last):
            cand = last | lax.shift_left(jnp.int32(1), idx_bits - 1 - i)
            below = count(lambda kc, c: (kc == thr) & (c * blk + krow < cand))
            return jnp.where(below < need, cand, last)

        return lax.fori_loop(0, idx_bits, tie_body, jnp.zeros((1, blk), I32))

    last = lax.cond(jnp.max(n_ge) > kf, resolve_ties,
                    lambda: jnp.full((1, blk), 2 ** idx_bits, I32))

    def write_bias(c, diagonal):
        kc = key_ref[c]
        tie_pos = jnp.where(kc == thr, c * blk + krow, -1)
        bias = jnp.where(kc >= thr, jnp.where(tie_pos > last, NEG_INF, 0.0), NEG_INF)
        if diagonal:
            bias = jnp.where(krow > qcol, NEG_INF, bias)
        bias_ref[c] = bias

    def bias_body(c, carry):
        write_bias(c, False)
        return carry

    lax.fori_loop(0, qi, bias_body, 0)
    write_bias(qi, True)

    heads = tuple((hd, slice(hd * DSA_HEAD_DIM, (hd + 1) * DSA_HEAD_DIM)) for hd in range(DSA_HEADS))

    def fold(x, op):
        return op(x.reshape(blk // 8, 8, blk), axis=0)

    m_ref[...] = jnp.full_like(m_ref, NEG_INF)
    l_ref[...] = jnp.zeros_like(l_ref)
    acc_ref[...] = jnp.zeros_like(acc_ref)

    def attn_body(c, carry):
        ks = pl.multiple_of(c * blk, blk)

        @pl.when(c >= 0)
        def _():
            for hd, hs in heads:
                s = lax.dot_general(k_ref[pl.ds(ks, blk), hs], q_ref[:, hs], NT_DIMS,
                                    preferred_element_type=F32) + bias_ref[c]
                s_ref[hd] = s
                m_prev = m_ref[hd, 0:1, :]
                m_new = jnp.maximum(m_prev, jnp.max(fold(s, jnp.max), axis=0, keepdims=True))
                alpha_ref[hd, 0:1, :] = jnp.exp(m_prev - m_new)
                m_ref[hd, 0:1, :] = m_new

        @pl.when(c <= qi)
        def _():
            for hd, hs in heads:
                p = jnp.exp(s_ref[hd] - m_ref[hd, 0:1, :])
                l_ref[hd] = alpha_ref[hd, 0:1, :] * l_ref[hd] + fold(p, jnp.sum)
                p_ref[hd] = p.astype(BF16)
            for hd, hs in heads:
                pv = jnp.dot(vt_ref[c, hs, :], p_ref[hd], preferred_element_type=F32)
                acc_ref[hs, :] = alpha_ref[hd, 0:1, :] * acc_ref[hs, :] + pv

        return carry

    lax.fori_loop(0, nch, attn_body, 0)

    for hd, hs in heads:
        o_t = acc_ref[hs, :] / jnp.sum(l_ref[hd], axis=0, keepdims=True)
        o_ref[:, hs] = o_t.T.astype(BF16)


def _dsa(main, ikd, iwt, dvt, batch, seq):
    t = batch * seq
    blk = DSA_BLK
    nq = seq // blk
    topk = min(DSA_TOPK_MAX, seq // 4)
    idx_bits = max(1, (seq - 1).bit_length())
    assert (blk // COUNT16_ROWS) * nq <= BF16_EXACT_INT
    kern = functools.partial(_dsa_kernel, blk=blk, topk=topk, idx_bits=idx_bits)

    def qmap(col):
        return lambda b, i: (b * nq + i, col)

    return pl.pallas_call(
        kern,
        grid=(batch, nq),
        in_specs=[
            pl.BlockSpec((blk, N_DSA), qmap(OFF_DQ // N_DSA)),
            pl.BlockSpec((blk, N_IQ), qmap(OFF_IQ // N_IQ)),
            pl.BlockSpec((LANES, blk), lambda b, i: (0, b * nq + i)),
            pl.BlockSpec((seq, N_DSA), lambda b, i: (b, OFF_DK // N_DSA)),
            pl.BlockSpec((nq, N_DSA, blk), lambda b, i: (b, 0, 0)),
            pl.BlockSpec((seq, 2 * LANES), lambda b, i: (b, 0)),
        ],
        out_specs=pl.BlockSpec((blk, N_DSA), qmap(0)),
        out_shape=jax.ShapeDtypeStruct((t, N_DSA), BF16),
        scratch_shapes=[
            pltpu.VMEM((nq, blk, blk), I32),
            pltpu.VMEM((nq, blk, blk), I16),
            pltpu.VMEM((nq, blk, blk), I16),
            pltpu.VMEM((nq, blk, blk), F32),
            pltpu.VMEM((N_DSA, blk), F32),
            pltpu.VMEM((DSA_HEADS, blk, blk), F32),
            pltpu.VMEM((DSA_HEADS, blk, blk), BF16),
            pltpu.VMEM((DSA_HEADS, 8, blk), F32),
            pltpu.VMEM((DSA_HEADS, 8, blk), F32),
            pltpu.VMEM((DSA_HEADS, 8, blk), F32),
        ],
        compiler_params=_params("parallel", "arbitrary"),
        name="dsa",
    )(main, main, iwt, main, dvt, ikd)


def _outproj_kernel(og_ref, od_ref, h_ref, w_ref, g_ref, o_ref):
    m = jnp.dot(og_ref[...], w_ref[:N_GV, :], preferred_element_type=F32)
    m = m + jnp.dot(od_ref[...], w_ref[N_GV:, :], preferred_element_type=F32)
    ms = jnp.mean(m * m, axis=-1, keepdims=True)
    o_ref[...] = h_ref[...] + m * lax.rsqrt(ms + EPS) * g_ref[...]


def _outproj(o_gla, o_dsa, h, w_out, gain, layer):
    t, d = h.shape
    tm = min(512, t)
    return pl.pallas_call(
        _outproj_kernel,
        grid=(t // tm,),
        in_specs=[
            pl.BlockSpec((tm, N_GV), lambda i: (i, 0)),
            pl.BlockSpec((tm, N_DSA), lambda i: (i, 0)),
            pl.BlockSpec((tm, d), lambda i: (i, 0)),
            pl.BlockSpec((None, N_GV + N_DSA, d), lambda i: (layer, 0, 0)),
            pl.BlockSpec((1, d), lambda i: (0, 0)),
        ],
        out_specs=pl.BlockSpec((tm, d), lambda i: (i, 0)),
        out_shape=jax.ShapeDtypeStruct((t, d), F32),
        compiler_params=_params("parallel"),
        name="outproj",
    )(o_gla, o_dsa, h, w_out, gain)


def _ffn_kernel(h_ref, gpre_ref, wu_ref, wd_ref, gpost_ref, o_ref, xn_ref, acc_ref):
    f = pl.program_id(1)
    last = pl.num_programs(1) - 1
    rows = h_ref.shape[0] // FFN_ROW_GROUPS
    groups = [slice(g * rows, (g + 1) * rows) for g in range(FFN_ROW_GROUPS)]

    def up(rs):
        u = jnp.maximum(jnp.dot(xn_ref[rs, :], wu_ref[...], preferred_element_type=F32), 0.0)
        return (u * u).astype(BF16)

    @pl.when(f == 0)
    def _():
        for rs in groups:
            x = h_ref[rs, :]
            ms = jnp.mean(x * x, axis=-1, keepdims=True)
            xn_ref[rs, :] = (x * lax.rsqrt(ms + EPS) * gpre_ref[...]).astype(BF16)
            acc_ref[rs, :] = jnp.dot(up(rs), wd_ref[...], preferred_element_type=F32)

    @pl.when((f > 0) & (f < last))
    def _():
        acc_ref[...] += jnp.dot(up(slice(None)), wd_ref[...], preferred_element_type=F32)

    @pl.when(f == last)
    def _():
        for rs in groups:
            y = acc_ref[rs, :] + jnp.dot(up(rs), wd_ref[...], preferred_element_type=F32)
            ms = jnp.mean(y * y, axis=-1, keepdims=True)
            o_ref[rs, :] = h_ref[rs, :] + y * lax.rsqrt(ms + EPS) * gpost_ref[...]


def _ffn(h, g_pre, w_up, w_down, g_post, layer):
    t, d = h.shape
    d_ff = w_up.shape[2]
    tm = min(512, t)
    tf = 1024
    assert d_ff // tf >= 2
    return pl.pallas_call(
        _ffn_kernel,
        grid=(t // tm, d_ff // tf),
        in_specs=[
            pl.BlockSpec((tm, d), lambda i, f: (i, 0)),
            pl.BlockSpec((1, d), lambda i, f: (0, 0)),
            pl.BlockSpec((None, d, tf), lambda i, f: (layer, 0, f)),
            pl.BlockSpec((None, tf, d), lambda i, f: (layer, f, 0)),
            pl.BlockSpec((1, d), lambda i, f: (0, 0)),
        ],
        out_specs=pl.BlockSpec((tm, d), lambda i, f: (i, 0)),
        out_shape=jax.ShapeDtypeStruct((t, d), F32),
        scratch_shapes=[pltpu.VMEM((tm, d), BF16), pltpu.VMEM((tm, d), F32)],
        compiler_params=_params("parallel", "arbitrary"),
        name="ffn",
    )(h, g_pre, w_up, w_down, g_post)


def _rope_table(positions, head_dim):
    r = head_dim // ROPE_FRACTION
    n_freq = r // 2
    inv_freq = ROPE_THETA ** (-(jnp.arange(0, r, 2, dtype=F32) / r))
    rep = ROPE_LANES // n_freq
    t = positions.size
    ang = positions.reshape(t, 1).astype(F32) * jnp.tile(inv_freq, rep)
    flat = lax.optimization_barrier(ang.reshape(-1))
    cos, sin = lax.optimization_barrier((jnp.cos(flat), jnp.sin(flat)))
    cos, sin = cos.reshape(t, ROPE_LANES), sin.reshape(t, ROPE_LANES)
    one = jnp.ones((cos.shape[0], ROPE_PAIR_SHIFT - ROPE_LANES), F32)
    return jnp.concatenate([cos, one, sin, one], axis=1)


def _rope_tables(positions):
    return jnp.stack([_rope_table(positions, DSA_HEAD_DIM), _rope_table(positions, IDX_DIM)])


def _dsa_lane_sources():
    half = DSA_HEAD_DIM // ROPE_FRACTION // 2
    assert half == ROPE_LANES
    cut = 2 * half + ROPE_PAIR_SHIFT - half
    return (list(range(half)) + list(range(2 * half, cut)) + list(range(half, 2 * half))
            + list(range(cut, DSA_HEAD_DIM)))


def _idx_lane_sources():
    half = IDX_DIM // ROPE_FRACTION // 2
    assert 2 * half == ROPE_LANES

    def a(lo, hi):
        return list(range(lo, hi))

    def b(lo, hi):
        return list(range(IDX_DIM + lo, IDX_DIM + hi))

    return (a(0, half) + b(0, half) + a(2 * half, IDX_DIM)
            + a(half, 2 * half) + b(half, 2 * half) + b(2 * half, IDX_DIM))


def _weight_plan(d_in):
    o_gr = 2 * N_GQ + 2 * N_GV
    o_dq = o_gr + GLA_GATE_RANK
    o_dk = o_dq + N_DSA
    o_dv = o_dk + N_DSA
    o_iq = o_dv + N_DSA
    o_ik = o_iq + N_IQ
    o_iw = o_ik + IDX_DIM
    assert o_iw + IDX_HEADS == d_in
    half = IDX_DIM // ROPE_FRACTION // 2
    none = [-1]

    def shifted(base, sources):
        return [base + s for s in sources]

    dsa, idx, ident = _dsa_lane_sources(), _idx_lane_sources(), list(range(LANES))
    tiles = {
        "rot": ([shifted(o_dq + LANES * h, dsa) for h in range(DSA_HEADS)]
                + [shifted(o_dk + LANES * h, dsa) for h in range(DSA_HEADS)]
                + [shifted(o_iq + LANES * p, idx) for p in range(IDX_HEADS // 2)]),
        "small": [
            (shifted(o_ik, range(half)) + none * half + shifted(o_ik, range(2 * half, IDX_DIM))
             + shifted(o_ik, range(half, 2 * half)) + none * (LANES - ROPE_PAIR_SHIFT - half)),
            (none * half + shifted(o_ik, range(half)) + none * (ROPE_PAIR_SHIFT - 2 * half) + none * half
             + shifted(o_ik, range(half, 2 * half)) + shifted(o_ik, range(2 * half, IDX_DIM))),
            (shifted(o_iw, range(IDX_HEADS)) + shifted(o_gr, range(GLA_GATE_RANK))
             + none * (LANES - IDX_HEADS - GLA_GATE_RANK)),
        ],
        "fm": [shifted(o_dv + LANES * t, ident) for t in range(N_DSA // LANES)],
    }
    mats, plans = [], {}
    for name, tile_list in tiles.items():
        plans[name] = []
        for cols in tile_list:
            assert len(cols) == LANES
            parts = []
            for src_tile in sorted({c // LANES for c in cols if c >= 0}):
                m = np.zeros((LANES, LANES), np.float32)
                for dst, c in enumerate(cols):
                    if c >= 0 and c // LANES == src_tile:
                        m[c % LANES, dst] = 1.0
                for mat_id, known in enumerate(mats):
                    if np.array_equal(known, m):
                        break
                else:
                    mat_id = len(mats)
                    mats.append(m)
                parts.append((src_tile, mat_id))
            plans[name].append(parts)
    return plans, np.stack(mats)


def _prep_kernel(w_ref, p_ref, main_ref, small_ref, fm_ref, *, plans, n_plain, d_in):
    cols = w_ref.shape[1]

    def src_tile(t):
        if (t + 1) * LANES > d_in:
            v = w_ref[t * LANES:d_in, :].astype(BF16)
            return jnp.concatenate([v, jnp.zeros(((t + 1) * LANES - d_in, cols), BF16)], axis=0)
        return w_ref[t * LANES:(t + 1) * LANES, :].astype(BF16)

    def gathered(parts):
        acc = None
        for t, mat_id in parts:
            y = jnp.dot(p_ref[mat_id], src_tile(t), preferred_element_type=F32)
            acc = y if acc is None else acc + y
        return acc.astype(BF16)

    main_ref[:n_plain, :] = w_ref[:n_plain, :].astype(BF16)
    for i, parts in enumerate(plans["rot"]):
        main_ref[n_plain + i * LANES:n_plain + (i + 1) * LANES, :] = gathered(parts)
    for i, parts in enumerate(plans["small"]):
        small_ref[i * LANES:(i + 1) * LANES, :] = gathered(parts)
    for i, parts in enumerate(plans["fm"]):
        fm_ref[i * LANES:(i + 1) * LANES, :] = gathered(parts)


def _prep_weights(w_t):
    depth, d_in, d = w_t.shape
    plans, mats = _weight_plan(d_in)
    cb = min(256, d)
    n_plain = OFF_DQ
    n_fm = N_DSA
    kern = functools.partial(_prep_kernel, plans=plans, n_plain=n_plain, d_in=d_in)
    return pl.pallas_call(
        kern,
        grid=(depth, d // cb),
        in_specs=[
            pl.BlockSpec((None, d_in, cb), lambda l, r: (l, 0, r)),
            pl.BlockSpec(mats.shape, lambda l, r: (0, 0, 0)),
        ],
        out_specs=[
            pl.BlockSpec((None, N_MAIN, cb), lambda l, r: (l, 0, r)),
            pl.BlockSpec((None, N_SMALL, cb), lambda l, r: (l, 0, r)),
            pl.BlockSpec((None, n_fm, cb), lambda l, r: (l, 0, r)),
        ],
        out_shape=[
            jax.ShapeDtypeStruct((depth, N_MAIN, d), BF16),
            jax.ShapeDtypeStruct((depth, N_SMALL, d), BF16),
            jax.ShapeDtypeStruct((depth, n_fm, d), BF16),
        ],
        compiler_params=_params("parallel", "parallel"),
        name="prep_weights",
    )(w_t, jnp.asarray(mats.transpose(0, 2, 1), BF16))


def kernel(x, positions, norm_mix_pre, w_in, gla_wa2, gla_ba, gla_norm, w_out, norm_mix_post,
           norm_ffn_pre, w_up, w_down, norm_ffn_post):
    batch, seq, d = x.shape
    depth = w_in.shape[0]
    assert seq % DSA_BLK == 0
    tab = _rope_tables(positions)
    w2 = jnp.pad(gla_wa2, ((0, 0), (IDX_HEADS, LANES - IDX_HEADS - GLA_GATE_RANK), (0, 0)))
    h = x.reshape(batch * seq, d)
    w_main, w_small, w_fm = _prep_weights(jnp.swapaxes(w_in, 1, 2))
    w_out, w_up, w_down = (w.astype(BF16) for w in (w_out, w_up, w_down))
    for l in range(depth):
        main, ikd, misc, dvt, iwt = _inproj(h, norm_mix_pre[l][None, :], w_main, w_small, w_fm, tab, l)
        o_gla = _gla(main, misc, w2[l], gla_ba[l][None, :], gla_norm[l][None, :], batch, seq)
        o_dsa = _dsa(main, ikd, iwt, dvt, batch, seq)
        h = _outproj(o_gla, o_dsa, h, w_out, norm_mix_post[l][None, :], l)
        h = _ffn(h, norm_ffn_pre[l][None, :], w_up, w_down, norm_ffn_post[l][None, :], l)
    return h.reshape(batch, seq, d)
```

```python
import functools

import jax
import jax.numpy as jnp
import numpy as np
from jax import lax
from jax.experimental import pallas as pl
from jax.experimental.pallas import tpu as pltpu

F32 = jnp.float32
BF16 = jnp.bfloat16
I32 = jnp.int32
I16 = jnp.int16

GLA_HEADS = 4
GLA_DK = 128
GLA_DV = 256
GLA_GATE_RANK = 16
GLA_GATE_TAU = 16.0
GLA_CHUNK = 64
DSA_HEADS = 8
DSA_HEAD_DIM = 128
IDX_HEADS = 16
IDX_DIM = 64
DSA_TOPK_MAX = 256
ROPE_THETA = 500000.0
ROPE_FRACTION = 4
EPS = 1e-6
NEG_INF = -1e30
INT_MIN = -(2 ** 31)

LANES = 128
N_GQ = GLA_HEADS * GLA_DK
N_GV = GLA_HEADS * GLA_DV
N_DSA = DSA_HEADS * DSA_HEAD_DIM
N_IQ = IDX_HEADS * IDX_DIM
OFF_GQ = 0
OFF_GK = OFF_GQ + N_GQ
OFF_GV = OFF_GK + N_GQ
OFF_GG = OFF_GV + N_GV
OFF_DQ = OFF_GG + N_GV
OFF_DK = OFF_DQ + N_DSA
OFF_IQ = OFF_DK + N_DSA
N_MAIN = OFF_IQ + N_IQ
N_SMALL = 3 * LANES
DSA_BLK = 256
ROPE_LANES = 16
ROPE_PAIR_SHIFT = LANES // 2
TAB_DSA, TAB_IDX = range(2)
INPROJ_ROW_GROUPS = 4
FFN_ROW_GROUPS = 2
COUNT_ROWS = 32
COUNT16_ROWS = 64
BF16_EXACT_INT = 256
VMEM_LIMIT = 60 * 1024 * 1024

NT_DIMS = (((1,), (1,)), ((), ()))
TN_DIMS = (((0,), (0,)), ((), ()))


def _params(*sem):
    return pltpu.CompilerParams(dimension_semantics=sem, vmem_limit_bytes=VMEM_LIMIT)


def _rope_cs(tab, scale, rot):
    lane = lax.broadcasted_iota(I32, (1, LANES), 1)
    lo = lane < ROPE_LANES
    hi = (lane >= ROPE_PAIR_SHIFT) & (lane < ROPE_PAIR_SHIFT + ROPE_LANES)
    swapped = pltpu.roll(tab, ROPE_PAIR_SHIFT, 1)
    c = jnp.where(hi, swapped, tab)
    s = jnp.where(lo, -swapped, jnp.where(hi, tab, 0.0))
    return (1.0 + rot * (c - 1.0)) * scale, s * (rot * scale)


def _rope(x, c, s):
    parts = []
    for g in range(x.shape[1] // LANES):
        xg = x[:, g * LANES:(g + 1) * LANES]
        parts.append(xg * c + pltpu.roll(xg, ROPE_PAIR_SHIFT, 1) * s)
    return parts[0] if len(parts) == 1 else jnp.concatenate(parts, axis=1)


def _inproj_kernel(x_ref, g_ref, w_ref, ws_ref, wt_ref, tab_ref, tabk_ref,
                   main_ref, ikd_ref, misc_ref, dvt_ref, iwt_ref, xn_ref, *, j_dq, j_dk, j_iq):
    j = pl.program_id(1)

    @pl.when(j == 0)
    def _():
        x = x_ref[...]
        ms = jnp.mean(x * x, axis=-1, keepdims=True)
        xn_ref[...] = (x * lax.rsqrt(ms + EPS) * g_ref[...]).astype(BF16)
        small = lax.dot_general(xn_ref[...], ws_ref[...], NT_DIMS, preferred_element_type=F32)
        ikd_ref[...] = _rope(small[:, :2 * LANES], *_rope_cs(tabk_ref[...], 1.0, 1.0)).astype(BF16)
        misc_ref[...] = small[:, 2 * LANES:]
        iwt_ref[...] = small[:, 2 * LANES:].T
        fm = lax.dot_general(wt_ref[...], xn_ref[...], NT_DIMS, preferred_element_type=F32)
        for ci in range(dvt_ref.shape[0]):
            dvt_ref[ci] = fm[:, ci * DSA_BLK:(ci + 1) * DSA_BLK].astype(BF16)

    rot = jnp.where(j >= j_dq, 1.0, 0.0)
    scale = jnp.where(j < j_dq, 1.0,
                      jnp.where(j < j_dk, DSA_HEAD_DIM ** -0.5, jnp.where(j < j_iq, 1.0, IDX_DIM ** -0.5)))
    c, s = _rope_cs(tab_ref[...], scale, rot)
    rows = xn_ref.shape[0] // INPROJ_ROW_GROUPS
    for g in range(INPROJ_ROW_GROUPS):
        rs = slice(g * rows, (g + 1) * rows)
        acc = lax.dot_general(xn_ref[rs, :], w_ref[...], NT_DIMS, preferred_element_type=F32)
        main_ref[rs, :] = _rope(acc, c[rs], s[rs]).astype(BF16)


def _inproj(h, gain, w_main, w_small, w_t, tab, layer):
    t, d = h.shape
    tm = min(1024, t)
    tn = 1024
    nt = w_t.shape[1]
    j_dq, j_dk, j_iq = OFF_DQ // tn, OFF_DK // tn, OFF_IQ // tn
    kern = functools.partial(_inproj_kernel, j_dq=j_dq, j_dk=j_dk, j_iq=j_iq)
    return pl.pallas_call(
        kern,
        grid=(t // tm, N_MAIN // tn),
        in_specs=[
            pl.BlockSpec((tm, d), lambda i, j: (i, 0)),
            pl.BlockSpec((1, d), lambda i, j: (0, 0)),
            pl.BlockSpec((None, tn, d), lambda i, j: (layer, j, 0)),
            pl.BlockSpec((None, N_SMALL, d), lambda i, j: (layer, 0, 0), pipeline_mode=pl.Buffered(1)),
            pl.BlockSpec((None, nt, d), lambda i, j: (layer, 0, 0), pipeline_mode=pl.Buffered(1)),
            pl.BlockSpec((None, tm, LANES), lambda i, j: ((j >= j_iq).astype(I32), i, 0)),
            pl.BlockSpec((None, tm, LANES), lambda i, j: (TAB_IDX, i, 0)),
        ],
        out_specs=[
            pl.BlockSpec((tm, tn), lambda i, j: (i, j)),
            pl.BlockSpec((tm, 2 * LANES), lambda i, j: (i, 0)),
            pl.BlockSpec((tm, LANES), lambda i, j: (i, 0)),
            pl.BlockSpec((tm // DSA_BLK, N_DSA, DSA_BLK), lambda i, j: (i, 0, 0)),
            pl.BlockSpec((LANES, tm), lambda i, j: (0, i)),
        ],
        out_shape=[
            jax.ShapeDtypeStruct((t, N_MAIN), BF16),
            jax.ShapeDtypeStruct((t, 2 * LANES), BF16),
            jax.ShapeDtypeStruct((t, LANES), F32),
            jax.ShapeDtypeStruct((t // DSA_BLK, N_DSA, DSA_BLK), BF16),
            jax.ShapeDtypeStruct((LANES, t), F32),
        ],
        scratch_shapes=[pltpu.VMEM((tm, d), BF16)],
        compiler_params=_params("parallel", "arbitrary"),
        name="inproj",
    )(h, gain, w_main, w_small, w_t, tab, tab)


def _gla_kernel(q_ref, k_ref, v_ref, gg_ref, misc_ref, w2_ref, ba_ref, gn_ref, o_ref, state_ref, *, blk):
    c_len = GLA_CHUNK
    n_ch = blk // c_len
    shift = c_len.bit_length() - 1
    assert 1 << shift == c_len

    @pl.when(pl.program_id(1) == 0)
    def _():
        state_ref[...] = jnp.zeros_like(state_ref)

    gpre = jnp.dot(misc_ref[...], w2_ref[...], preferred_element_type=F32,
                   precision=lax.Precision.HIGHEST) + ba_ref[...]
    log_a = (jnp.minimum(gpre, 0.0) - jnp.log1p(jnp.exp(-jnp.abs(gpre)))) * (1.0 / GLA_GATE_TAU)

    r = lax.broadcasted_iota(I32, (blk, blk), 0)
    c = lax.broadcasted_iota(I32, (blk, blk), 1)
    causal = (c <= r) & ((r >> shift) == (c >> shift))
    hi = log_a.astype(BF16)
    r1 = log_a - hi.astype(F32)
    mid = r1.astype(BF16)
    lo = (r1 - mid.astype(F32)).astype(BF16)
    tri = causal.astype(BF16)
    b = (jnp.dot(tri, hi, preferred_element_type=F32) + jnp.dot(tri, mid, preferred_element_type=F32)
         + jnp.dot(tri, lo, preferred_element_type=F32))
    b3 = b.reshape(n_ch, c_len, N_GQ)
    b_last = b3[:, c_len - 1:c_len, :]
    decay = jnp.exp(b_last)
    k = k_ref[...].astype(F32)
    q_dec = (q_ref[...].astype(F32) * (GLA_DK ** -0.5) * jnp.exp(b)).astype(BF16)
    k_inv = (k * jnp.exp(-b)).astype(BF16)
    k_dec = (k * jnp.exp(b_last - b3).reshape(blk, N_GQ)).astype(BF16)
    gn = gn_ref[...]

    for hd in range(GLA_HEADS):
        ck = slice(hd * GLA_DK, (hd + 1) * GLA_DK)
        cv = slice(hd * GLA_DV, (hd + 1) * GLA_DV)
        attn = lax.dot_general(q_dec[:, ck], k_inv[:, ck], NT_DIMS, preferred_element_type=F32)
        attn = jnp.where(causal, attn, 0.0).astype(BF16)
        o = jnp.dot(attn, v_ref[:, cv], preferred_element_type=F32)
        st = state_ref[hd]
        inter = []
        for ci in range(n_ch):
            rows = slice(ci * c_len, (ci + 1) * c_len)
            inter.append(lax.dot_general(q_dec[rows, ck], st.astype(BF16), NT_DIMS,
                                         preferred_element_type=F32))
            d_st = lax.dot_general(v_ref[rows, cv], k_dec[rows, ck], TN_DIMS, preferred_element_type=F32)
            st = st * decay[ci, :, ck] + d_st
        state_ref[hd] = st
        o = o + jnp.concatenate(inter, axis=0)
        ms = jnp.mean(o * o, axis=-1, keepdims=True)
        y = o * lax.rsqrt(ms + EPS) * gn
        gate = gg_ref[:, cv].astype(F32)
        o_ref[:, cv] = (y * (gate / (1.0 + jnp.exp(-gate)))).astype(BF16)


def _gla(main, misc, w2, ba, gn, batch, seq):
    t = batch * seq
    blk = min(256, seq)
    nb = seq // blk
    kern = functools.partial(_gla_kernel, blk=blk)

    def rowmap(col):
        return lambda b, n: (b * nb + n, col)

    return pl.pallas_call(
        kern,
        grid=(batch, nb),
        in_specs=[
            pl.BlockSpec((blk, N_GQ), rowmap(OFF_GQ // N_GQ)),
            pl.BlockSpec((blk, N_GQ), rowmap(OFF_GK // N_GQ)),
            pl.BlockSpec((blk, N_GV), rowmap(OFF_GV // N_GV)),
            pl.BlockSpec((blk, N_GV), rowmap(OFF_GG // N_GV)),
            pl.BlockSpec((blk, LANES), rowmap(0)),
            pl.BlockSpec((LANES, N_GQ), lambda b, n: (0, 0)),
            pl.BlockSpec((1, N_GQ), lambda b, n: (0, 0)),
            pl.BlockSpec((1, GLA_DV), lambda b, n: (0, 0)),
        ],
        out_specs=pl.BlockSpec((blk, N_GV), rowmap(0)),
        out_shape=jax.ShapeDtypeStruct((t, N_GV), BF16),
        scratch_shapes=[pltpu.VMEM((GLA_HEADS, GLA_DV, GLA_DK), F32)],
        compiler_params=_params("parallel", "arbitrary"),
        name="gla",
    )(main, main, main, main, misc, w2, ba, gn)


def _dsa_kernel(q_ref, iq_ref, iwt_ref, k_ref, vt_ref, ikd_ref, o_ref,
                key_ref, hi_ref, lo_ref, bias_ref, acc_ref, s_ref, p_ref, m_ref, l_ref, alpha_ref,
                *, blk, topk, idx_bits):
    qi = pl.program_id(1)
    nch = qi + 1
    krow = lax.broadcasted_iota(I32, (blk, blk), 0)
    qcol = lax.broadcasted_iota(I32, (blk, blk), 1)
    w_idx = iwt_ref[0:IDX_HEADS, :] * (IDX_HEADS ** -0.5)

    def score_body(c, carry):
        ks = pl.multiple_of(c * blk, blk)
        k_lo = ikd_ref[pl.ds(ks, blk), 0:LANES]
        k_hi = ikd_ref[pl.ds(ks, blk), LANES:2 * LANES]
        acc = jnp.zeros((blk, blk), F32)
        for p in range(IDX_HEADS // 2):
            iq_p = iq_ref[:, p * LANES:(p + 1) * LANES]
            l0 = lax.dot_general(k_lo, iq_p, NT_DIMS, preferred_element_type=F32)
            l1 = lax.dot_general(k_hi, iq_p, NT_DIMS, preferred_element_type=F32)
            acc = acc + w_idx[2 * p:2 * p + 1, :] * jnp.maximum(l0, 0.0)
            acc = acc + w_idx[2 * p + 1:2 * p + 2, :] * jnp.maximum(l1, 0.0)
        score = jnp.where(krow + (c - qi) * blk > qcol, NEG_INF, acc)
        score = jnp.where(score == 0.0, 0.0, score)
        bits = pltpu.bitcast(score, I32)
        key = bits ^ ((bits >> 31) & 0x7FFFFFFF)
        key_ref[c] = key
        hi_ref[c] = (key >> 16).astype(I16)
        return carry

    lax.fori_loop(0, nch, score_body, 0)

    def count16(ref, cand):
        def body(c, acc):
            m = jnp.where(ref[c] >= cand, jnp.ones((), BF16), jnp.zeros((), BF16))
            for i in range(blk // COUNT16_ROWS):
                acc = acc + m[i * COUNT16_ROWS:(i + 1) * COUNT16_ROWS]
            return acc
        acc = lax.fori_loop(0, nch, body, jnp.zeros((COUNT16_ROWS, blk), BF16))
        return jnp.sum(acc.astype(F32), axis=0, keepdims=True)

    def kth_largest16(ref, kth):
        v = jnp.where(count16(ref, jnp.zeros((1, blk), I16)) >= kth, 0, -(2 ** 15)).astype(I32)

        def body(i, v):
            cand = v | lax.shift_left(jnp.int32(1), 14 - i)
            return jnp.where(count16(ref, cand.astype(I16)) >= kth, cand, v)

        return lax.fori_loop(0, 15, body, v)

    def count(pred):
        def body(c, acc):
            m = jnp.where(pred(key_ref[c], c), 1.0, 0.0)
            return acc + jnp.sum(m.reshape(blk // COUNT_ROWS, COUNT_ROWS, blk), axis=0)
        acc = lax.fori_loop(0, nch, body, jnp.zeros((COUNT_ROWS, blk), F32))
        return jnp.sum(acc, axis=0, keepdims=True)

    kf = float(topk)
    thr_hi = kth_largest16(hi_ref, kf)
    n_above = count16(hi_ref, (thr_hi + 1).astype(I16))
    n_above = jnp.where(thr_hi == 2 ** 15 - 1, 0.0, n_above)

    def low_body(c, carry):
        key = key_ref[c]
        low = (key & 0xFFFF) - 2 ** 15
        lo_ref[c] = jnp.where((key >> 16) == thr_hi, low, -(2 ** 15)).astype(I16)
        return carry

    lax.fori_loop(0, nch, low_body, 0)
    thr_lo = kth_largest16(lo_ref, kf - n_above)
    thr = thr_hi * 2 ** 16 + (thr_lo + 2 ** 15)

    n_ge = n_above + count16(lo_ref, thr_lo.astype(I16))

    def resolve_ties():
        need = kf - count(lambda kc, c: kc > thr)

        def tie_body(i, last):
            cand = last | lax.shift_left(jnp.int32(1), idx_bits - 1 - i)
            below = count(lambda kc, c: (kc == thr) & (c * blk + krow < cand))
            return jnp.where(below < need, cand, last)

        return lax.fori_loop(0, idx_bits, tie_body, jnp.zeros((1, blk), I32))

    last = lax.cond(jnp.max(n_ge) > kf, resolve_ties,
                    lambda: jnp.full((1, blk), 2 ** idx_bits, I32))

    def write_bias(c, diagonal):
        kc = key_ref[c]
        tie_pos = jnp.where(kc == thr, c * blk + krow, -1)
        bias = jnp.where(kc >= thr, jnp.where(tie_pos > last, NEG_INF, 0.0), NEG_INF)
        if diagonal:
            bias = jnp.where(krow > qcol, NEG_INF, bias)
        bias_ref[c] = bias

    def bias_body(c, carry):
        write_bias(c, False)
        return carry

    lax.fori_loop(0, qi, bias_body, 0)
    write_bias(qi, True)

    heads = tuple((hd, slice(hd * DSA_HEAD_DIM, (hd + 1) * DSA_HEAD_DIM)) for hd in range(DSA_HEADS))

    def fold(x, op):
        return op(x.reshape(blk // 8, 8, blk), axis=0)

    m_ref[...] = jnp.full_like(m_ref, NEG_INF)
    l_ref[...] = jnp.zeros_like(l_ref)
    acc_ref[...] = jnp.zeros_like(acc_ref)

    def attn_body(c, carry):
        ks = pl.multiple_of(c * blk, blk)

        @pl.when(c >= 0)
        def _():
            for hd, hs in heads:
                s = lax.dot_general(k_ref[pl.ds(ks, blk), hs], q_ref[:, hs], NT_DIMS,
                                    preferred_element_type=F32) + bias_ref[c]
                s_ref[hd] = s
                m_prev = m_ref[hd, 0:1, :]
                m_new = jnp.maximum(m_prev, jnp.max(fold(s, jnp.max), axis=0, keepdims=True))
                alpha_ref[hd, 0:1, :] = jnp.exp(m_prev - m_new)
                m_ref[hd, 0:1, :] = m_new

        @pl.when(c <= qi)
        def _():
            for hd, hs in heads:
                p = jnp.exp(s_ref[hd] - m_ref[hd, 0:1, :])
                l_ref[hd] = alpha_ref[hd, 0:1, :] * l_ref[hd] + fold(p, jnp.sum)
                p_ref[hd] = p.astype(BF16)
            for hd, hs in heads:
                pv = jnp.dot(vt_ref[c, hs, :], p_ref[hd], preferred_element_type=F32)
                acc_ref[hs, :] = alpha_ref[hd, 0:1, :] * acc_ref[hs, :] + pv

        return carry

    lax.fori_loop(0, nch, attn_body, 0)

    for hd, hs in heads:
        o_t = acc_ref[hs, :] / jnp.sum(l_ref[hd], axis=0, keepdims=True)
        o_ref[:, hs] = o_t.T.astype(BF16)


def _dsa(main, ikd, iwt, dvt, batch, seq):
    t = batch * seq
    blk = DSA_BLK
    nq = seq // blk
    topk = min(DSA_TOPK_MAX, seq // 4)
    idx_bits = max(1, (seq - 1).bit_length())
    assert (blk // COUNT16_ROWS) * nq <= BF16_EXACT_INT
    kern = functools.partial(_dsa_kernel, blk=blk, topk=topk, idx_bits=idx_bits)

    def qmap(col):
        return lambda b, i: (b * nq + i, col)

    return pl.pallas_call(
        kern,
        grid=(batch, nq),
        in_specs=[
            pl.BlockSpec((blk, N_DSA), qmap(OFF_DQ // N_DSA)),
            pl.BlockSpec((blk, N_IQ), qmap(OFF_IQ // N_IQ)),
            pl.BlockSpec((LANES, blk), lambda b, i: (0, b * nq + i)),
            pl.BlockSpec((seq, N_DSA), lambda b, i: (b, OFF_DK // N_DSA)),
            pl.BlockSpec((nq, N_DSA, blk), lambda b, i: (b, 0, 0)),
            pl.BlockSpec((seq, 2 * LANES), lambda b, i: (b, 0)),
        ],
        out_specs=pl.BlockSpec((blk, N_DSA), qmap(0)),
        out_shape=jax.ShapeDtypeStruct((t, N_DSA), BF16),
        scratch_shapes=[
            pltpu.VMEM((nq, blk, blk), I32),
            pltpu.VMEM((nq, blk, blk), I16),
            pltpu.VMEM((nq, blk, blk), I16),
            pltpu.VMEM((nq, blk, blk), F32),
            pltpu.VMEM((N_DSA, blk), F32),
            pltpu.VMEM((DSA_HEADS, blk, blk), F32),
            pltpu.VMEM((DSA_HEADS, blk, blk), BF16),
            pltpu.VMEM((DSA_HEADS, 8, blk), F32),
            pltpu.VMEM((DSA_HEADS, 8, blk), F32),
            pltpu.VMEM((DSA_HEADS, 8, blk), F32),
        ],
        compiler_params=_params("parallel", "arbitrary"),
        name="dsa",
    )(main, main, iwt, main, dvt, ikd)


def _outproj_kernel(og_ref, od_ref, h_ref, w_ref, g_ref, o_ref):
    m = jnp.dot(og_ref[...], w_ref[:N_GV, :], preferred_element_type=F32)
    m = m + jnp.dot(od_ref[...], w_ref[N_GV:, :], preferred_element_type=F32)
    ms = jnp.mean(m * m, axis=-1, keepdims=True)
    o_ref[...] = h_ref[...] + m * lax.rsqrt(ms + EPS) * g_ref[...]


def _outproj(o_gla, o_dsa, h, w_out, gain, layer):
    t, d = h.shape
    tm = min(512, t)
    return pl.pallas_call(
        _outproj_kernel,
        grid=(t // tm,),
        in_specs=[
            pl.BlockSpec((tm, N_GV), lambda i: (i, 0)),
            pl.BlockSpec((tm, N_DSA), lambda i: (i, 0)),
            pl.BlockSpec((tm, d), lambda i: (i, 0)),
            pl.BlockSpec((None, N_GV + N_DSA, d), lambda i: (layer, 0, 0)),
            pl.BlockSpec((1, d), lambda i: (0, 0)),
        ],
        out_specs=pl.BlockSpec((tm, d), lambda i: (i, 0)),
        out_shape=jax.ShapeDtypeStruct((t, d), F32),
        compiler_params=_params("parallel"),
        name="outproj",
    )(o_gla, o_dsa, h, w_out, gain)


def _ffn_kernel(h_ref, gpre_ref, wu_ref, wd_ref, gpost_ref, o_ref, xn_ref, acc_ref):
    f = pl.program_id(1)
    last = pl.num_programs(1) - 1
    rows = h_ref.shape[0] // FFN_ROW_GROUPS
    groups = [slice(g * rows, (g + 1) * rows) for g in range(FFN_ROW_GROUPS)]

    def up(rs):
        u = jnp.maximum(jnp.dot(xn_ref[rs, :], wu_ref[...], preferred_element_type=F32), 0.0)
        return (u * u).astype(BF16)

    @pl.when(f == 0)
    def _():
        for rs in groups:
            x = h_ref[rs, :]
            ms = jnp.mean(x * x, axis=-1, keepdims=True)
            xn_ref[rs, :] = (x * lax.rsqrt(ms + EPS) * gpre_ref[...]).astype(BF16)
            acc_ref[rs, :] = jnp.dot(up(rs), wd_ref[...], preferred_element_type=F32)

    @pl.when((f > 0) & (f < last))
    def _():
        acc_ref[...] += jnp.dot(up(slice(None)), wd_ref[...], preferred_element_type=F32)

    @pl.when(f == last)
    def _():
        for rs in groups:
            y = acc_ref[rs, :] + jnp.dot(up(rs), wd_ref[...], preferred_element_type=F32)
            ms = jnp.mean(y * y, axis=-1, keepdims=True)
            o_ref[rs, :] = h_ref[rs, :] + y * lax.rsqrt(ms + EPS) * gpost_ref[...]


def _ffn(h, g_pre, w_up, w_down, g_post, layer):
    t, d = h.shape
    d_ff = w_up.shape[2]
    tm = min(512, t)
    tf = 1024
    assert d_ff // tf >= 2
    return pl.pallas_call(
        _ffn_kernel,
        grid=(t // tm, d_ff // tf),
        in_specs=[
            pl.BlockSpec((tm, d), lambda i, f: (i, 0)),
            pl.BlockSpec((1, d), lambda i, f: (0, 0)),
            pl.BlockSpec((None, d, tf), lambda i, f: (layer, 0, f)),
            pl.BlockSpec((None, tf, d), lambda i, f: (layer, f, 0)),
            pl.BlockSpec((1, d), lambda i, f: (0, 0)),
        ],
        out_specs=pl.BlockSpec((tm, d), lambda i, f: (i, 0)),
        out_shape=jax.ShapeDtypeStruct((t, d), F32),
        scratch_shapes=[pltpu.VMEM((tm, d), BF16), pltpu.VMEM((tm, d), F32)],
        compiler_params=_params("parallel", "arbitrary"),
        name="ffn",
    )(h, g_pre, w_up, w_down, g_post)


def _rope_table(positions, head_dim):
    r = head_dim // ROPE_FRACTION
    n_freq = r // 2
    inv_freq = ROPE_THETA ** (-(jnp.arange(0, r, 2, dtype=F32) / r))
    rep = ROPE_LANES // n_freq
    t = positions.size
    ang = positions.reshape(t, 1).astype(F32) * jnp.tile(inv_freq, rep)
    flat = lax.optimization_barrier(ang.reshape(-1))
    cos, sin = lax.optimization_barrier((jnp.cos(flat), jnp.sin(flat)))
    cos, sin = cos.reshape(t, ROPE_LANES), sin.reshape(t, ROPE_LANES)
    one = jnp.ones((cos.shape[0], ROPE_PAIR_SHIFT - ROPE_LANES), F32)
    return jnp.concatenate([cos, one, sin, one], axis=1)


def _rope_tables(positions):
    return jnp.stack([_rope_table(positions, DSA_HEAD_DIM), _rope_table(positions, IDX_DIM)])


def _dsa_lane_sources():
    half = DSA_HEAD_DIM // ROPE_FRACTION // 2
    assert half == ROPE_LANES
    cut = 2 * half + ROPE_PAIR_SHIFT - half
    return (list(range(half)) + list(range(2 * half, cut)) + list(range(half, 2 * half))
            + list(range(cut, DSA_HEAD_DIM)))


def _idx_lane_sources():
    half = IDX_DIM // ROPE_FRACTION // 2
    assert 2 * half == ROPE_LANES

    def a(lo, hi):
        return list(range(lo, hi))

    def b(lo, hi):
        return list(range(IDX_DIM + lo, IDX_DIM + hi))

    return (a(0, half) + b(0, half) + a(2 * half, IDX_DIM)
            + a(half, 2 * half) + b(half, 2 * half) + b(2 * half, IDX_DIM))


def _weight_plan(d_in):
    o_gr = 2 * N_GQ + 2 * N_GV
    o_dq = o_gr + GLA_GATE_RANK
    o_dk = o_dq + N_DSA
    o_dv = o_dk + N_DSA
    o_iq = o_dv + N_DSA
    o_ik = o_iq + N_IQ
    o_iw = o_ik + IDX_DIM
    assert o_iw + IDX_HEADS == d_in
    half = IDX_DIM // ROPE_FRACTION // 2
    none = [-1]

    def shifted(base, sources):
        return [base + s for s in sources]

    dsa, idx, ident = _dsa_lane_sources(), _idx_lane_sources(), list(range(LANES))
    tiles = {
        "rot": ([shifted(o_dq + LANES * h, dsa) for h in range(DSA_HEADS)]
                + [shifted(o_dk + LANES * h, dsa) for h in range(DSA_HEADS)]
                + [shifted(o_iq + LANES * p, idx) for p in range(IDX_HEADS // 2)]),
        "small": [
            (shifted(o_ik, range(half)) + none * half + shifted(o_ik, range(2 * half, IDX_DIM))
             + shifted(o_ik, range(half, 2 * half)) + none * (LANES - ROPE_PAIR_SHIFT - half)),
            (none * half + shifted(o_ik, range(half)) + none * (ROPE_PAIR_SHIFT - 2 * half) + none * half
             + shifted(o_ik, range(half, 2 * half)) + shifted(o_ik, range(2 * half, IDX_DIM))),
            (shifted(o_iw, range(IDX_HEADS)) + shifted(o_gr, range(GLA_GATE_RANK))
             + none * (LANES - IDX_HEADS - GLA_GATE_RANK)),
        ],
        "fm": [shifted(o_dv + LANES * t, ident) for t in range(N_DSA // LANES)],
    }
    mats, plans = [], {}
    for name, tile_list in tiles.items():
        plans[name] = []
        for cols in tile_list:
            assert len(cols) == LANES
            parts = []
            for src_tile in sorted({c // LANES for c in cols if c >= 0}):
                m = np.zeros((LANES, LANES), np.float32)
                for dst, c in enumerate(cols):
                    if c >= 0 and c // LANES == src_tile:
                        m[c % LANES, dst] = 1.0
                for mat_id, known in enumerate(mats):
                    if np.array_equal(known, m):
                        break
                else:
                    mat_id = len(mats)
                    mats.append(m)
                parts.append((src_tile, mat_id))
            plans[name].append(parts)
    return plans, np.stack(mats)


def _prep_kernel(w_ref, p_ref, main_ref, small_ref, fm_ref, *, plans, n_plain, d_in):
    cols = w_ref.shape[1]

    def src_tile(t):
        if (t + 1) * LANES > d_in:
            v = w_ref[t * LANES:d_in, :].astype(BF16)
            return jnp.concatenate([v, jnp.zeros(((t + 1) * LANES - d_in, cols), BF16)], axis=0)
        return w_ref[t * LANES:(t + 1) * LANES, :].astype(BF16)

    def gathered(parts):
        acc = None
        for t, mat_id in parts:
            y = jnp.dot(p_ref[mat_id], src_tile(t), preferred_element_type=F32)
            acc = y if acc is None else acc + y
        return acc.astype(BF16)

    main_ref[:n_plain, :] = w_ref[:n_plain, :].astype(BF16)
    for i, parts in enumerate(plans["rot"]):
        main_ref[n_plain + i * LANES:n_plain + (i + 1) * LANES, :] = gathered(parts)
    for i, parts in enumerate(plans["small"]):
        small_ref[i * LANES:(i + 1) * LANES, :] = gathered(parts)
    for i, parts in enumerate(plans["fm"]):
        fm_ref[i * LANES:(i + 1) * LANES, :] = gathered(parts)


def _prep_weights(w_t):
    depth, d_in, d = w_t.shape
    plans, mats = _weight_plan(d_in)
    cb = min(256, d)
    n_plain = OFF_DQ
    n_fm = N_DSA
    kern = functools.partial(_prep_kernel, plans=plans, n_plain=n_plain, d_in=d_in)
    return pl.pallas_call(
        kern,
        grid=(depth, d // cb),
        in_specs=[
            pl.BlockSpec((None, d_in, cb), lambda l, r: (l, 0, r)),
            pl.BlockSpec(mats.shape, lambda l, r: (0, 0, 0)),
        ],
        out_specs=[
            pl.BlockSpec((None, N_MAIN, cb), lambda l, r: (l, 0, r)),
            pl.BlockSpec((None, N_SMALL, cb), lambda l, r: (l, 0, r)),
            pl.BlockSpec((None, n_fm, cb), lambda l, r: (l, 0, r)),
        ],
        out_shape=[
            jax.ShapeDtypeStruct((depth, N_MAIN, d), BF16),
            jax.ShapeDtypeStruct((depth, N_SMALL, d), BF16),
            jax.ShapeDtypeStruct((depth, n_fm, d), BF16),
        ],
        compiler_params=_params("parallel", "parallel"),
        name="prep_weights",
    )(w_t, jnp.asarray(mats.transpose(0, 2, 1), BF16))


def kernel(x, positions, norm_mix_pre, w_in, gla_wa2, gla_ba, gla_norm, w_out, norm_mix_post,
           norm_ffn_pre, w_up, w_down, norm_ffn_post):
    batch, seq, d = x.shape
    depth = w_in.shape[0]
    assert seq % DSA_BLK == 0
    tab = _rope_tables(positions)
    w2 = jnp.pad(gla_wa2, ((0, 0), (IDX_HEADS, LANES - IDX_HEADS - GLA_GATE_RANK), (0, 0)))
    h = x.reshape(batch * seq, d)
    w_main, w_small, w_fm = _prep_weights(jnp.swapaxes(w_in, 1, 2))
    w_out, w_up, w_down = (w.astype(BF16) for w in (w_out, w_up, w_down))
    for l in range(depth):
        main, ikd, misc, dvt, iwt = _inproj(h, norm_mix_pre[l][None, :], w_main, w_small, w_fm, tab, l)
        o_gla = _gla(main, misc, w2[l], gla_ba[l][None, :], gla_norm[l][None, :], batch, seq)
        o_dsa = _dsa(main, ikd, iwt, dvt, batch, seq)
        h = _outproj(o_gla, o_dsa, h, w_out, norm_mix_post[l][None, :], l)
        h = _ffn(h, norm_ffn_pre[l][None, :], w_up, w_down, norm_ffn_post[l][None, :], l)
    return h.reshape(batch, seq, d)
```

```python
import functools

import jax
import jax.numpy as jnp
import numpy as np
from jax import lax
from jax.experimental import pallas as pl
from jax.experimental.pallas import tpu as pltpu

F32 = jnp.float32
BF16 = jnp.bfloat16
I32 = jnp.int32
I16 = jnp.int16

GLA_HEADS = 4
GLA_DK = 128
GLA_DV = 256
GLA_GATE_RANK = 16
GLA_GATE_TAU = 16.0
GLA_CHUNK = 64
DSA_HEADS = 8
DSA_HEAD_DIM = 128
IDX_HEADS = 16
IDX_DIM = 64
DSA_TOPK_MAX = 256
ROPE_THETA = 500000.0
ROPE_FRACTION = 4
EPS = 1e-6
NEG_INF = -1e30
INT_MIN = -(2 ** 31)

LANES = 128
N_GQ = GLA_HEADS * GLA_DK
N_GV = GLA_HEADS * GLA_DV
N_DSA = DSA_HEADS * DSA_HEAD_DIM
N_IQ = IDX_HEADS * IDX_DIM
OFF_GQ = 0
OFF_GK = OFF_GQ + N_GQ
OFF_GV = OFF_GK + N_GQ
OFF_GG = OFF_GV + N_GV
OFF_DQ = OFF_GG + N_GV
OFF_DK = OFF_DQ + N_DSA
OFF_IQ = OFF_DK + N_DSA
N_MAIN = OFF_IQ + N_IQ
N_SMALL = 3 * LANES
DSA_BLK = 256
ROPE_LANES = 16
ROPE_PAIR_SHIFT = LANES // 2
TAB_DSA, TAB_IDX = range(2)
INPROJ_ROW_GROUPS = 4
FFN_ROW_GROUPS = 2
COUNT_ROWS = 32
COUNT16_ROWS = 64
BF16_EXACT_INT = 256
VMEM_LIMIT = 60 * 1024 * 1024

NT_DIMS = (((1,), (1,)), ((), ()))
TN_DIMS = (((0,), (0,)), ((), ()))


def _params(*sem):
    return pltpu.CompilerParams(dimension_semantics=sem, vmem_limit_bytes=VMEM_LIMIT)


def _rope_cs(tab, scale, rot):
    lane = lax.broadcasted_iota(I32, (1, LANES), 1)
    lo = lane < ROPE_LANES
    hi = (lane >= ROPE_PAIR_SHIFT) & (lane < ROPE_PAIR_SHIFT + ROPE_LANES)
    swapped = pltpu.roll(tab, ROPE_PAIR_SHIFT, 1)
    c = jnp.where(hi, swapped, tab)
    s = jnp.where(lo, -swapped, jnp.where(hi, tab, 0.0))
    return (1.0 + rot * (c - 1.0)) * scale, s * (rot * scale)


def _rope(x, c, s):
    parts = []
    for g in range(x.shape[1] // LANES):
        xg = x[:, g * LANES:(g + 1) * LANES]
        parts.append(xg * c + pltpu.roll(xg, ROPE_PAIR_SHIFT, 1) * s)
    return parts[0] if len(parts) == 1 else jnp.concatenate(parts, axis=1)


def _inproj_kernel(x_ref, g_ref, w_ref, ws_ref, wt_ref, tab_ref, tabk_ref,
                   main_ref, ikd_ref, misc_ref, dvt_ref, iwt_ref, xn_ref, *, j_dq, j_dk, j_iq):
    j = pl.program_id(1)

    @pl.when(j == 0)
    def _():
        x = x_ref[...]
        ms = jnp.mean(x * x, axis=-1, keepdims=True)
        xn_ref[...] = (x * lax.rsqrt(ms + EPS) * g_ref[...]).astype(BF16)
        small = lax.dot_general(xn_ref[...], ws_ref[...], NT_DIMS, preferred_element_type=F32)
        ikd_ref[...] = _rope(small[:, :2 * LANES], *_rope_cs(tabk_ref[...], 1.0, 1.0)).astype(BF16)
        misc_ref[...] = small[:, 2 * LANES:]
        iwt_ref[...] = small[:, 2 * LANES:].T
        fm = lax.dot_general(wt_ref[...], xn_ref[...], NT_DIMS, preferred_element_type=F32)
        for ci in range(dvt_ref.shape[0]):
            dvt_ref[ci] = fm[:, ci * DSA_BLK:(ci + 1) * DSA_BLK].astype(BF16)

    rot = jnp.where(j >= j_dq, 1.0, 0.0)
    scale = jnp.where(j < j_dq, 1.0,
                      jnp.where(j < j_dk, DSA_HEAD_DIM ** -0.5, jnp.where(j < j_iq, 1.0, IDX_DIM ** -0.5)))
    c, s = _rope_cs(tab_ref[...], scale, rot)
    rows = xn_ref.shape[0] // INPROJ_ROW_GROUPS
    for g in range(INPROJ_ROW_GROUPS):
        rs = slice(g * rows, (g + 1) * rows)
        acc = lax.dot_general(xn_ref[rs, :], w_ref[...], NT_DIMS, preferred_element_type=F32)
        main_ref[rs, :] = _rope(acc, c[rs], s[rs]).astype(BF16)


def _inproj(h, gain, w_main, w_small, w_t, tab, layer):
    t, d = h.shape
    tm = min(1024, t)
    tn = 1024
    nt = w_t.shape[1]
    j_dq, j_dk, j_iq = OFF_DQ // tn, OFF_DK // tn, OFF_IQ // tn
    kern = functools.partial(_inproj_kernel, j_dq=j_dq, j_dk=j_dk, j_iq=j_iq)
    return pl.pallas_call(
        kern,
        grid=(t // tm, N_MAIN // tn),
        in_specs=[
            pl.BlockSpec((tm, d), lambda i, j: (i, 0)),
            pl.BlockSpec((1, d), lambda i, j: (0, 0)),
            pl.BlockSpec((None, tn, d), lambda i, j: (layer, j, 0)),
            pl.BlockSpec((None, N_SMALL, d), lambda i, j: (layer, 0, 0), pipeline_mode=pl.Buffered(1)),
            pl.BlockSpec((None, nt, d), lambda i, j: (layer, 0, 0), pipeline_mode=pl.Buffered(1)),
            pl.BlockSpec((None, tm, LANES), lambda i, j: ((j >= j_iq).astype(I32), i, 0)),
            pl.BlockSpec((None, tm, LANES), lambda i, j: (TAB_IDX, i, 0)),
        ],
        out_specs=[
            pl.BlockSpec((tm, tn), lambda i, j: (i, j)),
            pl.BlockSpec((tm, 2 * LANES), lambda i, j: (i, 0)),
            pl.BlockSpec((tm, LANES), lambda i, j: (i, 0)),
            pl.BlockSpec((tm // DSA_BLK, N_DSA, DSA_BLK), lambda i, j: (i, 0, 0)),
            pl.BlockSpec((LANES, tm), lambda i, j: (0, i)),
        ],
        out_shape=[
            jax.ShapeDtypeStruct((t, N_MAIN), BF16),
            jax.ShapeDtypeStruct((t, 2 * LANES), BF16),
            jax.ShapeDtypeStruct((t, LANES), F32),
            jax.ShapeDtypeStruct((t // DSA_BLK, N_DSA, DSA_BLK), BF16),
            jax.ShapeDtypeStruct((LANES, t), F32),
        ],
        scratch_shapes=[pltpu.VMEM((tm, d), BF16)],
        compiler_params=_params("parallel", "arbitrary"),
        name="inproj",
    )(h, gain, w_main, w_small, w_t, tab, tab)


def _gla_kernel(q_ref, k_ref, v_ref, gg_ref, misc_ref, w2_ref, ba_ref, gn_ref, o_ref, state_ref, *, blk):
    c_len = GLA_CHUNK
    n_ch = blk // c_len
    shift = c_len.bit_length() - 1
    assert 1 << shift == c_len

    @pl.when(pl.program_id(1) == 0)
    def _():
        state_ref[...] = jnp.zeros_like(state_ref)

    def split(x):
        head = x.astype(BF16)
        return head, (x - head.astype(F32)).astype(BF16)

    (g_hi, g_lo), (w_hi, w_lo) = split(misc_ref[...]), split(w2_ref[...])
    gpre = (jnp.dot(g_hi, w_hi, preferred_element_type=F32) + jnp.dot(g_hi, w_lo, preferred_element_type=F32)
            + jnp.dot(g_lo, w_hi, preferred_element_type=F32)) + ba_ref[...]
    log_a = (jnp.minimum(gpre, 0.0) - jnp.log1p(jnp.exp(-jnp.abs(gpre)))) * (1.0 / GLA_GATE_TAU)

    r = lax.broadcasted_iota(I32, (blk, blk), 0)
    c = lax.broadcasted_iota(I32, (blk, blk), 1)
    causal = (c <= r) & ((r >> shift) == (c >> shift))
    hi = log_a.astype(BF16)
    r1 = log_a - hi.astype(F32)
    mid = r1.astype(BF16)
    lo = (r1 - mid.astype(F32)).astype(BF16)
    tri = causal.astype(BF16)
    b = (jnp.dot(tri, hi, preferred_element_type=F32) + jnp.dot(tri, mid, preferred_element_type=F32)
         + jnp.dot(tri, lo, preferred_element_type=F32))
    b3 = b.reshape(n_ch, c_len, N_GQ)
    b_last = b3[:, c_len - 1:c_len, :]
    decay = jnp.exp(b_last)
    k = k_ref[...].astype(F32)
    q_dec = (q_ref[...].astype(F32) * (GLA_DK ** -0.5) * jnp.exp(b)).astype(BF16)
    k_inv = (k * jnp.exp(-b)).astype(BF16)
    k_dec = (k * jnp.exp(b_last - b3).reshape(blk, N_GQ)).astype(BF16)
    gn = gn_ref[...]

    for hd in range(GLA_HEADS):
        ck = slice(hd * GLA_DK, (hd + 1) * GLA_DK)
        cv = slice(hd * GLA_DV, (hd + 1) * GLA_DV)
        attn = lax.dot_general(q_dec[:, ck], k_inv[:, ck], NT_DIMS, preferred_element_type=F32)
        attn = jnp.where(causal, attn, 0.0).astype(BF16)
        o = jnp.dot(attn, v_ref[:, cv], preferred_element_type=F32)
        st = state_ref[hd]
        inter = []
        for ci in range(n_ch):
            rows = slice(ci * c_len, (ci + 1) * c_len)
            inter.append(lax.dot_general(q_dec[rows, ck], st.astype(BF16), NT_DIMS,
                                         preferred_element_type=F32))
            d_st = lax.dot_general(v_ref[rows, cv], k_dec[rows, ck], TN_DIMS, preferred_element_type=F32)
            st = st * decay[ci, :, ck] + d_st
        state_ref[hd] = st
        o = o + jnp.concatenate(inter, axis=0)
        ms = jnp.mean(o * o, axis=-1, keepdims=True)
        y = o * lax.rsqrt(ms + EPS) * gn
        gate = gg_ref[:, cv].astype(F32)
        o_ref[:, cv] = (y * (gate / (1.0 + jnp.exp(-gate)))).astype(BF16)


def _gla(main, misc, w2, ba, gn, batch, seq):
    t = batch * seq
    blk = min(256, seq)
    nb = seq // blk
    kern = functools.partial(_gla_kernel, blk=blk)

    def rowmap(col):
        return lambda b, n: (b * nb + n, col)

    return pl.pallas_call(
        kern,
        grid=(batch, nb),
        in_specs=[
            pl.BlockSpec((blk, N_GQ), rowmap(OFF_GQ // N_GQ)),
            pl.BlockSpec((blk, N_GQ), rowmap(OFF_GK // N_GQ)),
            pl.BlockSpec((blk, N_GV), rowmap(OFF_GV // N_GV)),
            pl.BlockSpec((blk, N_GV), rowmap(OFF_GG // N_GV)),
            pl.BlockSpec((blk, LANES), rowmap(0)),
            pl.BlockSpec((LANES, N_GQ), lambda b, n: (0, 0)),
            pl.BlockSpec((1, N_GQ), lambda b, n: (0, 0)),
            pl.BlockSpec((1, GLA_DV), lambda b, n: (0, 0)),
        ],
        out_specs=pl.BlockSpec((blk, N_GV), rowmap(0)),
        out_shape=jax.ShapeDtypeStruct((t, N_GV), BF16),
        scratch_shapes=[pltpu.VMEM((GLA_HEADS, GLA_DV, GLA_DK), F32)],
        compiler_params=_params("parallel", "arbitrary"),
        name="gla",
    )(main, main, main, main, misc, w2, ba, gn)


def _dsa_kernel(q_ref, iq_ref, iwt_ref, k_ref, vt_ref, ikd_ref, o_ref,
                key_ref, hi_ref, lo_ref, bias_ref, acc_ref, s_ref, p_ref, m_ref, l_ref, alpha_ref,
                *, blk, topk, idx_bits):
    qi = pl.program_id(1)
    nch = qi + 1
    krow = lax.broadcasted_iota(I32, (blk, blk), 0)
    qcol = lax.broadcasted_iota(I32, (blk, blk), 1)
    w_idx = iwt_ref[0:IDX_HEADS, :] * (IDX_HEADS ** -0.5)

    def score_body(c, carry):
        ks = pl.multiple_of(c * blk, blk)
        k_lo = ikd_ref[pl.ds(ks, blk), 0:LANES]
        k_hi = ikd_ref[pl.ds(ks, blk), LANES:2 * LANES]
        acc = jnp.zeros((blk, blk), F32)
        for p in range(IDX_HEADS // 2):
            iq_p = iq_ref[:, p * LANES:(p + 1) * LANES]
            l0 = lax.dot_general(k_lo, iq_p, NT_DIMS, preferred_element_type=F32)
            l1 = lax.dot_general(k_hi, iq_p, NT_DIMS, preferred_element_type=F32)
            acc = acc + w_idx[2 * p:2 * p + 1, :] * jnp.maximum(l0, 0.0)
            acc = acc + w_idx[2 * p + 1:2 * p + 2, :] * jnp.maximum(l1, 0.0)
        score = jnp.where(krow + (c - qi) * blk > qcol, NEG_INF, acc)
        score = jnp.where(score == 0.0, 0.0, score)
        bits = pltpu.bitcast(score, I32)
        key = bits ^ ((bits >> 31) & 0x7FFFFFFF)
        key_ref[c] = key
        hi_ref[c] = (key >> 16).astype(I16)
        return carry

    lax.fori_loop(0, nch, score_body, 0)

    def count16(ref, cand):
        def body(c, acc):
            m = jnp.where(ref[c] >= cand, jnp.ones((), BF16), jnp.zeros((), BF16))
            for i in range(blk // COUNT16_ROWS):
                acc = acc + m[i * COUNT16_ROWS:(i + 1) * COUNT16_ROWS]
            return acc
        acc = lax.fori_loop(0, nch, body, jnp.zeros((COUNT16_ROWS, blk), BF16))
        return jnp.sum(acc.astype(F32), axis=0, keepdims=True)

    def kth_largest16(ref, kth):
        v = jnp.where(count16(ref, jnp.zeros((1, blk), I16)) >= kth, 0, -(2 ** 15)).astype(I32)

        def body(i, v):
            cand = v | lax.shift_left(jnp.int32(1), 14 - i)
            return jnp.where(count16(ref, cand.astype(I16)) >= kth, cand, v)

        return lax.fori_loop(0, 15, body, v)

    def count(pred):
        def body(c, acc):
            m = jnp.where(pred(key_ref[c], c), 1.0, 0.0)
            return acc + jnp.sum(m.reshape(blk // COUNT_ROWS, COUNT_ROWS, blk), axis=0)
        acc = lax.fori_loop(0, nch, body, jnp.zeros((COUNT_ROWS, blk), F32))
        return jnp.sum(acc, axis=0, keepdims=True)

    kf = float(topk)
    thr_hi = kth_largest16(hi_ref, kf)
    n_above = count16(hi_ref, (thr_hi + 1).astype(I16))
    n_above = jnp.where(thr_hi == 2 ** 15 - 1, 0.0, n_above)

    def low_body(c, carry):
        key = key_ref[c]
        low = (key & 0xFFFF) - 2 ** 15
        lo_ref[c] = jnp.where((key >> 16) == thr_hi, low, -(2 ** 15)).astype(I16)
        return carry

    lax.fori_loop(0, nch, low_body, 0)
    thr_lo = kth_largest16(lo_ref, kf - n_above)
    thr = thr_hi * 2 ** 16 + (thr_lo + 2 ** 15)

    n_ge = n_above + count16(lo_ref, thr_lo.astype(I16))

    def resolve_ties():
        need = kf - count(lambda kc, c: kc > thr)

        def tie_body(i, last):
            cand = last | lax.shift_left(jnp.int32(1), idx_bits - 1 - i)
            below = count(lambda kc, c: (kc == thr) & (c * blk + krow < cand))
            return jnp.where(below < need, cand, last)

        return lax.fori_loop(0, idx_bits, tie_body, jnp.zeros((1, blk), I32))

    last = lax.cond(jnp.max(n_ge) > kf, resolve_ties,
                    lambda: jnp.full((1, blk), 2 ** idx_bits, I32))

    def write_bias(c, diagonal):
        kc = key_ref[c]
        tie_pos = jnp.where(kc == thr, c * blk + krow, -1)
        bias = jnp.where(kc >= thr, jnp.where(tie_pos > last, NEG_INF, 0.0), NEG_INF)
        if diagonal:
            bias = jnp.where(krow > qcol, NEG_INF, bias)
        bias_ref[c] = bias

    def bias_body(c, carry):
        write_bias(c, False)
        return carry

    lax.fori_loop(0, qi, bias_body, 0)
    write_bias(qi, True)

    heads = tuple((hd, slice(hd * DSA_HEAD_DIM, (hd + 1) * DSA_HEAD_DIM)) for hd in range(DSA_HEADS))

    def fold(x, op):
        return op(x.reshape(blk // 8, 8, blk), axis=0)

    m_ref[...] = jnp.full_like(m_ref, NEG_INF)
    l_ref[...] = jnp.zeros_like(l_ref)
    acc_ref[...] = jnp.zeros_like(acc_ref)

    def attn_body(c, carry):
        ks = pl.multiple_of(c * blk, blk)

        @pl.when(c >= 0)
        def _():
            for hd, hs in heads:
                s = lax.dot_general(k_ref[pl.ds(ks, blk), hs], q_ref[:, hs], NT_DIMS,
                                    preferred_element_type=F32) + bias_ref[c]
                s_ref[hd] = s
                m_prev = m_ref[hd, 0:1, :]
                m_new = jnp.maximum(m_prev, jnp.max(fold(s, jnp.max), axis=0, keepdims=True))
                alpha_ref[hd, 0:1, :] = jnp.exp(m_prev - m_new)
                m_ref[hd, 0:1, :] = m_new

        @pl.when(c <= qi)
        def _():
            for hd, hs in heads:
                p = jnp.exp(s_ref[hd] - m_ref[hd, 0:1, :])
                l_ref[hd] = alpha_ref[hd, 0:1, :] * l_ref[hd] + fold(p, jnp.sum)
                p_ref[hd] = p.astype(BF16)
            for hd, hs in heads:
                pv = jnp.dot(vt_ref[c, hs, :], p_ref[hd], preferred_element_type=F32)
                acc_ref[hs, :] = alpha_ref[hd, 0:1, :] * acc_ref[hs, :] + pv

        return carry

    lax.fori_loop(0, nch, attn_body, 0)

    for hd, hs in heads:
        o_t = acc_ref[hs, :] / jnp.sum(l_ref[hd], axis=0, keepdims=True)
        o_ref[:, hs] = o_t.T.astype(BF16)


def _dsa(main, ikd, iwt, dvt, batch, seq):
    t = batch * seq
    blk = DSA_BLK
    nq = seq // blk
    topk = min(DSA_TOPK_MAX, seq // 4)
    idx_bits = max(1, (seq - 1).bit_length())
    assert (blk // COUNT16_ROWS) * nq <= BF16_EXACT_INT
    kern = functools.partial(_dsa_kernel, blk=blk, topk=topk, idx_bits=idx_bits)

    def qmap(col):
        return lambda b, i: (b * nq + i, col)

    return pl.pallas_call(
        kern,
        grid=(batch, nq),
        in_specs=[
            pl.BlockSpec((blk, N_DSA), qmap(OFF_DQ // N_DSA)),
            pl.BlockSpec((blk, N_IQ), qmap(OFF_IQ // N_IQ)),
            pl.BlockSpec((LANES, blk), lambda b, i: (0, b * nq + i)),
            pl.BlockSpec((seq, N_DSA), lambda b, i: (b, OFF_DK // N_DSA)),
            pl.BlockSpec((nq, N_DSA, blk), lambda b, i: (b, 0, 0)),
            pl.BlockSpec((seq, 2 * LANES), lambda b, i: (b, 0)),
        ],
        out_specs=pl.BlockSpec((blk, N_DSA), qmap(0)),
        out_shape=jax.ShapeDtypeStruct((t, N_DSA), BF16),
        scratch_shapes=[
            pltpu.VMEM((nq, blk, blk), I32),
            pltpu.VMEM((nq, blk, blk), I16),
            pltpu.VMEM((nq, blk, blk), I16),
            pltpu.VMEM((nq, blk, blk), F32),
            pltpu.VMEM((N_DSA, blk), F32),
            pltpu.VMEM((DSA_HEADS, blk, blk), F32),
            pltpu.VMEM((DSA_HEADS, blk, blk), BF16),
            pltpu.VMEM((DSA_HEADS, 8, blk), F32),
            pltpu.VMEM((DSA_HEADS, 8, blk), F32),
            pltpu.VMEM((DSA_HEADS, 8, blk), F32),
        ],
        compiler_params=_params("parallel", "arbitrary"),
        name="dsa",
    )(main, main, iwt, main, dvt, ikd)


def _outproj_kernel(og_ref, od_ref, h_ref, w_ref, g_ref, o_ref):
    m = jnp.dot(og_ref[...], w_ref[:N_GV, :], preferred_element_type=F32)
    m = m + jnp.dot(od_ref[...], w_ref[N_GV:, :], preferred_element_type=F32)
    ms = jnp.mean(m * m, axis=-1, keepdims=True)
    o_ref[...] = h_ref[...] + m * lax.rsqrt(ms + EPS) * g_ref[...]


def _outproj(o_gla, o_dsa, h, w_out, gain, layer):
    t, d = h.shape
    tm = min(512, t)
    return pl.pallas_call(
        _outproj_kernel,
        grid=(t // tm,),
        in_specs=[
            pl.BlockSpec((tm, N_GV), lambda i: (i, 0)),
            pl.BlockSpec((tm, N_DSA), lambda i: (i, 0)),
            pl.BlockSpec((tm, d), lambda i: (i, 0)),
            pl.BlockSpec((None, N_GV + N_DSA, d), lambda i: (layer, 0, 0)),
            pl.BlockSpec((1, d), lambda i: (0, 0)),
        ],
        out_specs=pl.BlockSpec((tm, d), lambda i: (i, 0)),
        out_shape=jax.ShapeDtypeStruct((t, d), F32),
        compiler_params=_params("parallel"),
        name="outproj",
    )(o_gla, o_dsa, h, w_out, gain)


def _ffn_kernel(h_ref, gpre_ref, wu_ref, wd_ref, gpost_ref, o_ref, xn_ref, acc_ref):
    f = pl.program_id(1)
    last = pl.num_programs(1) - 1
    rows = h_ref.shape[0] // FFN_ROW_GROUPS
    groups = [slice(g * rows, (g + 1) * rows) for g in range(FFN_ROW_GROUPS)]

    def up(rs):
        u = jnp.maximum(jnp.dot(xn_ref[rs, :], wu_ref[...], preferred_element_type=F32), 0.0)
        return (u * u).astype(BF16)

    @pl.when(f == 0)
    def _():
        for rs in groups:
            x = h_ref[rs, :]
            ms = jnp.mean(x * x, axis=-1, keepdims=True)
            xn_ref[rs, :] = (x * lax.rsqrt(ms + EPS) * gpre_ref[...]).astype(BF16)
            acc_ref[rs, :] = jnp.dot(up(rs), wd_ref[...], preferred_element_type=F32)

    @pl.when((f > 0) & (f < last))
    def _():
        acc_ref[...] += jnp.dot(up(slice(None)), wd_ref[...], preferred_element_type=F32)

    @pl.when(f == last)
    def _():
        for rs in groups:
            y = acc_ref[rs, :] + jnp.dot(up(rs), wd_ref[...], preferred_element_type=F32)
            ms = jnp.mean(y * y, axis=-1, keepdims=True)
            o_ref[rs, :] = h_ref[rs, :] + y * lax.rsqrt(ms + EPS) * gpost_ref[...]


def _ffn(h, g_pre, w_up, w_down, g_post, layer):
    t, d = h.shape
    d_ff = w_up.shape[2]
    tm = min(512, t)
    tf = 1024
    assert d_ff // tf >= 2
    return pl.pallas_call(
        _ffn_kernel,
        grid=(t // tm, d_ff // tf),
        in_specs=[
            pl.BlockSpec((tm, d), lambda i, f: (i, 0)),
            pl.BlockSpec((1, d), lambda i, f: (0, 0)),
            pl.BlockSpec((None, d, tf), lambda i, f: (layer, 0, f)),
            pl.BlockSpec((None, tf, d), lambda i, f: (layer, f, 0)),
            pl.BlockSpec((1, d), lambda i, f: (0, 0)),
        ],
        out_specs=pl.BlockSpec((tm, d), lambda i, f: (i, 0)),
        out_shape=jax.ShapeDtypeStruct((t, d), F32),
        scratch_shapes=[pltpu.VMEM((tm, d), BF16), pltpu.VMEM((tm, d), F32)],
        compiler_params=_params("parallel", "arbitrary"),
        name="ffn",
    )(h, g_pre, w_up, w_down, g_post)


def _rope_table(positions, head_dim):
    r = head_dim // ROPE_FRACTION
    n_freq = r // 2
    inv_freq = ROPE_THETA ** (-(jnp.arange(0, r, 2, dtype=F32) / r))
    rep = ROPE_LANES // n_freq
    t = positions.size
    ang = positions.reshape(t, 1).astype(F32) * jnp.tile(inv_freq, rep)
    flat = lax.optimization_barrier(ang.reshape(-1))
    cos, sin = lax.optimization_barrier((jnp.cos(flat), jnp.sin(flat)))
    cos, sin = cos.reshape(t, ROPE_LANES), sin.reshape(t, ROPE_LANES)
    one = jnp.ones((cos.shape[0], ROPE_PAIR_SHIFT - ROPE_LANES), F32)
    return jnp.concatenate([cos, one, sin, one], axis=1)


def _rope_tables(positions):
    return jnp.stack([_rope_table(positions, DSA_HEAD_DIM), _rope_table(positions, IDX_DIM)])


def _dsa_lane_sources():
    half = DSA_HEAD_DIM // ROPE_FRACTION // 2
    assert half == ROPE_LANES
    cut = 2 * half + ROPE_PAIR_SHIFT - half
    return (list(range(half)) + list(range(2 * half, cut)) + list(range(half, 2 * half))
            + list(range(cut, DSA_HEAD_DIM)))


def _idx_lane_sources():
    half = IDX_DIM // ROPE_FRACTION // 2
    assert 2 * half == ROPE_LANES

    def a(lo, hi):
        return list(range(lo, hi))

    def b(lo, hi):
        return list(range(IDX_DIM + lo, IDX_DIM + hi))

    return (a(0, half) + b(0, half) + a(2 * half, IDX_DIM)
            + a(half, 2 * half) + b(half, 2 * half) + b(2 * half, IDX_DIM))


def _weight_plan(d_in):
    o_gr = 2 * N_GQ + 2 * N_GV
    o_dq = o_gr + GLA_GATE_RANK
    o_dk = o_dq + N_DSA
    o_dv = o_dk + N_DSA
    o_iq = o_dv + N_DSA
    o_ik = o_iq + N_IQ
    o_iw = o_ik + IDX_DIM
    assert o_iw + IDX_HEADS == d_in
    half = IDX_DIM // ROPE_FRACTION // 2
    none = [-1]

    def shifted(base, sources):
        return [base + s for s in sources]

    dsa, idx, ident = _dsa_lane_sources(), _idx_lane_sources(), list(range(LANES))
    tiles = {
        "rot": ([shifted(o_dq + LANES * h, dsa) for h in range(DSA_HEADS)]
                + [shifted(o_dk + LANES * h, dsa) for h in range(DSA_HEADS)]
                + [shifted(o_iq + LANES * p, idx) for p in range(IDX_HEADS // 2)]),
        "small": [
            (shifted(o_ik, range(half)) + none * half + shifted(o_ik, range(2 * half, IDX_DIM))
             + shifted(o_ik, range(half, 2 * half)) + none * (LANES - ROPE_PAIR_SHIFT - half)),
            (none * half + shifted(o_ik, range(half)) + none * (ROPE_PAIR_SHIFT - 2 * half) + none * half
             + shifted(o_ik, range(half, 2 * half)) + shifted(o_ik, range(2 * half, IDX_DIM))),
            (shifted(o_iw, range(IDX_HEADS)) + shifted(o_gr, range(GLA_GATE_RANK))
             + none * (LANES - IDX_HEADS - GLA_GATE_RANK)),
        ],
        "fm": [shifted(o_dv + LANES * t, ident) for t in range(N_DSA // LANES)],
    }
    mats, plans = [], {}
    for name, tile_list in tiles.items():
        plans[name] = []
        for cols in tile_list:
            assert len(cols) == LANES
            parts = []
            for src_tile in sorted({c // LANES for c in cols if c >= 0}):
                m = np.zeros((LANES, LANES), np.float32)
                for dst, c in enumerate(cols):
                    if c >= 0 and c // LANES == src_tile:
                        m[c % LANES, dst] = 1.0
                for mat_id, known in enumerate(mats):
                    if np.array_equal(known, m):
                        break
                else:
                    mat_id = len(mats)
                    mats.append(m)
                parts.append((src_tile, mat_id))
            plans[name].append(parts)
    return plans, np.stack(mats)


def _prep_kernel(w_ref, p_ref, main_ref, small_ref, fm_ref, *, plans, n_plain, d_in):
    cols = w_ref.shape[1]

    def src_tile(t):
        if (t + 1) * LANES > d_in:
            v = w_ref[t * LANES:d_in, :].astype(BF16)
            return jnp.concatenate([v, jnp.zeros(((t + 1) * LANES - d_in, cols), BF16)], axis=0)
        return w_ref[t * LANES:(t + 1) * LANES, :].astype(BF16)

    def gathered(parts):
        acc = None
        for t, mat_id in parts:
            y = jnp.dot(p_ref[mat_id], src_tile(t), preferred_element_type=F32)
            acc = y if acc is None else acc + y
        return acc.astype(BF16)

    main_ref[:n_plain, :] = w_ref[:n_plain, :].astype(BF16)
    for i, parts in enumerate(plans["rot"]):
        main_ref[n_plain + i * LANES:n_plain + (i + 1) * LANES, :] = gathered(parts)
    for i, parts in enumerate(plans["small"]):
        small_ref[i * LANES:(i + 1) * LANES, :] = gathered(parts)
    for i, parts in enumerate(plans["fm"]):
        fm_ref[i * LANES:(i + 1) * LANES, :] = gathered(parts)


def _prep_weights(w_t):
    depth, d_in, d = w_t.shape
    plans, mats = _weight_plan(d_in)
    cb = min(256, d)
    n_plain = OFF_DQ
    n_fm = N_DSA
    kern = functools.partial(_prep_kernel, plans=plans, n_plain=n_plain, d_in=d_in)
    return pl.pallas_call(
        kern,
        grid=(depth, d // cb),
        in_specs=[
            pl.BlockSpec((None, d_in, cb), lambda l, r: (l, 0, r)),
            pl.BlockSpec(mats.shape, lambda l, r: (0, 0, 0)),
        ],
        out_specs=[
            pl.BlockSpec((None, N_MAIN, cb), lambda l, r: (l, 0, r)),
            pl.BlockSpec((None, N_SMALL, cb), lambda l, r: (l, 0, r)),
            pl.BlockSpec((None, n_fm, cb), lambda l, r: (l, 0, r)),
        ],
        out_shape=[
            jax.ShapeDtypeStruct((depth, N_MAIN, d), BF16),
            jax.ShapeDtypeStruct((depth, N_SMALL, d), BF16),
            jax.ShapeDtypeStruct((depth, n_fm, d), BF16),
        ],
        compiler_params=_params("parallel", "parallel"),
        name="prep_weights",
    )(w_t, jnp.asarray(mats.transpose(0, 2, 1), BF16))


def kernel(x, positions, norm_mix_pre, w_in, gla_wa2, gla_ba, gla_norm, w_out, norm_mix_post,
           norm_ffn_pre, w_up, w_down, norm_ffn_post):
    batch, seq, d = x.shape
    depth = w_in.shape[0]
    assert seq % DSA_BLK == 0
    tab = _rope_tables(positions)
    w2 = jnp.pad(gla_wa2, ((0, 0), (IDX_HEADS, LANES - IDX_HEADS - GLA_GATE_RANK), (0, 0)))
    h = x.reshape(batch * seq, d)
    w_main, w_small, w_fm = _prep_weights(jnp.swapaxes(w_in, 1, 2))
    w_out, w_up, w_down = (w.astype(BF16) for w in (w_out, w_up, w_down))
    for l in range(depth):
        main, ikd, misc, dvt, iwt = _inproj(h, norm_mix_pre[l][None, :], w_main, w_small, w_fm, tab, l)
        o_gla = _gla(main, misc, w2[l], gla_ba[l][None, :], gla_norm[l][None, :], batch, seq)
        o_dsa = _dsa(main, ikd, iwt, dvt, batch, seq)
        h = _outproj(o_gla, o_dsa, h, w_out, norm_mix_post[l][None, :], l)
        h = _ffn(h, norm_ffn_pre[l][None, :], w_up, w_down, norm_ffn_post[l][None, :], l)
    return h.reshape(batch, seq, d)
```

```python
import functools

import jax
import jax.numpy as jnp
import numpy as np
from jax import lax
from jax.experimental import pallas as pl
from jax.experimental.pallas import tpu as pltpu

F32 = jnp.float32
BF16 = jnp.bfloat16
I32 = jnp.int32
I16 = jnp.int16

GLA_HEADS = 4
GLA_DK = 128
GLA_DV = 256
GLA_GATE_RANK = 16
GLA_GATE_TAU = 16.0
GLA_CHUNK = 64
DSA_HEADS = 8
DSA_HEAD_DIM = 128
IDX_HEADS = 16
IDX_DIM = 64
DSA_TOPK_MAX = 256
ROPE_THETA = 500000.0
ROPE_FRACTION = 4
EPS = 1e-6
NEG_INF = -1e30
INT_MIN = -(2 ** 31)

LANES = 128
N_GQ = GLA_HEADS * GLA_DK
N_GV = GLA_HEADS * GLA_DV
N_DSA = DSA_HEADS * DSA_HEAD_DIM
N_IQ = IDX_HEADS * IDX_DIM
OFF_GQ = 0
OFF_GK = OFF_GQ + N_GQ
OFF_GV = OFF_GK + N_GQ
OFF_GG = OFF_GV + N_GV
OFF_DQ = OFF_GG + N_GV
OFF_DK = OFF_DQ + N_DSA
OFF_IQ = OFF_DK + N_DSA
N_MAIN = OFF_IQ + N_IQ
N_SMALL = 3 * LANES
DSA_BLK = 256
ROPE_LANES = 16
ROPE_PAIR_SHIFT = LANES // 2
TAB_DSA, TAB_IDX = range(2)
INPROJ_ROW_GROUPS = 4
FFN_ROW_GROUPS = 2
COUNT_ROWS = 32
COUNT16_ROWS = 64
BF16_EXACT_INT = 256
VMEM_LIMIT = 60 * 1024 * 1024

NT_DIMS = (((1,), (1,)), ((), ()))
TN_DIMS = (((0,), (0,)), ((), ()))


def _params(*sem):
    return pltpu.CompilerParams(dimension_semantics=sem, vmem_limit_bytes=VMEM_LIMIT)


def _rope_cs(tab, scale, rot):
    lane = lax.broadcasted_iota(I32, (1, LANES), 1)
    lo = lane < ROPE_LANES
    hi = (lane >= ROPE_PAIR_SHIFT) & (lane < ROPE_PAIR_SHIFT + ROPE_LANES)
    swapped = pltpu.roll(tab, ROPE_PAIR_SHIFT, 1)
    c = jnp.where(hi, swapped, tab)
    s = jnp.where(lo, -swapped, jnp.where(hi, tab, 0.0))
    return (1.0 + rot * (c - 1.0)) * scale, s * (rot * scale)


def _rope(x, c, s):
    parts = []
    for g in range(x.shape[1] // LANES):
        xg = x[:, g * LANES:(g + 1) * LANES]
        parts.append(xg * c + pltpu.roll(xg, ROPE_PAIR_SHIFT, 1) * s)
    return parts[0] if len(parts) == 1 else jnp.concatenate(parts, axis=1)


def _inproj_kernel(x_ref, g_ref, w_ref, ws_ref, wt_ref, tab_ref, tabk_ref,
                   main_ref, ikd_ref, misc_ref, dvt_ref, iwt_ref, xn_ref, *, j_dq, j_dk, j_iq):
    j = pl.program_id(1)

    @pl.when(j == 0)
    def _():
        x = x_ref[...]
        ms = jnp.mean(x * x, axis=-1, keepdims=True)
        xn_ref[...] = (x * lax.rsqrt(ms + EPS) * g_ref[...]).astype(BF16)
        small = lax.dot_general(xn_ref[...], ws_ref[...], NT_DIMS, preferred_element_type=F32)
        ikd_ref[...] = _rope(small[:, :2 * LANES], *_rope_cs(tabk_ref[...], 1.0, 1.0)).astype(BF16)
        misc_ref[...] = small[:, 2 * LANES:]
        iwt_ref[...] = small[:, 2 * LANES:].T
        fm = lax.dot_general(wt_ref[...], xn_ref[...], NT_DIMS, preferred_element_type=F32)
        for ci in range(dvt_ref.shape[0]):
            dvt_ref[ci] = fm[:, ci * DSA_BLK:(ci + 1) * DSA_BLK].astype(BF16)

    rot = jnp.where(j >= j_dq, 1.0, 0.0)
    scale = jnp.where(j < j_dq, 1.0,
                      jnp.where(j < j_dk, DSA_HEAD_DIM ** -0.5, jnp.where(j < j_iq, 1.0, IDX_DIM ** -0.5)))
    c, s = _rope_cs(tab_ref[...], scale, rot)
    rows = xn_ref.shape[0] // INPROJ_ROW_GROUPS
    for g in range(INPROJ_ROW_GROUPS):
        rs = slice(g * rows, (g + 1) * rows)
        acc = lax.dot_general(xn_ref[rs, :], w_ref[...], NT_DIMS, preferred_element_type=F32)
        main_ref[rs, :] = _rope(acc, c[rs], s[rs]).astype(BF16)


def _inproj(h, gain, w_main, w_small, w_t, tab, layer):
    t, d = h.shape
    tm = min(1024, t)
    tn = 1024
    nt = w_t.shape[1]
    j_dq, j_dk, j_iq = OFF_DQ // tn, OFF_DK // tn, OFF_IQ // tn
    kern = functools.partial(_inproj_kernel, j_dq=j_dq, j_dk=j_dk, j_iq=j_iq)
    return pl.pallas_call(
        kern,
        grid=(t // tm, N_MAIN // tn),
        in_specs=[
            pl.BlockSpec((tm, d), lambda i, j: (i, 0)),
            pl.BlockSpec((1, d), lambda i, j: (0, 0)),
            pl.BlockSpec((None, tn, d), lambda i, j: (layer, j, 0)),
            pl.BlockSpec((None, N_SMALL, d), lambda i, j: (layer, 0, 0), pipeline_mode=pl.Buffered(1)),
            pl.BlockSpec((None, nt, d), lambda i, j: (layer, 0, 0), pipeline_mode=pl.Buffered(1)),
            pl.BlockSpec((None, tm, LANES), lambda i, j: ((j >= j_iq).astype(I32), i, 0)),
            pl.BlockSpec((None, tm, LANES), lambda i, j: (TAB_IDX, i, 0)),
        ],
        out_specs=[
            pl.BlockSpec((tm, tn), lambda i, j: (i, j)),
            pl.BlockSpec((tm, 2 * LANES), lambda i, j: (i, 0)),
            pl.BlockSpec((tm, LANES), lambda i, j: (i, 0)),
            pl.BlockSpec((tm // DSA_BLK, N_DSA, DSA_BLK), lambda i, j: (i, 0, 0)),
            pl.BlockSpec((LANES, tm), lambda i, j: (0, i)),
        ],
        out_shape=[
            jax.ShapeDtypeStruct((t, N_MAIN), BF16),
            jax.ShapeDtypeStruct((t, 2 * LANES), BF16),
            jax.ShapeDtypeStruct((t, LANES), F32),
            jax.ShapeDtypeStruct((t // DSA_BLK, N_DSA, DSA_BLK), BF16),
            jax.ShapeDtypeStruct((LANES, t), F32),
        ],
        scratch_shapes=[pltpu.VMEM((tm, d), BF16)],
        compiler_params=_params("parallel", "arbitrary"),
        name="inproj",
    )(h, gain, w_main, w_small, w_t, tab, tab)


def _gla_kernel(q_ref, k_ref, v_ref, gg_ref, misc_ref, w2_ref, ba_ref, gn_ref, o_ref, state_ref, *, blk):
    c_len = GLA_CHUNK
    n_ch = blk // c_len
    shift = c_len.bit_length() - 1
    assert 1 << shift == c_len

    @pl.when(pl.program_id(1) == 0)
    def _():
        state_ref[...] = jnp.zeros_like(state_ref)

    def split(x):
        head = x.astype(BF16)
        return head, (x - head.astype(F32)).astype(BF16)

    (g_hi, g_lo), (w_hi, w_lo) = split(misc_ref[...]), split(w2_ref[...])
    gpre = (jnp.dot(g_hi, w_hi, preferred_element_type=F32) + jnp.dot(g_hi, w_lo, preferred_element_type=F32)
            + jnp.dot(g_lo, w_hi, preferred_element_type=F32)) + ba_ref[...]
    log_a = (jnp.minimum(gpre, 0.0) - jnp.log1p(jnp.exp(-jnp.abs(gpre)))) * (1.0 / GLA_GATE_TAU)

    r = lax.broadcasted_iota(I32, (blk, blk), 0)
    c = lax.broadcasted_iota(I32, (blk, blk), 1)
    causal = (c <= r) & ((r >> shift) == (c >> shift))
    hi = log_a.astype(BF16)
    r1 = log_a - hi.astype(F32)
    mid = r1.astype(BF16)
    lo = (r1 - mid.astype(F32)).astype(BF16)
    tri = causal.astype(BF16)
    b = (jnp.dot(tri, hi, preferred_element_type=F32) + jnp.dot(tri, mid, preferred_element_type=F32)
         + jnp.dot(tri, lo, preferred_element_type=F32))
    b3 = b.reshape(n_ch, c_len, N_GQ)
    b_last = b3[:, c_len - 1:c_len, :]
    decay = jnp.exp(b_last)
    k = k_ref[...].astype(F32)
    q_dec = (q_ref[...].astype(F32) * (GLA_DK ** -0.5) * jnp.exp(b)).astype(BF16)
    k_inv = (k * jnp.exp(-b)).astype(BF16)
    k_dec = (k * jnp.exp(b_last - b3).reshape(blk, N_GQ)).astype(BF16)
    gn = gn_ref[...]

    for hd in range(GLA_HEADS):
        ck = slice(hd * GLA_DK, (hd + 1) * GLA_DK)
        cv = slice(hd * GLA_DV, (hd + 1) * GLA_DV)
        attn = lax.dot_general(q_dec[:, ck], k_inv[:, ck], NT_DIMS, preferred_element_type=F32)
        attn = jnp.where(causal, attn, 0.0).astype(BF16)
        o = jnp.dot(attn, v_ref[:, cv], preferred_element_type=F32)
        st = state_ref[hd]
        inter = []
        for ci in range(n_ch):
            rows = slice(ci * c_len, (ci + 1) * c_len)
            inter.append(lax.dot_general(q_dec[rows, ck], st.astype(BF16), NT_DIMS,
                                         preferred_element_type=F32))
            d_st = lax.dot_general(v_ref[rows, cv], k_dec[rows, ck], TN_DIMS, preferred_element_type=F32)
            st = st * decay[ci, :, ck] + d_st
        state_ref[hd] = st
        o = o + jnp.concatenate(inter, axis=0)
        ms = jnp.mean(o * o, axis=-1, keepdims=True)
        y = o * lax.rsqrt(ms + EPS) * gn
        gate = gg_ref[:, cv].astype(F32)
        o_ref[:, cv] = (y * (gate / (1.0 + jnp.exp(-gate)))).astype(BF16)


def _gla(main, misc, w2, ba, gn, batch, seq):
    t = batch * seq
    blk = min(256, seq)
    nb = seq // blk
    kern = functools.partial(_gla_kernel, blk=blk)

    def rowmap(col):
        return lambda b, n: (b * nb + n, col)

    return pl.pallas_call(
        kern,
        grid=(batch, nb),
        in_specs=[
            pl.BlockSpec((blk, N_GQ), rowmap(OFF_GQ // N_GQ)),
            pl.BlockSpec((blk, N_GQ), rowmap(OFF_GK // N_GQ)),
            pl.BlockSpec((blk, N_GV), rowmap(OFF_GV // N_GV)),
            pl.BlockSpec((blk, N_GV), rowmap(OFF_GG // N_GV)),
            pl.BlockSpec((blk, LANES), rowmap(0)),
            pl.BlockSpec((LANES, N_GQ), lambda b, n: (0, 0)),
            pl.BlockSpec((1, N_GQ), lambda b, n: (0, 0)),
            pl.BlockSpec((1, GLA_DV), lambda b, n: (0, 0)),
        ],
        out_specs=pl.BlockSpec((blk, N_GV), rowmap(0)),
        out_shape=jax.ShapeDtypeStruct((t, N_GV), BF16),
        scratch_shapes=[pltpu.VMEM((GLA_HEADS, GLA_DV, GLA_DK), F32)],
        compiler_params=_params("parallel", "arbitrary"),
        name="gla",
    )(main, main, main, main, misc, w2, ba, gn)


def _dsa_kernel(q_ref, iq_ref, iwt_ref, k_ref, vt_ref, ikd_ref, o_ref,
                key_ref, hi_ref, lo_ref, bias_ref, acc_ref, s_ref, p_ref, m_ref, l_ref, alpha_ref,
                *, blk, topk, idx_bits):
    qi = pl.program_id(1)
    nch = qi + 1
    krow = lax.broadcasted_iota(I32, (blk, blk), 0)
    qcol = lax.broadcasted_iota(I32, (blk, blk), 1)
    w_idx = iwt_ref[0:IDX_HEADS, :] * (IDX_HEADS ** -0.5)

    def score_body(c, carry):
        ks = pl.multiple_of(c * blk, blk)
        k_lo = ikd_ref[pl.ds(ks, blk), 0:LANES]
        k_hi = ikd_ref[pl.ds(ks, blk), LANES:2 * LANES]
        acc = jnp.zeros((blk, blk), F32)
        for p in range(IDX_HEADS // 2):
            iq_p = iq_ref[:, p * LANES:(p + 1) * LANES]
            l0 = lax.dot_general(k_lo, iq_p, NT_DIMS, preferred_element_type=F32)
            l1 = lax.dot_general(k_hi, iq_p, NT_DIMS, preferred_element_type=F32)
            acc = acc + w_idx[2 * p:2 * p + 1, :] * jnp.maximum(l0, 0.0)
            acc = acc + w_idx[2 * p + 1:2 * p + 2, :] * jnp.maximum(l1, 0.0)
        score = jnp.where(krow + (c - qi) * blk > qcol, NEG_INF, acc)
        score = jnp.where(score == 0.0, 0.0, score)
        bits = pltpu.bitcast(score, I32)
        key = bits ^ ((bits >> 31) & 0x7FFFFFFF)
        key_ref[c] = key
        hi_ref[c] = (key >> 16).astype(I16)
        return carry

    lax.fori_loop(0, nch, score_body, 0)

    def count16(ref, cand):
        def body(c, acc):
            m = jnp.where(ref[c] >= cand, jnp.ones((), BF16), jnp.zeros((), BF16))
            for i in range(blk // COUNT16_ROWS):
                acc = acc + m[i * COUNT16_ROWS:(i + 1) * COUNT16_ROWS]
            return acc
        acc = lax.fori_loop(0, nch, body, jnp.zeros((COUNT16_ROWS, blk), BF16))
        return jnp.sum(acc.astype(F32), axis=0, keepdims=True)

    def kth_largest16(ref, kth):
        v = jnp.where(count16(ref, jnp.zeros((1, blk), I16)) >= kth, 0, -(2 ** 15)).astype(I32)

        def body(i, v):
            cand = v | lax.shift_left(jnp.int32(1), 14 - i)
            return jnp.where(count16(ref, cand.astype(I16)) >= kth, cand, v)

        return lax.fori_loop(0, 15, body, v)

    def count(pred):
        def body(c, acc):
            m = jnp.where(pred(key_ref[c], c), 1.0, 0.0)
            return acc + jnp.sum(m.reshape(blk // COUNT_ROWS, COUNT_ROWS, blk), axis=0)
        acc = lax.fori_loop(0, nch, body, jnp.zeros((COUNT_ROWS, blk), F32))
        return jnp.sum(acc, axis=0, keepdims=True)

    kf = float(topk)
    thr_hi = kth_largest16(hi_ref, kf)
    n_above = count16(hi_ref, (thr_hi + 1).astype(I16))
    n_above = jnp.where(thr_hi == 2 ** 15 - 1, 0.0, n_above)

    def low_body(c, carry):
        key = key_ref[c]
        low = (key & 0xFFFF) - 2 ** 15
        lo_ref[c] = jnp.where((key >> 16) == thr_hi, low, -(2 ** 15)).astype(I16)
        return carry

    lax.fori_loop(0, nch, low_body, 0)
    thr_lo = kth_largest16(lo_ref, kf - n_above)
    thr = thr_hi * 2 ** 16 + (thr_lo + 2 ** 15)

    n_ge = n_above + count16(lo_ref, thr_lo.astype(I16))

    def resolve_ties():
        need = kf - count(lambda kc, c: kc > thr)

        def tie_body(i, last):
            cand = last | lax.shift_left(jnp.int32(1), idx_bits - 1 - i)
            below = count(lambda kc, c: (kc == thr) & (c * blk + krow < cand))
            return jnp.where(below < need, cand, last)

        return lax.fori_loop(0, idx_bits, tie_body, jnp.zeros((1, blk), I32))

    last = lax.cond(jnp.max(n_ge) > kf, resolve_ties,
                    lambda: jnp.full((1, blk), 2 ** idx_bits, I32))

    def write_bias(c, diagonal):
        kc = key_ref[c]
        tie_pos = jnp.where(kc == thr, c * blk + krow, -1)
        bias = jnp.where(kc >= thr, jnp.where(tie_pos > last, NEG_INF, 0.0), NEG_INF)
        if diagonal:
            bias = jnp.where(krow > qcol, NEG_INF, bias)
        bias_ref[c] = bias

    def bias_body(c, carry):
        write_bias(c, False)
        return carry

    lax.fori_loop(0, qi, bias_body, 0)
    write_bias(qi, True)

    heads = tuple((hd, slice(hd * DSA_HEAD_DIM, (hd + 1) * DSA_HEAD_DIM)) for hd in range(DSA_HEADS))

    def fold(x, op):
        return op(x.reshape(blk // 8, 8, blk), axis=0)

    m_ref[...] = jnp.full_like(m_ref, NEG_INF)
    l_ref[...] = jnp.zeros_like(l_ref)
    acc_ref[...] = jnp.zeros_like(acc_ref)

    def stage_scores(c, slot):
        ks = pl.multiple_of(c * blk, blk)
        for hd, hs in heads:
            s = lax.dot_general(k_ref[pl.ds(ks, blk), hs], q_ref[:, hs], NT_DIMS,
                                preferred_element_type=F32) + bias_ref[c]
            s_ref[slot, hd] = s
            m_prev = m_ref[hd, 0:1, :]
            m_new = jnp.maximum(m_prev, jnp.max(fold(s, jnp.max), axis=0, keepdims=True))
            alpha_ref[slot, hd, 0:1, :] = jnp.exp(m_prev - m_new)
            m_ref[hd, 0:1, :] = m_new

    stage_scores(0, 0)

    def attn_body(c, carry):
        slot = c & 1
        for hd, hs in heads:
            p = jnp.exp(s_ref[slot, hd] - m_ref[hd, 0:1, :])
            l_ref[hd] = alpha_ref[slot, hd, 0:1, :] * l_ref[hd] + fold(p, jnp.sum)
            p_ref[hd] = p.astype(BF16)
        for hd, hs in heads:
            pv = jnp.dot(vt_ref[c, hs, :], p_ref[hd], preferred_element_type=F32)
            acc_ref[hs, :] = alpha_ref[slot, hd, 0:1, :] * acc_ref[hs, :] + pv
        stage_scores(jnp.minimum(c + 1, qi), 1 - slot)
        return carry

    lax.fori_loop(0, nch, attn_body, 0)

    for hd, hs in heads:
        o_t = acc_ref[hs, :] / jnp.sum(l_ref[hd], axis=0, keepdims=True)
        o_ref[:, hs] = o_t.T.astype(BF16)


def _dsa(main, ikd, iwt, dvt, batch, seq):
    t = batch * seq
    blk = DSA_BLK
    nq = seq // blk
    topk = min(DSA_TOPK_MAX, seq // 4)
    idx_bits = max(1, (seq - 1).bit_length())
    assert (blk // COUNT16_ROWS) * nq <= BF16_EXACT_INT
    kern = functools.partial(_dsa_kernel, blk=blk, topk=topk, idx_bits=idx_bits)

    def qmap(col):
        return lambda b, i: (b * nq + i, col)

    return pl.pallas_call(
        kern,
        grid=(batch, nq),
        in_specs=[
            pl.BlockSpec((blk, N_DSA), qmap(OFF_DQ // N_DSA)),
            pl.BlockSpec((blk, N_IQ), qmap(OFF_IQ // N_IQ)),
            pl.BlockSpec((LANES, blk), lambda b, i: (0, b * nq + i)),
            pl.BlockSpec((seq, N_DSA), lambda b, i: (b, OFF_DK // N_DSA)),
            pl.BlockSpec((nq, N_DSA, blk), lambda b, i: (b, 0, 0), pipeline_mode=pl.Buffered(1)),
            pl.BlockSpec((seq, 2 * LANES), lambda b, i: (b, 0), pipeline_mode=pl.Buffered(1)),
        ],
        out_specs=pl.BlockSpec((blk, N_DSA), qmap(0)),
        out_shape=jax.ShapeDtypeStruct((t, N_DSA), BF16),
        scratch_shapes=[
            pltpu.VMEM((nq, blk, blk), I32),
            pltpu.VMEM((nq, blk, blk), I16),
            pltpu.VMEM((nq, blk, blk), I16),
            pltpu.VMEM((nq, blk, blk), F32),
            pltpu.VMEM((N_DSA, blk), F32),
            pltpu.VMEM((2, DSA_HEADS, blk, blk), F32),
            pltpu.VMEM((DSA_HEADS, blk, blk), BF16),
            pltpu.VMEM((DSA_HEADS, 8, blk), F32),
            pltpu.VMEM((DSA_HEADS, 8, blk), F32),
            pltpu.VMEM((2, DSA_HEADS, 8, blk), F32),
        ],
        compiler_params=_params("parallel", "arbitrary"),
        name="dsa",
    )(main, main, iwt, main, dvt, ikd)


def _outproj_kernel(og_ref, od_ref, h_ref, w_ref, g_ref, o_ref):
    m = jnp.dot(og_ref[...], w_ref[:N_GV, :], preferred_element_type=F32)
    m = m + jnp.dot(od_ref[...], w_ref[N_GV:, :], preferred_element_type=F32)
    ms = jnp.mean(m * m, axis=-1, keepdims=True)
    o_ref[...] = h_ref[...] + m * lax.rsqrt(ms + EPS) * g_ref[...]


def _outproj(o_gla, o_dsa, h, w_out, gain, layer):
    t, d = h.shape
    tm = min(512, t)
    return pl.pallas_call(
        _outproj_kernel,
        grid=(t // tm,),
        in_specs=[
            pl.BlockSpec((tm, N_GV), lambda i: (i, 0)),
            pl.BlockSpec((tm, N_DSA), lambda i: (i, 0)),
            pl.BlockSpec((tm, d), lambda i: (i, 0)),
            pl.BlockSpec((None, N_GV + N_DSA, d), lambda i: (layer, 0, 0)),
            pl.BlockSpec((1, d), lambda i: (0, 0)),
        ],
        out_specs=pl.BlockSpec((tm, d), lambda i: (i, 0)),
        out_shape=jax.ShapeDtypeStruct((t, d), F32),
        compiler_params=_params("parallel"),
        name="outproj",
    )(o_gla, o_dsa, h, w_out, gain)


def _ffn_kernel(h_ref, gpre_ref, wu_ref, wd_ref, gpost_ref, o_ref, xn_ref, acc_ref):
    f = pl.program_id(1)
    last = pl.num_programs(1) - 1
    rows = h_ref.shape[0] // FFN_ROW_GROUPS
    groups = [slice(g * rows, (g + 1) * rows) for g in range(FFN_ROW_GROUPS)]

    def up(rs):
        u = jnp.maximum(jnp.dot(xn_ref[rs, :], wu_ref[...], preferred_element_type=F32), 0.0)
        return (u * u).astype(BF16)

    @pl.when(f == 0)
    def _():
        for rs in groups:
            x = h_ref[rs, :]
            ms = jnp.mean(x * x, axis=-1, keepdims=True)
            xn_ref[rs, :] = (x * lax.rsqrt(ms + EPS) * gpre_ref[...]).astype(BF16)
            acc_ref[rs, :] = jnp.dot(up(rs), wd_ref[...], preferred_element_type=F32)

    @pl.when((f > 0) & (f < last))
    def _():
        acc_ref[...] += jnp.dot(up(slice(None)), wd_ref[...], preferred_element_type=F32)

    @pl.when(f == last)
    def _():
        for rs in groups:
            y = acc_ref[rs, :] + jnp.dot(up(rs), wd_ref[...], preferred_element_type=F32)
            ms = jnp.mean(y * y, axis=-1, keepdims=True)
            o_ref[rs, :] = h_ref[rs, :] + y * lax.rsqrt(ms + EPS) * gpost_ref[...]


def _ffn(h, g_pre, w_up, w_down, g_post, layer):
    t, d = h.shape
    d_ff = w_up.shape[2]
    tm = min(512, t)
    tf = 1024
    assert d_ff // tf >= 2
    return pl.pallas_call(
        _ffn_kernel,
        grid=(t // tm, d_ff // tf),
        in_specs=[
            pl.BlockSpec((tm, d), lambda i, f: (i, 0)),
            pl.BlockSpec((1, d), lambda i, f: (0, 0)),
            pl.BlockSpec((None, d, tf), lambda i, f: (layer, 0, f)),
            pl.BlockSpec((None, tf, d), lambda i, f: (layer, f, 0)),
            pl.BlockSpec((1, d), lambda i, f: (0, 0)),
        ],
        out_specs=pl.BlockSpec((tm, d), lambda i, f: (i, 0)),
        out_shape=jax.ShapeDtypeStruct((t, d), F32),
        scratch_shapes=[pltpu.VMEM((tm, d), BF16), pltpu.VMEM((tm, d), F32)],
        compiler_params=_params("parallel", "arbitrary"),
        name="ffn",
    )(h, g_pre, w_up, w_down, g_post)


def _rope_table(positions, head_dim):
    r = head_dim // ROPE_FRACTION
    n_freq = r // 2
    inv_freq = ROPE_THETA ** (-(jnp.arange(0, r, 2, dtype=F32) / r))
    rep = ROPE_LANES // n_freq
    t = positions.size
    ang = positions.reshape(t, 1).astype(F32) * jnp.tile(inv_freq, rep)
    flat = lax.optimization_barrier(ang.reshape(-1))
    cos, sin = lax.optimization_barrier((jnp.cos(flat), jnp.sin(flat)))
    cos, sin = cos.reshape(t, ROPE_LANES), sin.reshape(t, ROPE_LANES)
    one = jnp.ones((cos.shape[0], ROPE_PAIR_SHIFT - ROPE_LANES), F32)
    return jnp.concatenate([cos, one, sin, one], axis=1)


def _rope_tables(positions):
    return jnp.stack([_rope_table(positions, DSA_HEAD_DIM), _rope_table(positions, IDX_DIM)])


def _dsa_lane_sources():
    half = DSA_HEAD_DIM // ROPE_FRACTION // 2
    assert half == ROPE_LANES
    cut = 2 * half + ROPE_PAIR_SHIFT - half
    return (list(range(half)) + list(range(2 * half, cut)) + list(range(half, 2 * half))
            + list(range(cut, DSA_HEAD_DIM)))


def _idx_lane_sources():
    half = IDX_DIM // ROPE_FRACTION // 2
    assert 2 * half == ROPE_LANES

    def a(lo, hi):
        return list(range(lo, hi))

    def b(lo, hi):
        return list(range(IDX_DIM + lo, IDX_DIM + hi))

    return (a(0, half) + b(0, half) + a(2 * half, IDX_DIM)
            + a(half, 2 * half) + b(half, 2 * half) + b(2 * half, IDX_DIM))


def _weight_plan(d_in):
    o_gr = 2 * N_GQ + 2 * N_GV
    o_dq = o_gr + GLA_GATE_RANK
    o_dk = o_dq + N_DSA
    o_dv = o_dk + N_DSA
    o_iq = o_dv + N_DSA
    o_ik = o_iq + N_IQ
    o_iw = o_ik + IDX_DIM
    assert o_iw + IDX_HEADS == d_in
    half = IDX_DIM // ROPE_FRACTION // 2
    none = [-1]

    def shifted(base, sources):
        return [base + s for s in sources]

    dsa, idx, ident = _dsa_lane_sources(), _idx_lane_sources(), list(range(LANES))
    tiles = {
        "rot": ([shifted(o_dq + LANES * h, dsa) for h in range(DSA_HEADS)]
                + [shifted(o_dk + LANES * h, dsa) for h in range(DSA_HEADS)]
                + [shifted(o_iq + LANES * p, idx) for p in range(IDX_HEADS // 2)]),
        "small": [
            (shifted(o_ik, range(half)) + none * half + shifted(o_ik, range(2 * half, IDX_DIM))
             + shifted(o_ik, range(half, 2 * half)) + none * (LANES - ROPE_PAIR_SHIFT - half)),
            (none * half + shifted(o_ik, range(half)) + none * (ROPE_PAIR_SHIFT - 2 * half) + none * half
             + shifted(o_ik, range(half, 2 * half)) + shifted(o_ik, range(2 * half, IDX_DIM))),
            (shifted(o_iw, range(IDX_HEADS)) + shifted(o_gr, range(GLA_GATE_RANK))
             + none * (LANES - IDX_HEADS - GLA_GATE_RANK)),
        ],
        "fm": [shifted(o_dv + LANES * t, ident) for t in range(N_DSA // LANES)],
    }
    mats, plans = [], {}
    for name, tile_list in tiles.items():
        plans[name] = []
        for cols in tile_list:
            assert len(cols) == LANES
            parts = []
            for src_tile in sorted({c // LANES for c in cols if c >= 0}):
                m = np.zeros((LANES, LANES), np.float32)
                for dst, c in enumerate(cols):
                    if c >= 0 and c // LANES == src_tile:
                        m[c % LANES, dst] = 1.0
                for mat_id, known in enumerate(mats):
                    if np.array_equal(known, m):
                        break
                else:
                    mat_id = len(mats)
                    mats.append(m)
                parts.append((src_tile, mat_id))
            plans[name].append(parts)
    return plans, np.stack(mats)


def _prep_kernel(w_ref, p_ref, main_ref, small_ref, fm_ref, *, plans, n_plain, d_in):
    cols = w_ref.shape[1]

    def src_tile(t):
        if (t + 1) * LANES > d_in:
            v = w_ref[t * LANES:d_in, :].astype(BF16)
            return jnp.concatenate([v, jnp.zeros(((t + 1) * LANES - d_in, cols), BF16)], axis=0)
        return w_ref[t * LANES:(t + 1) * LANES, :].astype(BF16)

    def gathered(parts):
        acc = None
        for t, mat_id in parts:
            y = jnp.dot(p_ref[mat_id], src_tile(t), preferred_element_type=F32)
            acc = y if acc is None else acc + y
        return acc.astype(BF16)

    main_ref[:n_plain, :] = w_ref[:n_plain, :].astype(BF16)
    for i, parts in enumerate(plans["rot"]):
        main_ref[n_plain + i * LANES:n_plain + (i + 1) * LANES, :] = gathered(parts)
    for i, parts in enumerate(plans["small"]):
        small_ref[i * LANES:(i + 1) * LANES, :] = gathered(parts)
    for i, parts in enumerate(plans["fm"]):
        fm_ref[i * LANES:(i + 1) * LANES, :] = gathered(parts)


def _prep_weights(w_t):
    depth, d_in, d = w_t.shape
    plans, mats = _weight_plan(d_in)
    cb = min(256, d)
    n_plain = OFF_DQ
    n_fm = N_DSA
    kern = functools.partial(_prep_kernel, plans=plans, n_plain=n_plain, d_in=d_in)
    return pl.pallas_call(
        kern,
        grid=(depth, d // cb),
        in_specs=[
            pl.BlockSpec((None, d_in, cb), lambda l, r: (l, 0, r)),
            pl.BlockSpec(mats.shape, lambda l, r: (0, 0, 0)),
        ],
        out_specs=[
            pl.BlockSpec((None, N_MAIN, cb), lambda l, r: (l, 0, r)),
            pl.BlockSpec((None, N_SMALL, cb), lambda l, r: (l, 0, r)),
            pl.BlockSpec((None, n_fm, cb), lambda l, r: (l, 0, r)),
        ],
        out_shape=[
            jax.ShapeDtypeStruct((depth, N_MAIN, d), BF16),
            jax.ShapeDtypeStruct((depth, N_SMALL, d), BF16),
            jax.ShapeDtypeStruct((depth, n_fm, d), BF16),
        ],
        compiler_params=_params("parallel", "parallel"),
        name="prep_weights",
    )(w_t, jnp.asarray(mats.transpose(0, 2, 1), BF16))


def kernel(x, positions, norm_mix_pre, w_in, gla_wa2, gla_ba, gla_norm, w_out, norm_mix_post,
           norm_ffn_pre, w_up, w_down, norm_ffn_post):
    batch, seq, d = x.shape
    depth = w_in.shape[0]
    assert seq % DSA_BLK == 0
    tab = _rope_tables(positions)
    w2 = jnp.pad(gla_wa2, ((0, 0), (IDX_HEADS, LANES - IDX_HEADS - GLA_GATE_RANK), (0, 0)))
    h = x.reshape(batch * seq, d)
    w_main, w_small, w_fm = _prep_weights(jnp.swapaxes(w_in, 1, 2))
    w_out, w_up, w_down = (w.astype(BF16) for w in (w_out, w_up, w_down))
    for l in range(depth):
        main, ikd, misc, dvt, iwt = _inproj(h, norm_mix_pre[l][None, :], w_main, w_small, w_fm, tab, l)
        o_gla = _gla(main, misc, w2[l], gla_ba[l][None, :], gla_norm[l][None, :], batch, seq)
        o_dsa = _dsa(main, ikd, iwt, dvt, batch, seq)
        h = _outproj(o_gla, o_dsa, h, w_out, norm_mix_post[l][None, :], l)
        h = _ffn(h, norm_ffn_pre[l][None, :], w_up, w_down, norm_ffn_post[l][None, :], l)
    return h.reshape(batch, seq, d)
```

```python
import functools

import jax
import jax.numpy as jnp
import numpy as np
from jax import lax
from jax.experimental import pallas as pl
from jax.experimental.pallas import tpu as pltpu

F32 = jnp.float32
BF16 = jnp.bfloat16
I32 = jnp.int32
I16 = jnp.int16

GLA_HEADS = 4
GLA_DK = 128
GLA_DV = 256
GLA_GATE_RANK = 16
GLA_GATE_TAU = 16.0
GLA_CHUNK = 64
DSA_HEADS = 8
DSA_HEAD_DIM = 128
IDX_HEADS = 16
IDX_DIM = 64
DSA_TOPK_MAX = 256
ROPE_THETA = 500000.0
ROPE_FRACTION = 4
EPS = 1e-6
NEG_INF = -1e30
INT_MIN = -(2 ** 31)

LANES = 128
N_GQ = GLA_HEADS * GLA_DK
N_GV = GLA_HEADS * GLA_DV
N_DSA = DSA_HEADS * DSA_HEAD_DIM
N_IQ = IDX_HEADS * IDX_DIM
OFF_GQ = 0
OFF_GK = OFF_GQ + N_GQ
OFF_GV = OFF_GK + N_GQ
OFF_GG = OFF_GV + N_GV
OFF_DQ = OFF_GG + N_GV
OFF_DK = OFF_DQ + N_DSA
OFF_IQ = OFF_DK + N_DSA
N_MAIN = OFF_IQ + N_IQ
N_SMALL = 3 * LANES
DSA_BLK = 256
DSA_Q_SCALE = DSA_HEAD_DIM ** -0.5 * 1.4426950408889634
ROPE_LANES = 16
ROPE_PAIR_SHIFT = LANES // 2
TAB_DSA, TAB_IDX = range(2)
INPROJ_ROW_GROUPS = 4
FFN_ROW_GROUPS = 2
COUNT_ROWS = 32
COUNT16_ROWS = 64
BF16_EXACT_INT = 256
VMEM_LIMIT = 60 * 1024 * 1024

NT_DIMS = (((1,), (1,)), ((), ()))
TN_DIMS = (((0,), (0,)), ((), ()))


def _params(*sem):
    return pltpu.CompilerParams(dimension_semantics=sem, vmem_limit_bytes=VMEM_LIMIT)


def _rope_cs(tab, scale, rot):
    lane = lax.broadcasted_iota(I32, (1, LANES), 1)
    lo = lane < ROPE_LANES
    hi = (lane >= ROPE_PAIR_SHIFT) & (lane < ROPE_PAIR_SHIFT + ROPE_LANES)
    swapped = pltpu.roll(tab, ROPE_PAIR_SHIFT, 1)
    c = jnp.where(hi, swapped, tab)
    s = jnp.where(lo, -swapped, jnp.where(hi, tab, 0.0))
    return (1.0 + rot * (c - 1.0)) * scale, s * (rot * scale)


def _rope(x, c, s):
    parts = []
    for g in range(x.shape[1] // LANES):
        xg = x[:, g * LANES:(g + 1) * LANES]
        parts.append(xg * c + pltpu.roll(xg, ROPE_PAIR_SHIFT, 1) * s)
    return parts[0] if len(parts) == 1 else jnp.concatenate(parts, axis=1)


def _inproj_kernel(x_ref, g_ref, w_ref, ws_ref, wt_ref, tab_ref, tabk_ref,
                   main_ref, ikd_ref, misc_ref, dvt_ref, iwt_ref, xn_ref, *, j_dq, j_dk, j_iq):
    j = pl.program_id(1)

    @pl.when(j == 0)
    def _():
        x = x_ref[...]
        ms = jnp.mean(x * x, axis=-1, keepdims=True)
        xn_ref[...] = (x * lax.rsqrt(ms + EPS) * g_ref[...]).astype(BF16)
        small = lax.dot_general(xn_ref[...], ws_ref[...], NT_DIMS, preferred_element_type=F32)
        ikd_ref[...] = _rope(small[:, :2 * LANES], *_rope_cs(tabk_ref[...], 1.0, 1.0)).astype(BF16)
        misc_ref[...] = small[:, 2 * LANES:]
        iwt_ref[...] = small[:, 2 * LANES:].T
        fm = lax.dot_general(wt_ref[...], xn_ref[...], NT_DIMS, preferred_element_type=F32)
        for ci in range(dvt_ref.shape[0]):
            dvt_ref[ci] = fm[:, ci * DSA_BLK:(ci + 1) * DSA_BLK].astype(BF16)

    rot = jnp.where(j >= j_dq, 1.0, 0.0)
    scale = jnp.where(j < j_dq, 1.0,
                      jnp.where(j < j_dk, DSA_Q_SCALE, jnp.where(j < j_iq, 1.0, IDX_DIM ** -0.5)))
    c, s = _rope_cs(tab_ref[...], scale, rot)
    rows = xn_ref.shape[0] // INPROJ_ROW_GROUPS
    for g in range(INPROJ_ROW_GROUPS):
        rs = slice(g * rows, (g + 1) * rows)
        acc = lax.dot_general(xn_ref[rs, :], w_ref[...], NT_DIMS, preferred_element_type=F32)
        main_ref[rs, :] = _rope(acc, c[rs], s[rs]).astype(BF16)


def _inproj(h, gain, w_main, w_small, w_t, tab, layer):
    t, d = h.shape
    tm = min(1024, t)
    tn = 1024
    nt = w_t.shape[1]
    j_dq, j_dk, j_iq = OFF_DQ // tn, OFF_DK // tn, OFF_IQ // tn
    kern = functools.partial(_inproj_kernel, j_dq=j_dq, j_dk=j_dk, j_iq=j_iq)
    return pl.pallas_call(
        kern,
        grid=(t // tm, N_MAIN // tn),
        in_specs=[
            pl.BlockSpec((tm, d), lambda i, j: (i, 0)),
            pl.BlockSpec((1, d), lambda i, j: (0, 0)),
            pl.BlockSpec((None, tn, d), lambda i, j: (layer, j, 0)),
            pl.BlockSpec((None, N_SMALL, d), lambda i, j: (layer, 0, 0), pipeline_mode=pl.Buffered(1)),
            pl.BlockSpec((None, nt, d), lambda i, j: (layer, 0, 0), pipeline_mode=pl.Buffered(1)),
            pl.BlockSpec((None, tm, LANES), lambda i, j: ((j >= j_iq).astype(I32), i, 0)),
            pl.BlockSpec((None, tm, LANES), lambda i, j: (TAB_IDX, i, 0)),
        ],
        out_specs=[
            pl.BlockSpec((tm, tn), lambda i, j: (i, j)),
            pl.BlockSpec((tm, 2 * LANES), lambda i, j: (i, 0)),
            pl.BlockSpec((tm, LANES), lambda i, j: (i, 0)),
            pl.BlockSpec((tm // DSA_BLK, N_DSA, DSA_BLK), lambda i, j: (i, 0, 0)),
            pl.BlockSpec((LANES, tm), lambda i, j: (0, i)),
        ],
        out_shape=[
            jax.ShapeDtypeStruct((t, N_MAIN), BF16),
            jax.ShapeDtypeStruct((t, 2 * LANES), BF16),
            jax.ShapeDtypeStruct((t, LANES), F32),
            jax.ShapeDtypeStruct((t // DSA_BLK, N_DSA, DSA_BLK), BF16),
            jax.ShapeDtypeStruct((LANES, t), F32),
        ],
        scratch_shapes=[pltpu.VMEM((tm, d), BF16)],
        compiler_params=_params("parallel", "arbitrary"),
        name="inproj",
    )(h, gain, w_main, w_small, w_t, tab, tab)


def _gla_kernel(q_ref, k_ref, v_ref, gg_ref, misc_ref, w2_ref, ba_ref, gn_ref, o_ref, state_ref, *, blk):
    c_len = GLA_CHUNK
    n_ch = blk // c_len
    shift = c_len.bit_length() - 1
    assert 1 << shift == c_len

    @pl.when(pl.program_id(1) == 0)
    def _():
        state_ref[...] = jnp.zeros_like(state_ref)

    def split(x):
        head = x.astype(BF16)
        return head, (x - head.astype(F32)).astype(BF16)

    (g_hi, g_lo), (w_hi, w_lo) = split(misc_ref[...]), split(w2_ref[...])
    gpre = (jnp.dot(g_hi, w_hi, preferred_element_type=F32) + jnp.dot(g_hi, w_lo, preferred_element_type=F32)
            + jnp.dot(g_lo, w_hi, preferred_element_type=F32)) + ba_ref[...]
    log_a = (jnp.minimum(gpre, 0.0) - jnp.log1p(jnp.exp(-jnp.abs(gpre)))) * (1.0 / GLA_GATE_TAU)

    r = lax.broadcasted_iota(I32, (blk, blk), 0)
    c = lax.broadcasted_iota(I32, (blk, blk), 1)
    causal = (c <= r) & ((r >> shift) == (c >> shift))
    hi = log_a.astype(BF16)
    r1 = log_a - hi.astype(F32)
    mid = r1.astype(BF16)
    lo = (r1 - mid.astype(F32)).astype(BF16)
    tri = causal.astype(BF16)
    b = (jnp.dot(tri, hi, preferred_element_type=F32) + jnp.dot(tri, mid, preferred_element_type=F32)
         + jnp.dot(tri, lo, preferred_element_type=F32))
    b3 = b.reshape(n_ch, c_len, N_GQ)
    b_last = b3[:, c_len - 1:c_len, :]
    decay = jnp.exp(b_last)
    k = k_ref[...].astype(F32)
    q_dec = (q_ref[...].astype(F32) * (GLA_DK ** -0.5) * jnp.exp(b)).astype(BF16)
    k_inv = (k * jnp.exp(-b)).astype(BF16)
    k_dec = (k * jnp.exp(b_last - b3).reshape(blk, N_GQ)).astype(BF16)
    gn = gn_ref[...]

    for hd in range(GLA_HEADS):
        ck = slice(hd * GLA_DK, (hd + 1) * GLA_DK)
        cv = slice(hd * GLA_DV, (hd + 1) * GLA_DV)
        attn = lax.dot_general(q_dec[:, ck], k_inv[:, ck], NT_DIMS, preferred_element_type=F32)
        attn = jnp.where(causal, attn, 0.0).astype(BF16)
        o = jnp.dot(attn, v_ref[:, cv], preferred_element_type=F32)
        st = state_ref[hd]
        inter = []
        for ci in range(n_ch):
            rows = slice(ci * c_len, (ci + 1) * c_len)
            inter.append(lax.dot_general(q_dec[rows, ck], st.astype(BF16), NT_DIMS,
                                         preferred_element_type=F32))
            d_st = lax.dot_general(v_ref[rows, cv], k_dec[rows, ck], TN_DIMS, preferred_element_type=F32)
            st = st * decay[ci, :, ck] + d_st
        state_ref[hd] = st
        o = o + jnp.concatenate(inter, axis=0)
        ms = jnp.mean(o * o, axis=-1, keepdims=True)
        y = o * lax.rsqrt(ms + EPS) * gn
        gate = gg_ref[:, cv].astype(F32)
        o_ref[:, cv] = (y * (gate / (1.0 + jnp.exp(-gate)))).astype(BF16)


def _gla(main, misc, w2, ba, gn, batch, seq):
    t = batch * seq
    blk = min(256, seq)
    nb = seq // blk
    kern = functools.partial(_gla_kernel, blk=blk)

    def rowmap(col):
        return lambda b, n: (b * nb + n, col)

    return pl.pallas_call(
        kern,
        grid=(batch, nb),
        in_specs=[
            pl.BlockSpec((blk, N_GQ), rowmap(OFF_GQ // N_GQ)),
            pl.BlockSpec((blk, N_GQ), rowmap(OFF_GK // N_GQ)),
            pl.BlockSpec((blk, N_GV), rowmap(OFF_GV // N_GV)),
            pl.BlockSpec((blk, N_GV), rowmap(OFF_GG // N_GV)),
            pl.BlockSpec((blk, LANES), rowmap(0)),
            pl.BlockSpec((LANES, N_GQ), lambda b, n: (0, 0)),
            pl.BlockSpec((1, N_GQ), lambda b, n: (0, 0)),
            pl.BlockSpec((1, GLA_DV), lambda b, n: (0, 0)),
        ],
        out_specs=pl.BlockSpec((blk, N_GV), rowmap(0)),
        out_shape=jax.ShapeDtypeStruct((t, N_GV), BF16),
        scratch_shapes=[pltpu.VMEM((GLA_HEADS, GLA_DV, GLA_DK), F32)],
        compiler_params=_params("parallel", "arbitrary"),
        name="gla",
    )(main, main, main, main, misc, w2, ba, gn)


def _dsa_kernel(q_ref, iq_ref, iwt_ref, k_ref, vt_ref, ikd_ref, o_ref,
                key_ref, hi_ref, lo_ref, bias_ref, acc_ref, s_ref, p_ref, m_ref, l_ref, alpha_ref,
                *, blk, topk, idx_bits):
    qi = pl.program_id(1)
    nch = qi + 1
    krow = lax.broadcasted_iota(I32, (blk, blk), 0)
    qcol = lax.broadcasted_iota(I32, (blk, blk), 1)
    w_idx = iwt_ref[0:IDX_HEADS, :] * (IDX_HEADS ** -0.5)

    def score_body(c, carry):
        ks = pl.multiple_of(c * blk, blk)
        k_lo = ikd_ref[pl.ds(ks, blk), 0:LANES]
        k_hi = ikd_ref[pl.ds(ks, blk), LANES:2 * LANES]
        acc = jnp.zeros((blk, blk), F32)
        for p in range(IDX_HEADS // 2):
            iq_p = iq_ref[:, p * LANES:(p + 1) * LANES]
            l0 = lax.dot_general(k_lo, iq_p, NT_DIMS, preferred_element_type=F32)
            l1 = lax.dot_general(k_hi, iq_p, NT_DIMS, preferred_element_type=F32)
            acc = acc + w_idx[2 * p:2 * p + 1, :] * jnp.maximum(l0, 0.0)
            acc = acc + w_idx[2 * p + 1:2 * p + 2, :] * jnp.maximum(l1, 0.0)
        score = jnp.where(krow + (c - qi) * blk > qcol, NEG_INF, acc)
        score = jnp.where(score == 0.0, 0.0, score)
        bits = pltpu.bitcast(score, I32)
        key = bits ^ ((bits >> 31) & 0x7FFFFFFF)
        key_ref[c] = key
        hi_ref[c] = (key >> 16).astype(I16)
        return carry

    lax.fori_loop(0, nch, score_body, 0)

    def count16(ref, cand):
        def body(c, acc):
            m = jnp.where(ref[c] >= cand, jnp.ones((), BF16), jnp.zeros((), BF16))
            for i in range(blk // COUNT16_ROWS):
                acc = acc + m[i * COUNT16_ROWS:(i + 1) * COUNT16_ROWS]
            return acc
        acc = lax.fori_loop(0, nch, body, jnp.zeros((COUNT16_ROWS, blk), BF16))
        return jnp.sum(acc.astype(F32), axis=0, keepdims=True)

    def kth_largest16(ref, kth):
        v = jnp.where(count16(ref, jnp.zeros((1, blk), I16)) >= kth, 0, -(2 ** 15)).astype(I32)

        def body(i, v):
            cand = v | lax.shift_left(jnp.int32(1), 14 - i)
            return jnp.where(count16(ref, cand.astype(I16)) >= kth, cand, v)

        return lax.fori_loop(0, 15, body, v)

    def count(pred):
        def body(c, acc):
            m = jnp.where(pred(key_ref[c], c), 1.0, 0.0)
            return acc + jnp.sum(m.reshape(blk // COUNT_ROWS, COUNT_ROWS, blk), axis=0)
        acc = lax.fori_loop(0, nch, body, jnp.zeros((COUNT_ROWS, blk), F32))
        return jnp.sum(acc, axis=0, keepdims=True)

    kf = float(topk)
    thr_hi = kth_largest16(hi_ref, kf)
    n_above = count16(hi_ref, (thr_hi + 1).astype(I16))
    n_above = jnp.where(thr_hi == 2 ** 15 - 1, 0.0, n_above)

    def low_body(c, carry):
        key = key_ref[c]
        low = (key & 0xFFFF) - 2 ** 15
        lo_ref[c] = jnp.where((key >> 16) == thr_hi, low, -(2 ** 15)).astype(I16)
        return carry

    lax.fori_loop(0, nch, low_body, 0)
    thr_lo = kth_largest16(lo_ref, kf - n_above)
    thr = thr_hi * 2 ** 16 + (thr_lo + 2 ** 15)

    n_ge = n_above + count16(lo_ref, thr_lo.astype(I16))

    def resolve_ties():
        need = kf - count(lambda kc, c: kc > thr)

        def tie_body(i, last):
            cand = last | lax.shift_left(jnp.int32(1), idx_bits - 1 - i)
            below = count(lambda kc, c: (kc == thr) & (c * blk + krow < cand))
            return jnp.where(below < need, cand, last)

        return lax.fori_loop(0, idx_bits, tie_body, jnp.zeros((1, blk), I32))

    last = lax.cond(jnp.max(n_ge) > kf, resolve_ties,
                    lambda: jnp.full((1, blk), 2 ** idx_bits, I32))

    def write_bias(c, diagonal):
        kc = key_ref[c]
        tie_pos = jnp.where(kc == thr, c * blk + krow, -1)
        bias = jnp.where(kc >= thr, jnp.where(tie_pos > last, NEG_INF, 0.0), NEG_INF)
        if diagonal:
            bias = jnp.where(krow > qcol, NEG_INF, bias)
        bias_ref[c] = bias

    def bias_body(c, carry):
        write_bias(c, False)
        return carry

    lax.fori_loop(0, qi, bias_body, 0)
    write_bias(qi, True)

    heads = tuple((hd, slice(hd * DSA_HEAD_DIM, (hd + 1) * DSA_HEAD_DIM)) for hd in range(DSA_HEADS))

    def fold(x, op):
        return op(x.reshape(blk // 8, 8, blk), axis=0)

    m_ref[...] = jnp.full_like(m_ref, NEG_INF)
    l_ref[...] = jnp.zeros_like(l_ref)
    acc_ref[...] = jnp.zeros_like(acc_ref)

    def stage_scores(c, slot):
        ks = pl.multiple_of(c * blk, blk)
        for hd, hs in heads:
            s = lax.dot_general(k_ref[pl.ds(ks, blk), hs], q_ref[:, hs], NT_DIMS,
                                preferred_element_type=F32) + bias_ref[c]
            s_ref[slot, hd] = s
            m_prev = m_ref[hd, 0:1, :]
            m_new = jnp.maximum(m_prev, jnp.max(fold(s, jnp.max), axis=0, keepdims=True))
            alpha_ref[slot, hd, 0:1, :] = jnp.exp2(m_prev - m_new)
            m_ref[hd, 0:1, :] = m_new

    stage_scores(0, 0)

    def attn_body(c, carry):
        slot = c & 1
        for hd, hs in heads:
            p = jnp.exp2(s_ref[slot, hd] - m_ref[hd, 0:1, :])
            l_ref[hd] = alpha_ref[slot, hd, 0:1, :] * l_ref[hd] + fold(p, jnp.sum)
            p_ref[hd] = p.astype(BF16)
        for hd, hs in heads:
            pv = jnp.dot(vt_ref[c, hs, :], p_ref[hd], preferred_element_type=F32)
            acc_ref[hs, :] = alpha_ref[slot, hd, 0:1, :] * acc_ref[hs, :] + pv
        stage_scores(jnp.minimum(c + 1, qi), 1 - slot)
        return carry

    lax.fori_loop(0, nch, attn_body, 0)

    for hd, hs in heads:
        o_t = acc_ref[hs, :] / jnp.sum(l_ref[hd], axis=0, keepdims=True)
        o_ref[:, hs] = o_t.T.astype(BF16)


def _dsa(main, ikd, iwt, dvt, batch, seq):
    t = batch * seq
    blk = DSA_BLK
    nq = seq // blk
    topk = min(DSA_TOPK_MAX, seq // 4)
    idx_bits = max(1, (seq - 1).bit_length())
    assert (blk // COUNT16_ROWS) * nq <= BF16_EXACT_INT
    kern = functools.partial(_dsa_kernel, blk=blk, topk=topk, idx_bits=idx_bits)

    def qmap(col):
        return lambda b, i: (b * nq + i, col)

    return pl.pallas_call(
        kern,
        grid=(batch, nq),
        in_specs=[
            pl.BlockSpec((blk, N_DSA), qmap(OFF_DQ // N_DSA)),
            pl.BlockSpec((blk, N_IQ), qmap(OFF_IQ // N_IQ)),
            pl.BlockSpec((LANES, blk), lambda b, i: (0, b * nq + i)),
            pl.BlockSpec((seq, N_DSA), lambda b, i: (b, OFF_DK // N_DSA)),
            pl.BlockSpec((nq, N_DSA, blk), lambda b, i: (b, 0, 0), pipeline_mode=pl.Buffered(1)),
            pl.BlockSpec((seq, 2 * LANES), lambda b, i: (b, 0), pipeline_mode=pl.Buffered(1)),
        ],
        out_specs=pl.BlockSpec((blk, N_DSA), qmap(0)),
        out_shape=jax.ShapeDtypeStruct((t, N_DSA), BF16),
        scratch_shapes=[
            pltpu.VMEM((nq, blk, blk), I32),
            pltpu.VMEM((nq, blk, blk), I16),
            pltpu.VMEM((nq, blk, blk), I16),
            pltpu.VMEM((nq, blk, blk), F32),
            pltpu.VMEM((N_DSA, blk), F32),
            pltpu.VMEM((2, DSA_HEADS, blk, blk), F32),
            pltpu.VMEM((DSA_HEADS, blk, blk), BF16),
            pltpu.VMEM((DSA_HEADS, 8, blk), F32),
            pltpu.VMEM((DSA_HEADS, 8, blk), F32),
            pltpu.VMEM((2, DSA_HEADS, 8, blk), F32),
        ],
        compiler_params=_params("parallel", "arbitrary"),
        name="dsa",
    )(main, main, iwt, main, dvt, ikd)


def _outproj_kernel(og_ref, od_ref, h_ref, w_ref, g_ref, o_ref):
    m = jnp.dot(og_ref[...], w_ref[:N_GV, :], preferred_element_type=F32)
    m = m + jnp.dot(od_ref[...], w_ref[N_GV:, :], preferred_element_type=F32)
    ms = jnp.mean(m * m, axis=-1, keepdims=True)
    o_ref[...] = h_ref[...] + m * lax.rsqrt(ms + EPS) * g_ref[...]


def _outproj(o_gla, o_dsa, h, w_out, gain, layer):
    t, d = h.shape
    tm = min(512, t)
    return pl.pallas_call(
        _outproj_kernel,
        grid=(t // tm,),
        in_specs=[
            pl.BlockSpec((tm, N_GV), lambda i: (i, 0)),
            pl.BlockSpec((tm, N_DSA), lambda i: (i, 0)),
            pl.BlockSpec((tm, d), lambda i: (i, 0)),
            pl.BlockSpec((None, N_GV + N_DSA, d), lambda i: (layer, 0, 0)),
            pl.BlockSpec((1, d), lambda i: (0, 0)),
        ],
        out_specs=pl.BlockSpec((tm, d), lambda i: (i, 0)),
        out_shape=jax.ShapeDtypeStruct((t, d), F32),
        compiler_params=_params("parallel"),
        name="outproj",
    )(o_gla, o_dsa, h, w_out, gain)


def _ffn_kernel(h_ref, gpre_ref, wu_ref, wd_ref, gpost_ref, o_ref, xn_ref, acc_ref):
    f = pl.program_id(1)
    last = pl.num_programs(1) - 1
    rows = h_ref.shape[0] // FFN_ROW_GROUPS
    groups = [slice(g * rows, (g + 1) * rows) for g in range(FFN_ROW_GROUPS)]

    def up(rs):
        u = jnp.maximum(jnp.dot(xn_ref[rs, :], wu_ref[...], preferred_element_type=F32), 0.0)
        return (u * u).astype(BF16)

    @pl.when(f == 0)
    def _():
        for rs in groups:
            x = h_ref[rs, :]
            ms = jnp.mean(x * x, axis=-1, keepdims=True)
            xn_ref[rs, :] = (x * lax.rsqrt(ms + EPS) * gpre_ref[...]).astype(BF16)
            acc_ref[rs, :] = jnp.dot(up(rs), wd_ref[...], preferred_element_type=F32)

    @pl.when((f > 0) & (f < last))
    def _():
        acc_ref[...] += jnp.dot(up(slice(None)), wd_ref[...], preferred_element_type=F32)

    @pl.when(f == last)
    def _():
        for rs in groups:
            y = acc_ref[rs, :] + jnp.dot(up(rs), wd_ref[...], preferred_element_type=F32)
            ms = jnp.mean(y * y, axis=-1, keepdims=True)
            o_ref[rs, :] = h_ref[rs, :] + y * lax.rsqrt(ms + EPS) * gpost_ref[...]


def _ffn(h, g_pre, w_up, w_down, g_post, layer):
    t, d = h.shape
    d_ff = w_up.shape[2]
    tm = min(512, t)
    tf = 1024
    assert d_ff // tf >= 2
    return pl.pallas_call(
        _ffn_kernel,
        grid=(t // tm, d_ff // tf),
        in_specs=[
            pl.BlockSpec((tm, d), lambda i, f: (i, 0)),
            pl.BlockSpec((1, d), lambda i, f: (0, 0)),
            pl.BlockSpec((None, d, tf), lambda i, f: (layer, 0, f)),
            pl.BlockSpec((None, tf, d), lambda i, f: (layer, f, 0)),
            pl.BlockSpec((1, d), lambda i, f: (0, 0)),
        ],
        out_specs=pl.BlockSpec((tm, d), lambda i, f: (i, 0)),
        out_shape=jax.ShapeDtypeStruct((t, d), F32),
        scratch_shapes=[pltpu.VMEM((tm, d), BF16), pltpu.VMEM((tm, d), F32)],
        compiler_params=_params("parallel", "arbitrary"),
        name="ffn",
    )(h, g_pre, w_up, w_down, g_post)


def _rope_table(positions, head_dim):
    r = head_dim // ROPE_FRACTION
    n_freq = r // 2
    inv_freq = ROPE_THETA ** (-(jnp.arange(0, r, 2, dtype=F32) / r))
    rep = ROPE_LANES // n_freq
    t = positions.size
    ang = positions.reshape(t, 1).astype(F32) * jnp.tile(inv_freq, rep)
    flat = lax.optimization_barrier(ang.reshape(-1))
    cos, sin = lax.optimization_barrier((jnp.cos(flat), jnp.sin(flat)))
    cos, sin = cos.reshape(t, ROPE_LANES), sin.reshape(t, ROPE_LANES)
    one = jnp.ones((cos.shape[0], ROPE_PAIR_SHIFT - ROPE_LANES), F32)
    return jnp.concatenate([cos, one, sin, one], axis=1)


def _rope_tables(positions):
    return jnp.stack([_rope_table(positions, DSA_HEAD_DIM), _rope_table(positions, IDX_DIM)])


def _dsa_lane_sources():
    half = DSA_HEAD_DIM // ROPE_FRACTION // 2
    assert half == ROPE_LANES
    cut = 2 * half + ROPE_PAIR_SHIFT - half
    return (list(range(half)) + list(range(2 * half, cut)) + list(range(half, 2 * half))
            + list(range(cut, DSA_HEAD_DIM)))


def _idx_lane_sources():
    half = IDX_DIM // ROPE_FRACTION // 2
    assert 2 * half == ROPE_LANES

    def a(lo, hi):
        return list(range(lo, hi))

    def b(lo, hi):
        return list(range(IDX_DIM + lo, IDX_DIM + hi))

    return (a(0, half) + b(0, half) + a(2 * half, IDX_DIM)
            + a(half, 2 * half) + b(half, 2 * half) + b(2 * half, IDX_DIM))


def _weight_plan(d_in):
    o_gr = 2 * N_GQ + 2 * N_GV
    o_dq = o_gr + GLA_GATE_RANK
    o_dk = o_dq + N_DSA
    o_dv = o_dk + N_DSA
    o_iq = o_dv + N_DSA
    o_ik = o_iq + N_IQ
    o_iw = o_ik + IDX_DIM
    assert o_iw + IDX_HEADS == d_in
    half = IDX_DIM // ROPE_FRACTION // 2
    none = [-1]

    def shifted(base, sources):
        return [base + s for s in sources]

    dsa, idx, ident = _dsa_lane_sources(), _idx_lane_sources(), list(range(LANES))
    tiles = {
        "rot": ([shifted(o_dq + LANES * h, dsa) for h in range(DSA_HEADS)]
                + [shifted(o_dk + LANES * h, dsa) for h in range(DSA_HEADS)]
                + [shifted(o_iq + LANES * p, idx) for p in range(IDX_HEADS // 2)]),
        "small": [
            (shifted(o_ik, range(half)) + none * half + shifted(o_ik, range(2 * half, IDX_DIM))
             + shifted(o_ik, range(half, 2 * half)) + none * (LANES - ROPE_PAIR_SHIFT - half)),
            (none * half + shifted(o_ik, range(half)) + none * (ROPE_PAIR_SHIFT - 2 * half) + none * half
             + shifted(o_ik, range(half, 2 * half)) + shifted(o_ik, range(2 * half, IDX_DIM))),
            (shifted(o_iw, range(IDX_HEADS)) + shifted(o_gr, range(GLA_GATE_RANK))
             + none * (LANES - IDX_HEADS - GLA_GATE_RANK)),
        ],
        "fm": [shifted(o_dv + LANES * t, ident) for t in range(N_DSA // LANES)],
    }
    mats, plans = [], {}
    for name, tile_list in tiles.items():
        plans[name] = []
        for cols in tile_list:
            assert len(cols) == LANES
            parts = []
            for src_tile in sorted({c // LANES for c in cols if c >= 0}):
                m = np.zeros((LANES, LANES), np.float32)
                for dst, c in enumerate(cols):
                    if c >= 0 and c // LANES == src_tile:
                        m[c % LANES, dst] = 1.0
                for mat_id, known in enumerate(mats):
                    if np.array_equal(known, m):
                        break
                else:
                    mat_id = len(mats)
                    mats.append(m)
                parts.append((src_tile, mat_id))
            plans[name].append(parts)
    return plans, np.stack(mats)


def _prep_kernel(w_ref, p_ref, main_ref, small_ref, fm_ref, *, plans, n_plain, d_in):
    cols = w_ref.shape[1]

    def src_tile(t):
        if (t + 1) * LANES > d_in:
            v = w_ref[t * LANES:d_in, :].astype(BF16)
            return jnp.concatenate([v, jnp.zeros(((t + 1) * LANES - d_in, cols), BF16)], axis=0)
        return w_ref[t * LANES:(t + 1) * LANES, :].astype(BF16)

    def gathered(parts):
        acc = None
        for t, mat_id in parts:
            y = jnp.dot(p_ref[mat_id], src_tile(t), preferred_element_type=F32)
            acc = y if acc is None else acc + y
        return acc.astype(BF16)

    main_ref[:n_plain, :] = w_ref[:n_plain, :].astype(BF16)
    for i, parts in enumerate(plans["rot"]):
        main_ref[n_plain + i * LANES:n_plain + (i + 1) * LANES, :] = gathered(parts)
    for i, parts in enumerate(plans["small"]):
        small_ref[i * LANES:(i + 1) * LANES, :] = gathered(parts)
    for i, parts in enumerate(plans["fm"]):
        fm_ref[i * LANES:(i + 1) * LANES, :] = gathered(parts)


def _prep_weights(w_t):
    depth, d_in, d = w_t.shape
    plans, mats = _weight_plan(d_in)
    cb = min(256, d)
    n_plain = OFF_DQ
    n_fm = N_DSA
    kern = functools.partial(_prep_kernel, plans=plans, n_plain=n_plain, d_in=d_in)
    return pl.pallas_call(
        kern,
        grid=(depth, d // cb),
        in_specs=[
            pl.BlockSpec((None, d_in, cb), lambda l, r: (l, 0, r)),
            pl.BlockSpec(mats.shape, lambda l, r: (0, 0, 0)),
        ],
        out_specs=[
            pl.BlockSpec((None, N_MAIN, cb), lambda l, r: (l, 0, r)),
            pl.BlockSpec((None, N_SMALL, cb), lambda l, r: (l, 0, r)),
            pl.BlockSpec((None, n_fm, cb), lambda l, r: (l, 0, r)),
        ],
        out_shape=[
            jax.ShapeDtypeStruct((depth, N_MAIN, d), BF16),
            jax.ShapeDtypeStruct((depth, N_SMALL, d), BF16),
            jax.ShapeDtypeStruct((depth, n_fm, d), BF16),
        ],
        compiler_params=_params("parallel", "parallel"),
        name="prep_weights",
    )(w_t, jnp.asarray(mats.transpose(0, 2, 1), BF16))


def kernel(x, positions, norm_mix_pre, w_in, gla_wa2, gla_ba, gla_norm, w_out, norm_mix_post,
           norm_ffn_pre, w_up, w_down, norm_ffn_post):
    batch, seq, d = x.shape
    depth = w_in.shape[0]
    assert seq % DSA_BLK == 0
    tab = _rope_tables(positions)
    w2 = jnp.pad(gla_wa2, ((0, 0), (IDX_HEADS, LANES - IDX_HEADS - GLA_GATE_RANK), (0, 0)))
    h = x.reshape(batch * seq, d)
    w_main, w_small, w_fm = _prep_weights(jnp.swapaxes(w_in, 1, 2))
    w_out, w_up, w_down = (w.astype(BF16) for w in (w_out, w_up, w_down))
    for l in range(depth):
        main, ikd, misc, dvt, iwt = _inproj(h, norm_mix_pre[l][None, :], w_main, w_small, w_fm, tab, l)
        o_gla = _gla(main, misc, w2[l], gla_ba[l][None, :], gla_norm[l][None, :], batch, seq)
        o_dsa = _dsa(main, ikd, iwt, dvt, batch, seq)
        h = _outproj(o_gla, o_dsa, h, w_out, norm_mix_post[l][None, :], l)
        h = _ffn(h, norm_ffn_pre[l][None, :], w_up, w_down, norm_ffn_post[l][None, :], l)
    return h.reshape(batch, seq, d)
```
